```python
import math
import jax, jax.numpy as jnp
from jax import lax
import numpy as np

D_MODEL = 2048
BATCH = 1
SEQ = 8192
DEPTH = 1

MEM_LEN = 256
HEAD_DIM = 128
GLA_HEADS = 4
GLA_DK = 64
GLA_DV = 128
GLA_LOWRANK = 16
GLA_TAU = 16.0
GLA_CHUNK = 64
NSA_HEADS = 8
NSA_KV_HEADS = 2
NSA_DK = HEAD_DIM
CMP_LEN = 32
CMP_STRIDE = 16
CMP_HIDDEN = 256
SEL_LEN = 64
SEL_TOPK = 16
WINDOW = 512
MEM_HEADS = 4
MEM_DK = HEAD_DIM
D_MIX = GLA_HEADS * GLA_DV + NSA_HEADS * NSA_DK + MEM_HEADS * MEM_DK
D_FF = 5632
MACARON_W = 0.5
QBLK = 128
ROPE_THETA = 10000.0
EPS = 1e-6
NEG_INF = -1e30
TINY = 1e-30
FORCE_SCORE = 1e4

IN_SIZES = (GLA_HEADS * GLA_DK, GLA_HEADS * GLA_DK, GLA_HEADS * GLA_DV, GLA_HEADS * GLA_DV, GLA_LOWRANK,
            NSA_HEADS * NSA_DK) + (NSA_KV_HEADS * NSA_DK,) * 6 + (NSA_HEADS * 3, MEM_HEADS * MEM_DK)
D_IN = sum(IN_SIZES)

kernel_name = 'hybrid_gla_nsa_memx_macaron'


def rms_norm(x, g):
    xf = x.astype(jnp.float32)
    y = xf * lax.rsqrt(jnp.mean(xf * xf, axis=-1, keepdims=True) + EPS)
    return (y * g.astype(jnp.float32)).astype(x.dtype)


def rope(x, pos):
    half = x.shape[-1] // 2
    inv = ROPE_THETA ** (-jnp.arange(half, dtype=jnp.float32) / half)
    ang = pos.astype(jnp.float32)[:, None, :, None] * inv
    cos, sin = jnp.cos(ang), jnp.sin(ang)
    x1 = x[..., :half].astype(jnp.float32)
    x2 = x[..., half:].astype(jnp.float32)
    return jnp.concatenate([x1 * cos - x2 * sin, x2 * cos + x1 * sin], axis=-1).astype(x.dtype)


def heads(t, h):
    b, s, _ = t.shape
    return t.reshape(b, s, h, -1).transpose(0, 2, 1, 3)


def merge(t):
    b, h, s, d = t.shape
    return t.transpose(0, 2, 1, 3).reshape(b, s, h * d)


def masked_softmax(s, valid):
    s = jnp.where(valid, s, NEG_INF)
    e = jnp.exp(s - jnp.max(s, axis=-1, keepdims=True)) * valid
    return e / jnp.maximum(jnp.sum(e, axis=-1, keepdims=True), TINY)


def swiglu(x, w_gate, w_up, w_down):
    return (jax.nn.silu(x @ w_gate) * (x @ w_up)) @ w_down


def gla_chunked(q, k, v, log_a):
    b_, h_, s_, dk = q.shape
    dv = v.shape[-1]
    c = GLA_CHUNK
    n = s_ // c
    scale = dk ** -0.5
    causal = jnp.tril(jnp.ones((c, c), dtype=bool))

    def chunks(t):
        return jnp.moveaxis(t.reshape(b_, h_, n, c, t.shape[-1]), 2, 0)

    def step(state, inp):
        qc, kc, vc, lac = inp
        qc = qc.astype(jnp.float32) * scale
        kc = kc.astype(jnp.float32)
        vc = vc.astype(jnp.float32)
        bcum = jnp.cumsum(lac, axis=2)
        blast = bcum[:, :, -1:, :]
        o_inter = jnp.einsum('bhcd,bhde->bhce', qc * jnp.exp(bcum), state)
        decay = jnp.exp(jnp.where(causal[:, :, None], bcum[:, :, :, None, :] - bcum[:, :, None, :, :], -jnp.inf))
        attn = jnp.einsum('bhid,bhjd,bhijd->bhij', qc, kc, decay)
        o = o_inter + jnp.einsum('bhij,bhje->bhie', attn, vc)
        state = state * jnp.exp(blast)[:, :, 0, :, None] + jnp.einsum('bhcd,bhce->bhde', kc * jnp.exp(blast - bcum), vc)
        return state, o

    state0 = jnp.zeros((b_, h_, dk, dv), jnp.float32)
    _, o = lax.scan(step, state0, (chunks(q), chunks(k), chunks(v), chunks(log_a)))
    return jnp.moveaxis(o, 0, 2).reshape(b_, h_, s_, dv)


def nsa_attention(q, kc_raw, vc_raw, ks, vs, kw, vw, gates, positions, k_norm,
                  cmp_pos_k, cmp_w1_k, cmp_w2_k, cmp_pos_v, cmp_w1_v, cmp_w2_v):
    b_, hq, s_, dk = q.shape
    g_ = NSA_KV_HEADS
    hpg = hq // g_
    scale = dk ** -0.5

    n_cmp = (s_ - CMP_LEN) // CMP_STRIDE + 1
    cmp_start = jnp.arange(n_cmp) * CMP_STRIDE
    cmp_last = cmp_start + CMP_LEN - 1
    blk_idx = cmp_start[:, None] + jnp.arange(CMP_LEN)[None, :]

    def compress(t, pos_emb, w1, w2):
        blocks = t[:, :, blk_idx, :] + pos_emb
        flat = blocks.reshape(b_, g_, n_cmp, CMP_LEN * dk)
        return jax.nn.silu(flat @ w1) @ w2

    k_cmp = rope(rms_norm(compress(kc_raw, cmp_pos_k, cmp_w1_k, cmp_w2_k), k_norm[0]), positions[:, cmp_last])
    v_cmp = compress(vc_raw, cmp_pos_v, cmp_w1_v, cmp_w2_v)

    n_sel = s_ // SEL_LEN
    topk = min(SEL_TOPK, n_sel)
    sel_start = jnp.arange(n_sel) * SEL_LEN
    sel_ids = jnp.arange(n_sel)
    overlap = jnp.clip(jnp.minimum(cmp_start[:, None] + CMP_LEN, sel_start[None, :] + SEL_LEN)
                       - jnp.maximum(cmp_start[:, None], sel_start[None, :]), 0).astype(jnp.float32) / CMP_STRIDE

    kw_pad = jnp.pad(kw, ((0, 0), (0, 0), (WINDOW, 0), (0, 0)))
    vw_pad = jnp.pad(vw, ((0, 0), (0, 0), (WINDOW, 0), (0, 0)))
    bidx = jnp.arange(b_)[:, None, None, None]
    gidx = jnp.arange(g_)[None, :, None, None]
    qg = q.reshape(b_, g_, hpg, s_, dk)

    def block(qb):
        t0 = qb * QBLK
        tq = t0 + jnp.arange(QBLK)
        qs = lax.dynamic_slice_in_dim(qg, t0, QBLK, axis=3)
        gs = jax.nn.sigmoid(lax.dynamic_slice_in_dim(gates, t0, QBLK, axis=3).astype(jnp.float32))

        s = jnp.einsum('bghqd,bgnd->bghqn', qs, k_cmp).astype(jnp.float32) * scale
        p_cmp = masked_softmax(s, cmp_last[None, :] <= tq[:, None])
        o_cmp = jnp.einsum('bghqn,bgnd->bghqd', p_cmp, v_cmp)

        imp = jnp.einsum('bghqn,nm->bgqm', p_cmp, overlap)
        cur = tq // SEL_LEN
        causal = sel_start[None, :] <= tq[:, None]
        forced = (sel_ids[None, :] == 0) | (sel_ids[None, :] == cur[:, None]) | (sel_ids[None, :] == cur[:, None] - 1)
        score = jnp.where(causal, jnp.where(forced, FORCE_SCORE, imp), -FORCE_SCORE)
        _, sel = lax.top_k(score, topk)
        tok = (sel[..., None] * SEL_LEN + jnp.arange(SEL_LEN)).reshape(b_, g_, QBLK, topk * SEL_LEN)
        k_sel = ks[bidx, gidx, tok]
        v_sel = vs[bidx, gidx, tok]
        s = jnp.einsum('bghqd,bgqtd->bghqt', qs, k_sel).astype(jnp.float32) * scale
        p = masked_softmax(s, (tok <= tq[:, None])[:, :, None])
        o_slc = jnp.einsum('bghqt,bgqtd->bghqd', p, v_sel)

        k_win = lax.dynamic_slice_in_dim(kw_pad, t0, WINDOW + QBLK, axis=2)
        v_win = lax.dynamic_slice_in_dim(vw_pad, t0, WINDOW + QBLK, axis=2)
        kpos = t0 - WINDOW + jnp.arange(WINDOW + QBLK)
        valid = (kpos[None, :] <= tq[:, None]) & (kpos[None, :] > tq[:, None] - WINDOW) & (kpos[None, :] >= 0)
        s = jnp.einsum('bghqd,bgkd->bghqk', qs, k_win).astype(jnp.float32) * scale
        p = masked_softmax(s, valid)
        o_win = jnp.einsum('bghqk,bgkd->bghqd', p, v_win)

        return gs[..., 0:1] * o_cmp + gs[..., 1:2] * o_slc + gs[..., 2:3] * o_win

    out = lax.map(block, jnp.arange(s_ // QBLK))
    return out.transpose(1, 0, 4, 2, 3, 5).reshape(b_, s_, hq * dk)


def memory_cross_attention(q, mem, mem_in_norm, w_mem_kv, mem_q_norm, mem_k_norm):
    qh = rms_norm(heads(q, MEM_HEADS), mem_q_norm)
    kv = rms_norm(mem, mem_in_norm) @ w_mem_kv
    k, v = jnp.split(kv, 2, axis=-1)
    kh = rms_norm(heads(k, MEM_HEADS), mem_k_norm)
    vh = heads(v, MEM_HEADS)
    s = jnp.einsum('bhsd,bhmd->bhsm', qh, kh).astype(jnp.float32) * (MEM_DK ** -0.5)
    p = jax.nn.softmax(s, axis=-1)
    return merge(jnp.einsum('bhsm,bhmd->bhsd', p, vh))


def hybrid_layer(x, mem, positions, ffn1_norm, ffn1_w_gate, ffn1_w_up, ffn1_w_down, mix_norm, w_in,
                 gla_w_a, gla_b_a, gla_o_norm, nsa_q_norm, nsa_k_norm, nsa_cmp_pos_k, nsa_cmp_w1_k,
                 nsa_cmp_w2_k, nsa_cmp_pos_v, nsa_cmp_w1_v, nsa_cmp_w2_v, mem_in_norm, w_mem_kv,
                 mem_q_norm, mem_k_norm, w_out, ffn2_norm, ffn2_w_gate, ffn2_w_up, ffn2_w_down, final_norm):
    dt = x.dtype
    x = x + MACARON_W * swiglu(rms_norm(x, ffn1_norm), ffn1_w_gate, ffn1_w_up, ffn1_w_down)

    h = rms_norm(x, mix_norm)
    proj = h @ w_in
    splits = [int(v) for v in np.cumsum(IN_SIZES)[:-1]]
    (g_q, g_k, g_v, g_r, g_a, n_q, n_kc, n_vc, n_ks, n_vs, n_kw, n_vw, n_g, m_q) = jnp.split(proj, splits, axis=-1)
    b_, s_, _ = x.shape

    log_a = jax.nn.log_sigmoid((g_a @ gla_w_a + gla_b_a).astype(jnp.float32)) / GLA_TAU
    o = gla_chunked(heads(g_q, GLA_HEADS), heads(g_k, GLA_HEADS), heads(g_v, GLA_HEADS), heads(log_a, GLA_HEADS))
    o_gla = (merge(rms_norm(o, gla_o_norm)) * jax.nn.silu(g_r.astype(jnp.float32))).astype(dt)

    q = rope(rms_norm(heads(n_q, NSA_HEADS), nsa_q_norm), positions)
    ks = rope(rms_norm(heads(n_ks, NSA_KV_HEADS), nsa_k_norm[1]), positions)
    kw = rope(rms_norm(heads(n_kw, NSA_KV_HEADS), nsa_k_norm[2]), positions)
    gates = n_g.reshape(b_, s_, NSA_KV_HEADS, NSA_HEADS // NSA_KV_HEADS, 3).transpose(0, 2, 3, 1, 4)
    o_nsa = nsa_attention(q, heads(n_kc, NSA_KV_HEADS), heads(n_vc, NSA_KV_HEADS), ks, heads(n_vs, NSA_KV_HEADS),
                          kw, heads(n_vw, NSA_KV_HEADS), gates, positions, nsa_k_norm,
                          nsa_cmp_pos_k, nsa_cmp_w1_k, nsa_cmp_w2_k, nsa_cmp_pos_v, nsa_cmp_w1_v, nsa_cmp_w2_v).astype(dt)

    o_mem = memory_cross_attention(m_q, mem, mem_in_norm, w_mem_kv, mem_q_norm, mem_k_norm).astype(dt)

    x = x + jnp.concatenate([o_gla, o_nsa, o_mem], axis=-1) @ w_out

    x = x + MACARON_W * swiglu(rms_norm(x, ffn2_norm), ffn2_w_gate, ffn2_w_up, ffn2_w_down)
    return rms_norm(x, final_norm)


def setup_inputs(seed: int = 0) -> dict:
    key = jax.random.key(seed)
    k = jax.random.split(key, 40)
    f32 = jnp.float32

    def w(kk, shape, fan_in):
        return jax.random.normal(kk, (DEPTH,) + shape, f32) * (fan_in ** -0.5)

    def gain(kk, shape):
        return 1.0 + 0.02 * jax.random.normal(kk, (DEPTH,) + shape, f32)

    def small(kk, shape, s):
        return s * jax.random.normal(kk, (DEPTH,) + shape, f32)

    return {
        'x': jax.random.normal(k[0], (BATCH, SEQ, D_MODEL), f32),
        'mem': jax.random.normal(k[1], (BATCH, MEM_LEN, D_MODEL), f32),
        'positions': jnp.broadcast_to(jnp.arange(SEQ, dtype=jnp.int32), (BATCH, SEQ)),
        'ffn1_norm': gain(k[2], (D_MODEL,)),
        'ffn1_w_gate': w(k[3], (D_MODEL, D_FF), D_MODEL),
        'ffn1_w_up': w(k[4], (D_MODEL, D_FF), D_MODEL),
        'ffn1_w_down': w(k[5], (D_FF, D_MODEL), D_FF),
        'mix_norm': gain(k[6], (D_MODEL,)),
        'w_in': w(k[7], (D_MODEL, D_IN), D_MODEL),
        'gla_w_a': w(k[8], (GLA_LOWRANK, GLA_HEADS * GLA_DK), GLA_LOWRANK),
        'gla_b_a': small(k[9], (GLA_HEADS * GLA_DK,), 0.1),
        'gla_o_norm': gain(k[10], (GLA_DV,)),
        'nsa_q_norm': gain(k[11], (NSA_DK,)),
        'nsa_k_norm': gain(k[12], (3, NSA_DK)),
        'nsa_cmp_pos_k': small(k[13], (CMP_LEN, NSA_DK), 0.1),
        'nsa_cmp_w1_k': w(k[14], (CMP_LEN * NSA_DK, CMP_HIDDEN), CMP_LEN * NSA_DK),
        'nsa_cmp_w2_k': w(k[15], (CMP_HIDDEN, NSA_DK), CMP_HIDDEN),
        'nsa_cmp_pos_v': small(k[16], (CMP_LEN, NSA_DK), 0.1),
        'nsa_cmp_w1_v': w(k[17], (CMP_LEN * NSA_DK, CMP_HIDDEN), CMP_LEN * NSA_DK),
        'nsa_cmp_w2_v': w(k[18], (CMP_HIDDEN, NSA_DK), CMP_HIDDEN),
        'mem_in_norm': gain(k[19], (D_MODEL,)),
        'w_mem_kv': w(k[20], (D_MODEL, 2 * MEM_HEADS * MEM_DK), D_MODEL),
        'mem_q_norm': gain(k[21], (MEM_DK,)),
        'mem_k_norm': gain(k[22], (MEM_DK,)),
        'w_out': w(k[23], (D_MIX, D_MODEL), D_MIX),
        'ffn2_norm': gain(k[24], (D_MODEL,)),
        'ffn2_w_gate': w(k[25], (D_MODEL, D_FF), D_MODEL),
        'ffn2_w_up': w(k[26], (D_MODEL, D_FF), D_MODEL),
        'ffn2_w_down': w(k[27], (D_FF, D_MODEL), D_FF),
        'final_norm': gain(k[28], (D_MODEL,)),
    }


def reference(x, mem, positions, ffn1_norm, ffn1_w_gate, ffn1_w_up, ffn1_w_down, mix_norm, w_in,
              gla_w_a, gla_b_a, gla_o_norm, nsa_q_norm, nsa_k_norm, nsa_cmp_pos_k, nsa_cmp_w1_k,
              nsa_cmp_w2_k, nsa_cmp_pos_v, nsa_cmp_w1_v, nsa_cmp_w2_v, mem_in_norm, w_mem_kv,
              mem_q_norm, mem_k_norm, w_out, ffn2_norm, ffn2_w_gate, ffn2_w_up, ffn2_w_down, final_norm):
    for l in range(DEPTH):
        x = hybrid_layer(x, mem, positions, ffn1_norm[l], ffn1_w_gate[l], ffn1_w_up[l], ffn1_w_down[l],
                         mix_norm[l], w_in[l], gla_w_a[l], gla_b_a[l], gla_o_norm[l], nsa_q_norm[l],
                         nsa_k_norm[l], nsa_cmp_pos_k[l], nsa_cmp_w1_k[l], nsa_cmp_w2_k[l], nsa_cmp_pos_v[l],
                         nsa_cmp_w1_v[l], nsa_cmp_w2_v[l], mem_in_norm[l], w_mem_kv[l], mem_q_norm[l],
                         mem_k_norm[l], w_out[l], ffn2_norm[l], ffn2_w_gate[l], ffn2_w_up[l], ffn2_w_down[l],
                         final_norm[l])
    return x
```

```python
import functools

import numpy as np
import jax
import jax.numpy as jnp
from jax import lax
from jax.experimental import pallas as pl
from jax.experimental.pallas import tpu as pltpu

F32 = jnp.float32
BF16 = jnp.bfloat16

HEAD_DIM = 128
GLA_HEADS = 4
GLA_DK = 64
GLA_DV = 128
GLA_LOWRANK = 16
GLA_TAU = 16.0
NSA_HEADS = 8
NSA_KV_HEADS = 2
NSA_HPG = NSA_HEADS // NSA_KV_HEADS
CMP_LEN = 32
CMP_STRIDE = 16
SEL_LEN = 64
SEL_TOPK = 16
WINDOW = 512
MEM_HEADS = 4
MACARON_W = 0.5
QBLK = 128
ROPE_THETA = 10000.0
EPS = 1e-6
NEG_INF = -1e30
TINY = 1e-30
FORCE_SCORE = 1e4

LANES = 128
VMEM_LIMIT = 56 * 1024 * 1024

GLA_SUB = 16
GLA_ROWS = 128
SEL_KT = 256
SEL_MASK_BIAS = -32768.0
SEL_PAD_SCORE = -3e4


def _dot(a, b):
    return jnp.dot(a, b, preferred_element_type=F32)


def _dot_nt(a, b):
    return lax.dot_general(a, b, (((1,), (1,)), ((), ())), preferred_element_type=F32)


def _dot_tn(a, b):
    return lax.dot_general(a, b, (((0,), (0,)), ((), ())), preferred_element_type=F32)


def _split3(x):
    hi = x.astype(BF16)
    r1 = x - hi.astype(F32)
    mid = r1.astype(BF16)
    lo = (r1 - mid.astype(F32)).astype(BF16)
    return hi, mid, lo


def _rms(x, g):
    return x * lax.rsqrt(jnp.mean(x * x, axis=-1, keepdims=True) + EPS) * g


def _sigmoid(x):
    return 1.0 / (1.0 + jnp.exp(-x))


def _params(sem):
    return pltpu.CompilerParams(dimension_semantics=sem, vmem_limit_bytes=VMEM_LIMIT)


def _resident(shape):
    nd = len(shape)
    return pl.BlockSpec(shape, lambda *_: (0,) * nd, pipeline_mode=pl.Buffered(1))


def _ffn_body(*refs, final, nf):
    if final:
        x_ref, g_ref, wg_ref, wu_ref, wd_ref, fg_ref, o_ref, h_ref = refs
    else:
        x_ref, g_ref, wg_ref, wu_ref, wd_ref, o_ref, h_ref = refs
    f = pl.program_id(1)

    @pl.when(f == 0)
    def _():
        x = x_ref[...]
        h_ref[...] = _rms(x, g_ref[...]).astype(BF16)
        o_ref[...] = x

    h = h_ref[...]
    g = _dot(h, wg_ref[...])
    u = _dot(h, wu_ref[...])
    a = (g * _sigmoid(g)) * u * MACARON_W
    o_ref[...] += _dot(a.astype(BF16), wd_ref[...])

    if final:
        @pl.when(f == nf - 1)
        def _():
            o_ref[...] = _rms(o_ref[...], fg_ref[...])


def _ffn(x, norm_g, wg, wu, wd, final_g=None, *, tm=512, tf=512):
    s, d = x.shape
    ff = wg.shape[1]
    nf = ff // tf
    final = final_g is not None
    in_specs = [
        pl.BlockSpec((tm, d), lambda i, f: (i, 0)),
        pl.BlockSpec((1, d), lambda i, f: (0, 0)),
        pl.BlockSpec((d, tf), lambda i, f: (0, f)),
        pl.BlockSpec((d, tf), lambda i, f: (0, f)),
        pl.BlockSpec((tf, d), lambda i, f: (f, 0)),
    ]
    args = [x, norm_g, wg, wu, wd]
    if final:
        in_specs.append(pl.BlockSpec((1, d), lambda i, f: (0, 0)))
        args.append(final_g)
    return pl.pallas_call(
        functools.partial(_ffn_body, final=final, nf=nf),
        grid=(s // tm, nf),
        in_specs=in_specs,
        out_specs=pl.BlockSpec((tm, d), lambda i, f: (i, 0)),
        out_shape=jax.ShapeDtypeStruct((s, d), F32),
        scratch_shapes=[pltpu.VMEM((tm, d), BF16)],
        compiler_params=_params(("parallel", "arbitrary")),
        name="ffn_final" if final else "ffn",
    )(*args)


_C_NQ = 0
_C_GV = 1024
_C_GR = 1536
_C_MQ = 2048
_C_GQK = 2560
_C_SEL = 3072
_C_WIN = 3584
_C_KCVC = 4096
_C_GA = 4608
_C_NG = 4736
_C_TOTAL = 4992


def _rope(x, cos, sin_signed):
    return x * cos + pltpu.roll(x, HEAD_DIM // 2, 1) * sin_signed


def _proj_body(x_ref, g_ref, w_ref, pos_ref, inv_ref, sgn_ref, qn_ref, kn_ref, mqn_ref,
               q_ref, ksel_ref, vsel_ref, kwin_ref, vwin_ref, mq_ref, gqk_ref, gv_ref, gr_ref,
               kcvc_ref, ga_ref, ng_ref, *, tm):
    i = pl.program_id(0)
    h = _rms(x_ref[...], g_ref[...]).astype(BF16)

    def proj(c0, width):
        return _dot(h, w_ref[:, c0:c0 + width])

    ang = pos_ref[...].astype(F32) * inv_ref[...]
    cos = jnp.cos(ang)
    sin_signed = jnp.sin(ang) * sgn_ref[...]

    def norm_rope(t, gain):
        return _rope(_rms(t, gain), cos, sin_signed)

    scale = HEAD_DIM ** -0.5
    nq = proj(_C_NQ, NSA_HEADS * HEAD_DIM)
    for hd in range(NSA_HEADS):
        sl = slice(hd * HEAD_DIM, (hd + 1) * HEAD_DIM)
        q_ref[:, sl] = (norm_rope(nq[:, sl], qn_ref[...]) * scale).astype(BF16)

    tok = i * tm + lax.broadcasted_iota(jnp.int32, (tm, LANES), 0)
    blk = lax.broadcasted_iota(jnp.int32, (tm, LANES), 1)
    onehot = jnp.where(lax.shift_right_logical(tok, 6) == blk, 1.0, 0.0).astype(BF16)
    sel = proj(_C_SEL, 512)
    win = proj(_C_WIN, 512)
    for g in range(NSA_KV_HEADS):
        sl = slice(g * HEAD_DIM, (g + 1) * HEAD_DIM)
        slv = slice(256 + g * HEAD_DIM, 256 + (g + 1) * HEAD_DIM)
        ksel_ref[g, :, 0:HEAD_DIM] = norm_rope(sel[:, sl], kn_ref[1:2, :]).astype(BF16)
        ksel_ref[g, :, HEAD_DIM:2 * HEAD_DIM] = onehot
        vsel_ref[g] = sel[:, slv].astype(BF16)
        kwin_ref[g] = norm_rope(win[:, sl], kn_ref[2:3, :]).astype(BF16)
        vwin_ref[g] = win[:, slv].astype(BF16)

    mq = proj(_C_MQ, MEM_HEADS * HEAD_DIM)
    for hd in range(MEM_HEADS):
        sl = slice(hd * HEAD_DIM, (hd + 1) * HEAD_DIM)
        mq_ref[:, sl] = (_rms(mq[:, sl], mqn_ref[...]) * scale).astype(BF16)

    gqk_ref[...] = proj(_C_GQK, 512)
    gv_ref[...] = proj(_C_GV, 512).astype(BF16)
    gr_ref[...] = proj(_C_GR, 512)
    kcvc = proj(_C_KCVC, 512)
    for j in range(2 * NSA_KV_HEADS):
        kcvc_ref[j] = kcvc[:, j * HEAD_DIM:(j + 1) * HEAD_DIM]
    ga_ref[...] = proj(_C_GA, LANES)
    ng = proj(_C_NG, 2 * LANES)
    for g in range(NSA_KV_HEADS):
        ng_ref[g] = ng[:, g * LANES:(g + 1) * LANES]


def _proj(x1, mix_g, w_perm, pos_col, inv128, sgn128, q_norm, k_norm, mq_norm, *, tm=512):
    s, d = x1.shape
    row = lambda w: pl.BlockSpec((tm, w), lambda i: (i, 0))
    grp = lambda w: pl.BlockSpec((NSA_KV_HEADS, tm, w), lambda i: (0, i, 0))
    out_shapes = [
        (jax.ShapeDtypeStruct((s, NSA_HEADS * HEAD_DIM), BF16), row(NSA_HEADS * HEAD_DIM)),
        (jax.ShapeDtypeStruct((NSA_KV_HEADS, s, 2 * HEAD_DIM), BF16), grp(2 * HEAD_DIM)),
        (jax.ShapeDtypeStruct((NSA_KV_HEADS, s, HEAD_DIM), BF16), grp(HEAD_DIM)),
        (jax.ShapeDtypeStruct((NSA_KV_HEADS, s, HEAD_DIM), BF16), grp(HEAD_DIM)),
        (jax.ShapeDtypeStruct((NSA_KV_HEADS, s, HEAD_DIM), BF16), grp(HEAD_DIM)),
        (jax.ShapeDtypeStruct((s, MEM_HEADS * HEAD_DIM), BF16), row(MEM_HEADS * HEAD_DIM)),
        (jax.ShapeDtypeStruct((s, 512), F32), row(512)),
        (jax.ShapeDtypeStruct((s, 512), BF16), row(512)),
        (jax.ShapeDtypeStruct((s, 512), F32), row(512)),
        (jax.ShapeDtypeStruct((2 * NSA_KV_HEADS, s, HEAD_DIM), F32),
         pl.BlockSpec((2 * NSA_KV_HEADS, tm, HEAD_DIM), lambda i: (0, i, 0))),
        (jax.ShapeDtypeStruct((s, LANES), F32), row(LANES)),
        (jax.ShapeDtypeStruct((NSA_KV_HEADS, s, LANES), F32), grp(LANES)),
    ]
    return pl.pallas_call(
        functools.partial(_proj_body, tm=tm),
        grid=(s // tm,),
        in_specs=[
            row(d),
            _resident((1, d)),
            _resident(w_perm.shape),
            pl.BlockSpec((tm, 1), lambda i: (i, 0)),
            _resident((1, LANES)), _resident((1, LANES)),
            _resident((1, HEAD_DIM)), _resident((3, HEAD_DIM)), _resident((1, HEAD_DIM)),
        ],
        out_specs=[o[1] for o in out_shapes],
        out_shape=[o[0] for o in out_shapes],
        compiler_params=_params(("parallel",)),
        name="proj",
    )(x1, mix_g, w_perm, pos_col, inv128, sgn128, q_norm, k_norm, mq_norm)


def _compress_body(kcvc_ref, w1k_ref, w2k_ref, pek_ref, w1v_ref, w2v_ref, pev_ref, kn_ref,
                   pos_ref, inv_ref, sgn_ref, kcmp_ref, vcmp_ref, *, units):
    half = CMP_LEN // 2
    ang = pos_ref[...].astype(F32) * inv_ref[...]
    cos = jnp.cos(ang)
    sin_signed = jnp.sin(ang) * sgn_ref[...]
    for kind, (w1_ref, w2_ref, pe_ref) in enumerate(((w1k_ref, w2k_ref, pek_ref),
                                                     (w1v_ref, w2v_ref, pev_ref))):
        for g in range(NSA_KV_HEADS):
            slab = kind * NSA_KV_HEADS + g
            a = jnp.zeros((units, w1_ref.shape[1]), F32)
            b = jnp.zeros((units, w1_ref.shape[1]), F32)
            for l in range(half):
                t = kcvc_ref[slab, pl.ds(l, units, stride=CMP_STRIDE), :]
                a = a + _dot((t + pe_ref[l:l + 1, :]).astype(BF16),
                             w1_ref[l * HEAD_DIM:(l + 1) * HEAD_DIM, :])
                b = b + _dot((t + pe_ref[half + l:half + l + 1, :]).astype(BF16),
                             w1_ref[(half + l) * HEAD_DIM:(half + l + 1) * HEAD_DIM, :])
            hid = a + pltpu.roll(b, units - 1, 0)
            c = _dot((hid * _sigmoid(hid)).astype(BF16), w2_ref[...])
            if kind == 0:
                c = _rope(_rms(c, kn_ref[0:1, :]), cos, sin_signed)
                kcmp_ref[g] = c.astype(BF16)
            else:
                vcmp_ref[g] = c.astype(BF16)


def _compress(kcvc, w1k, w2k, pek, w1v, w2v, pev, k_norm, pos_cmp, inv128, sgn128):
    s = kcvc.shape[1]
    units = s // CMP_STRIDE
    shp = jax.ShapeDtypeStruct((NSA_KV_HEADS, units, HEAD_DIM), BF16)
    return pl.pallas_call(
        functools.partial(_compress_body, units=units),
        out_shape=[shp, shp],
        compiler_params=pltpu.CompilerParams(vmem_limit_bytes=VMEM_LIMIT),
        name="compress",
    )(kcvc, w1k, w2k, pek, w1v, w2v, pev, k_norm, pos_cmp, inv128, sgn128)


def _gla_body(gqk_ref, gv_ref, gr_ref, ga_ref, wa_ref, ba_ref, on_ref, tcum_ref, bd_ref,
              o_ref, st_ref, q_s, k_s, b_s, o_s):
    rows = GLA_ROWS
    npair = GLA_HEADS // 2

    @pl.when(pl.program_id(0) == 0)
    def _():
        st_ref[...] = jnp.zeros_like(st_ref)

    ga_hi, ga_mid, ga_lo = _split3(ga_ref[...])
    z = ba_ref[...]
    for ga_t in (ga_hi, ga_mid, ga_lo):
        for wa_t in _split3(wa_ref[...]):
            z = z + _dot(ga_t, wa_t)
    la = (jnp.minimum(z, 0.0) - jnp.log(1.0 + jnp.exp(-jnp.abs(z)))) / GLA_TAU
    tc = tcum_ref[...]
    bcum = jnp.zeros_like(la)
    for la_t in _split3(la):
        bcum = bcum + _dot(tc, la_t)
    b_s[...] = bcum
    q_s[...] = gqk_ref[:, 0:256] * (GLA_DK ** -0.5)
    k_s[...] = gqk_ref[:, 256:512]

    row_i = lax.broadcasted_iota(jnp.int32, (GLA_SUB, LANES), 0)
    lane_i = lax.broadcasted_iota(jnp.int32, (GLA_SUB, LANES), 1)
    head0 = lane_i < GLA_DK

    def sub_block(sb, carry):
        r0 = pl.multiple_of(sb * GLA_SUB, GLA_SUB)
        for p in range(npair):
            cs = slice(p * LANES, (p + 1) * LANES)
            qs = q_s[pl.ds(r0, GLA_SUB), cs]
            kk = k_s[pl.ds(r0, GLA_SUB), cs]
            bb = b_s[pl.ds(r0, GLA_SUB), cs]
            vp = gv_ref[pl.ds(r0, GLA_SUB), p * 2 * GLA_DV:(p + 1) * 2 * GLA_DV]
            vpf = vp.astype(F32)
            blast = bb[GLA_SUB - 1:GLA_SUB, :]
            st = st_ref[p]
            o_inter = _dot_nt((qs * jnp.exp(bb)).astype(BF16), st.astype(BF16))
            acc0 = jnp.zeros((GLA_SUB, GLA_DV), F32)
            acc1 = jnp.zeros((GLA_SUB, GLA_DV), F32)
            for j in range(GLA_SUB):
                dlt = jnp.where(row_i >= j, bb - bb[j:j + 1, :], NEG_INF)
                xj = qs * jnp.exp(dlt) * kk[j:j + 1, :]
                a0 = jnp.sum(jnp.where(head0, xj, 0.0), axis=-1, keepdims=True)
                a1 = jnp.sum(jnp.where(head0, 0.0, xj), axis=-1, keepdims=True)
                acc0 = acc0 + a0 * vpf[j:j + 1, 0:GLA_DV]
                acc1 = acc1 + a1 * vpf[j:j + 1, GLA_DV:2 * GLA_DV]
            o_s[pl.ds(r0, GLA_SUB), p * 2 * GLA_DV:p * 2 * GLA_DV + GLA_DV] = o_inter[:, 0:GLA_DV] + acc0
            o_s[pl.ds(r0, GLA_SUB), p * 2 * GLA_DV + GLA_DV:(p + 1) * 2 * GLA_DV] = o_inter[:, GLA_DV:] + acc1
            kd = (kk * jnp.exp(blast - bb)).astype(BF16)
            upd = _dot_tn(vp, kd)
            st_ref[p] = st * jnp.exp(blast) + upd * bd_ref[...]
        return carry

    lax.fori_loop(0, rows // GLA_SUB, sub_block, 0)

    gr = gr_ref[...]
    for hd in range(GLA_HEADS):
        sl = slice(hd * GLA_DV, (hd + 1) * GLA_DV)
        r = gr[:, sl]
        o_ref[:, sl] = (_rms(o_s[:, sl], on_ref[...]) * (r * _sigmoid(r))).astype(BF16)


def _gla(gqk, gv, gr, ga, wa_pad, ba, o_norm):
    s = gqk.shape[0]
    rows = GLA_ROWS
    idx = np.arange(rows)
    tcum = ((idx[:, None] >= idx[None, :]) & (idx[:, None] // GLA_SUB == idx[None, :] // GLA_SUB))
    tcum = jnp.asarray(tcum, BF16)
    r256 = np.arange(2 * GLA_DV)[:, None] // GLA_DV
    c128 = np.arange(LANES)[None, :] // GLA_DK
    bdmask = jnp.asarray(r256 == c128, F32)
    row = lambda w: pl.BlockSpec((rows, w), lambda i: (i, 0))
    return pl.pallas_call(
        _gla_body,
        grid=(s // rows,),
        in_specs=[row(512), row(512), row(512), row(LANES),
                  _resident(wa_pad.shape), _resident(ba.shape), _resident(o_norm.shape),
                  _resident(tcum.shape), _resident(bdmask.shape)],
        out_specs=row(512),
        out_shape=jax.ShapeDtypeStruct((s, GLA_HEADS * GLA_DV), BF16),
        scratch_shapes=[pltpu.VMEM((GLA_HEADS // 2, 2 * GLA_DV, LANES), F32),
                        pltpu.VMEM((rows, 256), F32), pltpu.VMEM((rows, 256), F32),
                        pltpu.VMEM((rows, 256), F32), pltpu.VMEM((rows, 512), F32)],
        compiler_params=_params(("arbitrary",)),
        name="gla",
    )(gqk, gv, gr, ga, wa_pad, ba, o_norm, tcum, bdmask)


def _nsa_body(q_ref, kcmp_ref, vcmp_ref, ksel_ref, vsel_ref, kwin_ref, vwin_ref, ng_ref, ovl_ref,
              o_ref, qext_s, *, n_sel, s_len):
    qb = pl.program_id(1)
    t0 = qb * QBLK
    rows = NSA_HPG * QBLK
    ncmp = kcmp_ref.shape[0]

    for hd in range(NSA_HPG):
        qext_s[hd * QBLK:(hd + 1) * QBLK, 0:HEAD_DIM] = q_ref[:, hd * HEAD_DIM:(hd + 1) * HEAD_DIM]
    q4 = qext_s[:, 0:HEAD_DIM]

    def softmax_masked(sc, valid):
        sc = jnp.where(valid, sc, NEG_INF)
        e = jnp.where(valid, jnp.exp(sc - jnp.max(sc, axis=-1, keepdims=True)), 0.0)
        return e / jnp.maximum(jnp.sum(e, axis=-1, keepdims=True), TINY)

    tq_c = t0 + (lax.broadcasted_iota(jnp.int32, (rows, ncmp), 0) & (QBLK - 1))
    n_i = lax.broadcasted_iota(jnp.int32, (rows, ncmp), 1)
    p_cmp = softmax_masked(_dot_nt(q4, kcmp_ref[...]), n_i * CMP_STRIDE + (CMP_LEN - 1) <= tq_c)
    o_cmp = _dot(p_cmp.astype(BF16), vcmp_ref[...])

    psum = p_cmp[0:QBLK]
    for hd in range(1, NSA_HPG):
        psum = psum + p_cmp[hd * QBLK:(hd + 1) * QBLK]
    imp = jnp.zeros((QBLK, LANES), F32)
    for p_t in _split3(psum):
        imp = imp + _dot(p_t, ovl_ref[...])
    tq = t0 + lax.broadcasted_iota(jnp.int32, (QBLK, LANES), 0)
    m_i = lax.broadcasted_iota(jnp.int32, (QBLK, LANES), 1)
    cur = lax.shift_right_logical(tq, 6)
    forced = (m_i == 0) | (m_i == cur) | (m_i == cur - 1)
    score = jnp.where(m_i * SEL_LEN <= tq, jnp.where(forced, FORCE_SCORE, imp), -FORCE_SCORE)
    score = jnp.where(m_i < n_sel, score, SEL_PAD_SCORE)
    m_f = m_i.astype(F32)
    bias = jnp.full((QBLK, LANES), SEL_MASK_BIAS, F32)
    for _ in range(min(SEL_TOPK, n_sel)):
        mx = jnp.max(score, axis=-1, keepdims=True)
        first = jnp.min(jnp.where(score == mx, m_f, float(LANES)), axis=-1, keepdims=True)
        pick = m_f == first
        bias = jnp.where(pick, 0.0, bias)
        score = jnp.where(pick, -jnp.inf, score)
    bias = bias.astype(BF16)
    for hd in range(NSA_HPG):
        qext_s[hd * QBLK:(hd + 1) * QBLK, HEAD_DIM:2 * HEAD_DIM] = bias

    qext = qext_s[...]
    tq_s = t0 + (lax.broadcasted_iota(jnp.int32, (rows, SEL_KT), 0) & (QBLK - 1))
    j_i = lax.broadcasted_iota(jnp.int32, (rows, SEL_KT), 1)

    def sel_step(kt, carry):
        m, l, acc = carry
        k0 = pl.multiple_of(kt * SEL_KT, SEL_KT)
        sc = _dot_nt(qext, ksel_ref[pl.ds(k0, SEL_KT), :])
        sc = jnp.where(k0 + j_i <= tq_s, sc, NEG_INF)
        m_new = jnp.maximum(m, jnp.max(sc, axis=-1, keepdims=True))
        alpha = jnp.exp(m - m_new)
        p = jnp.exp(sc - m_new)
        l = alpha * l + jnp.sum(p, axis=-1, keepdims=True)
        acc = alpha * acc + _dot(p.astype(BF16), vsel_ref[pl.ds(k0, SEL_KT), :])
        return m_new, l, acc

    n_kt = (t0 + QBLK + SEL_KT - 1) // SEL_KT
    init = (jnp.full((rows, 1), NEG_INF, F32), jnp.zeros((rows, 1), F32), jnp.zeros((rows, HEAD_DIM), F32))
    _, l_sel, acc_sel = lax.fori_loop(0, n_kt, sel_step, init)
    o_slc = acc_sel / l_sel

    wlen = WINDOW + QBLK
    w0 = pl.multiple_of(jnp.maximum(t0 - WINDOW, 0), QBLK)
    tq_w = t0 + (lax.broadcasted_iota(jnp.int32, (rows, wlen), 0) & (QBLK - 1))
    kpos = w0 + lax.broadcasted_iota(jnp.int32, (rows, wlen), 1)
    p_win = softmax_masked(_dot_nt(q4, kwin_ref[pl.ds(w0, wlen), :]),
                           (kpos <= tq_w) & (kpos > tq_w - WINDOW))
    o_win = _dot(p_win.astype(BF16), vwin_ref[pl.ds(w0, wlen), :])

    gates = _sigmoid(ng_ref[...])
    for hd in range(NSA_HPG):
        rs = slice(hd * QBLK, (hd + 1) * QBLK)
        out = (gates[:, 3 * hd:3 * hd + 1] * o_cmp[rs]
               + gates[:, 3 * hd + 1:3 * hd + 2] * o_slc[rs]
               + gates[:, 3 * hd + 2:3 * hd + 3] * o_win[rs])
        o_ref[:, hd * HEAD_DIM:(hd + 1) * HEAD_DIM] = out.astype(BF16)


def _nsa(q, kcmp, vcmp, ksel, vsel, kwin, vwin, ng, overlap):
    s = q.shape[0]
    n_sel = s // SEL_LEN
    assert n_sel <= LANES and s >= WINDOW + QBLK
    ncmp = kcmp.shape[1]
    grp = lambda r, w: pl.BlockSpec((None, r, w), lambda g, b: (g, 0, 0))
    return pl.pallas_call(
        functools.partial(_nsa_body, n_sel=n_sel, s_len=s),
        grid=(NSA_KV_HEADS, s // QBLK),
        in_specs=[
            pl.BlockSpec((QBLK, NSA_HPG * HEAD_DIM), lambda g, b: (b, g)),
            grp(ncmp, HEAD_DIM), grp(ncmp, HEAD_DIM),
            grp(s, 2 * HEAD_DIM), grp(s, HEAD_DIM), grp(s, HEAD_DIM), grp(s, HEAD_DIM),
            pl.BlockSpec((None, QBLK, LANES), lambda g, b: (g, b, 0)),
            pl.BlockSpec(overlap.shape, lambda g, b: (0, 0)),
        ],
        out_specs=pl.BlockSpec((QBLK, NSA_HPG * HEAD_DIM), lambda g, b: (b, g)),
        out_shape=jax.ShapeDtypeStruct((s, NSA_HEADS * HEAD_DIM), BF16),
        scratch_shapes=[pltpu.VMEM((NSA_HPG * QBLK, 2 * HEAD_DIM), BF16)],
        compiler_params=_params(("parallel", "arbitrary")),
        name="nsa",
    )(q, kcmp, vcmp, ksel, vsel, kwin, vwin, ng, overlap)


def _memkv_body(mem_ref, g_ref, w_ref, kn_ref, k_ref, v_ref):
    kv = _dot(_rms(mem_ref[...], g_ref[...]).astype(BF16), w_ref[...])
    width = MEM_HEADS * HEAD_DIM
    for hd in range(MEM_HEADS):
        sl = slice(hd * HEAD_DIM, (hd + 1) * HEAD_DIM)
        k_ref[:, sl] = _rms(kv[:, sl], kn_ref[...]).astype(BF16)
    v_ref[...] = kv[:, width:].astype(BF16)


def _memkv(mem, in_g, w_kv, k_norm):
    m = mem.shape[0]
    shp = jax.ShapeDtypeStruct((m, MEM_HEADS * HEAD_DIM), BF16)
    return pl.pallas_call(
        _memkv_body, out_shape=[shp, shp],
        compiler_params=pltpu.CompilerParams(vmem_limit_bytes=VMEM_LIMIT),
        name="memkv",
    )(mem, in_g, w_kv, k_norm)


def _memattn_body(q_ref, k_ref, v_ref, o_ref):
    for hd in range(MEM_HEADS):
        sl = slice(hd * HEAD_DIM, (hd + 1) * HEAD_DIM)
        sc = _dot_nt(q_ref[:, sl], k_ref[:, sl])
        e = jnp.exp(sc - jnp.max(sc, axis=-1, keepdims=True))
        p = e / jnp.sum(e, axis=-1, keepdims=True)
        o_ref[:, sl] = _dot(p.astype(BF16), v_ref[:, sl]).astype(BF16)


def _memattn(q, k, v, *, tm=512):
    s, w = q.shape
    return pl.pallas_call(
        _memattn_body,
        grid=(s // tm,),
        in_specs=[pl.BlockSpec((tm, w), lambda i: (i, 0)), _resident(k.shape), _resident(v.shape)],
        out_specs=pl.BlockSpec((tm, w), lambda i: (i, 0)),
        out_shape=jax.ShapeDtypeStruct((s, w), BF16),
        compiler_params=_params(("parallel",)),
        name="memattn",
    )(q, k, v)


def _outproj_body(x_ref, a_ref, b_ref, c_ref, w_ref, o_ref):
    na, nb = a_ref.shape[1], b_ref.shape[1]
    o_ref[...] = (x_ref[...] + _dot(a_ref[...], w_ref[0:na, :]) + _dot(b_ref[...], w_ref[na:na + nb, :])
                  + _dot(c_ref[...], w_ref[na + nb:, :]))


def _outproj(x1, o_gla, o_nsa, o_mem, w_out, *, tm=512):
    s, d = x1.shape
    row = lambda w: pl.BlockSpec((tm, w), lambda i: (i, 0))
    return pl.pallas_call(
        _outproj_body,
        grid=(s // tm,),
        in_specs=[row(d), row(o_gla.shape[1]), row(o_nsa.shape[1]), row(o_mem.shape[1]),
                  _resident(w_out.shape)],
        out_specs=row(d),
        out_shape=jax.ShapeDtypeStruct((s, d), F32),
        compiler_params=_params(("parallel",)),
        name="outproj",
    )(x1, o_gla, o_nsa, o_mem, w_out)


def _permute_w_in(w_in):
    d = w_in.shape[0]
    sizes = (GLA_HEADS * GLA_DK, GLA_HEADS * GLA_DK, GLA_HEADS * GLA_DV, GLA_HEADS * GLA_DV, GLA_LOWRANK,
             NSA_HEADS * HEAD_DIM) + (NSA_KV_HEADS * HEAD_DIM,) * 6 + (NSA_HEADS * 3, MEM_HEADS * HEAD_DIM)
    offs = np.concatenate([[0], np.cumsum(sizes)])
    (g_q, g_k, g_v, g_r, g_a, n_q, n_kc, n_vc, n_ks, n_vs, n_kw, n_vw, n_g, m_q) = [
        w_in[:, int(offs[k]):int(offs[k + 1])] for k in range(len(sizes))]
    zeros = lambda n: jnp.zeros((d, n), w_in.dtype)
    per_g = NSA_HPG * 3
    cols = [n_q, g_v, g_r, m_q, g_q, g_k, n_ks, n_vs, n_kw, n_vw, n_kc, n_vc,
            g_a, zeros(LANES - GLA_LOWRANK)]
    for g in range(NSA_KV_HEADS):
        cols += [n_g[:, g * per_g:(g + 1) * per_g], zeros(LANES - per_g)]
    w = jnp.concatenate(cols, axis=1)
    assert w.shape[1] == _C_TOTAL
    return w.astype(BF16)


def _layer(x, mem, positions, ffn1_norm, ffn1_w_gate, ffn1_w_up, ffn1_w_down, mix_norm, w_in,
           gla_w_a, gla_b_a, gla_o_norm, nsa_q_norm, nsa_k_norm, nsa_cmp_pos_k, nsa_cmp_w1_k,
           nsa_cmp_w2_k, nsa_cmp_pos_v, nsa_cmp_w1_v, nsa_cmp_w2_v, mem_in_norm, w_mem_kv,
           mem_q_norm, mem_k_norm, w_out, ffn2_norm, ffn2_w_gate, ffn2_w_up, ffn2_w_down, final_norm):
    s, d = x.shape
    row = lambda v: v.reshape(1, -1)
    bf = lambda v: v.astype(BF16)

    x1 = _ffn(x, row(ffn1_norm), bf(ffn1_w_gate), bf(ffn1_w_up), bf(ffn1_w_down))

    half = HEAD_DIM // 2
    inv = ROPE_THETA ** (-jnp.arange(half, dtype=F32) / half)
    inv128 = jnp.concatenate([inv, inv]).reshape(1, HEAD_DIM)
    sgn128 = jnp.concatenate([-jnp.ones((half,), F32), jnp.ones((half,), F32)]).reshape(1, HEAD_DIM)
    (q, ksel, vsel, kwin, vwin, mq, gqk, gv, gr, kcvc, ga, ng) = _proj(
        x1, row(mix_norm), _permute_w_in(w_in), positions.reshape(s, 1), inv128, sgn128,
        row(nsa_q_norm), nsa_k_norm, row(mem_q_norm))

    wa_pad = jnp.zeros((LANES, gla_w_a.shape[1]), F32).at[:GLA_LOWRANK].set(gla_w_a)
    o_gla = _gla(gqk, gv, gr, ga, wa_pad, row(gla_b_a), row(gla_o_norm))

    units = s // CMP_STRIDE
    n_cmp = (s - CMP_LEN) // CMP_STRIDE + 1
    cmp_last = jnp.arange(units) * CMP_STRIDE + CMP_LEN - 1
    pos_cmp = positions[jnp.minimum(cmp_last, s - 1)].reshape(units, 1)
    kcmp, vcmp = _compress(kcvc, bf(nsa_cmp_w1_k), bf(nsa_cmp_w2_k), nsa_cmp_pos_k,
                           bf(nsa_cmp_w1_v), bf(nsa_cmp_w2_v), nsa_cmp_pos_v, nsa_k_norm,
                           pos_cmp, inv128, sgn128)
    n_sel = s // SEL_LEN
    cmp_start = np.arange(units) * CMP_STRIDE
    sel_start = np.arange(LANES) * SEL_LEN
    overlap = np.clip(np.minimum(cmp_start[:, None] + CMP_LEN, sel_start[None, :] + SEL_LEN)
                      - np.maximum(cmp_start[:, None], sel_start[None, :]), 0, None) / CMP_STRIDE
    overlap = overlap * (np.arange(units)[:, None] < n_cmp) * (np.arange(LANES)[None, :] < n_sel)
    o_nsa = _nsa(q, kcmp, vcmp, ksel, vsel, kwin, vwin, ng, jnp.asarray(overlap, BF16))

    kmem, vmem = _memkv(mem, row(mem_in_norm), bf(w_mem_kv), row(mem_k_norm))
    o_mem = _memattn(mq, kmem, vmem)

    x2 = _outproj(x1, o_gla, o_nsa, o_mem, bf(w_out))
    return _ffn(x2, row(ffn2_norm), bf(ffn2_w_gate), bf(ffn2_w_up), bf(ffn2_w_down), row(final_norm))


def kernel(x, mem, positions, ffn1_norm, ffn1_w_gate, ffn1_w_up, ffn1_w_down, mix_norm, w_in, gla_w_a, gla_b_a, gla_o_norm, nsa_q_norm, nsa_k_norm, nsa_cmp_pos_k, nsa_cmp_w1_k, nsa_cmp_w2_k, nsa_cmp_pos_v, nsa_cmp_w1_v, nsa_cmp_w2_v, mem_in_norm, w_mem_kv, mem_q_norm, mem_k_norm, w_out, ffn2_norm, ffn2_w_gate, ffn2_w_up, ffn2_w_down, final_norm):
    depth = ffn1_norm.shape[0]
    outs = []
    for b in range(x.shape[0]):
        xb = x[b]
        for l in range(depth):
            xb = _layer(xb, mem[b], positions[b], ffn1_norm[l], ffn1_w_gate[l], ffn1_w_up[l], ffn1_w_down[l],
                        mix_norm[l], w_in[l], gla_w_a[l], gla_b_a[l], gla_o_norm[l], nsa_q_norm[l],
                        nsa_k_norm[l], nsa_cmp_pos_k[l], nsa_cmp_w1_k[l], nsa_cmp_w2_k[l], nsa_cmp_pos_v[l],
                        nsa_cmp_w1_v[l], nsa_cmp_w2_v[l], mem_in_norm[l], w_mem_kv[l], mem_q_norm[l],
                        mem_k_norm[l], w_out[l], ffn2_norm[l], ffn2_w_gate[l], ffn2_w_up[l], ffn2_w_down[l],
                        final_norm[l])
        outs.append(xb)
    return jnp.stack(outs)
```

```python
import functools

import numpy as np
import jax
import jax.numpy as jnp
from jax import lax
from jax.experimental import pallas as pl
from jax.experimental.pallas import tpu as pltpu

F32 = jnp.float32
BF16 = jnp.bfloat16

HEAD_DIM = 128
GLA_HEADS = 4
GLA_DK = 64
GLA_DV = 128
GLA_LOWRANK = 16
GLA_TAU = 16.0
NSA_HEADS = 8
NSA_KV_HEADS = 2
NSA_HPG = NSA_HEADS // NSA_KV_HEADS
CMP_LEN = 32
CMP_STRIDE = 16
SEL_LEN = 64
SEL_TOPK = 16
WINDOW = 512
MEM_HEADS = 4
MACARON_W = 0.5
QBLK = 128
ROPE_THETA = 10000.0
EPS = 1e-6
NEG_INF = -1e30
TINY = 1e-30
FORCE_SCORE = 1e4

LANES = 128
VMEM_LIMIT = 56 * 1024 * 1024

GLA_SUB = 16
GLA_ROWS = 128
SEL_KT = 256
SEL_MASK_BIAS = -32768.0
SEL_PAD_SCORE = -3e4


def _dot(a, b):
    return jnp.dot(a, b, preferred_element_type=F32)


def _dot_nt(a, b):
    return lax.dot_general(a, b, (((1,), (1,)), ((), ())), preferred_element_type=F32)


def _dot_tn(a, b):
    return lax.dot_general(a, b, (((0,), (0,)), ((), ())), preferred_element_type=F32)


def _split3(x):
    hi = x.astype(BF16)
    r1 = x - hi.astype(F32)
    mid = r1.astype(BF16)
    lo = (r1 - mid.astype(F32)).astype(BF16)
    return hi, mid, lo


def _rms(x, g):
    return x * lax.rsqrt(jnp.mean(x * x, axis=-1, keepdims=True) + EPS) * g


def _sigmoid(x):
    return 1.0 / (1.0 + jnp.exp(-x))


def _params(sem):
    return pltpu.CompilerParams(dimension_semantics=sem, vmem_limit_bytes=VMEM_LIMIT)


def _resident(shape):
    nd = len(shape)
    return pl.BlockSpec(shape, lambda *_: (0,) * nd, pipeline_mode=pl.Buffered(1))


def _ffn_body(*refs, final, nf):
    if final:
        x_ref, g_ref, wg_ref, wu_ref, wd_ref, fg_ref, o_ref, h_ref = refs
    else:
        x_ref, g_ref, wg_ref, wu_ref, wd_ref, o_ref, h_ref = refs
    f = pl.program_id(1)

    @pl.when(f == 0)
    def _():
        x = x_ref[...]
        h_ref[...] = _rms(x, g_ref[...]).astype(BF16)
        o_ref[...] = x

    h = h_ref[...]
    g = _dot(h, wg_ref[...])
    u = _dot(h, wu_ref[...])
    a = (g * _sigmoid(g)) * u * MACARON_W
    o_ref[...] += _dot(a.astype(BF16), wd_ref[...])

    if final:
        @pl.when(f == nf - 1)
        def _():
            o_ref[...] = _rms(o_ref[...], fg_ref[...])


def _ffn(x, norm_g, wg, wu, wd, final_g=None, *, tm=512, tf=512):
    s, d = x.shape
    ff = wg.shape[1]
    nf = ff // tf
    final = final_g is not None
    in_specs = [
        pl.BlockSpec((tm, d), lambda i, f: (i, 0)),
        pl.BlockSpec((1, d), lambda i, f: (0, 0)),
        pl.BlockSpec((d, tf), lambda i, f: (0, f)),
        pl.BlockSpec((d, tf), lambda i, f: (0, f)),
        pl.BlockSpec((tf, d), lambda i, f: (f, 0)),
    ]
    args = [x, norm_g, wg, wu, wd]
    if final:
        in_specs.append(pl.BlockSpec((1, d), lambda i, f: (0, 0)))
        args.append(final_g)
    return pl.pallas_call(
        functools.partial(_ffn_body, final=final, nf=nf),
        grid=(s // tm, nf),
        in_specs=in_specs,
        out_specs=pl.BlockSpec((tm, d), lambda i, f: (i, 0)),
        out_shape=jax.ShapeDtypeStruct((s, d), F32),
        scratch_shapes=[pltpu.VMEM((tm, d), BF16)],
        compiler_params=_params(("parallel", "arbitrary")),
        name="ffn_final" if final else "ffn",
    )(*args)


_C_NQ = 0
_C_GV = 1024
_C_GR = 1536
_C_MQ = 2048
_C_GQK = 2560
_C_KSW = 3072
_C_KCVC = 3584
_C_GA = 4096
_C_NG = 4224
_C_TOTAL = 4480


def _rope(x, cos, sin_signed):
    return x * cos + pltpu.roll(x, HEAD_DIM // 2, 1) * sin_signed


def _proj_body(x_ref, g_ref, w_ref, wvt_ref, pos_ref, inv_ref, sgn_ref, qn_ref, kn_ref, mqn_ref,
               q_ref, ksel_ref, vselt_ref, kwin_ref, vwint_ref, mq_ref, gqk_ref, gv_ref, gr_ref,
               kcvc_ref, ga_ref, ng_ref, *, tm):
    i = pl.program_id(0)
    h = _rms(x_ref[...], g_ref[...]).astype(BF16)

    def proj(c0, width):
        return _dot(h, w_ref[:, c0:c0 + width])

    ang = pos_ref[...].astype(F32) * inv_ref[...]
    cos = jnp.cos(ang)
    sin_signed = jnp.sin(ang) * sgn_ref[...]

    def norm_rope(t, gain):
        return _rope(_rms(t, gain), cos, sin_signed)

    scale = HEAD_DIM ** -0.5
    nq = proj(_C_NQ, NSA_HEADS * HEAD_DIM)
    for hd in range(NSA_HEADS):
        sl = slice(hd * HEAD_DIM, (hd + 1) * HEAD_DIM)
        q_ref[:, sl] = (norm_rope(nq[:, sl], qn_ref[...]) * scale).astype(BF16)

    tok = i * tm + lax.broadcasted_iota(jnp.int32, (tm, LANES), 0)
    blk = lax.broadcasted_iota(jnp.int32, (tm, LANES), 1)
    onehot = jnp.where(lax.shift_right_logical(tok, 6) == blk, 1.0, 0.0).astype(BF16)
    ksw = proj(_C_KSW, 512)
    vt = _dot_nt(wvt_ref[...], h)
    for g in range(NSA_KV_HEADS):
        sl = slice(g * HEAD_DIM, (g + 1) * HEAD_DIM)
        slw = slice(256 + g * HEAD_DIM, 256 + (g + 1) * HEAD_DIM)
        ksel_ref[g, :, 0:HEAD_DIM] = norm_rope(ksw[:, sl], kn_ref[1:2, :]).astype(BF16)
        ksel_ref[g, :, HEAD_DIM:2 * HEAD_DIM] = onehot
        kwin_ref[g] = norm_rope(ksw[:, slw], kn_ref[2:3, :]).astype(BF16)
        vselt_ref[g] = vt[sl, :].astype(BF16)
        vwint_ref[g] = vt[slw, :].astype(BF16)

    mq = proj(_C_MQ, MEM_HEADS * HEAD_DIM)
    for hd in range(MEM_HEADS):
        sl = slice(hd * HEAD_DIM, (hd + 1) * HEAD_DIM)
        mq_ref[:, sl] = (_rms(mq[:, sl], mqn_ref[...]) * scale).astype(BF16)

    gqk_ref[...] = proj(_C_GQK, 512)
    gv_ref[...] = proj(_C_GV, 512).astype(BF16)
    gr_ref[...] = proj(_C_GR, 512)
    kcvc = proj(_C_KCVC, 512)
    for j in range(2 * NSA_KV_HEADS):
        kcvc_ref[j] = kcvc[:, j * HEAD_DIM:(j + 1) * HEAD_DIM]
    ga_ref[...] = proj(_C_GA, LANES)
    ng = proj(_C_NG, 2 * LANES)
    for g in range(NSA_KV_HEADS):
        ng_ref[g] = ng[:, g * LANES:(g + 1) * LANES]


def _proj(x1, mix_g, w_perm, wvt, pos_col, inv128, sgn128, q_norm, k_norm, mq_norm, *, tm=512):
    s, d = x1.shape
    row = lambda w: pl.BlockSpec((tm, w), lambda i: (i, 0))
    grp = lambda w: pl.BlockSpec((NSA_KV_HEADS, tm, w), lambda i: (0, i, 0))
    grpt = pl.BlockSpec((NSA_KV_HEADS, HEAD_DIM, tm), lambda i: (0, 0, i))
    vt_shape = jax.ShapeDtypeStruct((NSA_KV_HEADS, HEAD_DIM, s), BF16)
    out_shapes = [
        (jax.ShapeDtypeStruct((s, NSA_HEADS * HEAD_DIM), BF16), row(NSA_HEADS * HEAD_DIM)),
        (jax.ShapeDtypeStruct((NSA_KV_HEADS, s, 2 * HEAD_DIM), BF16), grp(2 * HEAD_DIM)),
        (vt_shape, grpt),
        (jax.ShapeDtypeStruct((NSA_KV_HEADS, s, HEAD_DIM), BF16), grp(HEAD_DIM)),
        (vt_shape, grpt),
        (jax.ShapeDtypeStruct((s, MEM_HEADS * HEAD_DIM), BF16), row(MEM_HEADS * HEAD_DIM)),
        (jax.ShapeDtypeStruct((s, 512), F32), row(512)),
        (jax.ShapeDtypeStruct((s, 512), BF16), row(512)),
        (jax.ShapeDtypeStruct((s, 512), F32), row(512)),
        (jax.ShapeDtypeStruct((2 * NSA_KV_HEADS, s, HEAD_DIM), F32),
         pl.BlockSpec((2 * NSA_KV_HEADS, tm, HEAD_DIM), lambda i: (0, i, 0))),
        (jax.ShapeDtypeStruct((s, LANES), F32), row(LANES)),
        (jax.ShapeDtypeStruct((NSA_KV_HEADS, s, LANES), F32), grp(LANES)),
    ]
    return pl.pallas_call(
        functools.partial(_proj_body, tm=tm),
        grid=(s // tm,),
        in_specs=[
            row(d),
            _resident((1, d)),
            _resident(w_perm.shape),
            _resident(wvt.shape),
            pl.BlockSpec((tm, 1), lambda i: (i, 0)),
            _resident((1, LANES)), _resident((1, LANES)),
            _resident((1, HEAD_DIM)), _resident((3, HEAD_DIM)), _resident((1, HEAD_DIM)),
        ],
        out_specs=[o[1] for o in out_shapes],
        out_shape=[o[0] for o in out_shapes],
        compiler_params=_params(("parallel",)),
        name="proj",
    )(x1, mix_g, w_perm, wvt, pos_col, inv128, sgn128, q_norm, k_norm, mq_norm)


def _compress_body(kcvc_ref, w1k_ref, w2k_ref, pek_ref, w1v_ref, w2v_ref, pev_ref, kn_ref,
                   pos_ref, inv_ref, sgn_ref, kcmp_ref, vcmp_ref, *, units):
    half = CMP_LEN // 2
    ang = pos_ref[...].astype(F32) * inv_ref[...]
    cos = jnp.cos(ang)
    sin_signed = jnp.sin(ang) * sgn_ref[...]
    for kind, (w1_ref, w2_ref, pe_ref) in enumerate(((w1k_ref, w2k_ref, pek_ref),
                                                     (w1v_ref, w2v_ref, pev_ref))):
        for g in range(NSA_KV_HEADS):
            slab = kind * NSA_KV_HEADS + g
            a = jnp.zeros((units, w1_ref.shape[1]), F32)
            b = jnp.zeros((units, w1_ref.shape[1]), F32)
            for l in range(half):
                t = kcvc_ref[slab, pl.ds(l, units, stride=CMP_STRIDE), :]
                a = a + _dot((t + pe_ref[l:l + 1, :]).astype(BF16),
                             w1_ref[l * HEAD_DIM:(l + 1) * HEAD_DIM, :])
                b = b + _dot((t + pe_ref[half + l:half + l + 1, :]).astype(BF16),
                             w1_ref[(half + l) * HEAD_DIM:(half + l + 1) * HEAD_DIM, :])
            hid = a + pltpu.roll(b, units - 1, 0)
            act = (hid * _sigmoid(hid)).astype(BF16)
            if kind == 0:
                c = _rope(_rms(_dot(act, w2_ref[...]), kn_ref[0:1, :]), cos, sin_signed)
                kcmp_ref[g] = c.astype(BF16)
            else:
                vcmp_ref[g] = _dot_nt(w2_ref[...], act).astype(BF16)


def _compress(kcvc, w1k, w2k, pek, w1v, w2v, pev, k_norm, pos_cmp, inv128, sgn128):
    s = kcvc.shape[1]
    units = s // CMP_STRIDE
    shp = jax.ShapeDtypeStruct((NSA_KV_HEADS, units, HEAD_DIM), BF16)
    shp_t = jax.ShapeDtypeStruct((NSA_KV_HEADS, HEAD_DIM, units), BF16)
    return pl.pallas_call(
        functools.partial(_compress_body, units=units),
        out_shape=[shp, shp_t],
        compiler_params=pltpu.CompilerParams(vmem_limit_bytes=VMEM_LIMIT),
        name="compress",
    )(kcvc, w1k, w2k, pek, w1v, w2v, pev, k_norm, pos_cmp, inv128, sgn128)


def _gla_body(gqk_ref, gv_ref, gr_ref, ga_ref, wa_ref, ba_ref, on_ref, tcum_ref, bd_ref,
              o_ref, st_ref, q_s, k_s, b_s, o_s):
    rows = GLA_ROWS
    npair = GLA_HEADS // 2

    @pl.when(pl.program_id(0) == 0)
    def _():
        st_ref[...] = jnp.zeros_like(st_ref)

    ga_hi, ga_mid, ga_lo = _split3(ga_ref[...])
    z = ba_ref[...]
    for ga_t in (ga_hi, ga_mid, ga_lo):
        for wa_t in _split3(wa_ref[...]):
            z = z + _dot(ga_t, wa_t)
    la = (jnp.minimum(z, 0.0) - jnp.log(1.0 + jnp.exp(-jnp.abs(z)))) / GLA_TAU
    tc = tcum_ref[...]
    bcum = jnp.zeros_like(la)
    for la_t in _split3(la):
        bcum = bcum + _dot(tc, la_t)
    b_s[...] = bcum
    q_s[...] = gqk_ref[:, 0:256] * (GLA_DK ** -0.5)
    k_s[...] = gqk_ref[:, 256:512]

    row_i = lax.broadcasted_iota(jnp.int32, (GLA_SUB, LANES), 0)
    lane_i = lax.broadcasted_iota(jnp.int32, (GLA_SUB, LANES), 1)
    head0 = lane_i < GLA_DK

    def sub_block(sb, carry):
        r0 = pl.multiple_of(sb * GLA_SUB, GLA_SUB)
        for p in range(npair):
            cs = slice(p * LANES, (p + 1) * LANES)
            qs = q_s[pl.ds(r0, GLA_SUB), cs]
            kk = k_s[pl.ds(r0, GLA_SUB), cs]
            bb = b_s[pl.ds(r0, GLA_SUB), cs]
            vp = gv_ref[pl.ds(r0, GLA_SUB), p * 2 * GLA_DV:(p + 1) * 2 * GLA_DV]
            vpf = vp.astype(F32)
            blast = bb[GLA_SUB - 1:GLA_SUB, :]
            st = st_ref[p]
            o_inter = _dot_nt((qs * jnp.exp(bb)).astype(BF16), st.astype(BF16))
            acc0 = jnp.zeros((GLA_SUB, GLA_DV), F32)
            acc1 = jnp.zeros((GLA_SUB, GLA_DV), F32)
            for j in range(GLA_SUB):
                dlt = jnp.where(row_i >= j, bb - bb[j:j + 1, :], NEG_INF)
                xj = qs * jnp.exp(dlt) * kk[j:j + 1, :]
                a0 = jnp.sum(jnp.where(head0, xj, 0.0), axis=-1, keepdims=True)
                a1 = jnp.sum(jnp.where(head0, 0.0, xj), axis=-1, keepdims=True)
                acc0 = acc0 + a0 * vpf[j:j + 1, 0:GLA_DV]
                acc1 = acc1 + a1 * vpf[j:j + 1, GLA_DV:2 * GLA_DV]
            o_s[pl.ds(r0, GLA_SUB), p * 2 * GLA_DV:p * 2 * GLA_DV + GLA_DV] = o_inter[:, 0:GLA_DV] + acc0
            o_s[pl.ds(r0, GLA_SUB), p * 2 * GLA_DV + GLA_DV:(p + 1) * 2 * GLA_DV] = o_inter[:, GLA_DV:] + acc1
            kd = (kk * jnp.exp(blast - bb)).astype(BF16)
            upd = _dot_tn(vp, kd)
            st_ref[p] = st * jnp.exp(blast) + upd * bd_ref[...]
        return carry

    lax.fori_loop(0, rows // GLA_SUB, sub_block, 0)

    gr = gr_ref[...]
    for hd in range(GLA_HEADS):
        sl = slice(hd * GLA_DV, (hd + 1) * GLA_DV)
        r = gr[:, sl]
        o_ref[:, sl] = (_rms(o_s[:, sl], on_ref[...]) * (r * _sigmoid(r))).astype(BF16)


def _gla(gqk, gv, gr, ga, wa_pad, ba, o_norm):
    s = gqk.shape[0]
    rows = GLA_ROWS
    idx = np.arange(rows)
    tcum = ((idx[:, None] >= idx[None, :]) & (idx[:, None] // GLA_SUB == idx[None, :] // GLA_SUB))
    tcum = jnp.asarray(tcum, BF16)
    r256 = np.arange(2 * GLA_DV)[:, None] // GLA_DV
    c128 = np.arange(LANES)[None, :] // GLA_DK
    bdmask = jnp.asarray(r256 == c128, F32)
    row = lambda w: pl.BlockSpec((rows, w), lambda i: (i, 0))
    return pl.pallas_call(
        _gla_body,
        grid=(s // rows,),
        in_specs=[row(512), row(512), row(512), row(LANES),
                  _resident(wa_pad.shape), _resident(ba.shape), _resident(o_norm.shape),
                  _resident(tcum.shape), _resident(bdmask.shape)],
        out_specs=row(512),
        out_shape=jax.ShapeDtypeStruct((s, GLA_HEADS * GLA_DV), BF16),
        scratch_shapes=[pltpu.VMEM((GLA_HEADS // 2, 2 * GLA_DV, LANES), F32),
                        pltpu.VMEM((rows, 256), F32), pltpu.VMEM((rows, 256), F32),
                        pltpu.VMEM((rows, 256), F32), pltpu.VMEM((rows, 512), F32)],
        compiler_params=_params(("arbitrary",)),
        name="gla",
    )(gqk, gv, gr, ga, wa_pad, ba, o_norm, tcum, bdmask)


def _nsa_body(q_ref, kcmp_ref, vcmpt_ref, ksel_ref, vselt_ref, kwin_ref, vwint_ref, ng_ref, ovlt_ref,
              o_ref, qt_s, sa_s, sb_s, pa_s, pb_s, acc_s, out_s, *, n_sel):
    qb = pl.program_id(1)
    t0 = qb * QBLK
    cols = NSA_HPG * QBLK
    ncmp = kcmp_ref.shape[0]

    for hd in range(NSA_HPG):
        qh = q_ref[:, hd * HEAD_DIM:(hd + 1) * HEAD_DIM].astype(F32)
        qt_s[0:HEAD_DIM, hd * QBLK:(hd + 1) * QBLK] = qh.T.astype(BF16)
    qt = qt_s[0:HEAD_DIM, :]

    def tq_of(rows):
        return t0 + (lax.broadcasted_iota(jnp.int32, (rows, cols), 1) & (QBLK - 1))

    def flash(carry, s, valid, vt_tile):
        m, l, acc = carry
        if valid is not None:
            s = jnp.where(valid, s, NEG_INF)
        m_new = jnp.maximum(m, jnp.max(s, axis=0, keepdims=True))
        alpha = jnp.exp(m - m_new)
        p = jnp.exp(s - m_new)
        if valid is not None:
            p = jnp.where(valid, p, 0.0)
        l = alpha * l + jnp.sum(p, axis=0, keepdims=True)
        acc = alpha * acc + _dot(vt_tile, p.astype(BF16))
        return (m_new, l, acc), alpha, p

    gates_t = _sigmoid(ng_ref[...]).T

    def gate_row(c):
        return jnp.concatenate([gates_t[3 * hd + c:3 * hd + c + 1, :] for hd in range(NSA_HPG)], axis=1)

    n_row = lax.broadcasted_iota(jnp.int32, (ncmp, cols), 0)
    valid_c = n_row * CMP_STRIDE + (CMP_LEN - 1) <= tq_of(ncmp)
    s_c = jnp.where(valid_c, _dot(kcmp_ref[...], qt), NEG_INF)
    e_c = jnp.where(valid_c, jnp.exp(s_c - jnp.max(s_c, axis=0, keepdims=True)), 0.0)
    p_c = e_c / jnp.maximum(jnp.sum(e_c, axis=0, keepdims=True), TINY)
    out_s[...] = gate_row(0) * _dot(vcmpt_ref[...], p_c.astype(BF16))
    psum = p_c[:, 0:QBLK]
    for hd in range(1, NSA_HPG):
        psum = psum + p_c[:, hd * QBLK:(hd + 1) * QBLK]
    imp = jnp.zeros((LANES, QBLK), F32)
    for p_t in _split3(psum):
        imp = imp + _dot(ovlt_ref[...], p_t)

    def win_part(start, length):
        start = pl.multiple_of(jnp.maximum(start, 0), QBLK)
        return _dot(kwin_ref[pl.ds(start, length), :], qt), vwint_ref[:, pl.ds(start, length)], start

    w_row = lax.broadcasted_iota(jnp.int32, (QBLK, cols), 0)
    tq_w = tq_of(QBLK)
    s_old, v_old, _ = win_part(t0 - WINDOW, QBLK)
    kp_old = t0 - WINDOW + w_row
    valid_old = (kp_old > tq_w - WINDOW) & (kp_old >= 0)
    s_old = jnp.where(valid_old, s_old, NEG_INF)
    s_mid, v_mid, mid0 = win_part(t0 - WINDOW + QBLK, WINDOW - QBLK)
    mid_row = mid0 + lax.broadcasted_iota(jnp.int32, (WINDOW - QBLK, cols), 0)
    s_mid = jnp.where(mid_row < t0, s_mid, NEG_INF)
    s_dg, v_dg, _ = win_part(t0, QBLK)
    valid_dg = t0 + w_row <= tq_w
    s_dg = jnp.where(valid_dg, s_dg, NEG_INF)
    m_w = jnp.maximum(jnp.maximum(jnp.max(s_old, axis=0, keepdims=True), jnp.max(s_mid, axis=0, keepdims=True)),
                      jnp.max(s_dg, axis=0, keepdims=True))
    p_old = jnp.where(valid_old, jnp.exp(s_old - m_w), 0.0)
    p_mid = jnp.exp(s_mid - m_w)
    p_dg = jnp.where(valid_dg, jnp.exp(s_dg - m_w), 0.0)
    l_w = (jnp.sum(p_old, axis=0, keepdims=True) + jnp.sum(p_mid, axis=0, keepdims=True)
           + jnp.sum(p_dg, axis=0, keepdims=True))
    acc_w = (_dot(v_old, p_old.astype(BF16)) + _dot(v_mid, p_mid.astype(BF16)) + _dot(v_dg, p_dg.astype(BF16)))
    out_s[...] += gate_row(2) * (acc_w / jnp.maximum(l_w, TINY))

    tq = t0 + lax.broadcasted_iota(jnp.int32, (LANES, QBLK), 1)
    m_i = lax.broadcasted_iota(jnp.int32, (LANES, QBLK), 0)
    cur = lax.shift_right_logical(tq, 6)
    forced = (m_i == 0) | (m_i == cur) | (m_i == cur - 1)
    score = jnp.where(m_i * SEL_LEN <= tq, jnp.where(forced, FORCE_SCORE, imp), -FORCE_SCORE)
    score = jnp.where(m_i < n_sel, score, SEL_PAD_SCORE)
    m_f = m_i.astype(F32)
    bias = jnp.full((LANES, QBLK), SEL_MASK_BIAS, F32)
    for _ in range(min(SEL_TOPK, n_sel)):
        mx = jnp.max(score, axis=0, keepdims=True)
        first = jnp.min(jnp.where(score == mx, m_f, float(LANES)), axis=0, keepdims=True)
        pick = m_f == first
        bias = jnp.where(pick, 0.0, bias)
        score = jnp.where(pick, -jnp.inf, score)
    bias = bias.astype(BF16)
    for hd in range(NSA_HPG):
        qt_s[HEAD_DIM:2 * HEAD_DIM, hd * QBLK:(hd + 1) * QBLK] = bias

    qext = qt_s[...]
    k_row = lax.broadcasted_iota(jnp.int32, (SEL_KT, cols), 0)
    tq_s = tq_of(SEL_KT)

    def sel_scores(kt):
        k0 = pl.multiple_of(kt * SEL_KT, SEL_KT)
        return _dot(ksel_ref[pl.ds(k0, SEL_KT), :], qext)

    def sel_pv(kt, p):
        k0 = pl.multiple_of(kt * SEL_KT, SEL_KT)
        return _dot(vselt_ref[:, pl.ds(k0, SEL_KT)], p)

    def soft(s_buf, p_buf, pv, ml):
        m, l = ml
        s = s_buf[...]
        m_new = jnp.maximum(m, jnp.max(s, axis=0, keepdims=True))
        alpha = jnp.exp(m - m_new)
        p = jnp.exp(s - m_new)
        p_buf[...] = p.astype(BF16)
        acc_s[...] = alpha * (acc_s[...] + pv)
        return m_new, alpha * l + jnp.sum(p, axis=0, keepdims=True)

    def sel_pair(j, ml):
        sb_s[...] = sel_scores(2 * j + 1)
        ml = soft(sa_s, pa_s, sel_pv(jnp.maximum(2 * j - 1, 0), pb_s[...]), ml)
        sa_s[...] = sel_scores(2 * j + 2)
        return soft(sb_s, pb_s, sel_pv(2 * j, pa_s[...]), ml)

    n_pair = t0 // (2 * SEL_KT)
    sa_s[...] = sel_scores(0)
    pb_s[...] = jnp.zeros_like(pb_s)
    acc_s[...] = jnp.zeros_like(acc_s)
    ml = lax.fori_loop(0, n_pair, sel_pair, (jnp.full((1, cols), NEG_INF, F32), jnp.zeros((1, cols), F32)))
    e0 = 2 * n_pair
    acc_s[...] += sel_pv(jnp.maximum(e0 - 1, 0), pb_s[...])

    def diag_tile(e, s, ml):
        k0 = pl.multiple_of(e * SEL_KT, SEL_KT)
        (m, l, acc), _, _ = flash((ml[0], ml[1], acc_s[...]), s, k0 + k_row <= tq_s,
                                  vselt_ref[:, pl.ds(k0, SEL_KT)])
        acc_s[...] = acc
        return m, l

    ml = diag_tile(e0, sa_s[...], ml)
    ml = lax.cond(t0 >= (e0 + 1) * SEL_KT, lambda c: diag_tile(e0 + 1, sel_scores(e0 + 1), c), lambda c: c, ml)

    out = out_s[...] + gate_row(1) * (acc_s[...] / jnp.maximum(ml[1], TINY))
    for hd in range(NSA_HPG):
        o_ref[:, hd * HEAD_DIM:(hd + 1) * HEAD_DIM] = out[:, hd * QBLK:(hd + 1) * QBLK].T.astype(BF16)


def _nsa(q, kcmp, vcmpt, ksel, vselt, kwin, vwint, ng, overlap_t):
    s = q.shape[0]
    n_sel = s // SEL_LEN
    assert n_sel <= LANES and s % (2 * SEL_KT) == 0 and s >= WINDOW + QBLK
    ncmp = kcmp.shape[1]
    cols = NSA_HPG * QBLK
    grp = lambda r, w: pl.BlockSpec((None, r, w), lambda g, b: (g, 0, 0))
    return pl.pallas_call(
        functools.partial(_nsa_body, n_sel=n_sel),
        grid=(NSA_KV_HEADS, s // QBLK),
        in_specs=[
            pl.BlockSpec((QBLK, NSA_HPG * HEAD_DIM), lambda g, b: (b, g)),
            grp(ncmp, HEAD_DIM), grp(HEAD_DIM, ncmp),
            grp(s, 2 * HEAD_DIM), grp(HEAD_DIM, s), grp(s, HEAD_DIM), grp(HEAD_DIM, s),
            pl.BlockSpec((None, QBLK, LANES), lambda g, b: (g, b, 0)),
            pl.BlockSpec(overlap_t.shape, lambda g, b: (0, 0)),
        ],
        out_specs=pl.BlockSpec((QBLK, NSA_HPG * HEAD_DIM), lambda g, b: (b, g)),
        out_shape=jax.ShapeDtypeStruct((s, NSA_HEADS * HEAD_DIM), BF16),
        scratch_shapes=[pltpu.VMEM((2 * HEAD_DIM, cols), BF16),
                        pltpu.VMEM((SEL_KT, cols), F32), pltpu.VMEM((SEL_KT, cols), F32),
                        pltpu.VMEM((SEL_KT, cols), BF16), pltpu.VMEM((SEL_KT, cols), BF16),
                        pltpu.VMEM((HEAD_DIM, cols), F32), pltpu.VMEM((HEAD_DIM, cols), F32)],
        compiler_params=_params(("parallel", "arbitrary")),
        name="nsa",
    )(q, kcmp, vcmpt, ksel, vselt, kwin, vwint, ng, overlap_t)


def _memkv_body(mem_ref, g_ref, w_ref, kn_ref, k_ref, v_ref):
    kv = _dot(_rms(mem_ref[...], g_ref[...]).astype(BF16), w_ref[...])
    width = MEM_HEADS * HEAD_DIM
    for hd in range(MEM_HEADS):
        sl = slice(hd * HEAD_DIM, (hd + 1) * HEAD_DIM)
        k_ref[:, sl] = _rms(kv[:, sl], kn_ref[...]).astype(BF16)
    v_ref[...] = kv[:, width:].astype(BF16)


def _memkv(mem, in_g, w_kv, k_norm):
    m = mem.shape[0]
    shp = jax.ShapeDtypeStruct((m, MEM_HEADS * HEAD_DIM), BF16)
    return pl.pallas_call(
        _memkv_body, out_shape=[shp, shp],
        compiler_params=pltpu.CompilerParams(vmem_limit_bytes=VMEM_LIMIT),
        name="memkv",
    )(mem, in_g, w_kv, k_norm)


def _memattn_body(q_ref, k_ref, v_ref, o_ref):
    for hd in range(MEM_HEADS):
        sl = slice(hd * HEAD_DIM, (hd + 1) * HEAD_DIM)
        sc = _dot_nt(q_ref[:, sl], k_ref[:, sl])
        e = jnp.exp(sc - jnp.max(sc, axis=-1, keepdims=True))
        p = e / jnp.sum(e, axis=-1, keepdims=True)
        o_ref[:, sl] = _dot(p.astype(BF16), v_ref[:, sl]).astype(BF16)


def _memattn(q, k, v, *, tm=512):
    s, w = q.shape
    return pl.pallas_call(
        _memattn_body,
        grid=(s // tm,),
        in_specs=[pl.BlockSpec((tm, w), lambda i: (i, 0)), _resident(k.shape), _resident(v.shape)],
        out_specs=pl.BlockSpec((tm, w), lambda i: (i, 0)),
        out_shape=jax.ShapeDtypeStruct((s, w), BF16),
        compiler_params=_params(("parallel",)),
        name="memattn",
    )(q, k, v)


def _outproj_body(x_ref, a_ref, b_ref, c_ref, w_ref, o_ref):
    na, nb = a_ref.shape[1], b_ref.shape[1]
    o_ref[...] = (x_ref[...] + _dot(a_ref[...], w_ref[0:na, :]) + _dot(b_ref[...], w_ref[na:na + nb, :])
                  + _dot(c_ref[...], w_ref[na + nb:, :]))


def _outproj(x1, o_gla, o_nsa, o_mem, w_out, *, tm=512):
    s, d = x1.shape
    row = lambda w: pl.BlockSpec((tm, w), lambda i: (i, 0))
    return pl.pallas_call(
        _outproj_body,
        grid=(s // tm,),
        in_specs=[row(d), row(o_gla.shape[1]), row(o_nsa.shape[1]), row(o_mem.shape[1]),
                  _resident(w_out.shape)],
        out_specs=row(d),
        out_shape=jax.ShapeDtypeStruct((s, d), F32),
        compiler_params=_params(("parallel",)),
        name="outproj",
    )(x1, o_gla, o_nsa, o_mem, w_out)


def _permute_w_in(w_in):
    d = w_in.shape[0]
    sizes = (GLA_HEADS * GLA_DK, GLA_HEADS * GLA_DK, GLA_HEADS * GLA_DV, GLA_HEADS * GLA_DV, GLA_LOWRANK,
             NSA_HEADS * HEAD_DIM) + (NSA_KV_HEADS * HEAD_DIM,) * 6 + (NSA_HEADS * 3, MEM_HEADS * HEAD_DIM)
    offs = np.concatenate([[0], np.cumsum(sizes)])
    (g_q, g_k, g_v, g_r, g_a, n_q, n_kc, n_vc, n_ks, n_vs, n_kw, n_vw, n_g, m_q) = [
        w_in[:, int(offs[k]):int(offs[k + 1])] for k in range(len(sizes))]
    zeros = lambda n: jnp.zeros((d, n), w_in.dtype)
    per_g = NSA_HPG * 3
    cols = [n_q, g_v, g_r, m_q, g_q, g_k, n_ks, n_kw, n_kc, n_vc,
            g_a, zeros(LANES - GLA_LOWRANK)]
    for g in range(NSA_KV_HEADS):
        cols += [n_g[:, g * per_g:(g + 1) * per_g], zeros(LANES - per_g)]
    w = jnp.concatenate(cols, axis=1)
    assert w.shape[1] == _C_TOTAL
    wvt = jnp.concatenate([n_vs, n_vw], axis=1).T
    return w.astype(BF16), wvt.astype(BF16)


def _layer(x, mem, positions, ffn1_norm, ffn1_w_gate, ffn1_w_up, ffn1_w_down, mix_norm, w_in,
           gla_w_a, gla_b_a, gla_o_norm, nsa_q_norm, nsa_k_norm, nsa_cmp_pos_k, nsa_cmp_w1_k,
           nsa_cmp_w2_k, nsa_cmp_pos_v, nsa_cmp_w1_v, nsa_cmp_w2_v, mem_in_norm, w_mem_kv,
           mem_q_norm, mem_k_norm, w_out, ffn2_norm, ffn2_w_gate, ffn2_w_up, ffn2_w_down, final_norm):
    s, d = x.shape
    row = lambda v: v.reshape(1, -1)
    bf = lambda v: v.astype(BF16)

    x1 = _ffn(x, row(ffn1_norm), bf(ffn1_w_gate), bf(ffn1_w_up), bf(ffn1_w_down))

    half = HEAD_DIM // 2
    inv = ROPE_THETA ** (-jnp.arange(half, dtype=F32) / half)
    inv128 = jnp.concatenate([inv, inv]).reshape(1, HEAD_DIM)
    sgn128 = jnp.concatenate([-jnp.ones((half,), F32), jnp.ones((half,), F32)]).reshape(1, HEAD_DIM)
    w_perm, wvt = _permute_w_in(w_in)
    (q, ksel, vselt, kwin, vwint, mq, gqk, gv, gr, kcvc, ga, ng) = _proj(
        x1, row(mix_norm), w_perm, wvt, positions.reshape(s, 1), inv128, sgn128,
        row(nsa_q_norm), nsa_k_norm, row(mem_q_norm))

    wa_pad = jnp.zeros((LANES, gla_w_a.shape[1]), F32).at[:GLA_LOWRANK].set(gla_w_a)
    o_gla = _gla(gqk, gv, gr, ga, wa_pad, row(gla_b_a), row(gla_o_norm))

    units = s // CMP_STRIDE
    n_cmp = (s - CMP_LEN) // CMP_STRIDE + 1
    cmp_last = jnp.arange(units) * CMP_STRIDE + CMP_LEN - 1
    pos_cmp = positions[jnp.minimum(cmp_last, s - 1)].reshape(units, 1)
    kcmp, vcmpt = _compress(kcvc, bf(nsa_cmp_w1_k), bf(nsa_cmp_w2_k), nsa_cmp_pos_k,
                            bf(nsa_cmp_w1_v), bf(nsa_cmp_w2_v.T), nsa_cmp_pos_v, nsa_k_norm,
                            pos_cmp, inv128, sgn128)
    n_sel = s // SEL_LEN
    cmp_start = np.arange(units) * CMP_STRIDE
    sel_start = np.arange(LANES) * SEL_LEN
    overlap = np.clip(np.minimum(cmp_start[:, None] + CMP_LEN, sel_start[None, :] + SEL_LEN)
                      - np.maximum(cmp_start[:, None], sel_start[None, :]), 0, None) / CMP_STRIDE
    overlap = overlap * (np.arange(units)[:, None] < n_cmp) * (np.arange(LANES)[None, :] < n_sel)
    o_nsa = _nsa(q, kcmp, vcmpt, ksel, vselt, kwin, vwint, ng, jnp.asarray(overlap.T, BF16))

    kmem, vmem = _memkv(mem, row(mem_in_norm), bf(w_mem_kv), row(mem_k_norm))
    o_mem = _memattn(mq, kmem, vmem)

    x2 = _outproj(x1, o_gla, o_nsa, o_mem, bf(w_out))
    return _ffn(x2, row(ffn2_norm), bf(ffn2_w_gate), bf(ffn2_w_up), bf(ffn2_w_down), row(final_norm))


def kernel(x, mem, positions, ffn1_norm, ffn1_w_gate, ffn1_w_up, ffn1_w_down, mix_norm, w_in, gla_w_a, gla_b_a, gla_o_norm, nsa_q_norm, nsa_k_norm, nsa_cmp_pos_k, nsa_cmp_w1_k, nsa_cmp_w2_k, nsa_cmp_pos_v, nsa_cmp_w1_v, nsa_cmp_w2_v, mem_in_norm, w_mem_kv, mem_q_norm, mem_k_norm, w_out, ffn2_norm, ffn2_w_gate, ffn2_w_up, ffn2_w_down, final_norm):
    depth = ffn1_norm.shape[0]
    outs = []
    for b in range(x.shape[0]):
        xb = x[b]
        for l in range(depth):
            xb = _layer(xb, mem[b], positions[b], ffn1_norm[l], ffn1_w_gate[l], ffn1_w_up[l], ffn1_w_down[l],
                        mix_norm[l], w_in[l], gla_w_a[l], gla_b_a[l], gla_o_norm[l], nsa_q_norm[l],
                        nsa_k_norm[l], nsa_cmp_pos_k[l], nsa_cmp_w1_k[l], nsa_cmp_w2_k[l], nsa_cmp_pos_v[l],
                        nsa_cmp_w1_v[l], nsa_cmp_w2_v[l], mem_in_norm[l], w_mem_kv[l], mem_q_norm[l],
                        mem_k_norm[l], w_out[l], ffn2_norm[l], ffn2_w_gate[l], ffn2_w_up[l], ffn2_w_down[l],
                        final_norm[l])
        outs.append(xb)
    return jnp.stack(outs)
```

```python
import functools

import numpy as np
import jax
import jax.numpy as jnp
from jax import lax
from jax.experimental import pallas as pl
from jax.experimental.pallas import tpu as pltpu

F32 = jnp.float32
BF16 = jnp.bfloat16

HEAD_DIM = 128
GLA_HEADS = 4
GLA_DK = 64
GLA_DV = 128
GLA_LOWRANK = 16
GLA_TAU = 16.0
NSA_HEADS = 8
NSA_KV_HEADS = 2
NSA_HPG = NSA_HEADS // NSA_KV_HEADS
CMP_LEN = 32
CMP_STRIDE = 16
SEL_LEN = 64
SEL_TOPK = 16
WINDOW = 512
MEM_HEADS = 4
MACARON_W = 0.5
QBLK = 128
ROPE_THETA = 10000.0
EPS = 1e-6
NEG_INF = -1e30
TINY = 1e-30
FORCE_SCORE = 1e4
LOG2_E = 1.4426950408889634

LANES = 128
VMEM_LIMIT = 56 * 1024 * 1024

GLA_SUB = 16
GLA_ROWS = 128
SEL_KT = 256
SEL_MASK_BIAS = -32768.0
SEL_PAD_SCORE = -3e4


def _dot(a, b):
    return jnp.dot(a, b, preferred_element_type=F32)


def _dot_nt(a, b):
    return lax.dot_general(a, b, (((1,), (1,)), ((), ())), preferred_element_type=F32)


def _dot_tn(a, b):
    return lax.dot_general(a, b, (((0,), (0,)), ((), ())), preferred_element_type=F32)


def _split2(x):
    hi = x.astype(BF16)
    return hi, (x - hi.astype(F32)).astype(BF16)


def _rms(x, g):
    return x * lax.rsqrt(jnp.mean(x * x, axis=-1, keepdims=True) + EPS) * g


def _sigmoid(x):
    return 1.0 / (1.0 + jnp.exp(-x))


def _params(sem):
    return pltpu.CompilerParams(dimension_semantics=sem, vmem_limit_bytes=VMEM_LIMIT)


def _resident(shape):
    nd = len(shape)
    return pl.BlockSpec(shape, lambda *_: (0,) * nd, pipeline_mode=pl.Buffered(1))


def _ffn_body(*refs, final, nf):
    if final:
        x_ref, g_ref, wg_ref, wu_ref, wd_ref, fg_ref, o_ref, h_ref = refs
    else:
        x_ref, g_ref, wg_ref, wu_ref, wd_ref, o_ref, h_ref = refs
    f = pl.program_id(1)

    @pl.when(f == 0)
    def _():
        x = x_ref[...]
        h_ref[...] = _rms(x, g_ref[...]).astype(BF16)
        o_ref[...] = x

    h = h_ref[...]
    g = _dot(h, wg_ref[...])
    u = _dot(h, wu_ref[...])
    a = (g * _sigmoid(g)) * u * MACARON_W
    o_ref[...] += _dot(a.astype(BF16), wd_ref[...])

    if final:
        @pl.when(f == nf - 1)
        def _():
            o_ref[...] = _rms(o_ref[...], fg_ref[...])


def _ffn(x, norm_g, wg, wu, wd, final_g=None, *, tm=512, tf=512):
    s, d = x.shape
    ff = wg.shape[1]
    nf = ff // tf
    final = final_g is not None
    in_specs = [
        pl.BlockSpec((tm, d), lambda i, f: (i, 0)),
        pl.BlockSpec((1, d), lambda i, f: (0, 0)),
        pl.BlockSpec((d, tf), lambda i, f: (0, f)),
        pl.BlockSpec((d, tf), lambda i, f: (0, f)),
        pl.BlockSpec((tf, d), lambda i, f: (f, 0)),
    ]
    args = [x, norm_g, wg, wu, wd]
    if final:
        in_specs.append(pl.BlockSpec((1, d), lambda i, f: (0, 0)))
        args.append(final_g)
    return pl.pallas_call(
        functools.partial(_ffn_body, final=final, nf=nf),
        grid=(s // tm, nf),
        in_specs=in_specs,
        out_specs=pl.BlockSpec((tm, d), lambda i, f: (i, 0)),
        out_shape=jax.ShapeDtypeStruct((s, d), F32),
        scratch_shapes=[pltpu.VMEM((tm, d), BF16)],
        compiler_params=_params(("parallel", "arbitrary")),
        name="ffn_final" if final else "ffn",
    )(*args)


_GQK_W = 2 * GLA_HEADS * GLA_DK
_GV_W = GLA_HEADS * GLA_DV
_H_GV = _GQK_W
_H_GR = _H_GV + _GV_W
_H_GA = _H_GR + _GV_W
_HEAD_W = _H_GA + GLA_LOWRANK
_KV_W = NSA_KV_HEADS * HEAD_DIM
_B_KC = NSA_HEADS * HEAD_DIM
_B_KS = _B_KC + 2 * _KV_W
_B_KW = _B_KS + 2 * _KV_W
_BODY_W = _B_KW + 2 * _KV_W
_NG_PAD = 16
_T_NG = 2 * _KV_W


def _rope(x, cos, sin_signed):
    return x * cos + pltpu.roll(x, HEAD_DIM // 2, 1) * sin_signed


def _proj_body(x_ref, g_ref, wh_ref, wb_ref, wmq_ref, wt_ref, pos_ref, inv_ref, sgn_ref, qn_ref, kn_ref,
               mqn_ref, q_ref, ksel_ref, vselt_ref, kwin_ref, vwint_ref, mq_ref, gqk_ref, gv_ref, gr_ref,
               kcvc_ref, ga_ref, ngt_ref, *, tm):
    i = pl.program_id(0)
    h = _rms(x_ref[...], g_ref[...]).astype(BF16)

    def proj(w_ref, c0, width):
        return _dot(h, w_ref[:, c0:c0 + width])

    ang = pos_ref[...].astype(F32) * inv_ref[...]
    cos = jnp.cos(ang)
    sin_signed = jnp.sin(ang) * sgn_ref[...]

    def norm_rope(t, gain):
        return _rope(_rms(t, gain), cos, sin_signed)

    scale = HEAD_DIM ** -0.5
    nq = proj(wb_ref, 0, NSA_HEADS * HEAD_DIM)
    for hd in range(NSA_HEADS):
        sl = slice(hd * HEAD_DIM, (hd + 1) * HEAD_DIM)
        q_ref[:, sl] = (norm_rope(nq[:, sl], qn_ref[...]) * (scale * LOG2_E)).astype(BF16)

    tok = i * tm + lax.broadcasted_iota(jnp.int32, (tm, LANES), 0)
    blk = lax.broadcasted_iota(jnp.int32, (tm, LANES), 1)
    onehot = jnp.where(lax.shift_right_logical(tok, 6) == blk, 1.0, 0.0).astype(BF16)
    ks = proj(wb_ref, _B_KS, _KV_W)
    kw = proj(wb_ref, _B_KW, _KV_W)
    vt = _dot_nt(wt_ref[...], h)
    for g in range(NSA_KV_HEADS):
        sl = slice(g * HEAD_DIM, (g + 1) * HEAD_DIM)
        slw = slice(_KV_W + g * HEAD_DIM, _KV_W + (g + 1) * HEAD_DIM)
        ksel_ref[g, :, 0:HEAD_DIM] = norm_rope(ks[:, sl], kn_ref[1:2, :]).astype(BF16)
        ksel_ref[g, :, HEAD_DIM:2 * HEAD_DIM] = onehot
        kwin_ref[g] = norm_rope(kw[:, sl], kn_ref[2:3, :]).astype(BF16)
        vselt_ref[g] = vt[sl, :].astype(BF16)
        vwint_ref[g] = vt[slw, :].astype(BF16)
        ngt_ref[g] = vt[_T_NG + g * _NG_PAD:_T_NG + (g + 1) * _NG_PAD, :]

    mq = _dot(h, wmq_ref[...])
    for hd in range(MEM_HEADS):
        sl = slice(hd * HEAD_DIM, (hd + 1) * HEAD_DIM)
        mq_ref[:, sl] = (_rms(mq[:, sl], mqn_ref[...]) * scale).astype(BF16)

    gqk_ref[...] = proj(wh_ref, 0, _GQK_W)
    gv_ref[...] = proj(wh_ref, _H_GV, _GV_W).astype(BF16)
    gr_ref[...] = proj(wh_ref, _H_GR, _GV_W)
    ga_ref[...] = proj(wh_ref, _H_GA, GLA_LOWRANK)
    kcvc = proj(wb_ref, _B_KC, 2 * _KV_W)
    for j in range(2 * NSA_KV_HEADS):
        kcvc_ref[j] = kcvc[:, j * HEAD_DIM:(j + 1) * HEAD_DIM]


def _proj(x1, mix_g, w_head, w_body, w_mq, w_t, pos_col, inv128, sgn128, q_norm, k_norm, mq_norm, *, tm=512):
    s, d = x1.shape
    row = lambda w: pl.BlockSpec((tm, w), lambda i: (i, 0))
    grp = lambda w: pl.BlockSpec((NSA_KV_HEADS, tm, w), lambda i: (0, i, 0))
    grpt = pl.BlockSpec((NSA_KV_HEADS, HEAD_DIM, tm), lambda i: (0, 0, i))
    vt_shape = jax.ShapeDtypeStruct((NSA_KV_HEADS, HEAD_DIM, s), BF16)
    out_shapes = [
        (jax.ShapeDtypeStruct((s, NSA_HEADS * HEAD_DIM), BF16), row(NSA_HEADS * HEAD_DIM)),
        (jax.ShapeDtypeStruct((NSA_KV_HEADS, s, 2 * HEAD_DIM), BF16), grp(2 * HEAD_DIM)),
        (vt_shape, grpt),
        (jax.ShapeDtypeStruct((NSA_KV_HEADS, s, HEAD_DIM), BF16), grp(HEAD_DIM)),
        (vt_shape, grpt),
        (jax.ShapeDtypeStruct((s, MEM_HEADS * HEAD_DIM), BF16), row(MEM_HEADS * HEAD_DIM)),
        (jax.ShapeDtypeStruct((s, 512), F32), row(512)),
        (jax.ShapeDtypeStruct((s, 512), BF16), row(512)),
        (jax.ShapeDtypeStruct((s, 512), F32), row(512)),
        (jax.ShapeDtypeStruct((2 * NSA_KV_HEADS, s, HEAD_DIM), F32),
         pl.BlockSpec((2 * NSA_KV_HEADS, tm, HEAD_DIM), lambda i: (0, i, 0))),
        (jax.ShapeDtypeStruct((s, GLA_LOWRANK), F32), row(GLA_LOWRANK)),
        (jax.ShapeDtypeStruct((NSA_KV_HEADS, _NG_PAD, s), F32),
         pl.BlockSpec((NSA_KV_HEADS, _NG_PAD, tm), lambda i: (0, 0, i))),
    ]
    return pl.pallas_call(
        functools.partial(_proj_body, tm=tm),
        grid=(s // tm,),
        in_specs=[
            row(d),
            _resident((1, d)),
            _resident(w_head.shape), _resident(w_body.shape), _resident(w_mq.shape), _resident(w_t.shape),
            pl.BlockSpec((tm, 1), lambda i: (i, 0)),
            _resident((1, LANES)), _resident((1, LANES)),
            _resident((1, HEAD_DIM)), _resident((3, HEAD_DIM)), _resident((1, HEAD_DIM)),
        ],
        out_specs=[o[1] for o in out_shapes],
        out_shape=[o[0] for o in out_shapes],
        compiler_params=_params(("parallel",)),
        name="proj",
    )(x1, mix_g, w_head, w_body, w_mq, w_t, pos_col, inv128, sgn128, q_norm, k_norm, mq_norm)


def _compress_body(kcvc_ref, w1k_ref, w2k_ref, pek_ref, w1v_ref, w2v_ref, pev_ref, kn_ref,
                   pos_ref, inv_ref, sgn_ref, kcmp_ref, vcmp_ref, *, units):
    half = CMP_LEN // 2
    ang = pos_ref[...].astype(F32) * inv_ref[...]
    cos = jnp.cos(ang)
    sin_signed = jnp.sin(ang) * sgn_ref[...]
    for kind, (w1_ref, w2_ref, pe_ref) in enumerate(((w1k_ref, w2k_ref, pek_ref),
                                                     (w1v_ref, w2v_ref, pev_ref))):
        for g in range(NSA_KV_HEADS):
            slab = kind * NSA_KV_HEADS + g
            a = jnp.zeros((units, w1_ref.shape[1]), F32)
            b = jnp.zeros((units, w1_ref.shape[1]), F32)
            for l in range(half):
                t = kcvc_ref[slab, pl.ds(l, units, stride=CMP_STRIDE), :]
                a = a + _dot((t + pe_ref[l:l + 1, :]).astype(BF16),
                             w1_ref[l * HEAD_DIM:(l + 1) * HEAD_DIM, :])
                b = b + _dot((t + pe_ref[half + l:half + l + 1, :]).astype(BF16),
                             w1_ref[(half + l) * HEAD_DIM:(half + l + 1) * HEAD_DIM, :])
            hid = a + pltpu.roll(b, units - 1, 0)
            act = (hid * _sigmoid(hid)).astype(BF16)
            if kind == 0:
                c = _rope(_rms(_dot(act, w2_ref[...]), kn_ref[0:1, :]), cos, sin_signed)
                kcmp_ref[g] = c.astype(BF16)
            else:
                vcmp_ref[g] = _dot_nt(w2_ref[...], act).astype(BF16)


def _compress(kcvc, w1k, w2k, pek, w1v, w2v, pev, k_norm, pos_cmp, inv128, sgn128):
    s = kcvc.shape[1]
    units = s // CMP_STRIDE
    shp = jax.ShapeDtypeStruct((NSA_KV_HEADS, units, HEAD_DIM), BF16)
    shp_t = jax.ShapeDtypeStruct((NSA_KV_HEADS, HEAD_DIM, units), BF16)
    return pl.pallas_call(
        functools.partial(_compress_body, units=units),
        out_shape=[shp, shp_t],
        compiler_params=pltpu.CompilerParams(vmem_limit_bytes=VMEM_LIMIT),
        name="compress",
    )(kcvc, w1k, w2k, pek, w1v, w2v, pev, k_norm, pos_cmp, inv128, sgn128)


def _gla_body(gqk_ref, gv_ref, gr_ref, ga_ref, wa_ref, ba_ref, on_ref, tcum_ref, bd_ref, hsel_ref,
              o_ref, st_ref, q_s, k_s, b_s, o_s):
    rows = GLA_ROWS
    npair = GLA_HEADS // 2

    @pl.when(pl.program_id(0) == 0)
    def _():
        st_ref[...] = jnp.zeros_like(st_ref)

    z = ba_ref[...]
    for ga_t in _split2(ga_ref[...]):
        for wa_t in _split2(wa_ref[...]):
            z = z + _dot(ga_t, wa_t)
    la = (jnp.minimum(z, 0.0) - jnp.log(1.0 + jnp.exp(-jnp.abs(z)))) / GLA_TAU
    tc = tcum_ref[...]
    bcum = jnp.zeros_like(la)
    for la_t in _split2(la):
        bcum = bcum + _dot(tc, la_t)
    b_s[...] = bcum
    q_s[...] = gqk_ref[:, 0:256] * (GLA_DK ** -0.5)
    k_s[...] = gqk_ref[:, 256:512]

    row_i = lax.broadcasted_iota(jnp.int32, (GLA_SUB, LANES), 0)

    for sb in range(rows // GLA_SUB):
        rs = slice(sb * GLA_SUB, (sb + 1) * GLA_SUB)
        for p in range(npair):
            cs = slice(p * LANES, (p + 1) * LANES)
            vs = slice(p * 2 * GLA_DV, (p + 1) * 2 * GLA_DV)
            qs = q_s[rs, cs]
            kk = k_s[rs, cs]
            bb = b_s[rs, cs]
            vp = gv_ref[rs, vs]
            vpf = vp.astype(F32)
            blast = bb[GLA_SUB - 1:GLA_SUB, :]
            st = st_ref[p]
            o_inter = _dot_nt((qs * jnp.exp(bb)).astype(BF16), st.astype(BF16))
            xs = []
            for j in range(GLA_SUB):
                dlt = jnp.where(row_i >= j, bb - bb[j:j + 1, :], NEG_INF)
                xs.append(qs * jnp.exp(dlt) * kk[j:j + 1, :])
            red = _dot(jnp.concatenate(xs, axis=0).astype(BF16), hsel_ref[...])
            acc = o_inter
            for j in range(GLA_SUB):
                acc = acc + red[j * GLA_SUB:(j + 1) * GLA_SUB, :] * vpf[j:j + 1, :]
            o_s[rs, vs] = acc
            kd = (kk * jnp.exp(blast - bb)).astype(BF16)
            upd = _dot_tn(vp, kd)
            st_ref[p] = st * jnp.exp(blast) + upd * bd_ref[...]

    gr = gr_ref[...]
    for hd in range(GLA_HEADS):
        sl = slice(hd * GLA_DV, (hd + 1) * GLA_DV)
        r = gr[:, sl]
        o_ref[:, sl] = (_rms(o_s[:, sl], on_ref[...]) * (r * _sigmoid(r))).astype(BF16)


def _gla(gqk, gv, gr, ga, wa, ba, o_norm):
    s = gqk.shape[0]
    rows = GLA_ROWS
    idx = np.arange(rows)
    tcum = ((idx[:, None] >= idx[None, :]) & (idx[:, None] // GLA_SUB == idx[None, :] // GLA_SUB))
    tcum = jnp.asarray(tcum, BF16)
    r256 = np.arange(2 * GLA_DV)[:, None] // GLA_DV
    c128 = np.arange(LANES)[None, :] // GLA_DK
    bdmask = jnp.asarray(r256 == c128, F32)
    hsel = jnp.asarray((r256 == c128).T, BF16)
    row = lambda w: pl.BlockSpec((rows, w), lambda i: (i, 0))
    return pl.pallas_call(
        _gla_body,
        grid=(s // rows,),
        in_specs=[row(512), row(512), row(512), row(GLA_LOWRANK),
                  _resident(wa.shape), _resident(ba.shape), _resident(o_norm.shape),
                  _resident(tcum.shape), _resident(bdmask.shape), _resident(hsel.shape)],
        out_specs=row(512),
        out_shape=jax.ShapeDtypeStruct((s, GLA_HEADS * GLA_DV), BF16),
        scratch_shapes=[pltpu.VMEM((GLA_HEADS // 2, 2 * GLA_DV, LANES), F32),
                        pltpu.VMEM((rows, 256), F32), pltpu.VMEM((rows, 256), F32),
                        pltpu.VMEM((rows, 256), F32), pltpu.VMEM((rows, 512), F32)],
        compiler_params=_params(("arbitrary",)),
        name="gla",
    )(gqk, gv, gr, ga, wa, ba, o_norm, tcum, bdmask, hsel)


def _nsa_body(q_ref, kcmp_ref, vcmpt_ref, ksel_ref, vselt_ref, kwin_ref, vwint_ref, ngt_ref, ovlt_ref,
              o_ref, qt_s, sa_s, sb_s, pa_s, pb_s, acc_s, out_s, *, n_sel):
    qb = pl.program_id(1)
    t0 = qb * QBLK
    cols = NSA_HPG * QBLK
    ncmp = kcmp_ref.shape[0]

    for hd in range(NSA_HPG):
        qh = q_ref[:, hd * HEAD_DIM:(hd + 1) * HEAD_DIM].astype(F32)
        qt_s[0:HEAD_DIM, hd * QBLK:(hd + 1) * QBLK] = qh.T.astype(BF16)
    qt = qt_s[0:HEAD_DIM, :]

    def tq_of(rows):
        return t0 + (lax.broadcasted_iota(jnp.int32, (rows, cols), 1) & (QBLK - 1))

    def flash(carry, s, valid, vt_tile):
        m, l, acc = carry
        if valid is not None:
            s = jnp.where(valid, s, NEG_INF)
        m_new = jnp.maximum(m, jnp.max(s, axis=0, keepdims=True))
        alpha = jnp.exp2(m - m_new)
        p = jnp.exp2(s - m_new)
        if valid is not None:
            p = jnp.where(valid, p, 0.0)
        l = alpha * l + jnp.sum(p, axis=0, keepdims=True)
        acc = alpha * acc + _dot(vt_tile, p.astype(BF16))
        return (m_new, l, acc), alpha, p

    gates_t = _sigmoid(ngt_ref[...])

    def gate_row(c):
        return jnp.concatenate([gates_t[3 * hd + c:3 * hd + c + 1, :] for hd in range(NSA_HPG)], axis=1)

    n_row = lax.broadcasted_iota(jnp.int32, (ncmp, cols), 0)
    valid_c = n_row * CMP_STRIDE + (CMP_LEN - 1) <= tq_of(ncmp)
    s_c = jnp.where(valid_c, _dot(kcmp_ref[...], qt), NEG_INF)
    e_c = jnp.where(valid_c, jnp.exp2(s_c - jnp.max(s_c, axis=0, keepdims=True)), 0.0)
    p_c = e_c / jnp.maximum(jnp.sum(e_c, axis=0, keepdims=True), TINY)
    out_s[...] = gate_row(0) * _dot(vcmpt_ref[...], p_c.astype(BF16))
    psum = p_c[:, 0:QBLK]
    for hd in range(1, NSA_HPG):
        psum = psum + p_c[:, hd * QBLK:(hd + 1) * QBLK]
    imp = jnp.zeros((LANES, QBLK), F32)
    for p_t in _split2(psum):
        imp = imp + _dot(ovlt_ref[...], p_t)

    def win_part(start, length):
        start = pl.multiple_of(jnp.maximum(start, 0), QBLK)
        return _dot(kwin_ref[pl.ds(start, length), :], qt), vwint_ref[:, pl.ds(start, length)], start

    w_row = lax.broadcasted_iota(jnp.int32, (QBLK, cols), 0)
    tq_w = tq_of(QBLK)
    s_old, v_old, _ = win_part(t0 - WINDOW, QBLK)
    kp_old = t0 - WINDOW + w_row
    valid_old = (kp_old > tq_w - WINDOW) & (kp_old >= 0)
    s_old = jnp.where(valid_old, s_old, NEG_INF)
    s_mid, v_mid, mid0 = win_part(t0 - WINDOW + QBLK, WINDOW - QBLK)
    mid_row = mid0 + lax.broadcasted_iota(jnp.int32, (WINDOW - QBLK, cols), 0)
    s_mid = jnp.where(mid_row < t0, s_mid, NEG_INF)
    s_dg, v_dg, _ = win_part(t0, QBLK)
    valid_dg = t0 + w_row <= tq_w
    s_dg = jnp.where(valid_dg, s_dg, NEG_INF)
    m_w = jnp.maximum(jnp.maximum(jnp.max(s_old, axis=0, keepdims=True), jnp.max(s_mid, axis=0, keepdims=True)),
                      jnp.max(s_dg, axis=0, keepdims=True))
    p_old = jnp.where(valid_old, jnp.exp2(s_old - m_w), 0.0)
    p_mid = jnp.exp2(s_mid - m_w)
    p_dg = jnp.where(valid_dg, jnp.exp2(s_dg - m_w), 0.0)
    l_w = (jnp.sum(p_old, axis=0, keepdims=True) + jnp.sum(p_mid, axis=0, keepdims=True)
           + jnp.sum(p_dg, axis=0, keepdims=True))
    acc_w = (_dot(v_old, p_old.astype(BF16)) + _dot(v_mid, p_mid.astype(BF16)) + _dot(v_dg, p_dg.astype(BF16)))
    out_s[...] += gate_row(2) * (acc_w / jnp.maximum(l_w, TINY))

    tq = t0 + lax.broadcasted_iota(jnp.int32, (LANES, QBLK), 1)
    m_i = lax.broadcasted_iota(jnp.int32, (LANES, QBLK), 0)
    cur = lax.shift_right_logical(tq, 6)
    forced = (m_i == 0) | (m_i == cur) | (m_i == cur - 1)
    score = jnp.where(m_i * SEL_LEN <= tq, jnp.where(forced, FORCE_SCORE, imp), -FORCE_SCORE)
    score = jnp.where(m_i < n_sel, score, SEL_PAD_SCORE)
    m_f = m_i.astype(F32)
    bias = jnp.full((LANES, QBLK), SEL_MASK_BIAS, F32)
    for _ in range(min(SEL_TOPK, n_sel)):
        mx = jnp.max(score, axis=0, keepdims=True)
        first = jnp.min(jnp.where(score == mx, m_f, float(LANES)), axis=0, keepdims=True)
        pick = m_f == first
        bias = jnp.where(pick, 0.0, bias)
        score = jnp.where(pick, -jnp.inf, score)
    bias = bias.astype(BF16)
    for hd in range(NSA_HPG):
        qt_s[HEAD_DIM:2 * HEAD_DIM, hd * QBLK:(hd + 1) * QBLK] = bias

    qext = qt_s[...]
    k_row = lax.broadcasted_iota(jnp.int32, (SEL_KT, cols), 0)
    tq_s = tq_of(SEL_KT)

    def sel_scores(kt):
        k0 = pl.multiple_of(kt * SEL_KT, SEL_KT)
        return _dot(ksel_ref[pl.ds(k0, SEL_KT), :], qext)

    def sel_pv(kt, p):
        k0 = pl.multiple_of(kt * SEL_KT, SEL_KT)
        return _dot(vselt_ref[:, pl.ds(k0, SEL_KT)], p)

    def soft(s_buf, p_buf, pv, ml):
        m, l = ml
        s = s_buf[...]
        m_new = jnp.maximum(m, jnp.max(s, axis=0, keepdims=True))
        alpha = jnp.exp2(m - m_new)
        p = jnp.exp2(s - m_new)
        p_buf[...] = p.astype(BF16)
        acc_s[...] = alpha * (acc_s[...] + pv)
        return m_new, alpha * l + jnp.sum(p, axis=0, keepdims=True)

    def sel_pair(j, ml):
        sb_s[...] = sel_scores(2 * j + 1)
        ml = soft(sa_s, pa_s, sel_pv(jnp.maximum(2 * j - 1, 0), pb_s[...]), ml)
        sa_s[...] = sel_scores(2 * j + 2)
        return soft(sb_s, pb_s, sel_pv(2 * j, pa_s[...]), ml)

    n_pair = t0 // (2 * SEL_KT)
    sa_s[...] = sel_scores(0)
    pb_s[...] = jnp.zeros_like(pb_s)
    acc_s[...] = jnp.zeros_like(acc_s)
    ml = lax.fori_loop(0, n_pair, sel_pair, (jnp.full((1, cols), NEG_INF, F32), jnp.zeros((1, cols), F32)))
    e0 = 2 * n_pair
    acc_s[...] += sel_pv(jnp.maximum(e0 - 1, 0), pb_s[...])

    def diag_tile(e, s, ml):
        k0 = pl.multiple_of(e * SEL_KT, SEL_KT)
        (m, l, acc), _, _ = flash((ml[0], ml[1], acc_s[...]), s, k0 + k_row <= tq_s,
                                  vselt_ref[:, pl.ds(k0, SEL_KT)])
        acc_s[...] = acc
        return m, l

    ml = diag_tile(e0, sa_s[...], ml)
    ml = lax.cond(t0 >= (e0 + 1) * SEL_KT, lambda c: diag_tile(e0 + 1, sel_scores(e0 + 1), c), lambda c: c, ml)

    out = out_s[...] + gate_row(1) * (acc_s[...] / jnp.maximum(ml[1], TINY))
    for hd in range(NSA_HPG):
        o_ref[:, hd * HEAD_DIM:(hd + 1) * HEAD_DIM] = out[:, hd * QBLK:(hd + 1) * QBLK].T.astype(BF16)


def _nsa(q, kcmp, vcmpt, ksel, vselt, kwin, vwint, ng, overlap_t):
    s = q.shape[0]
    n_sel = s // SEL_LEN
    assert n_sel <= LANES and s % (2 * SEL_KT) == 0 and s >= WINDOW + QBLK
    ncmp = kcmp.shape[1]
    cols = NSA_HPG * QBLK
    grp = lambda r, w: pl.BlockSpec((None, r, w), lambda g, b: (g, 0, 0))
    return pl.pallas_call(
        functools.partial(_nsa_body, n_sel=n_sel),
        grid=(NSA_KV_HEADS, s // QBLK),
        in_specs=[
            pl.BlockSpec((QBLK, NSA_HPG * HEAD_DIM), lambda g, b: (b, g)),
            grp(ncmp, HEAD_DIM), grp(HEAD_DIM, ncmp),
            grp(s, 2 * HEAD_DIM), grp(HEAD_DIM, s), grp(s, HEAD_DIM), grp(HEAD_DIM, s),
            pl.BlockSpec((None, _NG_PAD, QBLK), lambda g, b: (g, 0, b)),
            pl.BlockSpec(overlap_t.shape, lambda g, b: (0, 0)),
        ],
        out_specs=pl.BlockSpec((QBLK, NSA_HPG * HEAD_DIM), lambda g, b: (b, g)),
        out_shape=jax.ShapeDtypeStruct((s, NSA_HEADS * HEAD_DIM), BF16),
        scratch_shapes=[pltpu.VMEM((2 * HEAD_DIM, cols), BF16),
                        pltpu.VMEM((SEL_KT, cols), F32), pltpu.VMEM((SEL_KT, cols), F32),
                        pltpu.VMEM((SEL_KT, cols), BF16), pltpu.VMEM((SEL_KT, cols), BF16),
                        pltpu.VMEM((HEAD_DIM, cols), F32), pltpu.VMEM((HEAD_DIM, cols), F32)],
        compiler_params=_params(("parallel", "arbitrary")),
        name="nsa",
    )(q, kcmp, vcmpt, ksel, vselt, kwin, vwint, ng, overlap_t)


def _memkv_body(mem_ref, g_ref, w_ref, kn_ref, k_ref, v_ref):
    kv = _dot(_rms(mem_ref[...], g_ref[...]).astype(BF16), w_ref[...])
    width = MEM_HEADS * HEAD_DIM
    for hd in range(MEM_HEADS):
        sl = slice(hd * HEAD_DIM, (hd + 1) * HEAD_DIM)
        k_ref[:, sl] = _rms(kv[:, sl], kn_ref[...]).astype(BF16)
    v_ref[...] = kv[:, width:].astype(BF16)


def _memkv(mem, in_g, w_kv, k_norm):
    m = mem.shape[0]
    shp = jax.ShapeDtypeStruct((m, MEM_HEADS * HEAD_DIM), BF16)
    return pl.pallas_call(
        _memkv_body, out_shape=[shp, shp],
        compiler_params=pltpu.CompilerParams(vmem_limit_bytes=VMEM_LIMIT),
        name="memkv",
    )(mem, in_g, w_kv, k_norm)


def _memattn_body(q_ref, k_ref, v_ref, o_ref):
    for hd in range(MEM_HEADS):
        sl = slice(hd * HEAD_DIM, (hd + 1) * HEAD_DIM)
        sc = _dot_nt(q_ref[:, sl], k_ref[:, sl])
        e = jnp.exp(sc - jnp.max(sc, axis=-1, keepdims=True))
        p = e / jnp.sum(e, axis=-1, keepdims=True)
        o_ref[:, sl] = _dot(p.astype(BF16), v_ref[:, sl]).astype(BF16)


def _memattn(q, k, v, *, tm=512):
    s, w = q.shape
    return pl.pallas_call(
        _memattn_body,
        grid=(s // tm,),
        in_specs=[pl.BlockSpec((tm, w), lambda i: (i, 0)), _resident(k.shape), _resident(v.shape)],
        out_specs=pl.BlockSpec((tm, w), lambda i: (i, 0)),
        out_shape=jax.ShapeDtypeStruct((s, w), BF16),
        compiler_params=_params(("parallel",)),
        name="memattn",
    )(q, k, v)


def _outproj_body(x_ref, a_ref, b_ref, c_ref, w_ref, o_ref):
    na, nb = a_ref.shape[1], b_ref.shape[1]
    o_ref[...] = (x_ref[...] + _dot(a_ref[...], w_ref[0:na, :]) + _dot(b_ref[...], w_ref[na:na + nb, :])
                  + _dot(c_ref[...], w_ref[na + nb:, :]))


def _outproj(x1, o_gla, o_nsa, o_mem, w_out, *, tm=512):
    s, d = x1.shape
    row = lambda w: pl.BlockSpec((tm, w), lambda i: (i, 0))
    return pl.pallas_call(
        _outproj_body,
        grid=(s // tm,),
        in_specs=[row(d), row(o_gla.shape[1]), row(o_nsa.shape[1]), row(o_mem.shape[1]),
                  _resident(w_out.shape)],
        out_specs=row(d),
        out_shape=jax.ShapeDtypeStruct((s, d), F32),
        compiler_params=_params(("parallel",)),
        name="outproj",
    )(x1, o_gla, o_nsa, o_mem, w_out)


def _split_w_in(w_in):
    d = w_in.shape[0]
    body0 = _HEAD_W
    ng0 = body0 + _BODY_W
    per_g = NSA_HPG * 3
    mq0 = ng0 + NSA_KV_HEADS * per_g
    w_head = w_in[:, :_HEAD_W].astype(BF16)
    w_body = w_in[:, body0:ng0].astype(BF16)
    w_mq = w_in[:, mq0:].astype(BF16)
    assert w_mq.shape[1] == MEM_HEADS * HEAD_DIM
    n_vs = w_in[:, body0 + _B_KS + _KV_W:body0 + _B_KS + 2 * _KV_W]
    n_vw = w_in[:, body0 + _B_KW + _KV_W:body0 + _B_KW + 2 * _KV_W]
    rows = [n_vs.T, n_vw.T]
    for g in range(NSA_KV_HEADS):
        rows += [w_in[:, ng0 + g * per_g:ng0 + (g + 1) * per_g].T, jnp.zeros((_NG_PAD - per_g, d), w_in.dtype)]
    w_t = jnp.concatenate(rows, axis=0).astype(BF16)
    return w_head, w_body, w_mq, w_t


def _layer(x, mem, positions, ffn1_norm, ffn1_w_gate, ffn1_w_up, ffn1_w_down, mix_norm, w_in,
           gla_w_a, gla_b_a, gla_o_norm, nsa_q_norm, nsa_k_norm, nsa_cmp_pos_k, nsa_cmp_w1_k,
           nsa_cmp_w2_k, nsa_cmp_pos_v, nsa_cmp_w1_v, nsa_cmp_w2_v, mem_in_norm, w_mem_kv,
           mem_q_norm, mem_k_norm, w_out, ffn2_norm, ffn2_w_gate, ffn2_w_up, ffn2_w_down, final_norm):
    s, d = x.shape
    row = lambda v: v.reshape(1, -1)
    bf = lambda v: v.astype(BF16)

    x1 = _ffn(x, row(ffn1_norm), bf(ffn1_w_gate), bf(ffn1_w_up), bf(ffn1_w_down))

    half = HEAD_DIM // 2
    inv = ROPE_THETA ** (-jnp.arange(half, dtype=F32) / half)
    inv128 = jnp.concatenate([inv, inv]).reshape(1, HEAD_DIM)
    sgn128 = jnp.concatenate([-jnp.ones((half,), F32), jnp.ones((half,), F32)]).reshape(1, HEAD_DIM)
    (q, ksel, vselt, kwin, vwint, mq, gqk, gv, gr, kcvc, ga, ngt) = _proj(
        x1, row(mix_norm), *_split_w_in(w_in), positions.reshape(s, 1), inv128, sgn128,
        row(nsa_q_norm), nsa_k_norm, row(mem_q_norm))

    o_gla = _gla(gqk, gv, gr, ga, gla_w_a, row(gla_b_a), row(gla_o_norm))

    units = s // CMP_STRIDE
    n_cmp = (s - CMP_LEN) // CMP_STRIDE + 1
    cmp_last = jnp.arange(units) * CMP_STRIDE + CMP_LEN - 1
    pos_cmp = positions[jnp.minimum(cmp_last, s - 1)].reshape(units, 1)
    kcmp, vcmpt = _compress(kcvc, bf(nsa_cmp_w1_k), bf(nsa_cmp_w2_k), nsa_cmp_pos_k,
                            bf(nsa_cmp_w1_v), bf(nsa_cmp_w2_v.T), nsa_cmp_pos_v, nsa_k_norm,
                            pos_cmp, inv128, sgn128)
    n_sel = s // SEL_LEN
    cmp_start = np.arange(units) * CMP_STRIDE
    sel_start = np.arange(LANES) * SEL_LEN
    overlap = np.clip(np.minimum(cmp_start[:, None] + CMP_LEN, sel_start[None, :] + SEL_LEN)
                      - np.maximum(cmp_start[:, None], sel_start[None, :]), 0, None) / CMP_STRIDE
    overlap = overlap * (np.arange(units)[:, None] < n_cmp) * (np.arange(LANES)[None, :] < n_sel)
    o_nsa = _nsa(q, kcmp, vcmpt, ksel, vselt, kwin, vwint, ngt, jnp.asarray(overlap.T, BF16))

    kmem, vmem = _memkv(mem, row(mem_in_norm), bf(w_mem_kv), row(mem_k_norm))
    o_mem = _memattn(mq, kmem, vmem)

    x2 = _outproj(x1, o_gla, o_nsa, o_mem, bf(w_out))
    return _ffn(x2, row(ffn2_norm), bf(ffn2_w_gate), bf(ffn2_w_up), bf(ffn2_w_down), row(final_norm))


def kernel(x, mem, positions, ffn1_norm, ffn1_w_gate, ffn1_w_up, ffn1_w_down, mix_norm, w_in, gla_w_a, gla_b_a, gla_o_norm, nsa_q_norm, nsa_k_norm, nsa_cmp_pos_k, nsa_cmp_w1_k, nsa_cmp_w2_k, nsa_cmp_pos_v, nsa_cmp_w1_v, nsa_cmp_w2_v, mem_in_norm, w_mem_kv, mem_q_norm, mem_k_norm, w_out, ffn2_norm, ffn2_w_gate, ffn2_w_up, ffn2_w_down, final_norm):
    depth = ffn1_norm.shape[0]
    outs = []
    for b in range(x.shape[0]):
        xb = x[b]
        for l in range(depth):
            xb = _layer(xb, mem[b], positions[b], ffn1_norm[l], ffn1_w_gate[l], ffn1_w_up[l], ffn1_w_down[l],
                        mix_norm[l], w_in[l], gla_w_a[l], gla_b_a[l], gla_o_norm[l], nsa_q_norm[l],
                        nsa_k_norm[l], nsa_cmp_pos_k[l], nsa_cmp_w1_k[l], nsa_cmp_w2_k[l], nsa_cmp_pos_v[l],
                        nsa_cmp_w1_v[l], nsa_cmp_w2_v[l], mem_in_norm[l], w_mem_kv[l], mem_q_norm[l],
                        mem_k_norm[l], w_out[l], ffn2_norm[l], ffn2_w_gate[l], ffn2_w_up[l], ffn2_w_down[l],
                        final_norm[l])
        outs.append(xb)
    return jnp.stack(outs)
```

```python
import functools

import numpy as np
import jax
import jax.numpy as jnp
from jax import lax
from jax.experimental import pallas as pl
from jax.experimental.pallas import tpu as pltpu

F32 = jnp.float32
BF16 = jnp.bfloat16

HEAD_DIM = 128
GLA_HEADS = 4
GLA_DK = 64
GLA_DV = 128
GLA_LOWRANK = 16
GLA_TAU = 16.0
NSA_HEADS = 8
NSA_KV_HEADS = 2
NSA_HPG = NSA_HEADS // NSA_KV_HEADS
CMP_LEN = 32
CMP_STRIDE = 16
SEL_LEN = 64
SEL_TOPK = 16
WINDOW = 512
MEM_HEADS = 4
MACARON_W = 0.5
QBLK = 128
ROPE_THETA = 10000.0
EPS = 1e-6
NEG_INF = -1e30
TINY = 1e-30
FORCE_SCORE = 1e4
LOG2_E = 1.4426950408889634

LANES = 128
VMEM_LIMIT = 56 * 1024 * 1024

GLA_SUB = 16
GLA_ROWS = 128
SEL_KT = 256
SEL_SPAN_TILES = 4
V_ROWS = HEAD_DIM + 16
SEL_MASK_BIAS = -32768.0
SEL_PAD_SCORE = -3e4


def _dot(a, b):
    return jnp.dot(a, b, preferred_element_type=F32)


def _dot_nt(a, b):
    return lax.dot_general(a, b, (((1,), (1,)), ((), ())), preferred_element_type=F32)


def _dot_tn(a, b):
    return lax.dot_general(a, b, (((0,), (0,)), ((), ())), preferred_element_type=F32)


def _split2(x):
    hi = x.astype(BF16)
    return hi, (x - hi.astype(F32)).astype(BF16)


def _rms(x, g):
    return x * lax.rsqrt(jnp.mean(x * x, axis=-1, keepdims=True) + EPS) * g


def _sigmoid(x):
    return 1.0 / (1.0 + jnp.exp(-x))


def _params(sem):
    return pltpu.CompilerParams(dimension_semantics=sem, vmem_limit_bytes=VMEM_LIMIT)


def _resident(shape):
    nd = len(shape)
    return pl.BlockSpec(shape, lambda *_: (0,) * nd, pipeline_mode=pl.Buffered(1))


def _ffn_body(*refs, final, nf):
    if final:
        x_ref, g_ref, wg_ref, wu_ref, wd_ref, fg_ref, o_ref, h_ref = refs
    else:
        x_ref, g_ref, wg_ref, wu_ref, wd_ref, o_ref, h_ref = refs
    f = pl.program_id(1)

    @pl.when(f == 0)
    def _():
        x = x_ref[...]
        h_ref[...] = _rms(x, g_ref[...]).astype(BF16)
        o_ref[...] = x

    h = h_ref[...]
    g = _dot(h, wg_ref[...].astype(BF16))
    u = _dot(h, wu_ref[...].astype(BF16))
    a = (g * _sigmoid(g)) * u * MACARON_W
    o_ref[...] += _dot(a.astype(BF16), wd_ref[...].astype(BF16))

    if final:
        @pl.when(f == nf - 1)
        def _():
            o_ref[...] = _rms(o_ref[...], fg_ref[...])


def _ffn(x, norm_g, wg, wu, wd, final_g=None, *, tm=1024, tf=256):
    s, d = x.shape
    ff = wg.shape[1]
    nf = ff // tf
    final = final_g is not None
    in_specs = [
        pl.BlockSpec((tm, d), lambda i, f: (i, 0), pipeline_mode=pl.Buffered(1)),
        pl.BlockSpec((1, d), lambda i, f: (0, 0)),
        pl.BlockSpec((d, tf), lambda i, f: (0, f)),
        pl.BlockSpec((d, tf), lambda i, f: (0, f)),
        pl.BlockSpec((tf, d), lambda i, f: (f, 0)),
    ]
    args = [x, norm_g, wg, wu, wd]
    if final:
        in_specs.append(pl.BlockSpec((1, d), lambda i, f: (0, 0)))
        args.append(final_g)
    return pl.pallas_call(
        functools.partial(_ffn_body, final=final, nf=nf),
        grid=(s // tm, nf),
        in_specs=in_specs,
        out_specs=pl.BlockSpec((tm, d), lambda i, f: (i, 0)),
        out_shape=jax.ShapeDtypeStruct((s, d), F32),
        scratch_shapes=[pltpu.VMEM((tm, d), BF16)],
        compiler_params=_params(("parallel", "arbitrary")),
        name="ffn_final" if final else "ffn",
    )(*args)


_GQK_W = 2 * GLA_HEADS * GLA_DK
_GV_W = GLA_HEADS * GLA_DV
_H_GV = _GQK_W
_H_GR = _H_GV + _GV_W
_H_GA = _H_GR + _GV_W
_HEAD_W = _H_GA + GLA_LOWRANK
_KV_W = NSA_KV_HEADS * HEAD_DIM
_B_KC = NSA_HEADS * HEAD_DIM
_B_KS = _B_KC + 2 * _KV_W
_B_KW = _B_KS + 2 * _KV_W
_BODY_W = _B_KW + 2 * _KV_W
_NG_PAD = 16
_T_NG = 2 * _KV_W


def _rope(x, cos, sin_signed):
    return x * cos + pltpu.roll(x, HEAD_DIM // 2, 1) * sin_signed


def _proj_body(x_ref, g_ref, wh_ref, wb_ref, wmq_ref, wt_ref, pos_ref, inv_ref, sgn_ref, qn_ref, kn_ref,
               mqn_ref, q_ref, ksel_ref, vselt_ref, kwin_ref, vwint_ref, mq_ref, gqk_ref, gv_ref, gr_ref,
               kcvc_ref, ga_ref, ngt_ref, *, tm):
    i = pl.program_id(0)
    h = _rms(x_ref[...], g_ref[...]).astype(BF16)

    def proj(w_ref, c0, width):
        return _dot(h, w_ref[:, c0:c0 + width])

    ang = pos_ref[...].astype(F32) * inv_ref[...]
    cos = jnp.cos(ang)
    sin_signed = jnp.sin(ang) * sgn_ref[...]

    def norm_rope(t, gain):
        return _rope(_rms(t, gain), cos, sin_signed)

    scale = HEAD_DIM ** -0.5
    nq = proj(wb_ref, 0, NSA_HEADS * HEAD_DIM)
    for hd in range(NSA_HEADS):
        sl = slice(hd * HEAD_DIM, (hd + 1) * HEAD_DIM)
        q_ref[:, sl] = (norm_rope(nq[:, sl], qn_ref[...]) * (scale * LOG2_E)).astype(BF16)

    tok = i * tm + lax.broadcasted_iota(jnp.int32, (tm, LANES), 0)
    blk = lax.broadcasted_iota(jnp.int32, (tm, LANES), 1)
    onehot = jnp.where(lax.shift_right_logical(tok, 6) == blk, 1.0, 0.0).astype(BF16)
    ks = proj(wb_ref, _B_KS, _KV_W)
    kw = proj(wb_ref, _B_KW, _KV_W)
    vt = _dot_nt(wt_ref[...], h)
    ones_rows = jnp.where(lax.broadcasted_iota(jnp.int32, (V_ROWS - HEAD_DIM, tm), 0) == 0, 1.0, 0.0).astype(BF16)
    for g in range(NSA_KV_HEADS):
        sl = slice(g * HEAD_DIM, (g + 1) * HEAD_DIM)
        slw = slice(_KV_W + g * HEAD_DIM, _KV_W + (g + 1) * HEAD_DIM)
        ksel_ref[g, :, 0:HEAD_DIM] = norm_rope(ks[:, sl], kn_ref[1:2, :]).astype(BF16)
        ksel_ref[g, :, HEAD_DIM:2 * HEAD_DIM] = onehot
        kwin_ref[g] = norm_rope(kw[:, sl], kn_ref[2:3, :]).astype(BF16)
        vselt_ref[g, 0:HEAD_DIM, :] = vt[sl, :].astype(BF16)
        vwint_ref[g, 0:HEAD_DIM, :] = vt[slw, :].astype(BF16)
        vselt_ref[g, HEAD_DIM:V_ROWS, :] = ones_rows
        vwint_ref[g, HEAD_DIM:V_ROWS, :] = ones_rows
        ngt_ref[g] = vt[_T_NG + g * _NG_PAD:_T_NG + (g + 1) * _NG_PAD, :]

    mq = _dot(h, wmq_ref[...])
    for hd in range(MEM_HEADS):
        sl = slice(hd * HEAD_DIM, (hd + 1) * HEAD_DIM)
        mq_ref[:, sl] = (_rms(mq[:, sl], mqn_ref[...]) * scale).astype(BF16)

    gqk_ref[...] = proj(wh_ref, 0, _GQK_W)
    gv_ref[...] = proj(wh_ref, _H_GV, _GV_W).astype(BF16)
    gr_ref[...] = proj(wh_ref, _H_GR, _GV_W)
    ga_ref[...] = proj(wh_ref, _H_GA, GLA_LOWRANK)
    kcvc = proj(wb_ref, _B_KC, 2 * _KV_W)
    for j in range(2 * NSA_KV_HEADS):
        kcvc_ref[j] = kcvc[:, j * HEAD_DIM:(j + 1) * HEAD_DIM]


def _proj(x1, mix_g, w_head, w_body, w_mq, w_t, pos_col, inv128, sgn128, q_norm, k_norm, mq_norm, *, tm=512):
    s, d = x1.shape
    row = lambda w: pl.BlockSpec((tm, w), lambda i: (i, 0))
    grp = lambda w: pl.BlockSpec((NSA_KV_HEADS, tm, w), lambda i: (0, i, 0))
    grpt = pl.BlockSpec((NSA_KV_HEADS, V_ROWS, tm), lambda i: (0, 0, i))
    vt_shape = jax.ShapeDtypeStruct((NSA_KV_HEADS, V_ROWS, s), BF16)
    out_shapes = [
        (jax.ShapeDtypeStruct((s, NSA_HEADS * HEAD_DIM), BF16), row(NSA_HEADS * HEAD_DIM)),
        (jax.ShapeDtypeStruct((NSA_KV_HEADS, s, 2 * HEAD_DIM), BF16), grp(2 * HEAD_DIM)),
        (vt_shape, grpt),
        (jax.ShapeDtypeStruct((NSA_KV_HEADS, s, HEAD_DIM), BF16), grp(HEAD_DIM)),
        (vt_shape, grpt),
        (jax.ShapeDtypeStruct((s, MEM_HEADS * HEAD_DIM), BF16), row(MEM_HEADS * HEAD_DIM)),
        (jax.ShapeDtypeStruct((s, 512), F32), row(512)),
        (jax.ShapeDtypeStruct((s, 512), BF16), row(512)),
        (jax.ShapeDtypeStruct((s, 512), F32), row(512)),
        (jax.ShapeDtypeStruct((2 * NSA_KV_HEADS, s, HEAD_DIM), F32),
         pl.BlockSpec((2 * NSA_KV_HEADS, tm, HEAD_DIM), lambda i: (0, i, 0))),
        (jax.ShapeDtypeStruct((s, GLA_LOWRANK), F32), row(GLA_LOWRANK)),
        (jax.ShapeDtypeStruct((NSA_KV_HEADS, _NG_PAD, s), F32),
         pl.BlockSpec((NSA_KV_HEADS, _NG_PAD, tm), lambda i: (0, 0, i))),
    ]
    return pl.pallas_call(
        functools.partial(_proj_body, tm=tm),
        grid=(s // tm,),
        in_specs=[
            row(d),
            _resident((1, d)),
            _resident(w_head.shape), _resident(w_body.shape), _resident(w_mq.shape), _resident(w_t.shape),
            pl.BlockSpec((tm, 1), lambda i: (i, 0)),
            _resident((1, LANES)), _resident((1, LANES)),
            _resident((1, HEAD_DIM)), _resident((3, HEAD_DIM)), _resident((1, HEAD_DIM)),
        ],
        out_specs=[o[1] for o in out_shapes],
        out_shape=[o[0] for o in out_shapes],
        compiler_params=_params(("parallel",)),
        name="proj",
    )(x1, mix_g, w_head, w_body, w_mq, w_t, pos_col, inv128, sgn128, q_norm, k_norm, mq_norm)


def _compress_body(kcvc_ref, w1k_ref, w2k_ref, pek_ref, w1v_ref, w2v_ref, pev_ref, kn_ref,
                   pos_ref, inv_ref, sgn_ref, kcmp_ref, vcmp_ref, *, units):
    half = CMP_LEN // 2
    ang = pos_ref[...].astype(F32) * inv_ref[...]
    cos = jnp.cos(ang)
    sin_signed = jnp.sin(ang) * sgn_ref[...]
    for kind, (w1_ref, w2_ref, pe_ref) in enumerate(((w1k_ref, w2k_ref, pek_ref),
                                                     (w1v_ref, w2v_ref, pev_ref))):
        for g in range(NSA_KV_HEADS):
            slab = kind * NSA_KV_HEADS + g
            a = jnp.zeros((units, w1_ref.shape[1]), F32)
            b = jnp.zeros((units, w1_ref.shape[1]), F32)
            for l in range(half):
                t = kcvc_ref[slab, pl.ds(l, units, stride=CMP_STRIDE), :]
                a = a + _dot((t + pe_ref[l:l + 1, :]).astype(BF16),
                             w1_ref[l * HEAD_DIM:(l + 1) * HEAD_DIM, :])
                b = b + _dot((t + pe_ref[half + l:half + l + 1, :]).astype(BF16),
                             w1_ref[(half + l) * HEAD_DIM:(half + l + 1) * HEAD_DIM, :])
            hid = a + pltpu.roll(b, units - 1, 0)
            act = (hid * _sigmoid(hid)).astype(BF16)
            if kind == 0:
                c = _rope(_rms(_dot(act, w2_ref[...]), kn_ref[0:1, :]), cos, sin_signed)
                kcmp_ref[g] = c.astype(BF16)
            else:
                vcmp_ref[g] = _dot_nt(w2_ref[...], act).astype(BF16)


def _compress(kcvc, w1k, w2k, pek, w1v, w2v, pev, k_norm, pos_cmp, inv128, sgn128):
    s = kcvc.shape[1]
    units = s // CMP_STRIDE
    shp = jax.ShapeDtypeStruct((NSA_KV_HEADS, units, HEAD_DIM), BF16)
    shp_t = jax.ShapeDtypeStruct((NSA_KV_HEADS, HEAD_DIM, units), BF16)
    return pl.pallas_call(
        functools.partial(_compress_body, units=units),
        out_shape=[shp, shp_t],
        compiler_params=pltpu.CompilerParams(vmem_limit_bytes=VMEM_LIMIT),
        name="compress",
    )(kcvc, w1k, w2k, pek, w1v, w2v, pev, k_norm, pos_cmp, inv128, sgn128)


def _gla_body(gqk_ref, gv_ref, gr_ref, ga_ref, wa_ref, ba_ref, on_ref, tcum_ref, bd_ref, hsel_ref,
              o_ref, st_ref, q_s, k_s, b_s, o_s):
    rows = GLA_ROWS
    npair = GLA_HEADS // 2

    @pl.when(pl.program_id(0) == 0)
    def _():
        st_ref[...] = jnp.zeros_like(st_ref)

    z = ba_ref[...]
    for ga_t in _split2(ga_ref[...]):
        for wa_t in _split2(wa_ref[...]):
            z = z + _dot(ga_t, wa_t)
    la = (jnp.minimum(z, 0.0) - jnp.log(1.0 + jnp.exp(-jnp.abs(z)))) / GLA_TAU
    tc = tcum_ref[...]
    bcum = jnp.zeros_like(la)
    for la_t in _split2(la):
        bcum = bcum + _dot(tc, la_t)
    b_s[...] = bcum
    q_s[...] = gqk_ref[:, 0:256] * (GLA_DK ** -0.5)
    k_s[...] = gqk_ref[:, 256:512]

    row_i = lax.broadcasted_iota(jnp.int32, (GLA_SUB, LANES), 0)

    for sb in range(rows // GLA_SUB):
        rs = slice(sb * GLA_SUB, (sb + 1) * GLA_SUB)
        for p in range(npair):
            cs = slice(p * LANES, (p + 1) * LANES)
            vs = slice(p * 2 * GLA_DV, (p + 1) * 2 * GLA_DV)
            qs = q_s[rs, cs]
            kk = k_s[rs, cs]
            bb = b_s[rs, cs]
            vp = gv_ref[rs, vs]
            vpf = vp.astype(F32)
            blast = bb[GLA_SUB - 1:GLA_SUB, :]
            st = st_ref[p]
            o_inter = _dot_nt((qs * jnp.exp(bb)).astype(BF16), st.astype(BF16))
            xs = []
            for j in range(GLA_SUB):
                dlt = jnp.where(row_i >= j, bb - bb[j:j + 1, :], NEG_INF)
                xs.append(qs * jnp.exp(dlt) * kk[j:j + 1, :])
            red = _dot(jnp.concatenate(xs, axis=0).astype(BF16), hsel_ref[...])
            acc = o_inter
            for j in range(GLA_SUB):
                acc = acc + red[j * GLA_SUB:(j + 1) * GLA_SUB, :] * vpf[j:j + 1, :]
            o_s[rs, vs] = acc
            kd = (kk * jnp.exp(blast - bb)).astype(BF16)
            upd = _dot_tn(vp, kd)
            st_ref[p] = st * jnp.exp(blast) + upd * bd_ref[...]

    gr = gr_ref[...]
    for hd in range(GLA_HEADS):
        sl = slice(hd * GLA_DV, (hd + 1) * GLA_DV)
        r = gr[:, sl]
        o_ref[:, sl] = (_rms(o_s[:, sl], on_ref[...]) * (r * _sigmoid(r))).astype(BF16)


def _gla(gqk, gv, gr, ga, wa, ba, o_norm):
    s = gqk.shape[0]
    rows = GLA_ROWS
    idx = np.arange(rows)
    tcum = ((idx[:, None] >= idx[None, :]) & (idx[:, None] // GLA_SUB == idx[None, :] // GLA_SUB))
    tcum = jnp.asarray(tcum, BF16)
    r256 = np.arange(2 * GLA_DV)[:, None] // GLA_DV
    c128 = np.arange(LANES)[None, :] // GLA_DK
    bdmask = jnp.asarray(r256 == c128, F32)
    hsel = jnp.asarray((r256 == c128).T, BF16)
    row = lambda w: pl.BlockSpec((rows, w), lambda i: (i, 0))
    return pl.pallas_call(
        _gla_body,
        grid=(s // rows,),
        in_specs=[row(512), row(512), row(512), row(GLA_LOWRANK),
                  _resident(wa.shape), _resident(ba.shape), _resident(o_norm.shape),
                  _resident(tcum.shape), _resident(bdmask.shape), _resident(hsel.shape)],
        out_specs=row(512),
        out_shape=jax.ShapeDtypeStruct((s, GLA_HEADS * GLA_DV), BF16),
        scratch_shapes=[pltpu.VMEM((GLA_HEADS // 2, 2 * GLA_DV, LANES), F32),
                        pltpu.VMEM((rows, 256), F32), pltpu.VMEM((rows, 256), F32),
                        pltpu.VMEM((rows, 256), F32), pltpu.VMEM((rows, 512), F32)],
        compiler_params=_params(("arbitrary",)),
        name="gla",
    )(gqk, gv, gr, ga, wa, ba, o_norm, tcum, bdmask, hsel)


def _nsa_body(q_ref, kcmp_ref, vcmpt_ref, ksel_ref, vselt_ref, kwin_ref, vwint_ref, ngt_ref, ovlt_ref,
              o_ref, qt_s, s_s, p_s, acc_s, *, n_sel):
    qb = pl.program_id(1)
    t0 = qb * QBLK
    cols = NSA_HPG * QBLK
    ncmp = kcmp_ref.shape[0]

    for hd in range(NSA_HPG):
        qh = q_ref[:, hd * HEAD_DIM:(hd + 1) * HEAD_DIM].astype(F32)
        qt_s[0:HEAD_DIM, hd * QBLK:(hd + 1) * QBLK] = qh.T.astype(BF16)
    qt = qt_s[0:HEAD_DIM, :]

    def tq_of(rows):
        return t0 + (lax.broadcasted_iota(jnp.int32, (rows, cols), 1) & (QBLK - 1))

    gates_t = _sigmoid(ngt_ref[...])

    def gate_row(c):
        return jnp.concatenate([gates_t[3 * hd + c:3 * hd + c + 1, :] for hd in range(NSA_HPG)], axis=1)

    def win_part(start, length):
        start = pl.multiple_of(jnp.maximum(start, 0), QBLK)
        return _dot(kwin_ref[pl.ds(start, length), :], qt), vwint_ref[:, pl.ds(start, length)], start

    s_c = _dot(kcmp_ref[...], qt)
    s_old, v_old, _ = win_part(t0 - WINDOW, QBLK)
    s_mid, v_mid, mid0 = win_part(t0 - WINDOW + QBLK, WINDOW - QBLK)
    s_dg, v_dg, _ = win_part(t0, QBLK)

    n_row = lax.broadcasted_iota(jnp.int32, (ncmp, cols), 0)
    valid_c = n_row * CMP_STRIDE + (CMP_LEN - 1) <= tq_of(ncmp)
    s_c = jnp.where(valid_c, s_c, NEG_INF)
    e_c = jnp.where(valid_c, jnp.exp2(s_c - jnp.max(s_c, axis=0, keepdims=True)), 0.0)
    p_c = e_c / jnp.maximum(jnp.sum(e_c, axis=0, keepdims=True), TINY)
    out_pre = gate_row(0) * _dot(vcmpt_ref[...], p_c.astype(BF16))
    psum = p_c[:, 0:QBLK]
    for hd in range(1, NSA_HPG):
        psum = psum + p_c[:, hd * QBLK:(hd + 1) * QBLK]
    imp = jnp.zeros((LANES, QBLK), F32)
    for p_t in _split2(psum):
        imp = imp + _dot(ovlt_ref[...], p_t)

    w_row = lax.broadcasted_iota(jnp.int32, (QBLK, cols), 0)
    tq_w = tq_of(QBLK)
    kp_old = t0 - WINDOW + w_row
    valid_old = (kp_old > tq_w - WINDOW) & (kp_old >= 0)
    s_old = jnp.where(valid_old, s_old, NEG_INF)
    mid_row = mid0 + lax.broadcasted_iota(jnp.int32, (WINDOW - QBLK, cols), 0)
    s_mid = jnp.where(mid_row < t0, s_mid, NEG_INF)
    valid_dg = t0 + w_row <= tq_w
    s_dg = jnp.where(valid_dg, s_dg, NEG_INF)
    m_w = jnp.maximum(jnp.maximum(jnp.max(s_old, axis=0, keepdims=True), jnp.max(s_mid, axis=0, keepdims=True)),
                      jnp.max(s_dg, axis=0, keepdims=True))
    p_old = jnp.where(valid_old, jnp.exp2(s_old - m_w), 0.0)
    p_mid = jnp.exp2(s_mid - m_w)
    p_dg = jnp.where(valid_dg, jnp.exp2(s_dg - m_w), 0.0)
    acc_w = (_dot(v_old, p_old.astype(BF16)) + _dot(v_mid, p_mid.astype(BF16)) + _dot(v_dg, p_dg.astype(BF16)))
    out_pre = out_pre + gate_row(2) * (acc_w[0:HEAD_DIM, :] / jnp.maximum(acc_w[HEAD_DIM:HEAD_DIM + 1, :], TINY))

    tq = t0 + lax.broadcasted_iota(jnp.int32, (LANES, QBLK), 1)
    m_i = lax.broadcasted_iota(jnp.int32, (LANES, QBLK), 0)
    cur = lax.shift_right_logical(tq, 6)
    forced = (m_i == 0) | (m_i == cur) | (m_i == cur - 1)
    score = jnp.where(m_i * SEL_LEN <= tq, jnp.where(forced, FORCE_SCORE, imp), -FORCE_SCORE)
    score = jnp.where(m_i < n_sel, score, SEL_PAD_SCORE)
    m_f = m_i.astype(F32)
    bias = jnp.full((LANES, QBLK), SEL_MASK_BIAS, F32)
    for _ in range(min(SEL_TOPK, n_sel)):
        mx = jnp.max(score, axis=0, keepdims=True)
        first = jnp.min(jnp.where(score == mx, m_f, float(LANES)), axis=0, keepdims=True)
        pick = m_f == first
        bias = jnp.where(pick, 0.0, bias)
        score = jnp.where(pick, -jnp.inf, score)
    bias = bias.astype(BF16)
    for hd in range(NSA_HPG):
        qt_s[HEAD_DIM:2 * HEAD_DIM, hd * QBLK:(hd + 1) * QBLK] = bias

    qext = qt_s[...]
    k_row = lax.broadcasted_iota(jnp.int32, (SEL_KT, cols), 0)
    tq_s = tq_of(SEL_KT)
    span = SEL_SPAN_TILES * SEL_KT

    def span_scores(j):
        k0 = pl.multiple_of(j * span, span)
        return _dot(ksel_ref[pl.ds(k0, span), :], qext)

    def span_pv(j, p):
        k0 = pl.multiple_of(j * span, span)
        return _dot(vselt_ref[:, pl.ds(k0, span)], p)

    def sel_step(cur, j, m):
        nxt = 1 - cur
        pv = span_pv(jnp.maximum(j - 1, 0), p_s[nxt])
        s = s_s[cur]
        m_new = jnp.maximum(m, jnp.max(s, axis=0, keepdims=True))
        p_s[cur] = jnp.exp2(s - m_new).astype(BF16)
        acc_s[...] = jnp.exp2(m - m_new) * (acc_s[...] + pv)
        s_s[nxt] = span_scores(j + 1)
        return m_new

    def sel_body(j, m):
        return lax.cond((j & 1) == 0, lambda c: sel_step(0, j, c), lambda c: sel_step(1, j, c), m)

    n_span = t0 // span
    s_s[0] = span_scores(0)
    p_s[1] = jnp.zeros((span, cols), BF16)
    acc_s[...] = jnp.zeros_like(acc_s)
    m = lax.fori_loop(0, n_span, sel_body, jnp.full((1, cols), NEG_INF, F32))

    slot = n_span & 1
    pv_last = span_pv(jnp.maximum(n_span - 1, 0), p_s[1 - slot])

    def diag_tile(half, m, pv):
        k0 = pl.multiple_of(n_span * span + half * SEL_KT, SEL_KT)
        valid = k0 + k_row <= tq_s
        s = jnp.where(valid, s_s[slot, half * SEL_KT:(half + 1) * SEL_KT, :], NEG_INF)
        m_new = jnp.maximum(m, jnp.max(s, axis=0, keepdims=True))
        p = jnp.where(valid, jnp.exp2(s - m_new), 0.0).astype(BF16)
        acc_s[...] = jnp.exp2(m - m_new) * (acc_s[...] + pv) + _dot(vselt_ref[:, pl.ds(k0, SEL_KT)], p)
        return m_new

    m = diag_tile(0, m, pv_last)
    for half in range(1, SEL_SPAN_TILES):
        m = lax.cond(t0 >= n_span * span + half * SEL_KT,
                     lambda c, half=half: diag_tile(half, c, 0.0), lambda c: c, m)

    acc = acc_s[...]
    o_slc = acc[0:HEAD_DIM, :] / jnp.maximum(acc[HEAD_DIM:HEAD_DIM + 1, :], TINY)
    out = out_pre + gate_row(1) * o_slc
    for hd in range(NSA_HPG):
        o_ref[:, hd * HEAD_DIM:(hd + 1) * HEAD_DIM] = out[:, hd * QBLK:(hd + 1) * QBLK].T.astype(BF16)


def _nsa(q, kcmp, vcmpt, ksel, vselt, kwin, vwint, ng, overlap_t):
    s = q.shape[0]
    n_sel = s // SEL_LEN
    assert n_sel <= LANES and s % (SEL_SPAN_TILES * SEL_KT) == 0 and s >= WINDOW + QBLK
    ncmp = kcmp.shape[1]
    cols = NSA_HPG * QBLK
    grp = lambda r, w: pl.BlockSpec((None, r, w), lambda g, b: (g, 0, 0))
    return pl.pallas_call(
        functools.partial(_nsa_body, n_sel=n_sel),
        grid=(NSA_KV_HEADS, s // QBLK),
        in_specs=[
            pl.BlockSpec((QBLK, NSA_HPG * HEAD_DIM), lambda g, b: (b, g)),
            grp(ncmp, HEAD_DIM), grp(HEAD_DIM, ncmp),
            grp(s, 2 * HEAD_DIM), grp(V_ROWS, s), grp(s, HEAD_DIM), grp(V_ROWS, s),
            pl.BlockSpec((None, _NG_PAD, QBLK), lambda g, b: (g, 0, b)),
            pl.BlockSpec(overlap_t.shape, lambda g, b: (0, 0)),
        ],
        out_specs=pl.BlockSpec((QBLK, NSA_HPG * HEAD_DIM), lambda g, b: (b, g)),
        out_shape=jax.ShapeDtypeStruct((s, NSA_HEADS * HEAD_DIM), BF16),
        scratch_shapes=[pltpu.VMEM((2 * HEAD_DIM, cols), BF16),
                        pltpu.VMEM((2, SEL_SPAN_TILES * SEL_KT, cols), F32),
                        pltpu.VMEM((2, SEL_SPAN_TILES * SEL_KT, cols), BF16),
                        pltpu.VMEM((V_ROWS, cols), F32)],
        compiler_params=_params(("parallel", "arbitrary")),
        name="nsa",
    )(q, kcmp, vcmpt, ksel, vselt, kwin, vwint, ng, overlap_t)


def _memkv_body(mem_ref, g_ref, w_ref, kn_ref, k_ref, v_ref):
    kv = _dot(_rms(mem_ref[...], g_ref[...]).astype(BF16), w_ref[...])
    width = MEM_HEADS * HEAD_DIM
    for hd in range(MEM_HEADS):
        sl = slice(hd * HEAD_DIM, (hd + 1) * HEAD_DIM)
        k_ref[:, sl] = _rms(kv[:, sl], kn_ref[...]).astype(BF16)
    v_ref[...] = kv[:, width:].astype(BF16)


def _memkv(mem, in_g, w_kv, k_norm):
    m = mem.shape[0]
    shp = jax.ShapeDtypeStruct((m, MEM_HEADS * HEAD_DIM), BF16)
    return pl.pallas_call(
        _memkv_body, out_shape=[shp, shp],
        compiler_params=pltpu.CompilerParams(vmem_limit_bytes=VMEM_LIMIT),
        name="memkv",
    )(mem, in_g, w_kv, k_norm)


def _memattn_body(q_ref, k_ref, v_ref, o_ref):
    for hd in range(MEM_HEADS):
        sl = slice(hd * HEAD_DIM, (hd + 1) * HEAD_DIM)
        sc = _dot_nt(q_ref[:, sl], k_ref[:, sl])
        e = jnp.exp(sc - jnp.max(sc, axis=-1, keepdims=True))
        p = e / jnp.sum(e, axis=-1, keepdims=True)
        o_ref[:, sl] = _dot(p.astype(BF16), v_ref[:, sl]).astype(BF16)


def _memattn(q, k, v, *, tm=512):
    s, w = q.shape
    return pl.pallas_call(
        _memattn_body,
        grid=(s // tm,),
        in_specs=[pl.BlockSpec((tm, w), lambda i: (i, 0)), _resident(k.shape), _resident(v.shape)],
        out_specs=pl.BlockSpec((tm, w), lambda i: (i, 0)),
        out_shape=jax.ShapeDtypeStruct((s, w), BF16),
        compiler_params=_params(("parallel",)),
        name="memattn",
    )(q, k, v)


def _outproj_body(x_ref, a_ref, b_ref, c_ref, w_ref, o_ref):
    na, nb = a_ref.shape[1], b_ref.shape[1]
    o_ref[...] = (x_ref[...] + _dot(a_ref[...], w_ref[0:na, :]) + _dot(b_ref[...], w_ref[na:na + nb, :])
                  + _dot(c_ref[...], w_ref[na + nb:, :]))


def _outproj(x1, o_gla, o_nsa, o_mem, w_out, *, tm=512):
    s, d = x1.shape
    row = lambda w: pl.BlockSpec((tm, w), lambda i: (i, 0))
    return pl.pallas_call(
        _outproj_body,
        grid=(s // tm,),
        in_specs=[row(d), row(o_gla.shape[1]), row(o_nsa.shape[1]), row(o_mem.shape[1]),
                  _resident(w_out.shape)],
        out_specs=row(d),
        out_shape=jax.ShapeDtypeStruct((s, d), F32),
        compiler_params=_params(("parallel",)),
        name="outproj",
    )(x1, o_gla, o_nsa, o_mem, w_out)


def _split_w_in(w_in):
    d = w_in.shape[0]
    body0 = _HEAD_W
    ng0 = body0 + _BODY_W
    per_g = NSA_HPG * 3
    mq0 = ng0 + NSA_KV_HEADS * per_g
    w_head = w_in[:, :_HEAD_W].astype(BF16)
    w_body = w_in[:, body0:ng0].astype(BF16)
    w_mq = w_in[:, mq0:].astype(BF16)
    assert w_mq.shape[1] == MEM_HEADS * HEAD_DIM
    n_vs = w_in[:, body0 + _B_KS + _KV_W:body0 + _B_KS + 2 * _KV_W]
    n_vw = w_in[:, body0 + _B_KW + _KV_W:body0 + _B_KW + 2 * _KV_W]
    rows = [n_vs.T, n_vw.T]
    for g in range(NSA_KV_HEADS):
        rows += [w_in[:, ng0 + g * per_g:ng0 + (g + 1) * per_g].T, jnp.zeros((_NG_PAD - per_g, d), w_in.dtype)]
    w_t = jnp.concatenate(rows, axis=0).astype(BF16)
    return w_head, w_body, w_mq, w_t


def _layer(x, mem, positions, ffn1_norm, ffn1_w_gate, ffn1_w_up, ffn1_w_down, mix_norm, w_in,
           gla_w_a, gla_b_a, gla_o_norm, nsa_q_norm, nsa_k_norm, nsa_cmp_pos_k, nsa_cmp_w1_k,
           nsa_cmp_w2_k, nsa_cmp_pos_v, nsa_cmp_w1_v, nsa_cmp_w2_v, mem_in_norm, w_mem_kv,
           mem_q_norm, mem_k_norm, w_out, ffn2_norm, ffn2_w_gate, ffn2_w_up, ffn2_w_down, final_norm):
    s, d = x.shape
    row = lambda v: v.reshape(1, -1)
    bf = lambda v: v.astype(BF16)

    x1 = _ffn(x, row(ffn1_norm), ffn1_w_gate, ffn1_w_up, ffn1_w_down)

    half = HEAD_DIM // 2
    inv = ROPE_THETA ** (-jnp.arange(half, dtype=F32) / half)
    inv128 = jnp.concatenate([inv, inv]).reshape(1, HEAD_DIM)
    sgn128 = jnp.concatenate([-jnp.ones((half,), F32), jnp.ones((half,), F32)]).reshape(1, HEAD_DIM)
    (q, ksel, vselt, kwin, vwint, mq, gqk, gv, gr, kcvc, ga, ngt) = _proj(
        x1, row(mix_norm), *_split_w_in(w_in), positions.reshape(s, 1), inv128, sgn128,
        row(nsa_q_norm), nsa_k_norm, row(mem_q_norm))

    o_gla = _gla(gqk, gv, gr, ga, gla_w_a, row(gla_b_a), row(gla_o_norm))

    units = s // CMP_STRIDE
    n_cmp = (s - CMP_LEN) // CMP_STRIDE + 1
    cmp_last = jnp.arange(units) * CMP_STRIDE + CMP_LEN - 1
    pos_cmp = positions[jnp.minimum(cmp_last, s - 1)].reshape(units, 1)
    kcmp, vcmpt = _compress(kcvc, bf(nsa_cmp_w1_k), bf(nsa_cmp_w2_k), nsa_cmp_pos_k,
                            bf(nsa_cmp_w1_v), bf(nsa_cmp_w2_v.T), nsa_cmp_pos_v, nsa_k_norm,
                            pos_cmp, inv128, sgn128)
    n_sel = s // SEL_LEN
    cmp_start = np.arange(units) * CMP_STRIDE
    sel_start = np.arange(LANES) * SEL_LEN
    overlap = np.clip(np.minimum(cmp_start[:, None] + CMP_LEN, sel_start[None, :] + SEL_LEN)
                      - np.maximum(cmp_start[:, None], sel_start[None, :]), 0, None) / CMP_STRIDE
    overlap = overlap * (np.arange(units)[:, None] < n_cmp) * (np.arange(LANES)[None, :] < n_sel)
    o_nsa = _nsa(q, kcmp, vcmpt, ksel, vselt, kwin, vwint, ngt, jnp.asarray(overlap.T, BF16))

    kmem, vmem = _memkv(mem, row(mem_in_norm), bf(w_mem_kv), row(mem_k_norm))
    o_mem = _memattn(mq, kmem, vmem)

    x2 = _outproj(x1, o_gla, o_nsa, o_mem, bf(w_out))
    return _ffn(x2, row(ffn2_norm), ffn2_w_gate, ffn2_w_up, ffn2_w_down, row(final_norm))


def kernel(x, mem, positions, ffn1_norm, ffn1_w_gate, ffn1_w_up, ffn1_w_down, mix_norm, w_in, gla_w_a, gla_b_a, gla_o_norm, nsa_q_norm, nsa_k_norm, nsa_cmp_pos_k, nsa_cmp_w1_k, nsa_cmp_w2_k, nsa_cmp_pos_v, nsa_cmp_w1_v, nsa_cmp_w2_v, mem_in_norm, w_mem_kv, mem_q_norm, mem_k_norm, w_out, ffn2_norm, ffn2_w_gate, ffn2_w_up, ffn2_w_down, final_norm):
    depth = ffn1_norm.shape[0]
    outs = []
    for b in range(x.shape[0]):
        xb = x[b]
        for l in range(depth):
            xb = _layer(xb, mem[b], positions[b], ffn1_norm[l], ffn1_w_gate[l], ffn1_w_up[l], ffn1_w_down[l],
                        mix_norm[l], w_in[l], gla_w_a[l], gla_b_a[l], gla_o_norm[l], nsa_q_norm[l],
                        nsa_k_norm[l], nsa_cmp_pos_k[l], nsa_cmp_w1_k[l], nsa_cmp_w2_k[l], nsa_cmp_pos_v[l],
                        nsa_cmp_w1_v[l], nsa_cmp_w2_v[l], mem_in_norm[l], w_mem_kv[l], mem_q_norm[l],
                        mem_k_norm[l], w_out[l], ffn2_norm[l], ffn2_w_gate[l], ffn2_w_up[l], ffn2_w_down[l],
                        final_norm[l])
        outs.append(xb)
    return jnp.stack(outs)
```

```python
import functools

import numpy as np
import jax
import jax.numpy as jnp
from jax import lax
from jax.experimental import pallas as pl
from jax.experimental.pallas import tpu as pltpu

F32 = jnp.float32
BF16 = jnp.bfloat16

HEAD_DIM = 128
GLA_HEADS = 4
GLA_DK = 64
GLA_DV = 128
GLA_LOWRANK = 16
GLA_TAU = 16.0
NSA_HEADS = 8
NSA_KV_HEADS = 2
NSA_HPG = NSA_HEADS // NSA_KV_HEADS
CMP_LEN = 32
CMP_STRIDE = 16
SEL_LEN = 64
SEL_TOPK = 16
WINDOW = 512
MEM_HEADS = 4
MACARON_W = 0.5
QBLK = 128
ROPE_THETA = 10000.0
EPS = 1e-6
NEG_INF = -1e30
TINY = 1e-30
FORCE_SCORE = 1e4
LOG2_E = 1.4426950408889634

LANES = 128
VMEM_LIMIT = 56 * 1024 * 1024

GLA_SUB = 16
GLA_ROWS = 128
SEL_KT = 256
SEL_SPAN_TILES = 4
V_ROWS = HEAD_DIM + 16
SEL_MASK_BIAS = -32768.0
SEL_PAD_SCORE = -3e4


def _dot(a, b):
    return jnp.dot(a, b, preferred_element_type=F32)


def _dot_nt(a, b):
    return lax.dot_general(a, b, (((1,), (1,)), ((), ())), preferred_element_type=F32)


def _dot_tn(a, b):
    return lax.dot_general(a, b, (((0,), (0,)), ((), ())), preferred_element_type=F32)


def _split2(x):
    hi = x.astype(BF16)
    return hi, (x - hi.astype(F32)).astype(BF16)


def _rms(x, g):
    return x * lax.rsqrt(jnp.mean(x * x, axis=-1, keepdims=True) + EPS) * g


def _sigmoid(x):
    return 1.0 / (1.0 + jnp.exp(-x))


def _params(sem):
    return pltpu.CompilerParams(dimension_semantics=sem, vmem_limit_bytes=VMEM_LIMIT)


def _resident(shape):
    nd = len(shape)
    return pl.BlockSpec(shape, lambda *_: (0,) * nd, pipeline_mode=pl.Buffered(1))


def _ffn_body(*refs, final, nf):
    if final:
        x_ref, g_ref, wg_ref, wu_ref, wd_ref, fg_ref, o_ref, h_ref = refs
    else:
        x_ref, g_ref, wg_ref, wu_ref, wd_ref, o_ref, h_ref = refs
    f = pl.program_id(1)

    @pl.when(f == 0)
    def _():
        x = x_ref[...]
        h_ref[...] = _rms(x, g_ref[...]).astype(BF16)
        o_ref[...] = x

    h = h_ref[...]
    g = _dot(h, wg_ref[...].astype(BF16))
    u = _dot(h, wu_ref[...].astype(BF16))
    a = (g * _sigmoid(g)) * u * MACARON_W
    o_ref[...] += _dot(a.astype(BF16), wd_ref[...].astype(BF16))

    if final:
        @pl.when(f == nf - 1)
        def _():
            o_ref[...] = _rms(o_ref[...], fg_ref[...])


def _ffn(x, norm_g, wg, wu, wd, final_g=None, *, tm=1024, tf=256):
    s, d = x.shape
    ff = wg.shape[1]
    nf = ff // tf
    final = final_g is not None
    in_specs = [
        pl.BlockSpec((tm, d), lambda i, f: (i, 0)),
        pl.BlockSpec((1, d), lambda i, f: (0, 0)),
        pl.BlockSpec((d, tf), lambda i, f: (0, f)),
        pl.BlockSpec((d, tf), lambda i, f: (0, f)),
        pl.BlockSpec((tf, d), lambda i, f: (f, 0)),
    ]
    args = [x, norm_g, wg, wu, wd]
    if final:
        in_specs.append(pl.BlockSpec((1, d), lambda i, f: (0, 0)))
        args.append(final_g)
    return pl.pallas_call(
        functools.partial(_ffn_body, final=final, nf=nf),
        grid=(s // tm, nf),
        in_specs=in_specs,
        out_specs=pl.BlockSpec((tm, d), lambda i, f: (i, 0)),
        out_shape=jax.ShapeDtypeStruct((s, d), F32),
        scratch_shapes=[pltpu.VMEM((tm, d), BF16)],
        compiler_params=_params(("parallel", "arbitrary")),
        name="ffn_final" if final else "ffn",
    )(*args)


_GQK_W = 2 * GLA_HEADS * GLA_DK
_GV_W = GLA_HEADS * GLA_DV
_H_GV = _GQK_W
_H_GR = _H_GV + _GV_W
_H_GA = _H_GR + _GV_W
_HEAD_W = _H_GA + GLA_LOWRANK
_KV_W = NSA_KV_HEADS * HEAD_DIM
_B_KC = NSA_HEADS * HEAD_DIM
_B_KS = _B_KC + 2 * _KV_W
_B_KW = _B_KS + 2 * _KV_W
_BODY_W = _B_KW + 2 * _KV_W
_NG_PAD = 16
_T_NG = 2 * _KV_W


def _rope(x, cos, sin_signed):
    return x * cos + pltpu.roll(x, HEAD_DIM // 2, 1) * sin_signed


def _proj_body(x_ref, g_ref, wh_ref, wb_ref, wmq_ref, wt_ref, pos_ref, inv_ref, sgn_ref, qn_ref, kn_ref,
               mqn_ref, q_ref, ksel_ref, vselt_ref, kwin_ref, vwint_ref, mq_ref, gqk_ref, gv_ref, gr_ref,
               kcvc_ref, ga_ref, ngt_ref, *, tm):
    i = pl.program_id(0)
    h = _rms(x_ref[...], g_ref[...]).astype(BF16)

    def proj(w_ref, c0, width):
        return _dot(h, w_ref[:, c0:c0 + width])

    ang = pos_ref[...].astype(F32) * inv_ref[...]
    cos = jnp.cos(ang)
    sin_signed = jnp.sin(ang) * sgn_ref[...]

    def norm_rope(t, gain):
        return _rope(_rms(t, gain), cos, sin_signed)

    scale = HEAD_DIM ** -0.5
    nq = proj(wb_ref, 0, NSA_HEADS * HEAD_DIM)
    for hd in range(NSA_HEADS):
        sl = slice(hd * HEAD_DIM, (hd + 1) * HEAD_DIM)
        q_ref[:, sl] = (norm_rope(nq[:, sl], qn_ref[...]) * (scale * LOG2_E)).astype(BF16)

    tok = i * tm + lax.broadcasted_iota(jnp.int32, (tm, LANES), 0)
    blk = lax.broadcasted_iota(jnp.int32, (tm, LANES), 1)
    onehot = jnp.where(lax.shift_right_logical(tok, 6) == blk, 1.0, 0.0).astype(BF16)
    ks = proj(wb_ref, _B_KS, _KV_W)
    kw = proj(wb_ref, _B_KW, _KV_W)
    vt = _dot_nt(wt_ref[...], h)
    ones_rows = jnp.where(lax.broadcasted_iota(jnp.int32, (V_ROWS - HEAD_DIM, tm), 0) == 0, 1.0, 0.0).astype(BF16)
    for g in range(NSA_KV_HEADS):
        sl = slice(g * HEAD_DIM, (g + 1) * HEAD_DIM)
        slw = slice(_KV_W + g * HEAD_DIM, _KV_W + (g + 1) * HEAD_DIM)
        ksel_ref[g, :, 0:HEAD_DIM] = norm_rope(ks[:, sl], kn_ref[1:2, :]).astype(BF16)
        ksel_ref[g, :, HEAD_DIM:2 * HEAD_DIM] = onehot
        kwin_ref[g] = norm_rope(kw[:, sl], kn_ref[2:3, :]).astype(BF16)
        vselt_ref[g, 0:HEAD_DIM, :] = vt[sl, :].astype(BF16)
        vwint_ref[g, 0:HEAD_DIM, :] = vt[slw, :].astype(BF16)
        vselt_ref[g, HEAD_DIM:V_ROWS, :] = ones_rows
        vwint_ref[g, HEAD_DIM:V_ROWS, :] = ones_rows
        ngt_ref[g] = vt[_T_NG + g * _NG_PAD:_T_NG + (g + 1) * _NG_PAD, :]

    mq = _dot(h, wmq_ref[...])
    for hd in range(MEM_HEADS):
        sl = slice(hd * HEAD_DIM, (hd + 1) * HEAD_DIM)
        mq_ref[:, sl] = (_rms(mq[:, sl], mqn_ref[...]) * scale).astype(BF16)

    gqk_ref[...] = proj(wh_ref, 0, _GQK_W)
    gv_ref[...] = proj(wh_ref, _H_GV, _GV_W).astype(BF16)
    gr_ref[...] = proj(wh_ref, _H_GR, _GV_W)
    ga_ref[...] = proj(wh_ref, _H_GA, GLA_LOWRANK)
    kcvc = proj(wb_ref, _B_KC, 2 * _KV_W)
    for j in range(2 * NSA_KV_HEADS):
        kcvc_ref[j] = kcvc[:, j * HEAD_DIM:(j + 1) * HEAD_DIM]


def _proj(x1, mix_g, w_head, w_body, w_mq, w_t, pos_col, inv128, sgn128, q_norm, k_norm, mq_norm, *, tm=512):
    s, d = x1.shape
    row = lambda w: pl.BlockSpec((tm, w), lambda i: (i, 0))
    grp = lambda w: pl.BlockSpec((NSA_KV_HEADS, tm, w), lambda i: (0, i, 0))
    grpt = pl.BlockSpec((NSA_KV_HEADS, V_ROWS, tm), lambda i: (0, 0, i))
    vt_shape = jax.ShapeDtypeStruct((NSA_KV_HEADS, V_ROWS, s), BF16)
    out_shapes = [
        (jax.ShapeDtypeStruct((s, NSA_HEADS * HEAD_DIM), BF16), row(NSA_HEADS * HEAD_DIM)),
        (jax.ShapeDtypeStruct((NSA_KV_HEADS, s, 2 * HEAD_DIM), BF16), grp(2 * HEAD_DIM)),
        (vt_shape, grpt),
        (jax.ShapeDtypeStruct((NSA_KV_HEADS, s, HEAD_DIM), BF16), grp(HEAD_DIM)),
        (vt_shape, grpt),
        (jax.ShapeDtypeStruct((s, MEM_HEADS * HEAD_DIM), BF16), row(MEM_HEADS * HEAD_DIM)),
        (jax.ShapeDtypeStruct((s, 512), F32), row(512)),
        (jax.ShapeDtypeStruct((s, 512), BF16), row(512)),
        (jax.ShapeDtypeStruct((s, 512), F32), row(512)),
        (jax.ShapeDtypeStruct((2 * NSA_KV_HEADS, s, HEAD_DIM), F32),
         pl.BlockSpec((2 * NSA_KV_HEADS, tm, HEAD_DIM), lambda i: (0, i, 0))),
        (jax.ShapeDtypeStruct((s, GLA_LOWRANK), F32), row(GLA_LOWRANK)),
        (jax.ShapeDtypeStruct((NSA_KV_HEADS, _NG_PAD, s), F32),
         pl.BlockSpec((NSA_KV_HEADS, _NG_PAD, tm), lambda i: (0, 0, i))),
    ]
    return pl.pallas_call(
        functools.partial(_proj_body, tm=tm),
        grid=(s // tm,),
        in_specs=[
            row(d),
            _resident((1, d)),
            _resident(w_head.shape), _resident(w_body.shape), _resident(w_mq.shape), _resident(w_t.shape),
            pl.BlockSpec((tm, 1), lambda i: (i, 0)),
            _resident((1, LANES)), _resident((1, LANES)),
            _resident((1, HEAD_DIM)), _resident((3, HEAD_DIM)), _resident((1, HEAD_DIM)),
        ],
        out_specs=[o[1] for o in out_shapes],
        out_shape=[o[0] for o in out_shapes],
        compiler_params=_params(("parallel",)),
        name="proj",
    )(x1, mix_g, w_head, w_body, w_mq, w_t, pos_col, inv128, sgn128, q_norm, k_norm, mq_norm)


def _compress_body(kcvc_ref, w1k_ref, w2k_ref, pek_ref, w1v_ref, w2v_ref, pev_ref, kn_ref,
                   pos_ref, inv_ref, sgn_ref, kcmp_ref, vcmp_ref, *, units):
    half = CMP_LEN // 2
    ang = pos_ref[...].astype(F32) * inv_ref[...]
    cos = jnp.cos(ang)
    sin_signed = jnp.sin(ang) * sgn_ref[...]
    for kind, (w1_ref, w2_ref, pe_ref) in enumerate(((w1k_ref, w2k_ref, pek_ref),
                                                     (w1v_ref, w2v_ref, pev_ref))):
        for g in range(NSA_KV_HEADS):
            slab = kind * NSA_KV_HEADS + g
            a = jnp.zeros((units, w1_ref.shape[1]), F32)
            b = jnp.zeros((units, w1_ref.shape[1]), F32)
            for l in range(half):
                t = kcvc_ref[slab, pl.ds(l, units, stride=CMP_STRIDE), :]
                a = a + _dot((t + pe_ref[l:l + 1, :]).astype(BF16),
                             w1_ref[l * HEAD_DIM:(l + 1) * HEAD_DIM, :])
                b = b + _dot((t + pe_ref[half + l:half + l + 1, :]).astype(BF16),
                             w1_ref[(half + l) * HEAD_DIM:(half + l + 1) * HEAD_DIM, :])
            hid = a + pltpu.roll(b, units - 1, 0)
            act = (hid * _sigmoid(hid)).astype(BF16)
            if kind == 0:
                c = _rope(_rms(_dot(act, w2_ref[...]), kn_ref[0:1, :]), cos, sin_signed)
                kcmp_ref[g] = c.astype(BF16)
            else:
                vcmp_ref[g] = _dot_nt(w2_ref[...], act).astype(BF16)


def _compress(kcvc, w1k, w2k, pek, w1v, w2v, pev, k_norm, pos_cmp, inv128, sgn128):
    s = kcvc.shape[1]
    units = s // CMP_STRIDE
    shp = jax.ShapeDtypeStruct((NSA_KV_HEADS, units, HEAD_DIM), BF16)
    shp_t = jax.ShapeDtypeStruct((NSA_KV_HEADS, HEAD_DIM, units), BF16)
    return pl.pallas_call(
        functools.partial(_compress_body, units=units),
        out_shape=[shp, shp_t],
        compiler_params=pltpu.CompilerParams(vmem_limit_bytes=VMEM_LIMIT),
        name="compress",
    )(kcvc, w1k, w2k, pek, w1v, w2v, pev, k_norm, pos_cmp, inv128, sgn128)


def _gla_body(gqk_ref, gv_ref, gr_ref, ga_ref, wa_ref, ba_ref, on_ref, tcum_ref, bd_ref, hsel_ref,
              o_ref, st_ref, q_s, k_s, b_s, o_s):
    rows = GLA_ROWS
    npair = GLA_HEADS // 2

    @pl.when(pl.program_id(0) == 0)
    def _():
        st_ref[...] = jnp.zeros_like(st_ref)

    z = ba_ref[...]
    for ga_t in _split2(ga_ref[...]):
        for wa_t in _split2(wa_ref[...]):
            z = z + _dot(ga_t, wa_t)
    la = (jnp.minimum(z, 0.0) - jnp.log(1.0 + jnp.exp(-jnp.abs(z)))) / GLA_TAU
    tc = tcum_ref[...]
    bcum = jnp.zeros_like(la)
    for la_t in _split2(la):
        bcum = bcum + _dot(tc, la_t)
    b_s[...] = bcum
    q_s[...] = gqk_ref[:, 0:256] * (GLA_DK ** -0.5)
    k_s[...] = gqk_ref[:, 256:512]

    row_i = lax.broadcasted_iota(jnp.int32, (GLA_SUB, LANES), 0)

    for sb in range(rows // GLA_SUB):
        rs = slice(sb * GLA_SUB, (sb + 1) * GLA_SUB)
        for p in range(npair):
            cs = slice(p * LANES, (p + 1) * LANES)
            vs = slice(p * 2 * GLA_DV, (p + 1) * 2 * GLA_DV)
            qs = q_s[rs, cs]
            kk = k_s[rs, cs]
            bb = b_s[rs, cs]
            vp = gv_ref[rs, vs]
            vpf = vp.astype(F32)
            blast = bb[GLA_SUB - 1:GLA_SUB, :]
            st = st_ref[p]
            o_inter = _dot_nt((qs * jnp.exp(bb)).astype(BF16), st.astype(BF16))
            xs = []
            for j in range(GLA_SUB):
                dlt = jnp.where(row_i >= j, bb - bb[j:j + 1, :], NEG_INF)
                xs.append(qs * jnp.exp(dlt) * kk[j:j + 1, :])
            red = _dot(jnp.concatenate(xs, axis=0).astype(BF16), hsel_ref[...])
            acc = o_inter
            for j in range(GLA_SUB):
                acc = acc + red[j * GLA_SUB:(j + 1) * GLA_SUB, :] * vpf[j:j + 1, :]
            o_s[rs, vs] = acc
            kd = (kk * jnp.exp(blast - bb)).astype(BF16)
            upd = _dot_tn(vp, kd)
            st_ref[p] = st * jnp.exp(blast) + upd * bd_ref[...]

    gr = gr_ref[...]
    for hd in range(GLA_HEADS):
        sl = slice(hd * GLA_DV, (hd + 1) * GLA_DV)
        r = gr[:, sl]
        o_ref[:, sl] = (_rms(o_s[:, sl], on_ref[...]) * (r * _sigmoid(r))).astype(BF16)


def _gla(gqk, gv, gr, ga, wa, ba, o_norm):
    s = gqk.shape[0]
    rows = GLA_ROWS
    idx = np.arange(rows)
    tcum = ((idx[:, None] >= idx[None, :]) & (idx[:, None] // GLA_SUB == idx[None, :] // GLA_SUB))
    tcum = jnp.asarray(tcum, BF16)
    r256 = np.arange(2 * GLA_DV)[:, None] // GLA_DV
    c128 = np.arange(LANES)[None, :] // GLA_DK
    bdmask = jnp.asarray(r256 == c128, F32)
    hsel = jnp.asarray((r256 == c128).T, BF16)
    row = lambda w: pl.BlockSpec((rows, w), lambda i: (i, 0))
    return pl.pallas_call(
        _gla_body,
        grid=(s // rows,),
        in_specs=[row(512), row(512), row(512), row(GLA_LOWRANK),
                  _resident(wa.shape), _resident(ba.shape), _resident(o_norm.shape),
                  _resident(tcum.shape), _resident(bdmask.shape), _resident(hsel.shape)],
        out_specs=row(512),
        out_shape=jax.ShapeDtypeStruct((s, GLA_HEADS * GLA_DV), BF16),
        scratch_shapes=[pltpu.VMEM((GLA_HEADS // 2, 2 * GLA_DV, LANES), F32),
                        pltpu.VMEM((rows, 256), F32), pltpu.VMEM((rows, 256), F32),
                        pltpu.VMEM((rows, 256), F32), pltpu.VMEM((rows, 512), F32)],
        compiler_params=_params(("arbitrary",)),
        name="gla",
    )(gqk, gv, gr, ga, wa, ba, o_norm, tcum, bdmask, hsel)


def _nsa_body(q_ref, kcmp_ref, vcmpt_ref, ksel_ref, vselt_ref, kwin_ref, vwint_ref, ngt_ref, ovlt_ref,
              o_ref, qt_s, s_s, p_s, acc_s, *, n_sel):
    qb = pl.program_id(0)
    t0 = qb * QBLK
    cols = NSA_HPG * QBLK
    gw = NSA_HPG * HEAD_DIM
    ncmp = kcmp_ref.shape[1]
    groups = range(NSA_KV_HEADS)
    span = SEL_SPAN_TILES * SEL_KT

    def tq_of(rows):
        return t0 + (lax.broadcasted_iota(jnp.int32, (rows, cols), 1) & (QBLK - 1))

    def gate_row(gates_t, c):
        return jnp.concatenate([gates_t[3 * hd + c:3 * hd + c + 1, :] for hd in range(NSA_HPG)], axis=1)

    def span_scores(g, j):
        k0 = pl.multiple_of(j * span, span)
        return _dot(ksel_ref[g, pl.ds(k0, span), :], qt_s[g])

    def span_pv(g, j, p):
        k0 = pl.multiple_of(j * span, span)
        return _dot(vselt_ref[g, :, pl.ds(k0, span)], p)

    def prologue(g):
        for hd in range(NSA_HPG):
            qh = q_ref[:, g * gw + hd * HEAD_DIM:g * gw + (hd + 1) * HEAD_DIM].astype(F32)
            qt_s[g, 0:HEAD_DIM, hd * QBLK:(hd + 1) * QBLK] = qh.T.astype(BF16)
        qt = qt_s[g, 0:HEAD_DIM, :]
        gates_t = _sigmoid(ngt_ref[g])

        def win_part(start, length):
            start = pl.multiple_of(jnp.maximum(start, 0), QBLK)
            return (_dot(kwin_ref[g, pl.ds(start, length), :], qt), vwint_ref[g, :, pl.ds(start, length)], start)

        s_c = _dot(kcmp_ref[g], qt)
        s_old, v_old, _ = win_part(t0 - WINDOW, QBLK)
        s_mid, v_mid, mid0 = win_part(t0 - WINDOW + QBLK, WINDOW - QBLK)
        s_dg, v_dg, _ = win_part(t0, QBLK)

        n_row = lax.broadcasted_iota(jnp.int32, (ncmp, cols), 0)
        valid_c = n_row * CMP_STRIDE + (CMP_LEN - 1) <= tq_of(ncmp)
        s_c = jnp.where(valid_c, s_c, NEG_INF)
        e_c = jnp.where(valid_c, jnp.exp2(s_c - jnp.max(s_c, axis=0, keepdims=True)), 0.0)
        p_c = e_c / jnp.maximum(jnp.sum(e_c, axis=0, keepdims=True), TINY)
        out_pre = gate_row(gates_t, 0) * _dot(vcmpt_ref[g], p_c.astype(BF16))
        psum = p_c[:, 0:QBLK]
        for hd in range(1, NSA_HPG):
            psum = psum + p_c[:, hd * QBLK:(hd + 1) * QBLK]
        imp = jnp.zeros((LANES, QBLK), F32)
        for p_t in _split2(psum):
            imp = imp + _dot(ovlt_ref[...], p_t)

        tq = t0 + lax.broadcasted_iota(jnp.int32, (LANES, QBLK), 1)
        m_i = lax.broadcasted_iota(jnp.int32, (LANES, QBLK), 0)
        cur = lax.shift_right_logical(tq, 6)
        forced = (m_i == 0) | (m_i == cur) | (m_i == cur - 1)
        score = jnp.where(m_i * SEL_LEN <= tq, jnp.where(forced, FORCE_SCORE, imp), -FORCE_SCORE)
        score = jnp.where(m_i < n_sel, score, SEL_PAD_SCORE)
        m_f = m_i.astype(F32)
        bias = jnp.full((LANES, QBLK), SEL_MASK_BIAS, F32)
        for _ in range(min(SEL_TOPK, n_sel)):
            mx = jnp.max(score, axis=0, keepdims=True)
            first = jnp.min(jnp.where(score == mx, m_f, float(LANES)), axis=0, keepdims=True)
            pick = m_f == first
            bias = jnp.where(pick, 0.0, bias)
            score = jnp.where(pick, -jnp.inf, score)
        bias = bias.astype(BF16)
        for hd in range(NSA_HPG):
            qt_s[g, HEAD_DIM:2 * HEAD_DIM, hd * QBLK:(hd + 1) * QBLK] = bias

        w_row = lax.broadcasted_iota(jnp.int32, (QBLK, cols), 0)
        tq_w = tq_of(QBLK)
        kp_old = t0 - WINDOW + w_row
        valid_old = (kp_old > tq_w - WINDOW) & (kp_old >= 0)
        s_old = jnp.where(valid_old, s_old, NEG_INF)
        mid_row = mid0 + lax.broadcasted_iota(jnp.int32, (WINDOW - QBLK, cols), 0)
        s_mid = jnp.where(mid_row < t0, s_mid, NEG_INF)
        valid_dg = t0 + w_row <= tq_w
        s_dg = jnp.where(valid_dg, s_dg, NEG_INF)
        m_w = jnp.maximum(jnp.maximum(jnp.max(s_old, axis=0, keepdims=True),
                                      jnp.max(s_mid, axis=0, keepdims=True)),
                          jnp.max(s_dg, axis=0, keepdims=True))
        p_old = jnp.where(valid_old, jnp.exp2(s_old - m_w), 0.0)
        p_mid = jnp.exp2(s_mid - m_w)
        p_dg = jnp.where(valid_dg, jnp.exp2(s_dg - m_w), 0.0)
        acc_w = (_dot(v_old, p_old.astype(BF16)) + _dot(v_mid, p_mid.astype(BF16))
                 + _dot(v_dg, p_dg.astype(BF16)))
        out_pre = out_pre + gate_row(gates_t, 2) * (acc_w[0:HEAD_DIM, :]
                                                    / jnp.maximum(acc_w[HEAD_DIM:HEAD_DIM + 1, :], TINY))

        s_s[g, 0] = span_scores(g, 0)
        p_s[g, 1] = jnp.zeros((span, cols), BF16)
        acc_s[g] = jnp.zeros((V_ROWS, cols), F32)
        return out_pre, gate_row(gates_t, 1)

    pre = [prologue(g) for g in groups]

    def sel_step(cur, j, ms):
        nxt = 1 - cur
        out = []
        for g in groups:
            pv = span_pv(g, jnp.maximum(j - 1, 0), p_s[g, nxt])
            s = s_s[g, cur]
            m_new = jnp.maximum(ms[g], jnp.max(s, axis=0, keepdims=True))
            p_s[g, cur] = jnp.exp2(s - m_new).astype(BF16)
            acc_s[g] = jnp.exp2(ms[g] - m_new) * (acc_s[g] + pv)
            out.append(m_new)
        for g in groups:
            s_s[g, nxt] = span_scores(g, j + 1)
        return tuple(out)

    def sel_body(j, ms):
        return lax.cond((j & 1) == 0, lambda c: sel_step(0, j, c), lambda c: sel_step(1, j, c), ms)

    n_span = t0 // span
    ms = lax.fori_loop(0, n_span, sel_body, tuple(jnp.full((1, cols), NEG_INF, F32) for _ in groups))

    slot = n_span & 1
    k_row = lax.broadcasted_iota(jnp.int32, (SEL_KT, cols), 0)
    tq_s = tq_of(SEL_KT)

    def diag_tile(half, ms, pvs):
        k0 = pl.multiple_of(n_span * span + half * SEL_KT, SEL_KT)
        valid = k0 + k_row <= tq_s
        out = []
        for g in groups:
            s = jnp.where(valid, s_s[g, slot, half * SEL_KT:(half + 1) * SEL_KT, :], NEG_INF)
            m_new = jnp.maximum(ms[g], jnp.max(s, axis=0, keepdims=True))
            p = jnp.where(valid, jnp.exp2(s - m_new), 0.0).astype(BF16)
            acc_s[g] = (jnp.exp2(ms[g] - m_new) * (acc_s[g] + pvs[g])
                        + _dot(vselt_ref[g, :, pl.ds(k0, SEL_KT)], p))
            out.append(m_new)
        return tuple(out)

    ms = diag_tile(0, ms, [span_pv(g, jnp.maximum(n_span - 1, 0), p_s[g, 1 - slot]) for g in groups])
    for half in range(1, SEL_SPAN_TILES):
        ms = lax.cond(t0 >= n_span * span + half * SEL_KT,
                      lambda c, half=half: diag_tile(half, c, [0.0 for _ in groups]), lambda c: c, ms)

    for g in groups:
        out_pre, gate_sel = pre[g]
        acc = acc_s[g]
        o_slc = acc[0:HEAD_DIM, :] / jnp.maximum(acc[HEAD_DIM:HEAD_DIM + 1, :], TINY)
        out = out_pre + gate_sel * o_slc
        for hd in range(NSA_HPG):
            o_ref[:, g * gw + hd * HEAD_DIM:g * gw + (hd + 1) * HEAD_DIM] = (
                out[:, hd * QBLK:(hd + 1) * QBLK].T.astype(BF16))


def _nsa(q, kcmp, vcmpt, ksel, vselt, kwin, vwint, ngt, overlap_t):
    s = q.shape[0]
    n_sel = s // SEL_LEN
    span = SEL_SPAN_TILES * SEL_KT
    assert n_sel <= LANES and s % span == 0 and s >= WINDOW + QBLK
    cols = NSA_HPG * QBLK
    ng = NSA_KV_HEADS
    return pl.pallas_call(
        functools.partial(_nsa_body, n_sel=n_sel),
        grid=(s // QBLK,),
        in_specs=[
            pl.BlockSpec((QBLK, NSA_HEADS * HEAD_DIM), lambda b: (b, 0)),
            _resident(kcmp.shape), _resident(vcmpt.shape),
            _resident(ksel.shape), _resident(vselt.shape), _resident(kwin.shape), _resident(vwint.shape),
            pl.BlockSpec((ng, _NG_PAD, QBLK), lambda b: (0, 0, b)),
            _resident(overlap_t.shape),
        ],
        out_specs=pl.BlockSpec((QBLK, NSA_HEADS * HEAD_DIM), lambda b: (b, 0)),
        out_shape=jax.ShapeDtypeStruct((s, NSA_HEADS * HEAD_DIM), BF16),
        scratch_shapes=[pltpu.VMEM((ng, 2 * HEAD_DIM, cols), BF16),
                        pltpu.VMEM((ng, 2, span, cols), F32),
                        pltpu.VMEM((ng, 2, span, cols), BF16),
                        pltpu.VMEM((ng, V_ROWS, cols), F32)],
        compiler_params=_params(("arbitrary",)),
        name="nsa",
    )(q, kcmp, vcmpt, ksel, vselt, kwin, vwint, ngt, overlap_t)


def _memkv_body(mem_ref, g_ref, w_ref, kn_ref, k_ref, v_ref):
    kv = _dot(_rms(mem_ref[...], g_ref[...]).astype(BF16), w_ref[...])
    width = MEM_HEADS * HEAD_DIM
    for hd in range(MEM_HEADS):
        sl = slice(hd * HEAD_DIM, (hd + 1) * HEAD_DIM)
        k_ref[:, sl] = _rms(kv[:, sl], kn_ref[...]).astype(BF16)
    v_ref[...] = kv[:, width:].astype(BF16)


def _memkv(mem, in_g, w_kv, k_norm):
    m = mem.shape[0]
    shp = jax.ShapeDtypeStruct((m, MEM_HEADS * HEAD_DIM), BF16)
    return pl.pallas_call(
        _memkv_body, out_shape=[shp, shp],
        compiler_params=pltpu.CompilerParams(vmem_limit_bytes=VMEM_LIMIT),
        name="memkv",
    )(mem, in_g, w_kv, k_norm)


def _memattn_body(q_ref, k_ref, v_ref, o_ref):
    for hd in range(MEM_HEADS):
        sl = slice(hd * HEAD_DIM, (hd + 1) * HEAD_DIM)
        sc = _dot_nt(q_ref[:, sl], k_ref[:, sl])
        e = jnp.exp(sc - jnp.max(sc, axis=-1, keepdims=True))
        p = e / jnp.sum(e, axis=-1, keepdims=True)
        o_ref[:, sl] = _dot(p.astype(BF16), v_ref[:, sl]).astype(BF16)


def _memattn(q, k, v, *, tm=512):
    s, w = q.shape
    return pl.pallas_call(
        _memattn_body,
        grid=(s // tm,),
        in_specs=[pl.BlockSpec((tm, w), lambda i: (i, 0)), _resident(k.shape), _resident(v.shape)],
        out_specs=pl.BlockSpec((tm, w), lambda i: (i, 0)),
        out_shape=jax.ShapeDtypeStruct((s, w), BF16),
        compiler_params=_params(("parallel",)),
        name="memattn",
    )(q, k, v)


def _outproj_body(x_ref, a_ref, b_ref, c_ref, w_ref, o_ref):
    na, nb = a_ref.shape[1], b_ref.shape[1]
    o_ref[...] = (x_ref[...] + _dot(a_ref[...], w_ref[0:na, :]) + _dot(b_ref[...], w_ref[na:na + nb, :])
                  + _dot(c_ref[...], w_ref[na + nb:, :]))


def _outproj(x1, o_gla, o_nsa, o_mem, w_out, *, tm=512):
    s, d = x1.shape
    row = lambda w: pl.BlockSpec((tm, w), lambda i: (i, 0))
    return pl.pallas_call(
        _outproj_body,
        grid=(s // tm,),
        in_specs=[row(d), row(o_gla.shape[1]), row(o_nsa.shape[1]), row(o_mem.shape[1]),
                  _resident(w_out.shape)],
        out_specs=row(d),
        out_shape=jax.ShapeDtypeStruct((s, d), F32),
        compiler_params=_params(("parallel",)),
        name="outproj",
    )(x1, o_gla, o_nsa, o_mem, w_out)


def _split_w_in(w_in):
    d = w_in.shape[0]
    body0 = _HEAD_W
    ng0 = body0 + _BODY_W
    per_g = NSA_HPG * 3
    mq0 = ng0 + NSA_KV_HEADS * per_g
    w_head = w_in[:, :_HEAD_W].astype(BF16)
    w_body = w_in[:, body0:ng0].astype(BF16)
    w_mq = w_in[:, mq0:].astype(BF16)
    assert w_mq.shape[1] == MEM_HEADS * HEAD_DIM
    n_vs = w_in[:, body0 + _B_KS + _KV_W:body0 + _B_KS + 2 * _KV_W]
    n_vw = w_in[:, body0 + _B_KW + _KV_W:body0 + _B_KW + 2 * _KV_W]
    rows = [n_vs.T, n_vw.T]
    for g in range(NSA_KV_HEADS):
        rows += [w_in[:, ng0 + g * per_g:ng0 + (g + 1) * per_g].T, jnp.zeros((_NG_PAD - per_g, d), w_in.dtype)]
    w_t = jnp.concatenate(rows, axis=0).astype(BF16)
    return w_head, w_body, w_mq, w_t


def _layer(x, mem, positions, ffn1_norm, ffn1_w_gate, ffn1_w_up, ffn1_w_down, mix_norm, w_in,
           gla_w_a, gla_b_a, gla_o_norm, nsa_q_norm, nsa_k_norm, nsa_cmp_pos_k, nsa_cmp_w1_k,
           nsa_cmp_w2_k, nsa_cmp_pos_v, nsa_cmp_w1_v, nsa_cmp_w2_v, mem_in_norm, w_mem_kv,
           mem_q_norm, mem_k_norm, w_out, ffn2_norm, ffn2_w_gate, ffn2_w_up, ffn2_w_down, final_norm):
    s, d = x.shape
    row = lambda v: v.reshape(1, -1)
    bf = lambda v: v.astype(BF16)

    x1 = _ffn(x, row(ffn1_norm), ffn1_w_gate, ffn1_w_up, ffn1_w_down)

    half = HEAD_DIM // 2
    inv = ROPE_THETA ** (-jnp.arange(half, dtype=F32) / half)
    inv128 = jnp.concatenate([inv, inv]).reshape(1, HEAD_DIM)
    sgn128 = jnp.concatenate([-jnp.ones((half,), F32), jnp.ones((half,), F32)]).reshape(1, HEAD_DIM)
    (q, ksel, vselt, kwin, vwint, mq, gqk, gv, gr, kcvc, ga, ngt) = _proj(
        x1, row(mix_norm), *_split_w_in(w_in), positions.reshape(s, 1), inv128, sgn128,
        row(nsa_q_norm), nsa_k_norm, row(mem_q_norm))

    o_gla = _gla(gqk, gv, gr, ga, gla_w_a, row(gla_b_a), row(gla_o_norm))

    units = s // CMP_STRIDE
    n_cmp = (s - CMP_LEN) // CMP_STRIDE + 1
    cmp_last = jnp.arange(units) * CMP_STRIDE + CMP_LEN - 1
    pos_cmp = positions[jnp.minimum(cmp_last, s - 1)].reshape(units, 1)
    kcmp, vcmpt = _compress(kcvc, bf(nsa_cmp_w1_k), bf(nsa_cmp_w2_k), nsa_cmp_pos_k,
                            bf(nsa_cmp_w1_v), bf(nsa_cmp_w2_v.T), nsa_cmp_pos_v, nsa_k_norm,
                            pos_cmp, inv128, sgn128)
    n_sel = s // SEL_LEN
    cmp_start = np.arange(units) * CMP_STRIDE
    sel_start = np.arange(LANES) * SEL_LEN
    overlap = np.clip(np.minimum(cmp_start[:, None] + CMP_LEN, sel_start[None, :] + SEL_LEN)
                      - np.maximum(cmp_start[:, None], sel_start[None, :]), 0, None) / CMP_STRIDE
    overlap = overlap * (np.arange(units)[:, None] < n_cmp) * (np.arange(LANES)[None, :] < n_sel)
    o_nsa = _nsa(q, kcmp, vcmpt, ksel, vselt, kwin, vwint, ngt, jnp.asarray(overlap.T, BF16))

    kmem, vmem = _memkv(mem, row(mem_in_norm), bf(w_mem_kv), row(mem_k_norm))
    o_mem = _memattn(mq, kmem, vmem)

    x2 = _outproj(x1, o_gla, o_nsa, o_mem, bf(w_out))
    return _ffn(x2, row(ffn2_norm), ffn2_w_gate, ffn2_w_up, ffn2_w_down, row(final_norm))


def kernel(x, mem, positions, ffn1_norm, ffn1_w_gate, ffn1_w_up, ffn1_w_down, mix_norm, w_in, gla_w_a, gla_b_a, gla_o_norm, nsa_q_norm, nsa_k_norm, nsa_cmp_pos_k, nsa_cmp_w1_k, nsa_cmp_w2_k, nsa_cmp_pos_v, nsa_cmp_w1_v, nsa_cmp_w2_v, mem_in_norm, w_mem_kv, mem_q_norm, mem_k_norm, w_out, ffn2_norm, ffn2_w_gate, ffn2_w_up, ffn2_w_down, final_norm):
    depth = ffn1_norm.shape[0]
    batch, s, d = x.shape
    outs = []
    for b in range(batch):
        xb, mem_b, pos_b = (x.reshape(s, d), mem.reshape(mem.shape[1:]), positions.reshape(s)) if batch == 1 \
            else (x[b], mem[b], positions[b])
        for l in range(depth):
            xb = _layer(xb, mem_b, pos_b, ffn1_norm[l], ffn1_w_gate[l], ffn1_w_up[l], ffn1_w_down[l],
                        mix_norm[l], w_in[l], gla_w_a[l], gla_b_a[l], gla_o_norm[l], nsa_q_norm[l],
                        nsa_k_norm[l], nsa_cmp_pos_k[l], nsa_cmp_w1_k[l], nsa_cmp_w2_k[l], nsa_cmp_pos_v[l],
                        nsa_cmp_w1_v[l], nsa_cmp_w2_v[l], mem_in_norm[l], w_mem_kv[l], mem_q_norm[l],
                        mem_k_norm[l], w_out[l], ffn2_norm[l], ffn2_w_gate[l], ffn2_w_up[l], ffn2_w_down[l],
                        final_norm[l])
        outs.append(xb)
    return outs[0].reshape(1, s, d) if batch == 1 else jnp.stack(outs)
```

```python
import functools

import numpy as np
import jax
import jax.numpy as jnp
from jax import lax
from jax.experimental import pallas as pl
from jax.experimental.pallas import tpu as pltpu

F32 = jnp.float32
BF16 = jnp.bfloat16

HEAD_DIM = 128
GLA_HEADS = 4
GLA_DK = 64
GLA_DV = 128
GLA_LOWRANK = 16
GLA_TAU = 16.0
NSA_HEADS = 8
NSA_KV_HEADS = 2
NSA_HPG = NSA_HEADS // NSA_KV_HEADS
CMP_LEN = 32
CMP_STRIDE = 16
SEL_LEN = 64
SEL_TOPK = 16
WINDOW = 512
MEM_HEADS = 4
MACARON_W = 0.5
QBLK = 128
ROPE_THETA = 10000.0
EPS = 1e-6
NEG_INF = -1e30
TINY = 1e-30
FORCE_SCORE = 1e4
LOG2_E = 1.4426950408889634

LANES = 128
VMEM_LIMIT = 56 * 1024 * 1024

GLA_SUB = 16
GLA_ROWS = 128
SEL_KT = 256
SEL_SPAN_TILES = 4
V_ROWS = HEAD_DIM + 16
SEL_MASK_BIAS = -32768.0
SEL_PAD_SCORE = -3e4


def _dot(a, b):
    return jnp.dot(a, b, preferred_element_type=F32)


def _dot_nt(a, b):
    return lax.dot_general(a, b, (((1,), (1,)), ((), ())), preferred_element_type=F32)


def _dot_tn(a, b):
    return lax.dot_general(a, b, (((0,), (0,)), ((), ())), preferred_element_type=F32)


def _split2(x):
    hi = x.astype(BF16)
    return hi, (x - hi.astype(F32)).astype(BF16)


def _rms(x, g):
    return x * lax.rsqrt(jnp.mean(x * x, axis=-1, keepdims=True) + EPS) * g


def _sigmoid(x):
    return 1.0 / (1.0 + jnp.exp(-x))


def _params(sem):
    return pltpu.CompilerParams(dimension_semantics=sem, vmem_limit_bytes=VMEM_LIMIT)


def _resident(shape):
    nd = len(shape)
    return pl.BlockSpec(shape, lambda *_: (0,) * nd, pipeline_mode=pl.Buffered(1))


def _ffn_body(*refs, final, nf):
    if final:
        x_ref, g_ref, wg_ref, wu_ref, wd_ref, fg_ref, o_ref, h_ref = refs
    else:
        x_ref, g_ref, wg_ref, wu_ref, wd_ref, o_ref, h_ref = refs
    f = pl.program_id(1)

    @pl.when(f == 0)
    def _():
        x = x_ref[...]
        h_ref[...] = _rms(x, g_ref[...]).astype(BF16)
        o_ref[...] = x

    h = h_ref[...]
    g = _dot(h, wg_ref[...].astype(BF16))
    u = _dot(h, wu_ref[...].astype(BF16))
    a = (g * _sigmoid(g)) * u * MACARON_W
    o_ref[...] += _dot(a.astype(BF16), wd_ref[...].astype(BF16))

    if final:
        @pl.when(f == nf - 1)
        def _():
            o_ref[...] = _rms(o_ref[...], fg_ref[...])


def _ffn(x, norm_g, wg, wu, wd, final_g=None, *, tm=1024, tf=256):
    s, d = x.shape
    ff = wg.shape[1]
    nf = ff // tf
    final = final_g is not None
    in_specs = [
        pl.BlockSpec((tm, d), lambda i, f: (i, 0)),
        pl.BlockSpec((1, d), lambda i, f: (0, 0)),
        pl.BlockSpec((d, tf), lambda i, f: (0, f)),
        pl.BlockSpec((d, tf), lambda i, f: (0, f)),
        pl.BlockSpec((tf, d), lambda i, f: (f, 0)),
    ]
    args = [x, norm_g, wg, wu, wd]
    if final:
        in_specs.append(pl.BlockSpec((1, d), lambda i, f: (0, 0)))
        args.append(final_g)
    return pl.pallas_call(
        functools.partial(_ffn_body, final=final, nf=nf),
        grid=(s // tm, nf),
        in_specs=in_specs,
        out_specs=pl.BlockSpec((tm, d), lambda i, f: (i, 0)),
        out_shape=jax.ShapeDtypeStruct((s, d), F32),
        scratch_shapes=[pltpu.VMEM((tm, d), BF16)],
        compiler_params=_params(("parallel", "arbitrary")),
        name="ffn_final" if final else "ffn",
    )(*args)


_GQK_W = 2 * GLA_HEADS * GLA_DK
_GV_W = GLA_HEADS * GLA_DV
_H_GV = _GQK_W
_H_GR = _H_GV + _GV_W
_H_GA = _H_GR + _GV_W
_HEAD_W = _H_GA + GLA_LOWRANK
_KV_W = NSA_KV_HEADS * HEAD_DIM
_B_KC = NSA_HEADS * HEAD_DIM
_B_KS = _B_KC + 2 * _KV_W
_B_KW = _B_KS + 2 * _KV_W
_BODY_W = _B_KW + 2 * _KV_W
_NG_PAD = 16
_T_NG = 2 * _KV_W


def _rope(x, cos, sin_signed):
    return x * cos + pltpu.roll(x, HEAD_DIM // 2, 1) * sin_signed


def _proj_body(x_ref, g_ref, wh_ref, wb_ref, wmq_ref, wt_ref, pos_ref, inv_ref, sgn_ref, qn_ref, kn_ref,
               mqn_ref, q_ref, ksel_ref, vselt_ref, kwin_ref, vwint_ref, mq_ref, gqk_ref, gv_ref, gr_ref,
               kcvc_ref, ga_ref, ngt_ref, *, tm):
    i = pl.program_id(0)
    h = _rms(x_ref[...], g_ref[...]).astype(BF16)

    def proj(w_ref, c0, width):
        return _dot(h, w_ref[:, c0:c0 + width])

    ang = pos_ref[...].astype(F32) * inv_ref[...]
    cos = jnp.cos(ang)
    sin_signed = jnp.sin(ang) * sgn_ref[...]

    def norm_rope(t, gain):
        return _rope(_rms(t, gain), cos, sin_signed)

    scale = HEAD_DIM ** -0.5
    nq = proj(wb_ref, 0, NSA_HEADS * HEAD_DIM)
    for hd in range(NSA_HEADS):
        sl = slice(hd * HEAD_DIM, (hd + 1) * HEAD_DIM)
        q_ref[:, sl] = (norm_rope(nq[:, sl], qn_ref[...]) * (scale * LOG2_E)).astype(BF16)

    tok = i * tm + lax.broadcasted_iota(jnp.int32, (tm, LANES), 0)
    blk = lax.broadcasted_iota(jnp.int32, (tm, LANES), 1)
    onehot = jnp.where(lax.shift_right_logical(tok, 6) == blk, 1.0, 0.0).astype(BF16)
    ks = proj(wb_ref, _B_KS, _KV_W)
    kw = proj(wb_ref, _B_KW, _KV_W)
    vt = _dot_nt(wt_ref[...], h)
    ones_rows = jnp.where(lax.broadcasted_iota(jnp.int32, (V_ROWS - HEAD_DIM, tm), 0) == 0, 1.0, 0.0).astype(BF16)
    for g in range(NSA_KV_HEADS):
        sl = slice(g * HEAD_DIM, (g + 1) * HEAD_DIM)
        slw = slice(_KV_W + g * HEAD_DIM, _KV_W + (g + 1) * HEAD_DIM)
        ksel_ref[g, :, 0:HEAD_DIM] = norm_rope(ks[:, sl], kn_ref[1:2, :]).astype(BF16)
        ksel_ref[g, :, HEAD_DIM:2 * HEAD_DIM] = onehot
        kwin_ref[g] = norm_rope(kw[:, sl], kn_ref[2:3, :]).astype(BF16)
        vselt_ref[g, 0:HEAD_DIM, :] = vt[sl, :].astype(BF16)
        vwint_ref[g, 0:HEAD_DIM, :] = vt[slw, :].astype(BF16)
        vselt_ref[g, HEAD_DIM:V_ROWS, :] = ones_rows
        vwint_ref[g, HEAD_DIM:V_ROWS, :] = ones_rows
        ngt_ref[g] = vt[_T_NG + g * _NG_PAD:_T_NG + (g + 1) * _NG_PAD, :]

    mq = _dot(h, wmq_ref[...])
    for hd in range(MEM_HEADS):
        sl = slice(hd * HEAD_DIM, (hd + 1) * HEAD_DIM)
        mq_ref[:, sl] = (_rms(mq[:, sl], mqn_ref[...]) * scale).astype(BF16)

    gqk_ref[...] = proj(wh_ref, 0, _GQK_W)
    gv_ref[...] = proj(wh_ref, _H_GV, _GV_W).astype(BF16)
    gr_ref[...] = proj(wh_ref, _H_GR, _GV_W)
    ga_ref[...] = proj(wh_ref, _H_GA, GLA_LOWRANK)
    kcvc = proj(wb_ref, _B_KC, 2 * _KV_W)
    for j in range(2 * NSA_KV_HEADS):
        kcvc_ref[j] = kcvc[:, j * HEAD_DIM:(j + 1) * HEAD_DIM]


def _proj(x1, mix_g, w_head, w_body, w_mq, w_t, pos_col, inv128, sgn128, q_norm, k_norm, mq_norm, *, tm=512):
    s, d = x1.shape
    row = lambda w: pl.BlockSpec((tm, w), lambda i: (i, 0))
    grp = lambda w: pl.BlockSpec((NSA_KV_HEADS, tm, w), lambda i: (0, i, 0))
    grpt = pl.BlockSpec((NSA_KV_HEADS, V_ROWS, tm), lambda i: (0, 0, i))
    vt_shape = jax.ShapeDtypeStruct((NSA_KV_HEADS, V_ROWS, s), BF16)
    out_shapes = [
        (jax.ShapeDtypeStruct((s, NSA_HEADS * HEAD_DIM), BF16), row(NSA_HEADS * HEAD_DIM)),
        (jax.ShapeDtypeStruct((NSA_KV_HEADS, s, 2 * HEAD_DIM), BF16), grp(2 * HEAD_DIM)),
        (vt_shape, grpt),
        (jax.ShapeDtypeStruct((NSA_KV_HEADS, s, HEAD_DIM), BF16), grp(HEAD_DIM)),
        (vt_shape, grpt),
        (jax.ShapeDtypeStruct((s, MEM_HEADS * HEAD_DIM), BF16), row(MEM_HEADS * HEAD_DIM)),
        (jax.ShapeDtypeStruct((s, 512), F32), row(512)),
        (jax.ShapeDtypeStruct((s, 512), BF16), row(512)),
        (jax.ShapeDtypeStruct((s, 512), F32), row(512)),
        (jax.ShapeDtypeStruct((2 * NSA_KV_HEADS, s, HEAD_DIM), F32),
         pl.BlockSpec((2 * NSA_KV_HEADS, tm, HEAD_DIM), lambda i: (0, i, 0))),
        (jax.ShapeDtypeStruct((s, GLA_LOWRANK), F32), row(GLA_LOWRANK)),
        (jax.ShapeDtypeStruct((NSA_KV_HEADS, _NG_PAD, s), F32),
         pl.BlockSpec((NSA_KV_HEADS, _NG_PAD, tm), lambda i: (0, 0, i))),
    ]
    return pl.pallas_call(
        functools.partial(_proj_body, tm=tm),
        grid=(s // tm,),
        in_specs=[
            row(d),
            _resident((1, d)),
            _resident(w_head.shape), _resident(w_body.shape), _resident(w_mq.shape), _resident(w_t.shape),
            pl.BlockSpec((tm, 1), lambda i: (i, 0)),
            _resident((1, LANES)), _resident((1, LANES)),
            _resident((1, HEAD_DIM)), _resident((3, HEAD_DIM)), _resident((1, HEAD_DIM)),
        ],
        out_specs=[o[1] for o in out_shapes],
        out_shape=[o[0] for o in out_shapes],
        compiler_params=_params(("parallel",)),
        name="proj",
    )(x1, mix_g, w_head, w_body, w_mq, w_t, pos_col, inv128, sgn128, q_norm, k_norm, mq_norm)


def _compress_body(kcvc_ref, w1k_ref, w2k_ref, pek_ref, w1v_ref, w2v_ref, pev_ref, kn_ref,
                   pos_ref, inv_ref, sgn_ref, kcmp_ref, vcmp_ref, *, units):
    half = CMP_LEN // 2
    ang = pos_ref[...].astype(F32) * inv_ref[...]
    cos = jnp.cos(ang)
    sin_signed = jnp.sin(ang) * sgn_ref[...]
    for kind, (w1_ref, w2_ref, pe_ref) in enumerate(((w1k_ref, w2k_ref, pek_ref),
                                                     (w1v_ref, w2v_ref, pev_ref))):
        for g in range(NSA_KV_HEADS):
            slab = kind * NSA_KV_HEADS + g
            a = jnp.zeros((units, w1_ref.shape[1]), F32)
            b = jnp.zeros((units, w1_ref.shape[1]), F32)
            for l in range(half):
                t = kcvc_ref[slab, pl.ds(l, units, stride=CMP_STRIDE), :]
                a = a + _dot((t + pe_ref[l:l + 1, :]).astype(BF16),
                             w1_ref[l * HEAD_DIM:(l + 1) * HEAD_DIM, :])
                b = b + _dot((t + pe_ref[half + l:half + l + 1, :]).astype(BF16),
                             w1_ref[(half + l) * HEAD_DIM:(half + l + 1) * HEAD_DIM, :])
            hid = a + pltpu.roll(b, units - 1, 0)
            act = (hid * _sigmoid(hid)).astype(BF16)
            if kind == 0:
                c = _rope(_rms(_dot(act, w2_ref[...]), kn_ref[0:1, :]), cos, sin_signed)
                kcmp_ref[g] = c.astype(BF16)
            else:
                vcmp_ref[g] = _dot_nt(w2_ref[...], act).astype(BF16)


def _compress(kcvc, w1k, w2k, pek, w1v, w2v, pev, k_norm, pos_cmp, inv128, sgn128):
    s = kcvc.shape[1]
    units = s // CMP_STRIDE
    shp = jax.ShapeDtypeStruct((NSA_KV_HEADS, units, HEAD_DIM), BF16)
    shp_t = jax.ShapeDtypeStruct((NSA_KV_HEADS, HEAD_DIM, units), BF16)
    return pl.pallas_call(
        functools.partial(_compress_body, units=units),
        out_shape=[shp, shp_t],
        compiler_params=pltpu.CompilerParams(vmem_limit_bytes=VMEM_LIMIT),
        name="compress",
    )(kcvc, w1k, w2k, pek, w1v, w2v, pev, k_norm, pos_cmp, inv128, sgn128)


def _gla_body(gqk_ref, gv_ref, gr_ref, ga_ref, wa_ref, ba_ref, on_ref, tcum_ref, bd_ref, hsel_ref,
              o_ref, st_ref, q_s, k_s, b_s, o_s):
    rows = GLA_ROWS
    npair = GLA_HEADS // 2

    @pl.when(pl.program_id(0) == 0)
    def _():
        st_ref[...] = jnp.zeros_like(st_ref)

    z = ba_ref[...]
    for ga_t in _split2(ga_ref[...]):
        for wa_t in _split2(wa_ref[...]):
            z = z + _dot(ga_t, wa_t)
    la = (jnp.minimum(z, 0.0) - jnp.log(1.0 + jnp.exp(-jnp.abs(z)))) / GLA_TAU
    tc = tcum_ref[...]
    bcum = jnp.zeros_like(la)
    for la_t in _split2(la):
        bcum = bcum + _dot(tc, la_t)
    b_s[...] = bcum * LOG2_E
    q_s[...] = gqk_ref[:, 0:256] * (GLA_DK ** -0.5)
    k_s[...] = gqk_ref[:, 256:512]

    row_i = lax.broadcasted_iota(jnp.int32, (GLA_SUB, LANES), 0)

    for sb in range(rows // GLA_SUB):
        rs = slice(sb * GLA_SUB, (sb + 1) * GLA_SUB)
        for p in range(npair):
            cs = slice(p * LANES, (p + 1) * LANES)
            vs = slice(p * 2 * GLA_DV, (p + 1) * 2 * GLA_DV)
            qs = q_s[rs, cs]
            kk = k_s[rs, cs]
            bb = b_s[rs, cs]
            vp = gv_ref[rs, vs]
            vpf = vp.astype(F32)
            blast = bb[GLA_SUB - 1:GLA_SUB, :]
            st = st_ref[p]
            o_inter = _dot_nt((qs * jnp.exp2(bb)).astype(BF16), st.astype(BF16))
            xs = []
            for j in range(GLA_SUB):
                dlt = jnp.where(row_i >= j, bb - bb[j:j + 1, :], NEG_INF)
                xs.append(qs * jnp.exp2(dlt) * kk[j:j + 1, :])
            red = _dot(jnp.concatenate(xs, axis=0).astype(BF16), hsel_ref[...])
            acc = o_inter
            for j in range(GLA_SUB):
                acc = acc + red[j * GLA_SUB:(j + 1) * GLA_SUB, :] * vpf[j:j + 1, :]
            o_s[rs, vs] = acc
            kd = (kk * jnp.exp2(blast - bb)).astype(BF16)
            upd = _dot_tn(vp, kd)
            st_ref[p] = st * jnp.exp2(blast) + upd * bd_ref[...]

    gr = gr_ref[...]
    for hd in range(GLA_HEADS):
        sl = slice(hd * GLA_DV, (hd + 1) * GLA_DV)
        r = gr[:, sl]
        o_ref[:, sl] = (_rms(o_s[:, sl], on_ref[...]) * (r * _sigmoid(r))).astype(BF16)


def _gla(gqk, gv, gr, ga, wa, ba, o_norm):
    s = gqk.shape[0]
    rows = GLA_ROWS
    idx = np.arange(rows)
    tcum = ((idx[:, None] >= idx[None, :]) & (idx[:, None] // GLA_SUB == idx[None, :] // GLA_SUB))
    tcum = jnp.asarray(tcum, BF16)
    r256 = np.arange(2 * GLA_DV)[:, None] // GLA_DV
    c128 = np.arange(LANES)[None, :] // GLA_DK
    bdmask = jnp.asarray(r256 == c128, F32)
    hsel = jnp.asarray((r256 == c128).T, BF16)
    row = lambda w: pl.BlockSpec((rows, w), lambda i: (i, 0))
    return pl.pallas_call(
        _gla_body,
        grid=(s // rows,),
        in_specs=[row(512), row(512), row(512), row(GLA_LOWRANK),
                  _resident(wa.shape), _resident(ba.shape), _resident(o_norm.shape),
                  _resident(tcum.shape), _resident(bdmask.shape), _resident(hsel.shape)],
        out_specs=row(512),
        out_shape=jax.ShapeDtypeStruct((s, GLA_HEADS * GLA_DV), BF16),
        scratch_shapes=[pltpu.VMEM((GLA_HEADS // 2, 2 * GLA_DV, LANES), F32),
                        pltpu.VMEM((rows, 256), F32), pltpu.VMEM((rows, 256), F32),
                        pltpu.VMEM((rows, 256), F32), pltpu.VMEM((rows, 512), F32)],
        compiler_params=_params(("arbitrary",)),
        name="gla",
    )(gqk, gv, gr, ga, wa, ba, o_norm, tcum, bdmask, hsel)


def _nsa_body(q_ref, kcmp_ref, vcmpt_ref, ksel_ref, vselt_ref, kwin_ref, vwint_ref, ngt_ref, ovlt_ref,
              o_ref, qt_s, s_s, p_s, acc_s, *, n_sel):
    qb = pl.program_id(0)
    t0 = qb * QBLK
    cols = NSA_HPG * QBLK
    gw = NSA_HPG * HEAD_DIM
    ncmp = kcmp_ref.shape[1]
    groups = range(NSA_KV_HEADS)
    span = SEL_SPAN_TILES * SEL_KT

    def tq_of(rows):
        return t0 + (lax.broadcasted_iota(jnp.int32, (rows, cols), 1) & (QBLK - 1))

    def gate_row(gates_t, c):
        return jnp.concatenate([gates_t[3 * hd + c:3 * hd + c + 1, :] for hd in range(NSA_HPG)], axis=1)

    def span_scores(g, j):
        k0 = pl.multiple_of(j * span, span)
        return _dot(ksel_ref[g, pl.ds(k0, span), :], qt_s[g])

    def span_pv(g, j, p):
        k0 = pl.multiple_of(j * span, span)
        return _dot(vselt_ref[g, :, pl.ds(k0, span)], p)

    def prologue(g):
        for hd in range(NSA_HPG):
            qh = q_ref[:, g * gw + hd * HEAD_DIM:g * gw + (hd + 1) * HEAD_DIM].astype(F32)
            qt_s[g, 0:HEAD_DIM, hd * QBLK:(hd + 1) * QBLK] = qh.T.astype(BF16)
        qt = qt_s[g, 0:HEAD_DIM, :]
        gates_t = _sigmoid(ngt_ref[g])

        def win_part(start, length):
            start = pl.multiple_of(jnp.maximum(start, 0), QBLK)
            return (_dot(kwin_ref[g, pl.ds(start, length), :], qt), vwint_ref[g, :, pl.ds(start, length)], start)

        s_c = _dot(kcmp_ref[g], qt)
        s_old, v_old, _ = win_part(t0 - WINDOW, QBLK)
        s_mid, v_mid, mid0 = win_part(t0 - WINDOW + QBLK, WINDOW - QBLK)
        s_dg, v_dg, _ = win_part(t0, QBLK)

        n_row = lax.broadcasted_iota(jnp.int32, (ncmp, cols), 0)
        valid_c = n_row * CMP_STRIDE + (CMP_LEN - 1) <= tq_of(ncmp)
        s_c = jnp.where(valid_c, s_c, NEG_INF)
        e_c = jnp.where(valid_c, jnp.exp2(s_c - jnp.max(s_c, axis=0, keepdims=True)), 0.0)
        p_c = e_c / jnp.maximum(jnp.sum(e_c, axis=0, keepdims=True), TINY)
        out_pre = gate_row(gates_t, 0) * _dot(vcmpt_ref[g], p_c.astype(BF16))
        psum = p_c[:, 0:QBLK]
        for hd in range(1, NSA_HPG):
            psum = psum + p_c[:, hd * QBLK:(hd + 1) * QBLK]
        imp = jnp.zeros((LANES, QBLK), F32)
        for p_t in _split2(psum):
            imp = imp + _dot(ovlt_ref[...], p_t)

        tq = t0 + lax.broadcasted_iota(jnp.int32, (LANES, QBLK), 1)
        m_i = lax.broadcasted_iota(jnp.int32, (LANES, QBLK), 0)
        cur = lax.shift_right_logical(tq, 6)
        forced = (m_i == 0) | (m_i == cur) | (m_i == cur - 1)
        causal = m_i * SEL_LEN <= tq
        n_forced = 3
        score = jnp.where(causal, jnp.where(forced, -jnp.inf, imp), -FORCE_SCORE)
        score = jnp.where(m_i < n_sel, score, SEL_PAD_SCORE)
        m_f = m_i.astype(F32)
        bias = jnp.where(forced & causal, 0.0, SEL_MASK_BIAS)
        for _ in range(min(SEL_TOPK, n_sel) - n_forced):
            mx = jnp.max(score, axis=0, keepdims=True)
            first = jnp.min(jnp.where(score == mx, m_f, float(LANES)), axis=0, keepdims=True)
            pick = m_f == first
            bias = jnp.where(pick, 0.0, bias)
            score = jnp.where(pick, -jnp.inf, score)
        bias = bias.astype(BF16)
        for hd in range(NSA_HPG):
            qt_s[g, HEAD_DIM:2 * HEAD_DIM, hd * QBLK:(hd + 1) * QBLK] = bias

        w_row = lax.broadcasted_iota(jnp.int32, (QBLK, cols), 0)
        tq_w = tq_of(QBLK)
        kp_old = t0 - WINDOW + w_row
        valid_old = (kp_old > tq_w - WINDOW) & (kp_old >= 0)
        s_old = jnp.where(valid_old, s_old, NEG_INF)
        mid_row = mid0 + lax.broadcasted_iota(jnp.int32, (WINDOW - QBLK, cols), 0)
        s_mid = jnp.where(mid_row < t0, s_mid, NEG_INF)
        valid_dg = t0 + w_row <= tq_w
        s_dg = jnp.where(valid_dg, s_dg, NEG_INF)
        m_w = jnp.maximum(jnp.maximum(jnp.max(s_old, axis=0, keepdims=True),
                                      jnp.max(s_mid, axis=0, keepdims=True)),
                          jnp.max(s_dg, axis=0, keepdims=True))
        p_old = jnp.where(valid_old, jnp.exp2(s_old - m_w), 0.0)
        p_mid = jnp.exp2(s_mid - m_w)
        p_dg = jnp.where(valid_dg, jnp.exp2(s_dg - m_w), 0.0)
        acc_w = (_dot(v_old, p_old.astype(BF16)) + _dot(v_mid, p_mid.astype(BF16))
                 + _dot(v_dg, p_dg.astype(BF16)))
        out_pre = out_pre + gate_row(gates_t, 2) * (acc_w[0:HEAD_DIM, :]
                                                    / jnp.maximum(acc_w[HEAD_DIM:HEAD_DIM + 1, :], TINY))

        s_s[g, 0] = span_scores(g, 0)
        p_s[g, 1] = jnp.zeros((span, cols), BF16)
        acc_s[g] = jnp.zeros((V_ROWS, cols), F32)
        return out_pre, gate_row(gates_t, 1)

    pre = [prologue(g) for g in groups]

    def sel_step(cur, j, ms):
        nxt = 1 - cur
        out = []
        for g in groups:
            pv = span_pv(g, jnp.maximum(j - 1, 0), p_s[g, nxt])
            s = s_s[g, cur]
            m_new = jnp.maximum(ms[g], jnp.max(s, axis=0, keepdims=True))
            p_s[g, cur] = jnp.exp2(s - m_new).astype(BF16)
            acc_s[g] = jnp.exp2(ms[g] - m_new) * (acc_s[g] + pv)
            out.append(m_new)
        for g in groups:
            s_s[g, nxt] = span_scores(g, j + 1)
        return tuple(out)

    def sel_body(j, ms):
        return lax.cond((j & 1) == 0, lambda c: sel_step(0, j, c), lambda c: sel_step(1, j, c), ms)

    n_span = t0 // span
    ms = lax.fori_loop(0, n_span, sel_body, tuple(jnp.full((1, cols), NEG_INF, F32) for _ in groups))

    slot = n_span & 1
    base = pl.multiple_of(n_span * span, span)

    def diag(nk):
        valid = base + lax.broadcasted_iota(jnp.int32, (nk, cols), 0) <= tq_of(nk)
        for g in groups:
            pv = span_pv(g, jnp.maximum(n_span - 1, 0), p_s[g, 1 - slot])
            s = jnp.where(valid, s_s[g, slot, 0:nk, :], NEG_INF)
            m_new = jnp.maximum(ms[g], jnp.max(s, axis=0, keepdims=True))
            p = jnp.where(valid, jnp.exp2(s - m_new), 0.0).astype(BF16)
            acc_s[g] = (jnp.exp2(ms[g] - m_new) * (acc_s[g] + pv)
                        + _dot(vselt_ref[g, :, pl.ds(base, nk)], p))

    lax.cond(t0 - base >= span // 2, lambda: diag(span), lambda: diag(span // 2))

    for g in groups:
        out_pre, gate_sel = pre[g]
        acc = acc_s[g]
        o_slc = acc[0:HEAD_DIM, :] / jnp.maximum(acc[HEAD_DIM:HEAD_DIM + 1, :], TINY)
        out = out_pre + gate_sel * o_slc
        for hd in range(NSA_HPG):
            o_ref[:, g * gw + hd * HEAD_DIM:g * gw + (hd + 1) * HEAD_DIM] = (
                out[:, hd * QBLK:(hd + 1) * QBLK].T.astype(BF16))


def _nsa(q, kcmp, vcmpt, ksel, vselt, kwin, vwint, ngt, overlap_t):
    s = q.shape[0]
    n_sel = s // SEL_LEN
    span = SEL_SPAN_TILES * SEL_KT
    assert n_sel <= LANES and s % span == 0 and s >= WINDOW + QBLK
    cols = NSA_HPG * QBLK
    ng = NSA_KV_HEADS
    return pl.pallas_call(
        functools.partial(_nsa_body, n_sel=n_sel),
        grid=(s // QBLK,),
        in_specs=[
            pl.BlockSpec((QBLK, NSA_HEADS * HEAD_DIM), lambda b: (b, 0)),
            _resident(kcmp.shape), _resident(vcmpt.shape),
            _resident(ksel.shape), _resident(vselt.shape), _resident(kwin.shape), _resident(vwint.shape),
            pl.BlockSpec((ng, _NG_PAD, QBLK), lambda b: (0, 0, b)),
            _resident(overlap_t.shape),
        ],
        out_specs=pl.BlockSpec((QBLK, NSA_HEADS * HEAD_DIM), lambda b: (b, 0)),
        out_shape=jax.ShapeDtypeStruct((s, NSA_HEADS * HEAD_DIM), BF16),
        scratch_shapes=[pltpu.VMEM((ng, 2 * HEAD_DIM, cols), BF16),
                        pltpu.VMEM((ng, 2, span, cols), F32),
                        pltpu.VMEM((ng, 2, span, cols), BF16),
                        pltpu.VMEM((ng, V_ROWS, cols), F32)],
        compiler_params=_params(("arbitrary",)),
        name="nsa",
    )(q, kcmp, vcmpt, ksel, vselt, kwin, vwint, ngt, overlap_t)


def _memkv_body(mem_ref, g_ref, w_ref, kn_ref, k_ref, v_ref):
    kv = _dot(_rms(mem_ref[...], g_ref[...]).astype(BF16), w_ref[...])
    width = MEM_HEADS * HEAD_DIM
    for hd in range(MEM_HEADS):
        sl = slice(hd * HEAD_DIM, (hd + 1) * HEAD_DIM)
        k_ref[:, sl] = _rms(kv[:, sl], kn_ref[...]).astype(BF16)
    v_ref[...] = kv[:, width:].astype(BF16)


def _memkv(mem, in_g, w_kv, k_norm):
    m = mem.shape[0]
    shp = jax.ShapeDtypeStruct((m, MEM_HEADS * HEAD_DIM), BF16)
    return pl.pallas_call(
        _memkv_body, out_shape=[shp, shp],
        compiler_params=pltpu.CompilerParams(vmem_limit_bytes=VMEM_LIMIT),
        name="memkv",
    )(mem, in_g, w_kv, k_norm)


def _memattn_body(q_ref, k_ref, v_ref, o_ref):
    for hd in range(MEM_HEADS):
        sl = slice(hd * HEAD_DIM, (hd + 1) * HEAD_DIM)
        sc = _dot_nt(q_ref[:, sl], k_ref[:, sl])
        e = jnp.exp(sc - jnp.max(sc, axis=-1, keepdims=True))
        p = e / jnp.sum(e, axis=-1, keepdims=True)
        o_ref[:, sl] = _dot(p.astype(BF16), v_ref[:, sl]).astype(BF16)


def _memattn(q, k, v, *, tm=512):
    s, w = q.shape
    return pl.pallas_call(
        _memattn_body,
        grid=(s // tm,),
        in_specs=[pl.BlockSpec((tm, w), lambda i: (i, 0)), _resident(k.shape), _resident(v.shape)],
        out_specs=pl.BlockSpec((tm, w), lambda i: (i, 0)),
        out_shape=jax.ShapeDtypeStruct((s, w), BF16),
        compiler_params=_params(("parallel",)),
        name="memattn",
    )(q, k, v)


def _outproj_body(x_ref, a_ref, b_ref, c_ref, w_ref, o_ref):
    na, nb = a_ref.shape[1], b_ref.shape[1]
    o_ref[...] = (x_ref[...] + _dot(a_ref[...], w_ref[0:na, :]) + _dot(b_ref[...], w_ref[na:na + nb, :])
                  + _dot(c_ref[...], w_ref[na + nb:, :]))


def _outproj(x1, o_gla, o_nsa, o_mem, w_out, *, tm=512):
    s, d = x1.shape
    row = lambda w: pl.BlockSpec((tm, w), lambda i: (i, 0))
    return pl.pallas_call(
        _outproj_body,
        grid=(s // tm,),
        in_specs=[row(d), row(o_gla.shape[1]), row(o_nsa.shape[1]), row(o_mem.shape[1]),
                  _resident(w_out.shape)],
        out_specs=row(d),
        out_shape=jax.ShapeDtypeStruct((s, d), F32),
        compiler_params=_params(("parallel",)),
        name="outproj",
    )(x1, o_gla, o_nsa, o_mem, w_out)


def _split_w_in(w_in):
    d = w_in.shape[0]
    body0 = _HEAD_W
    ng0 = body0 + _BODY_W
    per_g = NSA_HPG * 3
    mq0 = ng0 + NSA_KV_HEADS * per_g
    w_head = w_in[:, :_HEAD_W].astype(BF16)
    w_body = w_in[:, body0:ng0].astype(BF16)
    w_mq = w_in[:, mq0:].astype(BF16)
    assert w_mq.shape[1] == MEM_HEADS * HEAD_DIM
    n_vs = w_in[:, body0 + _B_KS + _KV_W:body0 + _B_KS + 2 * _KV_W]
    n_vw = w_in[:, body0 + _B_KW + _KV_W:body0 + _B_KW + 2 * _KV_W]
    rows = [n_vs.T, n_vw.T]
    for g in range(NSA_KV_HEADS):
        rows += [w_in[:, ng0 + g * per_g:ng0 + (g + 1) * per_g].T, jnp.zeros((_NG_PAD - per_g, d), w_in.dtype)]
    w_t = jnp.concatenate(rows, axis=0).astype(BF16)
    return w_head, w_body, w_mq, w_t


def _layer(x, mem, positions, ffn1_norm, ffn1_w_gate, ffn1_w_up, ffn1_w_down, mix_norm, w_in,
           gla_w_a, gla_b_a, gla_o_norm, nsa_q_norm, nsa_k_norm, nsa_cmp_pos_k, nsa_cmp_w1_k,
           nsa_cmp_w2_k, nsa_cmp_pos_v, nsa_cmp_w1_v, nsa_cmp_w2_v, mem_in_norm, w_mem_kv,
           mem_q_norm, mem_k_norm, w_out, ffn2_norm, ffn2_w_gate, ffn2_w_up, ffn2_w_down, final_norm):
    s, d = x.shape
    row = lambda v: v.reshape(1, -1)
    bf = lambda v: v.astype(BF16)

    x1 = _ffn(x, row(ffn1_norm), ffn1_w_gate, ffn1_w_up, ffn1_w_down)

    half = HEAD_DIM // 2
    inv = ROPE_THETA ** (-jnp.arange(half, dtype=F32) / half)
    inv128 = jnp.concatenate([inv, inv]).reshape(1, HEAD_DIM)
    sgn128 = jnp.concatenate([-jnp.ones((half,), F32), jnp.ones((half,), F32)]).reshape(1, HEAD_DIM)
    (q, ksel, vselt, kwin, vwint, mq, gqk, gv, gr, kcvc, ga, ngt) = _proj(
        x1, row(mix_norm), *_split_w_in(w_in), positions.reshape(s, 1), inv128, sgn128,
        row(nsa_q_norm), nsa_k_norm, row(mem_q_norm))

    o_gla = _gla(gqk, gv, gr, ga, gla_w_a, row(gla_b_a), row(gla_o_norm))

    units = s // CMP_STRIDE
    n_cmp = (s - CMP_LEN) // CMP_STRIDE + 1
    cmp_last = jnp.arange(units) * CMP_STRIDE + CMP_LEN - 1
    pos_cmp = positions[jnp.minimum(cmp_last, s - 1)].reshape(units, 1)
    kcmp, vcmpt = _compress(kcvc, bf(nsa_cmp_w1_k), bf(nsa_cmp_w2_k), nsa_cmp_pos_k,
                            bf(nsa_cmp_w1_v), bf(nsa_cmp_w2_v.T), nsa_cmp_pos_v, nsa_k_norm,
                            pos_cmp, inv128, sgn128)
    n_sel = s // SEL_LEN
    cmp_start = np.arange(units) * CMP_STRIDE
    sel_start = np.arange(LANES) * SEL_LEN
    overlap = np.clip(np.minimum(cmp_start[:, None] + CMP_LEN, sel_start[None, :] + SEL_LEN)
                      - np.maximum(cmp_start[:, None], sel_start[None, :]), 0, None) / CMP_STRIDE
    overlap = overlap * (np.arange(units)[:, None] < n_cmp) * (np.arange(LANES)[None, :] < n_sel)
    o_nsa = _nsa(q, kcmp, vcmpt, ksel, vselt, kwin, vwint, ngt, jnp.asarray(overlap.T, BF16))

    kmem, vmem = _memkv(mem, row(mem_in_norm), bf(w_mem_kv), row(mem_k_norm))
    o_mem = _memattn(mq, kmem, vmem)

    x2 = _outproj(x1, o_gla, o_nsa, o_mem, bf(w_out))
    return _ffn(x2, row(ffn2_norm), ffn2_w_gate, ffn2_w_up, ffn2_w_down, row(final_norm))


def kernel(x, mem, positions, ffn1_norm, ffn1_w_gate, ffn1_w_up, ffn1_w_down, mix_norm, w_in, gla_w_a, gla_b_a, gla_o_norm, nsa_q_norm, nsa_k_norm, nsa_cmp_pos_k, nsa_cmp_w1_k, nsa_cmp_w2_k, nsa_cmp_pos_v, nsa_cmp_w1_v, nsa_cmp_w2_v, mem_in_norm, w_mem_kv, mem_q_norm, mem_k_norm, w_out, ffn2_norm, ffn2_w_gate, ffn2_w_up, ffn2_w_down, final_norm):
    depth = ffn1_norm.shape[0]
    batch, s, d = x.shape
    outs = []
    for b in range(batch):
        xb, mem_b, pos_b = (x.reshape(s, d), mem.reshape(mem.shape[1:]), positions.reshape(s)) if batch == 1 \
            else (x[b], mem[b], positions[b])
        for l in range(depth):
            xb = _layer(xb, mem_b, pos_b, ffn1_norm[l], ffn1_w_gate[l], ffn1_w_up[l], ffn1_w_down[l],
                        mix_norm[l], w_in[l], gla_w_a[l], gla_b_a[l], gla_o_norm[l], nsa_q_norm[l],
                        nsa_k_norm[l], nsa_cmp_pos_k[l], nsa_cmp_w1_k[l], nsa_cmp_w2_k[l], nsa_cmp_pos_v[l],
                        nsa_cmp_w1_v[l], nsa_cmp_w2_v[l], mem_in_norm[l], w_mem_kv[l], mem_q_norm[l],
                        mem_k_norm[l], w_out[l], ffn2_norm[l], ffn2_w_gate[l], ffn2_w_up[l], ffn2_w_down[l],
                        final_norm[l])
        outs.append(xb)
    return outs[0].reshape(1, s, d) if batch == 1 else jnp.stack(outs)
```

```python
import functools

import numpy as np
import jax
import jax.numpy as jnp
from jax import lax
from jax.experimental import pallas as pl
from jax.experimental.pallas import tpu as pltpu

F32 = jnp.float32
BF16 = jnp.bfloat16

HEAD_DIM = 128
GLA_HEADS = 4
GLA_DK = 64
GLA_DV = 128
GLA_LOWRANK = 16
GLA_TAU = 16.0
NSA_HEADS = 8
NSA_KV_HEADS = 2
NSA_HPG = NSA_HEADS // NSA_KV_HEADS
CMP_LEN = 32
CMP_STRIDE = 16
SEL_LEN = 64
SEL_TOPK = 16
WINDOW = 512
MEM_HEADS = 4
MACARON_W = 0.5
QBLK = 128
ROPE_THETA = 10000.0
EPS = 1e-6
NEG_INF = -1e30
TINY = 1e-30
FORCE_SCORE = 1e4
LOG2_E = 1.4426950408889634

LANES = 128
VMEM_LIMIT = 56 * 1024 * 1024

GLA_SUB = 16
GLA_ROWS = 128
SEL_KT = 256
SEL_SPAN_TILES = 4
V_ROWS = HEAD_DIM + 16
SEL_MASK_BIAS = -32768.0
SEL_PAD_SCORE = -3e4


def _dot(a, b):
    return jnp.dot(a, b, preferred_element_type=F32)


def _dot_nt(a, b):
    return lax.dot_general(a, b, (((1,), (1,)), ((), ())), preferred_element_type=F32)


def _dot_tn(a, b):
    return lax.dot_general(a, b, (((0,), (0,)), ((), ())), preferred_element_type=F32)


def _split2(x):
    hi = x.astype(BF16)
    return hi, (x - hi.astype(F32)).astype(BF16)


def _rms(x, g):
    return x * lax.rsqrt(jnp.mean(x * x, axis=-1, keepdims=True) + EPS) * g


def _sigmoid(x):
    return 1.0 / (1.0 + jnp.exp(-x))


def _params(sem):
    return pltpu.CompilerParams(dimension_semantics=sem, vmem_limit_bytes=VMEM_LIMIT)


def _resident(shape):
    nd = len(shape)
    return pl.BlockSpec(shape, lambda *_: (0,) * nd, pipeline_mode=pl.Buffered(1))


def _ffn_body(*refs, final, nf):
    if final:
        x_ref, g_ref, wg_ref, wu_ref, wd_ref, fg_ref, o_ref, h_ref = refs
    else:
        x_ref, g_ref, wg_ref, wu_ref, wd_ref, o_ref, h_ref = refs
    f = pl.program_id(1)

    @pl.when(f == 0)
    def _():
        x = x_ref[...]
        h_ref[...] = _rms(x, g_ref[...]).astype(BF16)
        o_ref[...] = x

    h = h_ref[...]
    g = _dot(h, wg_ref[...].astype(BF16))
    u = _dot(h, wu_ref[...].astype(BF16))
    a = (g * _sigmoid(g)) * u * MACARON_W
    o_ref[...] += _dot(a.astype(BF16), wd_ref[...].astype(BF16))

    if final:
        @pl.when(f == nf - 1)
        def _():
            o_ref[...] = _rms(o_ref[...], fg_ref[...])


def _ffn(x, norm_g, wg, wu, wd, final_g=None, *, tm=1024, tf=256):
    s, d = x.shape
    ff = wg.shape[1]
    nf = ff // tf
    final = final_g is not None
    in_specs = [
        pl.BlockSpec((tm, d), lambda i, f: (i, 0)),
        pl.BlockSpec((1, d), lambda i, f: (0, 0)),
        pl.BlockSpec((d, tf), lambda i, f: (0, f)),
        pl.BlockSpec((d, tf), lambda i, f: (0, f)),
        pl.BlockSpec((tf, d), lambda i, f: (f, 0)),
    ]
    args = [x, norm_g, wg, wu, wd]
    if final:
        in_specs.append(pl.BlockSpec((1, d), lambda i, f: (0, 0)))
        args.append(final_g)
    return pl.pallas_call(
        functools.partial(_ffn_body, final=final, nf=nf),
        grid=(s // tm, nf),
        in_specs=in_specs,
        out_specs=pl.BlockSpec((tm, d), lambda i, f: (i, 0)),
        out_shape=jax.ShapeDtypeStruct((s, d), F32),
        scratch_shapes=[pltpu.VMEM((tm, d), BF16)],
        compiler_params=_params(("parallel", "arbitrary")),
        name="ffn_final" if final else "ffn",
    )(*args)


_GQK_W = 2 * GLA_HEADS * GLA_DK
_GV_W = GLA_HEADS * GLA_DV
_H_GV = _GQK_W
_H_GR = _H_GV + _GV_W
_H_GA = _H_GR + _GV_W
_HEAD_W = _H_GA + GLA_LOWRANK
_KV_W = NSA_KV_HEADS * HEAD_DIM
_B_KC = NSA_HEADS * HEAD_DIM
_B_KS = _B_KC + 2 * _KV_W
_B_KW = _B_KS + 2 * _KV_W
_BODY_W = _B_KW + 2 * _KV_W
_NG_PAD = 16


def _rope(x, cos, sin_signed):
    return x * cos + pltpu.roll(x, HEAD_DIM // 2, 1) * sin_signed


def _proj_body(x_ref, g_ref, wh_ref, wb_ref, wmq_ref, wng_ref, pos_ref, inv_ref, sgn_ref, qn_ref, kn_ref,
               mqn_ref, kmem_ref, vmem_ref, q_ref, ksel_ref, vselt_ref, kwin_ref, vwint_ref, omem_ref, gqk_ref,
               gv_ref, gr_ref, kcvc_ref, ga_ref, ngt_ref, *, tm):
    i = pl.program_id(0)
    h = _rms(x_ref[...], g_ref[...]).astype(BF16)

    def proj(w_ref, c0, width):
        return _dot(h, w_ref[:, c0:c0 + width])

    ang = pos_ref[...].astype(F32) * inv_ref[...]
    cos = jnp.cos(ang)
    sin_signed = jnp.sin(ang) * sgn_ref[...]

    def norm_rope(t, gain):
        return _rope(_rms(t, gain), cos, sin_signed)

    scale = HEAD_DIM ** -0.5
    nq = proj(wb_ref, 0, NSA_HEADS * HEAD_DIM)
    for hd in range(NSA_HEADS):
        sl = slice(hd * HEAD_DIM, (hd + 1) * HEAD_DIM)
        q_ref[:, sl] = (norm_rope(nq[:, sl], qn_ref[...]) * (scale * LOG2_E)).astype(BF16)

    tok = i * tm + lax.broadcasted_iota(jnp.int32, (tm, LANES), 0)
    blk = lax.broadcasted_iota(jnp.int32, (tm, LANES), 1)
    onehot = jnp.where(lax.shift_right_logical(tok, 6) == blk, 1.0, 0.0).astype(BF16)
    ksv = proj(wb_ref, _B_KS, 2 * _KV_W)
    kwv = proj(wb_ref, _B_KW, 2 * _KV_W)
    ng = _dot(h, wng_ref[...])
    ones_rows = jnp.where(lax.broadcasted_iota(jnp.int32, (V_ROWS - HEAD_DIM, tm), 0) == 0, 1.0, 0.0).astype(BF16)
    for g in range(NSA_KV_HEADS):
        sl = slice(g * HEAD_DIM, (g + 1) * HEAD_DIM)
        slv = slice(_KV_W + g * HEAD_DIM, _KV_W + (g + 1) * HEAD_DIM)
        ksel_ref[g, :, 0:HEAD_DIM] = norm_rope(ksv[:, sl], kn_ref[1:2, :]).astype(BF16)
        ksel_ref[g, :, HEAD_DIM:2 * HEAD_DIM] = onehot
        kwin_ref[g] = norm_rope(kwv[:, sl], kn_ref[2:3, :]).astype(BF16)
        for c in range(tm // LANES):
            rc = slice(c * LANES, (c + 1) * LANES)
            vselt_ref[g, 0:HEAD_DIM, rc] = ksv[rc, slv].T.astype(BF16)
            vwint_ref[g, 0:HEAD_DIM, rc] = kwv[rc, slv].T.astype(BF16)
            ngt_ref[g, :, rc] = ng[rc, :].T[g * _NG_PAD:(g + 1) * _NG_PAD, :]
        vselt_ref[g, HEAD_DIM:V_ROWS, :] = ones_rows
        vwint_ref[g, HEAD_DIM:V_ROWS, :] = ones_rows

    mq = _dot(h, wmq_ref[...])
    for hd in range(MEM_HEADS):
        sl = slice(hd * HEAD_DIM, (hd + 1) * HEAD_DIM)
        qh = (_rms(mq[:, sl], mqn_ref[...]) * scale).astype(BF16)
        sc = _dot_nt(qh, kmem_ref[:, sl])
        e = jnp.exp(sc - jnp.max(sc, axis=-1, keepdims=True))
        p = e / jnp.sum(e, axis=-1, keepdims=True)
        omem_ref[:, sl] = _dot(p.astype(BF16), vmem_ref[:, sl]).astype(BF16)

    gqk_ref[...] = proj(wh_ref, 0, _GQK_W)
    gv_ref[...] = proj(wh_ref, _H_GV, _GV_W).astype(BF16)
    gr_ref[...] = proj(wh_ref, _H_GR, _GV_W)
    ga_ref[...] = proj(wh_ref, _H_GA, GLA_LOWRANK)
    kcvc = proj(wb_ref, _B_KC, 2 * _KV_W)
    for j in range(2 * NSA_KV_HEADS):
        kcvc_ref[j] = kcvc[:, j * HEAD_DIM:(j + 1) * HEAD_DIM]


def _proj(x1, mix_g, w_head, w_body, w_mq, w_ng, pos_col, inv128, sgn128, q_norm, k_norm, mq_norm, kmem, vmem,
          *, tm=512):
    s, d = x1.shape
    row = lambda w: pl.BlockSpec((tm, w), lambda i: (i, 0))
    grp = lambda w: pl.BlockSpec((NSA_KV_HEADS, tm, w), lambda i: (0, i, 0))
    grpt = pl.BlockSpec((NSA_KV_HEADS, V_ROWS, tm), lambda i: (0, 0, i))
    vt_shape = jax.ShapeDtypeStruct((NSA_KV_HEADS, V_ROWS, s), BF16)
    out_shapes = [
        (jax.ShapeDtypeStruct((s, NSA_HEADS * HEAD_DIM), BF16), row(NSA_HEADS * HEAD_DIM)),
        (jax.ShapeDtypeStruct((NSA_KV_HEADS, s, 2 * HEAD_DIM), BF16), grp(2 * HEAD_DIM)),
        (vt_shape, grpt),
        (jax.ShapeDtypeStruct((NSA_KV_HEADS, s, HEAD_DIM), BF16), grp(HEAD_DIM)),
        (vt_shape, grpt),
        (jax.ShapeDtypeStruct((s, MEM_HEADS * HEAD_DIM), BF16), row(MEM_HEADS * HEAD_DIM)),
        (jax.ShapeDtypeStruct((s, 512), F32), row(512)),
        (jax.ShapeDtypeStruct((s, 512), BF16), row(512)),
        (jax.ShapeDtypeStruct((s, 512), F32), row(512)),
        (jax.ShapeDtypeStruct((2 * NSA_KV_HEADS, s, HEAD_DIM), F32),
         pl.BlockSpec((2 * NSA_KV_HEADS, tm, HEAD_DIM), lambda i: (0, i, 0))),
        (jax.ShapeDtypeStruct((s, GLA_LOWRANK), F32), row(GLA_LOWRANK)),
        (jax.ShapeDtypeStruct((NSA_KV_HEADS, _NG_PAD, s), F32),
         pl.BlockSpec((NSA_KV_HEADS, _NG_PAD, tm), lambda i: (0, 0, i))),
    ]
    return pl.pallas_call(
        functools.partial(_proj_body, tm=tm),
        grid=(s // tm,),
        in_specs=[
            row(d),
            _resident((1, d)),
            _resident(w_head.shape), _resident(w_body.shape), _resident(w_mq.shape), _resident(w_ng.shape),
            pl.BlockSpec((tm, 1), lambda i: (i, 0)),
            _resident((1, LANES)), _resident((1, LANES)),
            _resident((1, HEAD_DIM)), _resident((3, HEAD_DIM)), _resident((1, HEAD_DIM)),
            _resident(kmem.shape), _resident(vmem.shape),
        ],
        out_specs=[o[1] for o in out_shapes],
        out_shape=[o[0] for o in out_shapes],
        compiler_params=_params(("parallel",)),
        name="proj",
    )(x1, mix_g, w_head, w_body, w_mq, w_ng, pos_col, inv128, sgn128, q_norm, k_norm, mq_norm, kmem, vmem)


def _compress_body(kcvc_ref, w1k_ref, w2k_ref, pek_ref, w1v_ref, w2v_ref, pev_ref, kn_ref,
                   pos_ref, inv_ref, sgn_ref, kcmp_ref, vcmp_ref, *, units):
    half = CMP_LEN // 2
    ang = pos_ref[...].astype(F32) * inv_ref[...]
    cos = jnp.cos(ang)
    sin_signed = jnp.sin(ang) * sgn_ref[...]
    for kind, (w1_ref, w2_ref, pe_ref) in enumerate(((w1k_ref, w2k_ref, pek_ref),
                                                     (w1v_ref, w2v_ref, pev_ref))):
        for g in range(NSA_KV_HEADS):
            slab = kind * NSA_KV_HEADS + g
            a = jnp.zeros((units, w1_ref.shape[1]), F32)
            b = jnp.zeros((units, w1_ref.shape[1]), F32)
            for l in range(half):
                t = kcvc_ref[slab, pl.ds(l, units, stride=CMP_STRIDE), :]
                a = a + _dot((t + pe_ref[l:l + 1, :]).astype(BF16),
                             w1_ref[l * HEAD_DIM:(l + 1) * HEAD_DIM, :])
                b = b + _dot((t + pe_ref[half + l:half + l + 1, :]).astype(BF16),
                             w1_ref[(half + l) * HEAD_DIM:(half + l + 1) * HEAD_DIM, :])
            hid = a + pltpu.roll(b, units - 1, 0)
            act = (hid * _sigmoid(hid)).astype(BF16)
            if kind == 0:
                c = _rope(_rms(_dot(act, w2_ref[...]), kn_ref[0:1, :]), cos, sin_signed)
                kcmp_ref[g] = c.astype(BF16)
            else:
                vcmp_ref[g] = _dot_nt(w2_ref[...], act).astype(BF16)


def _compress(kcvc, w1k, w2k, pek, w1v, w2v, pev, k_norm, pos_cmp, inv128, sgn128):
    s = kcvc.shape[1]
    units = s // CMP_STRIDE
    shp = jax.ShapeDtypeStruct((NSA_KV_HEADS, units, HEAD_DIM), BF16)
    shp_t = jax.ShapeDtypeStruct((NSA_KV_HEADS, HEAD_DIM, units), BF16)
    return pl.pallas_call(
        functools.partial(_compress_body, units=units),
        out_shape=[shp, shp_t],
        compiler_params=pltpu.CompilerParams(vmem_limit_bytes=VMEM_LIMIT),
        name="compress",
    )(kcvc, w1k, w2k, pek, w1v, w2v, pev, k_norm, pos_cmp, inv128, sgn128)


def _gla_body(gqk_ref, gv_ref, gr_ref, ga_ref, wa_ref, ba_ref, on_ref, tcum_ref, bd_ref, hsel_ref,
              o_ref, st_ref, q_s, k_s, b_s, o_s):
    rows = GLA_ROWS
    npair = GLA_HEADS // 2

    @pl.when(pl.program_id(0) == 0)
    def _():
        st_ref[...] = jnp.zeros_like(st_ref)

    z = ba_ref[...]
    for ga_t in _split2(ga_ref[...]):
        for wa_t in _split2(wa_ref[...]):
            z = z + _dot(ga_t, wa_t)
    la = (jnp.minimum(z, 0.0) - jnp.log(1.0 + jnp.exp(-jnp.abs(z)))) / GLA_TAU
    tc = tcum_ref[...]
    bcum = jnp.zeros_like(la)
    for la_t in _split2(la):
        bcum = bcum + _dot(tc, la_t)
    b_s[...] = bcum * LOG2_E
    q_s[...] = gqk_ref[:, 0:256] * (GLA_DK ** -0.5)
    k_s[...] = gqk_ref[:, 256:512]

    row_i = lax.broadcasted_iota(jnp.int32, (GLA_SUB, LANES), 0)

    for sb in range(rows // GLA_SUB):
        rs = slice(sb * GLA_SUB, (sb + 1) * GLA_SUB)
        for p in range(npair):
            cs = slice(p * LANES, (p + 1) * LANES)
            vs = slice(p * 2 * GLA_DV, (p + 1) * 2 * GLA_DV)
            qs = q_s[rs, cs]
            kk = k_s[rs, cs]
            bb = b_s[rs, cs]
            vp = gv_ref[rs, vs]
            vpf = vp.astype(F32)
            blast = bb[GLA_SUB - 1:GLA_SUB, :]
            st = st_ref[p]
            o_inter = _dot_nt((qs * jnp.exp2(bb)).astype(BF16), st.astype(BF16))
            xs = []
            for j in range(GLA_SUB):
                dlt = jnp.where(row_i >= j, bb - bb[j:j + 1, :], NEG_INF)
                xs.append(qs * jnp.exp2(dlt) * kk[j:j + 1, :])
            red = _dot(jnp.concatenate(xs, axis=0).astype(BF16), hsel_ref[...])
            acc = o_inter
            for j in range(GLA_SUB):
                acc = acc + red[j * GLA_SUB:(j + 1) * GLA_SUB, :] * vpf[j:j + 1, :]
            o_s[rs, vs] = acc
            kd = (kk * jnp.exp2(blast - bb)).astype(BF16)
            upd = _dot_tn(vp, kd)
            st_ref[p] = st * jnp.exp2(blast) + upd * bd_ref[...]

    gr = gr_ref[...]
    for hd in range(GLA_HEADS):
        sl = slice(hd * GLA_DV, (hd + 1) * GLA_DV)
        r = gr[:, sl]
        o_ref[:, sl] = (_rms(o_s[:, sl], on_ref[...]) * (r * _sigmoid(r))).astype(BF16)


def _gla(gqk, gv, gr, ga, wa, ba, o_norm):
    s = gqk.shape[0]
    rows = GLA_ROWS
    idx = np.arange(rows)
    tcum = ((idx[:, None] >= idx[None, :]) & (idx[:, None] // GLA_SUB == idx[None, :] // GLA_SUB))
    tcum = jnp.asarray(tcum, BF16)
    r256 = np.arange(2 * GLA_DV)[:, None] // GLA_DV
    c128 = np.arange(LANES)[None, :] // GLA_DK
    bdmask = jnp.asarray(r256 == c128, F32)
    hsel = jnp.asarray((r256 == c128).T, BF16)
    row = lambda w: pl.BlockSpec((rows, w), lambda i: (i, 0))
    return pl.pallas_call(
        _gla_body,
        grid=(s // rows,),
        in_specs=[row(512), row(512), row(512), row(GLA_LOWRANK),
                  _resident(wa.shape), _resident(ba.shape), _resident(o_norm.shape),
                  _resident(tcum.shape), _resident(bdmask.shape), _resident(hsel.shape)],
        out_specs=row(512),
        out_shape=jax.ShapeDtypeStruct((s, GLA_HEADS * GLA_DV), BF16),
        scratch_shapes=[pltpu.VMEM((GLA_HEADS // 2, 2 * GLA_DV, LANES), F32),
                        pltpu.VMEM((rows, 256), F32), pltpu.VMEM((rows, 256), F32),
                        pltpu.VMEM((rows, 256), F32), pltpu.VMEM((rows, 512), F32)],
        compiler_params=_params(("arbitrary",)),
        name="gla",
    )(gqk, gv, gr, ga, wa, ba, o_norm, tcum, bdmask, hsel)


def _nsa_body(q_ref, kcmp_ref, vcmpt_ref, ksel_ref, vselt_ref, kwin_ref, vwint_ref, ngt_ref, ovlt_ref,
              o_ref, qt_s, s_s, p_s, acc_s, *, n_sel):
    qb = pl.program_id(0)
    t0 = qb * QBLK
    cols = NSA_HPG * QBLK
    gw = NSA_HPG * HEAD_DIM
    ncmp = kcmp_ref.shape[1]
    groups = range(NSA_KV_HEADS)
    span = SEL_SPAN_TILES * SEL_KT

    def tq_of(rows):
        return t0 + (lax.broadcasted_iota(jnp.int32, (rows, cols), 1) & (QBLK - 1))

    def gate_row(gates_t, c):
        return jnp.concatenate([gates_t[3 * hd + c:3 * hd + c + 1, :] for hd in range(NSA_HPG)], axis=1)

    def span_scores(g, j):
        k0 = pl.multiple_of(j * span, span)
        return _dot(ksel_ref[g, pl.ds(k0, span), :], qt_s[g])

    def span_pv(g, j, p):
        k0 = pl.multiple_of(j * span, span)
        return _dot(vselt_ref[g, :, pl.ds(k0, span)], p)

    def prologue(g):
        for hd in range(NSA_HPG):
            qh = q_ref[:, g * gw + hd * HEAD_DIM:g * gw + (hd + 1) * HEAD_DIM].astype(F32)
            qt_s[g, 0:HEAD_DIM, hd * QBLK:(hd + 1) * QBLK] = qh.T.astype(BF16)
        qt = qt_s[g, 0:HEAD_DIM, :]
        gates_t = _sigmoid(ngt_ref[g])

        def win_part(start, length):
            start = pl.multiple_of(jnp.maximum(start, 0), QBLK)
            return (_dot(kwin_ref[g, pl.ds(start, length), :], qt), vwint_ref[g, :, pl.ds(start, length)], start)

        s_c = _dot(kcmp_ref[g], qt)
        s_old, v_old, _ = win_part(t0 - WINDOW, QBLK)
        s_mid, v_mid, mid0 = win_part(t0 - WINDOW + QBLK, WINDOW - QBLK)
        s_dg, v_dg, _ = win_part(t0, QBLK)

        n_row = lax.broadcasted_iota(jnp.int32, (ncmp, cols), 0)
        valid_c = n_row * CMP_STRIDE + (CMP_LEN - 1) <= tq_of(ncmp)
        s_c = jnp.where(valid_c, s_c, NEG_INF)
        e_c = jnp.where(valid_c, jnp.exp2(s_c - jnp.max(s_c, axis=0, keepdims=True)), 0.0)
        p_c = e_c / jnp.maximum(jnp.sum(e_c, axis=0, keepdims=True), TINY)
        out_pre = gate_row(gates_t, 0) * _dot(vcmpt_ref[g], p_c.astype(BF16))
        psum = p_c[:, 0:QBLK]
        for hd in range(1, NSA_HPG):
            psum = psum + p_c[:, hd * QBLK:(hd + 1) * QBLK]
        imp = jnp.zeros((LANES, QBLK), F32)
        for p_t in _split2(psum):
            imp = imp + _dot(ovlt_ref[...], p_t)

        tq = t0 + lax.broadcasted_iota(jnp.int32, (LANES, QBLK), 1)
        m_i = lax.broadcasted_iota(jnp.int32, (LANES, QBLK), 0)
        cur = lax.shift_right_logical(tq, 6)
        forced = (m_i == 0) | (m_i == cur) | (m_i == cur - 1)
        causal = m_i * SEL_LEN <= tq
        n_forced = 3
        score = jnp.where(causal, jnp.where(forced, -jnp.inf, imp), -FORCE_SCORE)
        score = jnp.where(m_i < n_sel, score, SEL_PAD_SCORE)
        m_f = m_i.astype(F32)
        bias = jnp.where(forced & causal, 0.0, SEL_MASK_BIAS)
        for _ in range(min(SEL_TOPK, n_sel) - n_forced):
            mx = jnp.max(score, axis=0, keepdims=True)
            first = jnp.min(jnp.where(score == mx, m_f, float(LANES)), axis=0, keepdims=True)
            pick = m_f == first
            bias = jnp.where(pick, 0.0, bias)
            score = jnp.where(pick, -jnp.inf, score)
        bias = bias.astype(BF16)
        for hd in range(NSA_HPG):
            qt_s[g, HEAD_DIM:2 * HEAD_DIM, hd * QBLK:(hd + 1) * QBLK] = bias

        w_row = lax.broadcasted_iota(jnp.int32, (QBLK, cols), 0)
        tq_w = tq_of(QBLK)
        kp_old = t0 - WINDOW + w_row
        valid_old = (kp_old > tq_w - WINDOW) & (kp_old >= 0)
        s_old = jnp.where(valid_old, s_old, NEG_INF)
        mid_row = mid0 + lax.broadcasted_iota(jnp.int32, (WINDOW - QBLK, cols), 0)
        s_mid = jnp.where(mid_row < t0, s_mid, NEG_INF)
        valid_dg = t0 + w_row <= tq_w
        s_dg = jnp.where(valid_dg, s_dg, NEG_INF)
        m_w = jnp.maximum(jnp.maximum(jnp.max(s_old, axis=0, keepdims=True),
                                      jnp.max(s_mid, axis=0, keepdims=True)),
                          jnp.max(s_dg, axis=0, keepdims=True))
        p_old = jnp.where(valid_old, jnp.exp2(s_old - m_w), 0.0)
        p_mid = jnp.exp2(s_mid - m_w)
        p_dg = jnp.where(valid_dg, jnp.exp2(s_dg - m_w), 0.0)
        acc_w = (_dot(v_old, p_old.astype(BF16)) + _dot(v_mid, p_mid.astype(BF16))
                 + _dot(v_dg, p_dg.astype(BF16)))
        out_pre = out_pre + gate_row(gates_t, 2) * (acc_w[0:HEAD_DIM, :]
                                                    / jnp.maximum(acc_w[HEAD_DIM:HEAD_DIM + 1, :], TINY))

        s_s[g, 0] = span_scores(g, 0)
        p_s[g, 1] = jnp.zeros((span, cols), BF16)
        acc_s[g] = jnp.zeros((V_ROWS, cols), F32)
        return out_pre, gate_row(gates_t, 1)

    pre = [prologue(g) for g in groups]

    def sel_step(cur, j, ms):
        nxt = 1 - cur
        out = []
        for g in groups:
            pv = span_pv(g, jnp.maximum(j - 1, 0), p_s[g, nxt])
            s = s_s[g, cur]
            m_new = jnp.maximum(ms[g], jnp.max(s, axis=0, keepdims=True))
            p_s[g, cur] = jnp.exp2(s - m_new).astype(BF16)
            acc_s[g] = jnp.exp2(ms[g] - m_new) * (acc_s[g] + pv)
            out.append(m_new)
        for g in groups:
            s_s[g, nxt] = span_scores(g, j + 1)
        return tuple(out)

    def sel_body(j, ms):
        return lax.cond((j & 1) == 0, lambda c: sel_step(0, j, c), lambda c: sel_step(1, j, c), ms)

    n_span = t0 // span
    ms = lax.fori_loop(0, n_span, sel_body, tuple(jnp.full((1, cols), NEG_INF, F32) for _ in groups))

    slot = n_span & 1
    base = pl.multiple_of(n_span * span, span)

    def diag(nk):
        valid = base + lax.broadcasted_iota(jnp.int32, (nk, cols), 0) <= tq_of(nk)
        for g in groups:
            pv = span_pv(g, jnp.maximum(n_span - 1, 0), p_s[g, 1 - slot])
            s = jnp.where(valid, s_s[g, slot, 0:nk, :], NEG_INF)
            m_new = jnp.maximum(ms[g], jnp.max(s, axis=0, keepdims=True))
            p = jnp.where(valid, jnp.exp2(s - m_new), 0.0).astype(BF16)
            acc_s[g] = (jnp.exp2(ms[g] - m_new) * (acc_s[g] + pv)
                        + _dot(vselt_ref[g, :, pl.ds(base, nk)], p))

    lax.cond(t0 - base >= span // 2, lambda: diag(span), lambda: diag(span // 2))

    for g in groups:
        out_pre, gate_sel = pre[g]
        acc = acc_s[g]
        o_slc = acc[0:HEAD_DIM, :] / jnp.maximum(acc[HEAD_DIM:HEAD_DIM + 1, :], TINY)
        out = out_pre + gate_sel * o_slc
        for hd in range(NSA_HPG):
            o_ref[:, g * gw + hd * HEAD_DIM:g * gw + (hd + 1) * HEAD_DIM] = (
                out[:, hd * QBLK:(hd + 1) * QBLK].T.astype(BF16))


def _nsa(q, kcmp, vcmpt, ksel, vselt, kwin, vwint, ngt, overlap_t):
    s = q.shape[0]
    n_sel = s // SEL_LEN
    span = SEL_SPAN_TILES * SEL_KT
    assert n_sel <= LANES and s % span == 0 and s >= WINDOW + QBLK
    cols = NSA_HPG * QBLK
    ng = NSA_KV_HEADS
    return pl.pallas_call(
        functools.partial(_nsa_body, n_sel=n_sel),
        grid=(s // QBLK,),
        in_specs=[
            pl.BlockSpec((QBLK, NSA_HEADS * HEAD_DIM), lambda b: (b, 0)),
            _resident(kcmp.shape), _resident(vcmpt.shape),
            _resident(ksel.shape), _resident(vselt.shape), _resident(kwin.shape), _resident(vwint.shape),
            pl.BlockSpec((ng, _NG_PAD, QBLK), lambda b: (0, 0, b)),
            _resident(overlap_t.shape),
        ],
        out_specs=pl.BlockSpec((QBLK, NSA_HEADS * HEAD_DIM), lambda b: (b, 0)),
        out_shape=jax.ShapeDtypeStruct((s, NSA_HEADS * HEAD_DIM), BF16),
        scratch_shapes=[pltpu.VMEM((ng, 2 * HEAD_DIM, cols), BF16),
                        pltpu.VMEM((ng, 2, span, cols), F32),
                        pltpu.VMEM((ng, 2, span, cols), BF16),
                        pltpu.VMEM((ng, V_ROWS, cols), F32)],
        compiler_params=_params(("arbitrary",)),
        name="nsa",
    )(q, kcmp, vcmpt, ksel, vselt, kwin, vwint, ngt, overlap_t)


def _memkv_body(mem_ref, g_ref, w_ref, kn_ref, k_ref, v_ref):
    kv = _dot(_rms(mem_ref[...], g_ref[...]).astype(BF16), w_ref[...])
    width = MEM_HEADS * HEAD_DIM
    for hd in range(MEM_HEADS):
        sl = slice(hd * HEAD_DIM, (hd + 1) * HEAD_DIM)
        k_ref[:, sl] = _rms(kv[:, sl], kn_ref[...]).astype(BF16)
    v_ref[...] = kv[:, width:].astype(BF16)


def _memkv(mem, in_g, w_kv, k_norm):
    m = mem.shape[0]
    shp = jax.ShapeDtypeStruct((m, MEM_HEADS * HEAD_DIM), BF16)
    return pl.pallas_call(
        _memkv_body, out_shape=[shp, shp],
        compiler_params=pltpu.CompilerParams(vmem_limit_bytes=VMEM_LIMIT),
        name="memkv",
    )(mem, in_g, w_kv, k_norm)


def _outproj_body(x_ref, a_ref, b_ref, c_ref, w_ref, o_ref):
    na, nb = a_ref.shape[1], b_ref.shape[1]
    o_ref[...] = (x_ref[...] + _dot(a_ref[...], w_ref[0:na, :]) + _dot(b_ref[...], w_ref[na:na + nb, :])
                  + _dot(c_ref[...], w_ref[na + nb:, :]))


def _outproj(x1, o_gla, o_nsa, o_mem, w_out, *, tm=512):
    s, d = x1.shape
    row = lambda w: pl.BlockSpec((tm, w), lambda i: (i, 0))
    return pl.pallas_call(
        _outproj_body,
        grid=(s // tm,),
        in_specs=[row(d), row(o_gla.shape[1]), row(o_nsa.shape[1]), row(o_mem.shape[1]),
                  _resident(w_out.shape)],
        out_specs=row(d),
        out_shape=jax.ShapeDtypeStruct((s, d), F32),
        compiler_params=_params(("parallel",)),
        name="outproj",
    )(x1, o_gla, o_nsa, o_mem, w_out)


def _split_w_in(w_in):
    d = w_in.shape[0]
    body0 = _HEAD_W
    ng0 = body0 + _BODY_W
    per_g = NSA_HPG * 3
    mq0 = ng0 + NSA_KV_HEADS * per_g
    w_head = w_in[:, :_HEAD_W].astype(BF16)
    w_body = w_in[:, body0:ng0].astype(BF16)
    w_mq = w_in[:, mq0:].astype(BF16)
    assert w_mq.shape[1] == MEM_HEADS * HEAD_DIM
    cols = []
    for g in range(NSA_KV_HEADS):
        cols += [w_in[:, ng0 + g * per_g:ng0 + (g + 1) * per_g], jnp.zeros((d, _NG_PAD - per_g), w_in.dtype)]
    cols.append(jnp.zeros((d, LANES - NSA_KV_HEADS * _NG_PAD), w_in.dtype))
    w_ng = jnp.concatenate(cols, axis=1).astype(BF16)
    return w_head, w_body, w_mq, w_ng


def _layer(x, mem, positions, ffn1_norm, ffn1_w_gate, ffn1_w_up, ffn1_w_down, mix_norm, w_in,
           gla_w_a, gla_b_a, gla_o_norm, nsa_q_norm, nsa_k_norm, nsa_cmp_pos_k, nsa_cmp_w1_k,
           nsa_cmp_w2_k, nsa_cmp_pos_v, nsa_cmp_w1_v, nsa_cmp_w2_v, mem_in_norm, w_mem_kv,
           mem_q_norm, mem_k_norm, w_out, ffn2_norm, ffn2_w_gate, ffn2_w_up, ffn2_w_down, final_norm):
    s, d = x.shape
    row = lambda v: v.reshape(1, -1)
    bf = lambda v: v.astype(BF16)

    x1 = _ffn(x, row(ffn1_norm), ffn1_w_gate, ffn1_w_up, ffn1_w_down)

    half = HEAD_DIM // 2
    inv = ROPE_THETA ** (-jnp.arange(half, dtype=F32) / half)
    inv128 = jnp.concatenate([inv, inv]).reshape(1, HEAD_DIM)
    sgn128 = jnp.concatenate([-jnp.ones((half,), F32), jnp.ones((half,), F32)]).reshape(1, HEAD_DIM)
    kmem, vmem = _memkv(mem, row(mem_in_norm), bf(w_mem_kv), row(mem_k_norm))
    (q, ksel, vselt, kwin, vwint, o_mem, gqk, gv, gr, kcvc, ga, ngt) = _proj(
        x1, row(mix_norm), *_split_w_in(w_in), positions.reshape(s, 1), inv128, sgn128,
        row(nsa_q_norm), nsa_k_norm, row(mem_q_norm), kmem, vmem)

    o_gla = _gla(gqk, gv, gr, ga, gla_w_a, row(gla_b_a), row(gla_o_norm))

    units = s // CMP_STRIDE
    n_cmp = (s - CMP_LEN) // CMP_STRIDE + 1
    cmp_last = jnp.arange(units) * CMP_STRIDE + CMP_LEN - 1
    pos_cmp = positions[jnp.minimum(cmp_last, s - 1)].reshape(units, 1)
    kcmp, vcmpt = _compress(kcvc, bf(nsa_cmp_w1_k), bf(nsa_cmp_w2_k), nsa_cmp_pos_k,
                            bf(nsa_cmp_w1_v), bf(nsa_cmp_w2_v.T), nsa_cmp_pos_v, nsa_k_norm,
                            pos_cmp, inv128, sgn128)
    n_sel = s // SEL_LEN
    cmp_start = np.arange(units) * CMP_STRIDE
    sel_start = np.arange(LANES) * SEL_LEN
    overlap = np.clip(np.minimum(cmp_start[:, None] + CMP_LEN, sel_start[None, :] + SEL_LEN)
                      - np.maximum(cmp_start[:, None], sel_start[None, :]), 0, None) / CMP_STRIDE
    overlap = overlap * (np.arange(units)[:, None] < n_cmp) * (np.arange(LANES)[None, :] < n_sel)
    o_nsa = _nsa(q, kcmp, vcmpt, ksel, vselt, kwin, vwint, ngt, jnp.asarray(overlap.T, BF16))

    x2 = _outproj(x1, o_gla, o_nsa, o_mem, bf(w_out))
    return _ffn(x2, row(ffn2_norm), ffn2_w_gate, ffn2_w_up, ffn2_w_down, row(final_norm))


def kernel(x, mem, positions, ffn1_norm, ffn1_w_gate, ffn1_w_up, ffn1_w_down, mix_norm, w_in, gla_w_a, gla_b_a, gla_o_norm, nsa_q_norm, nsa_k_norm, nsa_cmp_pos_k, nsa_cmp_w1_k, nsa_cmp_w2_k, nsa_cmp_pos_v, nsa_cmp_w1_v, nsa_cmp_w2_v, mem_in_norm, w_mem_kv, mem_q_norm, mem_k_norm, w_out, ffn2_norm, ffn2_w_gate, ffn2_w_up, ffn2_w_down, final_norm):
    depth = ffn1_norm.shape[0]
    batch, s, d = x.shape
    outs = []
    for b in range(batch):
        xb, mem_b, pos_b = (x.reshape(s, d), mem.reshape(mem.shape[1:]), positions.reshape(s)) if batch == 1 \
            else (x[b], mem[b], positions[b])
        for l in range(depth):
            xb = _layer(xb, mem_b, pos_b, ffn1_norm[l], ffn1_w_gate[l], ffn1_w_up[l], ffn1_w_down[l],
                        mix_norm[l], w_in[l], gla_w_a[l], gla_b_a[l], gla_o_norm[l], nsa_q_norm[l],
                        nsa_k_norm[l], nsa_cmp_pos_k[l], nsa_cmp_w1_k[l], nsa_cmp_w2_k[l], nsa_cmp_pos_v[l],
                        nsa_cmp_w1_v[l], nsa_cmp_w2_v[l], mem_in_norm[l], w_mem_kv[l], mem_q_norm[l],
                        mem_k_norm[l], w_out[l], ffn2_norm[l], ffn2_w_gate[l], ffn2_w_up[l], ffn2_w_down[l],
                        final_norm[l])
        outs.append(xb)
    return outs[0].reshape(1, s, d) if batch == 1 else jnp.stack(outs)
```

```python
import functools

import numpy as np
import jax
import jax.numpy as jnp
from jax import lax
from jax.experimental import pallas as pl
from jax.experimental.pallas import tpu as pltpu

F32 = jnp.float32
BF16 = jnp.bfloat16

HEAD_DIM = 128
GLA_HEADS = 4
GLA_DK = 64
GLA_DV = 128
GLA_LOWRANK = 16
GLA_TAU = 16.0
NSA_HEADS = 8
NSA_KV_HEADS = 2
NSA_HPG = NSA_HEADS // NSA_KV_HEADS
CMP_LEN = 32
CMP_STRIDE = 16
SEL_LEN = 64
SEL_TOPK = 16
WINDOW = 512
MEM_HEADS = 4
MACARON_W = 0.5
QBLK = 128
ROPE_THETA = 10000.0
EPS = 1e-6
NEG_INF = -1e30
TINY = 1e-30
FORCE_SCORE = 1e4
LOG2_E = 1.4426950408889634

LANES = 128
VMEM_LIMIT = 56 * 1024 * 1024

GLA_SUB = 16
GLA_ROWS = 128
SEL_KT = 256
SEL_SPAN_TILES = 4
V_ROWS = HEAD_DIM + 16
SEL_MASK_BIAS = -32768.0
SEL_PAD_SCORE = -3e4


def _dot(a, b):
    return jnp.dot(a, b, preferred_element_type=F32)


def _dot_nt(a, b):
    return lax.dot_general(a, b, (((1,), (1,)), ((), ())), preferred_element_type=F32)


def _dot_tn(a, b):
    return lax.dot_general(a, b, (((0,), (0,)), ((), ())), preferred_element_type=F32)


def _split2(x):
    hi = x.astype(BF16)
    return hi, (x - hi.astype(F32)).astype(BF16)


def _rms(x, g):
    return x * lax.rsqrt(jnp.mean(x * x, axis=-1, keepdims=True) + EPS) * g


def _sigmoid(x):
    return 1.0 / (1.0 + jnp.exp(-x))


def _params(sem):
    return pltpu.CompilerParams(dimension_semantics=sem, vmem_limit_bytes=VMEM_LIMIT)


def _resident(shape):
    nd = len(shape)
    return pl.BlockSpec(shape, lambda *_: (0,) * nd, pipeline_mode=pl.Buffered(1))


def _ffn_body(*refs, final, nf):
    if final:
        x_ref, g_ref, wg_ref, wu_ref, wd_ref, fg_ref, o_ref, h_ref = refs
    else:
        x_ref, g_ref, wg_ref, wu_ref, wd_ref, o_ref, h_ref = refs
    f = pl.program_id(1)

    @pl.when(f == 0)
    def _():
        x = x_ref[...]
        h_ref[...] = _rms(x, g_ref[...]).astype(BF16)
        o_ref[...] = x

    h = h_ref[...]
    g = _dot(h, wg_ref[...].astype(BF16))
    u = _dot(h, wu_ref[...].astype(BF16))
    a = (g * _sigmoid(g)) * u * MACARON_W
    o_ref[...] += _dot(a.astype(BF16), wd_ref[...].astype(BF16))

    if final:
        @pl.when(f == nf - 1)
        def _():
            o_ref[...] = _rms(o_ref[...], fg_ref[...])


def _ffn(x, norm_g, wg, wu, wd, final_g=None, *, tm=1024, tf=256):
    s, d = x.shape
    ff = wg.shape[1]
    nf = ff // tf
    final = final_g is not None
    in_specs = [
        pl.BlockSpec((tm, d), lambda i, f: (i, 0)),
        pl.BlockSpec((1, d), lambda i, f: (0, 0)),
        pl.BlockSpec((d, tf), lambda i, f: (0, f)),
        pl.BlockSpec((d, tf), lambda i, f: (0, f)),
        pl.BlockSpec((tf, d), lambda i, f: (f, 0)),
    ]
    args = [x, norm_g, wg, wu, wd]
    if final:
        in_specs.append(pl.BlockSpec((1, d), lambda i, f: (0, 0)))
        args.append(final_g)
    return pl.pallas_call(
        functools.partial(_ffn_body, final=final, nf=nf),
        grid=(s // tm, nf),
        in_specs=in_specs,
        out_specs=pl.BlockSpec((tm, d), lambda i, f: (i, 0)),
        out_shape=jax.ShapeDtypeStruct((s, d), F32),
        scratch_shapes=[pltpu.VMEM((tm, d), BF16)],
        compiler_params=_params(("parallel", "arbitrary")),
        name="ffn_final" if final else "ffn",
    )(*args)


_GQK_W = 2 * GLA_HEADS * GLA_DK
_GV_W = GLA_HEADS * GLA_DV
_H_GV = _GQK_W
_H_GR = _H_GV + _GV_W
_H_GA = _H_GR + _GV_W
_HEAD_W = _H_GA + GLA_LOWRANK
_KV_W = NSA_KV_HEADS * HEAD_DIM
_B_KC = NSA_HEADS * HEAD_DIM
_B_KS = _B_KC + 2 * _KV_W
_B_KW = _B_KS + 2 * _KV_W
_BODY_W = _B_KW + 2 * _KV_W
_NG_PAD = 16


def _rope(x, cos, sin_signed):
    return x * cos + pltpu.roll(x, HEAD_DIM // 2, 1) * sin_signed


def _proj_body(x_ref, g_ref, wh_ref, wb_ref, wmq_ref, wng_ref, pos_ref, inv_ref, sgn_ref, qn_ref, kn_ref,
               mqn_ref, kmem_ref, vmem_ref, q_ref, ksel_ref, vselt_ref, kwin_ref, vwint_ref, omem_ref, gqk_ref,
               gv_ref, gr_ref, kcvc_ref, ga_ref, ngt_ref, *, tm):
    i = pl.program_id(0)
    h = _rms(x_ref[...], g_ref[...]).astype(BF16)

    def proj(w_ref, c0, width):
        return _dot(h, w_ref[:, c0:c0 + width])

    ang = pos_ref[...].astype(F32) * inv_ref[...]
    cos = jnp.cos(ang)
    sin_signed = jnp.sin(ang) * sgn_ref[...]

    def norm_rope(t, gain):
        return _rope(_rms(t, gain), cos, sin_signed)

    scale = HEAD_DIM ** -0.5
    nq = proj(wb_ref, 0, NSA_HEADS * HEAD_DIM)
    for hd in range(NSA_HEADS):
        sl = slice(hd * HEAD_DIM, (hd + 1) * HEAD_DIM)
        q_ref[:, sl] = (norm_rope(nq[:, sl], qn_ref[...]) * (scale * LOG2_E)).astype(BF16)

    tok = i * tm + lax.broadcasted_iota(jnp.int32, (tm, LANES), 0)
    blk = lax.broadcasted_iota(jnp.int32, (tm, LANES), 1)
    onehot = jnp.where(lax.shift_right_logical(tok, 6) == blk, 1.0, 0.0).astype(BF16)
    ksv = proj(wb_ref, _B_KS, 2 * _KV_W)
    kwv = proj(wb_ref, _B_KW, 2 * _KV_W)
    ng = _dot(h, wng_ref[...])
    ones_rows = jnp.where(lax.broadcasted_iota(jnp.int32, (V_ROWS - HEAD_DIM, tm), 0) == 0, 1.0, 0.0).astype(BF16)
    for g in range(NSA_KV_HEADS):
        sl = slice(g * HEAD_DIM, (g + 1) * HEAD_DIM)
        slv = slice(_KV_W + g * HEAD_DIM, _KV_W + (g + 1) * HEAD_DIM)
        ksel_ref[g, :, 0:HEAD_DIM] = norm_rope(ksv[:, sl], kn_ref[1:2, :]).astype(BF16)
        ksel_ref[g, :, HEAD_DIM:2 * HEAD_DIM] = onehot
        kwin_ref[g] = norm_rope(kwv[:, sl], kn_ref[2:3, :]).astype(BF16)
        for c in range(tm // LANES):
            rc = slice(c * LANES, (c + 1) * LANES)
            vselt_ref[g, 0:HEAD_DIM, rc] = ksv[rc, slv].T.astype(BF16)
            vwint_ref[g, 0:HEAD_DIM, rc] = kwv[rc, slv].T.astype(BF16)
            ngt_ref[g, :, rc] = ng[rc, :].T[g * _NG_PAD:(g + 1) * _NG_PAD, :]
        vselt_ref[g, HEAD_DIM:V_ROWS, :] = ones_rows
        vwint_ref[g, HEAD_DIM:V_ROWS, :] = ones_rows

    mq = _dot(h, wmq_ref[...])
    for hd in range(MEM_HEADS):
        sl = slice(hd * HEAD_DIM, (hd + 1) * HEAD_DIM)
        qh = (_rms(mq[:, sl], mqn_ref[...]) * scale).astype(BF16)
        sc = _dot_nt(qh, kmem_ref[:, sl])
        e = jnp.exp(sc - jnp.max(sc, axis=-1, keepdims=True))
        p = e / jnp.sum(e, axis=-1, keepdims=True)
        omem_ref[:, sl] = _dot(p.astype(BF16), vmem_ref[:, sl]).astype(BF16)

    gqk_ref[...] = proj(wh_ref, 0, _GQK_W)
    gv_ref[...] = proj(wh_ref, _H_GV, _GV_W).astype(BF16)
    gr_ref[...] = proj(wh_ref, _H_GR, _GV_W)
    ga_ref[...] = proj(wh_ref, _H_GA, GLA_LOWRANK)
    kcvc = proj(wb_ref, _B_KC, 2 * _KV_W)
    for j in range(2 * NSA_KV_HEADS):
        kcvc_ref[j] = kcvc[:, j * HEAD_DIM:(j + 1) * HEAD_DIM]


def _proj(x1, mix_g, w_head, w_body, w_mq, w_ng, pos_col, inv128, sgn128, q_norm, k_norm, mq_norm, kmem, vmem,
          *, tm=512):
    s, d = x1.shape
    row = lambda w: pl.BlockSpec((tm, w), lambda i: (i, 0))
    grp = lambda w: pl.BlockSpec((NSA_KV_HEADS, tm, w), lambda i: (0, i, 0))
    grpt = pl.BlockSpec((NSA_KV_HEADS, V_ROWS, tm), lambda i: (0, 0, i))
    vt_shape = jax.ShapeDtypeStruct((NSA_KV_HEADS, V_ROWS, s), BF16)
    out_shapes = [
        (jax.ShapeDtypeStruct((s, NSA_HEADS * HEAD_DIM), BF16), row(NSA_HEADS * HEAD_DIM)),
        (jax.ShapeDtypeStruct((NSA_KV_HEADS, s, 2 * HEAD_DIM), BF16), grp(2 * HEAD_DIM)),
        (vt_shape, grpt),
        (jax.ShapeDtypeStruct((NSA_KV_HEADS, s, HEAD_DIM), BF16), grp(HEAD_DIM)),
        (vt_shape, grpt),
        (jax.ShapeDtypeStruct((s, MEM_HEADS * HEAD_DIM), BF16), row(MEM_HEADS * HEAD_DIM)),
        (jax.ShapeDtypeStruct((s, 512), F32), row(512)),
        (jax.ShapeDtypeStruct((s, 512), BF16), row(512)),
        (jax.ShapeDtypeStruct((s, 512), F32), row(512)),
        (jax.ShapeDtypeStruct((2 * NSA_KV_HEADS, s, HEAD_DIM), F32),
         pl.BlockSpec((2 * NSA_KV_HEADS, tm, HEAD_DIM), lambda i: (0, i, 0))),
        (jax.ShapeDtypeStruct((s, GLA_LOWRANK), F32), row(GLA_LOWRANK)),
        (jax.ShapeDtypeStruct((NSA_KV_HEADS, _NG_PAD, s), F32),
         pl.BlockSpec((NSA_KV_HEADS, _NG_PAD, tm), lambda i: (0, 0, i))),
    ]
    return pl.pallas_call(
        functools.partial(_proj_body, tm=tm),
        grid=(s // tm,),
        in_specs=[
            row(d),
            _resident((1, d)),
            _resident(w_head.shape), _resident(w_body.shape), _resident(w_mq.shape), _resident(w_ng.shape),
            pl.BlockSpec((tm, 1), lambda i: (i, 0)),
            _resident((1, LANES)), _resident((1, LANES)),
            _resident((1, HEAD_DIM)), _resident((3, HEAD_DIM)), _resident((1, HEAD_DIM)),
            _resident(kmem.shape), _resident(vmem.shape),
        ],
        out_specs=[o[1] for o in out_shapes],
        out_shape=[o[0] for o in out_shapes],
        compiler_params=_params(("parallel",)),
        name="proj",
    )(x1, mix_g, w_head, w_body, w_mq, w_ng, pos_col, inv128, sgn128, q_norm, k_norm, mq_norm, kmem, vmem)


def _compress_body(kcvc_ref, w1k_ref, w2k_ref, pek_ref, w1v_ref, w2v_ref, pev_ref, kn_ref,
                   pos_ref, inv_ref, sgn_ref, kcmp_ref, vcmp_ref, *, units):
    half = CMP_LEN // 2
    ang = pos_ref[...].astype(F32) * inv_ref[...]
    cos = jnp.cos(ang)
    sin_signed = jnp.sin(ang) * sgn_ref[...]
    for kind, (w1_ref, w2_ref, pe_ref) in enumerate(((w1k_ref, w2k_ref, pek_ref),
                                                     (w1v_ref, w2v_ref, pev_ref))):
        for g in range(NSA_KV_HEADS):
            slab = kind * NSA_KV_HEADS + g
            a = jnp.zeros((units, w1_ref.shape[1]), F32)
            b = jnp.zeros((units, w1_ref.shape[1]), F32)
            for l in range(half):
                t = kcvc_ref[slab, pl.ds(l, units, stride=CMP_STRIDE), :]
                a = a + _dot((t + pe_ref[l:l + 1, :]).astype(BF16),
                             w1_ref[l * HEAD_DIM:(l + 1) * HEAD_DIM, :])
                b = b + _dot((t + pe_ref[half + l:half + l + 1, :]).astype(BF16),
                             w1_ref[(half + l) * HEAD_DIM:(half + l + 1) * HEAD_DIM, :])
            hid = a + pltpu.roll(b, units - 1, 0)
            act = (hid * _sigmoid(hid)).astype(BF16)
            if kind == 0:
                c = _rope(_rms(_dot(act, w2_ref[...]), kn_ref[0:1, :]), cos, sin_signed)
                kcmp_ref[g] = c.astype(BF16)
            else:
                vcmp_ref[g] = _dot_nt(w2_ref[...], act).astype(BF16)


def _compress(kcvc, w1k, w2k, pek, w1v, w2v, pev, k_norm, pos_cmp, inv128, sgn128):
    s = kcvc.shape[1]
    units = s // CMP_STRIDE
    shp = jax.ShapeDtypeStruct((NSA_KV_HEADS, units, HEAD_DIM), BF16)
    shp_t = jax.ShapeDtypeStruct((NSA_KV_HEADS, HEAD_DIM, units), BF16)
    return pl.pallas_call(
        functools.partial(_compress_body, units=units),
        out_shape=[shp, shp_t],
        compiler_params=pltpu.CompilerParams(vmem_limit_bytes=VMEM_LIMIT),
        name="compress",
    )(kcvc, w1k, w2k, pek, w1v, w2v, pev, k_norm, pos_cmp, inv128, sgn128)


def _gla_body(gqk_ref, gv_ref, gr_ref, ga_ref, wa_ref, ba_ref, on_ref, tcum_ref, bd_ref, hsel_ref,
              o_ref, st_ref, q_s, k_s, b_s, o_s):
    rows = GLA_ROWS
    npair = GLA_HEADS // 2

    @pl.when(pl.program_id(0) == 0)
    def _():
        st_ref[...] = jnp.zeros_like(st_ref)

    z = ba_ref[...]
    for ga_t in _split2(ga_ref[...]):
        for wa_t in _split2(wa_ref[...]):
            z = z + _dot(ga_t, wa_t)
    la = (jnp.minimum(z, 0.0) - jnp.log(1.0 + jnp.exp(-jnp.abs(z)))) / GLA_TAU
    tc = tcum_ref[...]
    bcum = jnp.zeros_like(la)
    for la_t in _split2(la):
        bcum = bcum + _dot(tc, la_t)
    b_s[...] = bcum * LOG2_E
    q_s[...] = gqk_ref[:, 0:256] * (GLA_DK ** -0.5)
    k_s[...] = gqk_ref[:, 256:512]

    row_i = lax.broadcasted_iota(jnp.int32, (GLA_SUB, LANES), 0)

    for sb in range(rows // GLA_SUB):
        rs = slice(sb * GLA_SUB, (sb + 1) * GLA_SUB)
        for p in range(npair):
            cs = slice(p * LANES, (p + 1) * LANES)
            vs = slice(p * 2 * GLA_DV, (p + 1) * 2 * GLA_DV)
            qs = q_s[rs, cs]
            kk = k_s[rs, cs]
            bb = b_s[rs, cs]
            vp = gv_ref[rs, vs]
            vpf = vp.astype(F32)
            blast = bb[GLA_SUB - 1:GLA_SUB, :]
            st = st_ref[p]
            o_inter = _dot_nt((qs * jnp.exp2(bb)).astype(BF16), st.astype(BF16))
            xs = []
            for j in range(GLA_SUB):
                dlt = jnp.where(row_i >= j, bb - bb[j:j + 1, :], NEG_INF)
                xs.append(qs * jnp.exp2(dlt) * kk[j:j + 1, :])
            red = _dot(jnp.concatenate(xs, axis=0).astype(BF16), hsel_ref[...])
            acc = o_inter
            for j in range(GLA_SUB):
                acc = acc + red[j * GLA_SUB:(j + 1) * GLA_SUB, :] * vpf[j:j + 1, :]
            o_s[rs, vs] = acc
            kd = (kk * jnp.exp2(blast - bb)).astype(BF16)
            upd = _dot_tn(vp, kd)
            st_ref[p] = st * jnp.exp2(blast) + upd * bd_ref[...]

    gr = gr_ref[...]
    for hd in range(GLA_HEADS):
        sl = slice(hd * GLA_DV, (hd + 1) * GLA_DV)
        r = gr[:, sl]
        o_ref[:, sl] = (_rms(o_s[:, sl], on_ref[...]) * (r * _sigmoid(r))).astype(BF16)


def _gla(gqk, gv, gr, ga, wa, ba, o_norm):
    s = gqk.shape[0]
    rows = GLA_ROWS
    idx = np.arange(rows)
    tcum = ((idx[:, None] >= idx[None, :]) & (idx[:, None] // GLA_SUB == idx[None, :] // GLA_SUB))
    tcum = jnp.asarray(tcum, BF16)
    r256 = np.arange(2 * GLA_DV)[:, None] // GLA_DV
    c128 = np.arange(LANES)[None, :] // GLA_DK
    bdmask = jnp.asarray(r256 == c128, F32)
    hsel = jnp.asarray((r256 == c128).T, BF16)
    row = lambda w: pl.BlockSpec((rows, w), lambda i: (i, 0))
    return pl.pallas_call(
        _gla_body,
        grid=(s // rows,),
        in_specs=[row(512), row(512), row(512), row(GLA_LOWRANK),
                  _resident(wa.shape), _resident(ba.shape), _resident(o_norm.shape),
                  _resident(tcum.shape), _resident(bdmask.shape), _resident(hsel.shape)],
        out_specs=row(512),
        out_shape=jax.ShapeDtypeStruct((s, GLA_HEADS * GLA_DV), BF16),
        scratch_shapes=[pltpu.VMEM((GLA_HEADS // 2, 2 * GLA_DV, LANES), F32),
                        pltpu.VMEM((rows, 256), F32), pltpu.VMEM((rows, 256), F32),
                        pltpu.VMEM((rows, 256), F32), pltpu.VMEM((rows, 512), F32)],
        compiler_params=_params(("arbitrary",)),
        name="gla",
    )(gqk, gv, gr, ga, wa, ba, o_norm, tcum, bdmask, hsel)


def _nsa_body(q_ref, kcmp_ref, vcmpt_ref, ksel_ref, vselt_ref, kwin_ref, vwint_ref, ngt_ref, ovlt_ref,
              o_ref, qt_s, s_s, p_s, acc_s, *, n_sel):
    qb = pl.program_id(0)
    t0 = qb * QBLK
    cols = NSA_HPG * QBLK
    gw = NSA_HPG * HEAD_DIM
    ncmp = kcmp_ref.shape[1]
    groups = range(NSA_KV_HEADS)
    span = SEL_SPAN_TILES * SEL_KT

    def tq_of(rows):
        return t0 + (lax.broadcasted_iota(jnp.int32, (rows, cols), 1) & (QBLK - 1))

    def gate_row(gates_t, c):
        return jnp.concatenate([gates_t[3 * hd + c:3 * hd + c + 1, :] for hd in range(NSA_HPG)], axis=1)

    def span_scores(g, j):
        k0 = pl.multiple_of(j * span, span)
        return _dot(ksel_ref[g, pl.ds(k0, span), :], qt_s[g])

    def span_pv(g, j, p):
        k0 = pl.multiple_of(j * span, span)
        return _dot(vselt_ref[g, :, pl.ds(k0, span)], p)

    def prologue(g):
        for hd in range(NSA_HPG):
            qh = q_ref[:, g * gw + hd * HEAD_DIM:g * gw + (hd + 1) * HEAD_DIM].astype(F32)
            qt_s[g, 0:HEAD_DIM, hd * QBLK:(hd + 1) * QBLK] = qh.T.astype(BF16)
        qt = qt_s[g, 0:HEAD_DIM, :]
        gates_t = _sigmoid(ngt_ref[g])

        def win_part(start, length):
            start = pl.multiple_of(jnp.maximum(start, 0), QBLK)
            return (_dot(kwin_ref[g, pl.ds(start, length), :], qt), vwint_ref[g, :, pl.ds(start, length)], start)

        s_c = _dot(kcmp_ref[g], qt)
        s_old, v_old, _ = win_part(t0 - WINDOW, QBLK)
        s_mid, v_mid, mid0 = win_part(t0 - WINDOW + QBLK, WINDOW - QBLK)
        s_dg, v_dg, _ = win_part(t0, QBLK)

        n_row = lax.broadcasted_iota(jnp.int32, (ncmp, cols), 0)
        valid_c = n_row * CMP_STRIDE + (CMP_LEN - 1) <= tq_of(ncmp)
        s_c = jnp.where(valid_c, s_c, NEG_INF)
        e_c = jnp.where(valid_c, jnp.exp2(s_c - jnp.max(s_c, axis=0, keepdims=True)), 0.0)
        p_c = e_c / jnp.maximum(jnp.sum(e_c, axis=0, keepdims=True), TINY)
        out_pre = gate_row(gates_t, 0) * _dot(vcmpt_ref[g], p_c.astype(BF16))
        psum = p_c[:, 0:QBLK]
        for hd in range(1, NSA_HPG):
            psum = psum + p_c[:, hd * QBLK:(hd + 1) * QBLK]
        imp = jnp.zeros((LANES, QBLK), F32)
        for p_t in _split2(psum):
            imp = imp + _dot(ovlt_ref[...], p_t)

        tq = t0 + lax.broadcasted_iota(jnp.int32, (LANES, QBLK), 1)
        m_i = lax.broadcasted_iota(jnp.int32, (LANES, QBLK), 0)
        cur = lax.shift_right_logical(tq, 6)
        forced = (m_i == 0) | (m_i == cur) | (m_i == cur - 1)
        causal = m_i * SEL_LEN <= tq
        n_forced = 3
        score = jnp.where(causal, jnp.where(forced, -jnp.inf, imp), -FORCE_SCORE)
        score = jnp.where(m_i < n_sel, score, SEL_PAD_SCORE)
        m_f = m_i.astype(F32)
        bias = jnp.where(forced & causal, 0.0, SEL_MASK_BIAS)
        for _ in range(min(SEL_TOPK, n_sel) - n_forced):
            mx = jnp.max(score, axis=0, keepdims=True)
            first = jnp.min(jnp.where(score == mx, m_f, float(LANES)), axis=0, keepdims=True)
            pick = m_f == first
            bias = jnp.where(pick, 0.0, bias)
            score = jnp.where(pick, -jnp.inf, score)
        bias = bias.astype(BF16)
        for hd in range(NSA_HPG):
            qt_s[g, HEAD_DIM:2 * HEAD_DIM, hd * QBLK:(hd + 1) * QBLK] = bias

        w_row = lax.broadcasted_iota(jnp.int32, (QBLK, cols), 0)
        tq_w = tq_of(QBLK)
        kp_old = t0 - WINDOW + w_row
        valid_old = (kp_old > tq_w - WINDOW) & (kp_old >= 0)
        s_old = jnp.where(valid_old, s_old, NEG_INF)
        mid_row = mid0 + lax.broadcasted_iota(jnp.int32, (WINDOW - QBLK, cols), 0)
        s_mid = jnp.where(mid_row < t0, s_mid, NEG_INF)
        valid_dg = t0 + w_row <= tq_w
        s_dg = jnp.where(valid_dg, s_dg, NEG_INF)
        m_w = jnp.maximum(jnp.maximum(jnp.max(s_old, axis=0, keepdims=True),
                                      jnp.max(s_mid, axis=0, keepdims=True)),
                          jnp.max(s_dg, axis=0, keepdims=True))
        p_old = jnp.where(valid_old, jnp.exp2(s_old - m_w), 0.0)
        p_mid = jnp.exp2(s_mid - m_w)
        p_dg = jnp.where(valid_dg, jnp.exp2(s_dg - m_w), 0.0)
        acc_w = (_dot(v_old, p_old.astype(BF16)) + _dot(v_mid, p_mid.astype(BF16))
                 + _dot(v_dg, p_dg.astype(BF16)))
        out_pre = out_pre + gate_row(gates_t, 2) * (acc_w[0:HEAD_DIM, :]
                                                    / jnp.maximum(acc_w[HEAD_DIM:HEAD_DIM + 1, :], TINY))

        s_s[g, 0] = span_scores(g, 0)
        p_s[g, 1] = jnp.zeros((span, cols), BF16)
        acc_s[g] = jnp.zeros((V_ROWS, cols), F32)
        return out_pre, gate_row(gates_t, 1)

    pre = [prologue(g) for g in groups]

    def sel_step(cur, j, ms):
        nxt = 1 - cur
        out = []
        for g in groups:
            pv = span_pv(g, jnp.maximum(j - 1, 0), p_s[g, nxt])
            s = s_s[g, cur]
            m_new = jnp.maximum(ms[g], jnp.max(s, axis=0, keepdims=True))
            p_s[g, cur] = jnp.exp2(s - m_new).astype(BF16)
            acc_s[g] = jnp.exp2(ms[g] - m_new) * (acc_s[g] + pv)
            out.append(m_new)
        for g in groups:
            s_s[g, nxt] = span_scores(g, j + 1)
        return tuple(out)

    def sel_body(j, ms):
        return lax.cond((j & 1) == 0, lambda c: sel_step(0, j, c), lambda c: sel_step(1, j, c), ms)

    n_span = t0 // span
    ms = lax.fori_loop(0, n_span, sel_body, tuple(jnp.full((1, cols), NEG_INF, F32) for _ in groups))

    slot = n_span & 1
    base = pl.multiple_of(n_span * span, span)

    def diag(nk):
        valid = base + lax.broadcasted_iota(jnp.int32, (nk, cols), 0) <= tq_of(nk)
        for g in groups:
            pv = span_pv(g, jnp.maximum(n_span - 1, 0), p_s[g, 1 - slot])
            s = jnp.where(valid, s_s[g, slot, 0:nk, :], NEG_INF)
            m_new = jnp.maximum(ms[g], jnp.max(s, axis=0, keepdims=True))
            p = jnp.where(valid, jnp.exp2(s - m_new), 0.0).astype(BF16)
            acc_s[g] = (jnp.exp2(ms[g] - m_new) * (acc_s[g] + pv)
                        + _dot(vselt_ref[g, :, pl.ds(base, nk)], p))

    lax.cond(t0 - base >= span // 2, lambda: diag(span), lambda: diag(span // 2))

    for g in groups:
        out_pre, gate_sel = pre[g]
        acc = acc_s[g]
        o_slc = acc[0:HEAD_DIM, :] / jnp.maximum(acc[HEAD_DIM:HEAD_DIM + 1, :], TINY)
        out = out_pre + gate_sel * o_slc
        for hd in range(NSA_HPG):
            o_ref[:, g * gw + hd * HEAD_DIM:g * gw + (hd + 1) * HEAD_DIM] = (
                out[:, hd * QBLK:(hd + 1) * QBLK].T.astype(BF16))


def _nsa(q, kcmp, vcmpt, ksel, vselt, kwin, vwint, ngt, overlap_t):
    s = q.shape[0]
    n_sel = s // SEL_LEN
    span = SEL_SPAN_TILES * SEL_KT
    assert n_sel <= LANES and s % span == 0 and s >= WINDOW + QBLK
    cols = NSA_HPG * QBLK
    ng = NSA_KV_HEADS
    return pl.pallas_call(
        functools.partial(_nsa_body, n_sel=n_sel),
        grid=(s // QBLK,),
        in_specs=[
            pl.BlockSpec((QBLK, NSA_HEADS * HEAD_DIM), lambda b: (b, 0)),
            _resident(kcmp.shape), _resident(vcmpt.shape),
            _resident(ksel.shape), _resident(vselt.shape), _resident(kwin.shape), _resident(vwint.shape),
            pl.BlockSpec((ng, _NG_PAD, QBLK), lambda b: (0, 0, b)),
            _resident(overlap_t.shape),
        ],
        out_specs=pl.BlockSpec((QBLK, NSA_HEADS * HEAD_DIM), lambda b: (b, 0)),
        out_shape=jax.ShapeDtypeStruct((s, NSA_HEADS * HEAD_DIM), BF16),
        scratch_shapes=[pltpu.VMEM((ng, 2 * HEAD_DIM, cols), BF16),
                        pltpu.VMEM((ng, 2, span, cols), F32),
                        pltpu.VMEM((ng, 2, span, cols), BF16),
                        pltpu.VMEM((ng, V_ROWS, cols), F32)],
        compiler_params=_params(("arbitrary",)),
        name="nsa",
    )(q, kcmp, vcmpt, ksel, vselt, kwin, vwint, ngt, overlap_t)


def _memkv_body(mem_ref, g_ref, w_ref, kn_ref, k_ref, v_ref):
    kv = _dot(_rms(mem_ref[...], g_ref[...]).astype(BF16), w_ref[...])
    width = MEM_HEADS * HEAD_DIM
    for hd in range(MEM_HEADS):
        sl = slice(hd * HEAD_DIM, (hd + 1) * HEAD_DIM)
        k_ref[:, sl] = _rms(kv[:, sl], kn_ref[...]).astype(BF16)
    v_ref[...] = kv[:, width:].astype(BF16)


def _memkv(mem, in_g, w_kv, k_norm):
    m = mem.shape[0]
    shp = jax.ShapeDtypeStruct((m, MEM_HEADS * HEAD_DIM), BF16)
    return pl.pallas_call(
        _memkv_body, out_shape=[shp, shp],
        compiler_params=pltpu.CompilerParams(vmem_limit_bytes=VMEM_LIMIT),
        name="memkv",
    )(mem, in_g, w_kv, k_norm)


def _outproj_body(x_ref, a_ref, b_ref, c_ref, w_ref, o_ref):
    na, nb = a_ref.shape[1], b_ref.shape[1]
    o_ref[...] = (x_ref[...] + _dot(a_ref[...], w_ref[0:na, :]) + _dot(b_ref[...], w_ref[na:na + nb, :])
                  + _dot(c_ref[...], w_ref[na + nb:, :]))


def _outproj(x1, o_gla, o_nsa, o_mem, w_out, *, tm=512):
    s, d = x1.shape
    row = lambda w: pl.BlockSpec((tm, w), lambda i: (i, 0))
    return pl.pallas_call(
        _outproj_body,
        grid=(s // tm,),
        in_specs=[row(d), row(o_gla.shape[1]), row(o_nsa.shape[1]), row(o_mem.shape[1]),
                  _resident(w_out.shape)],
        out_specs=row(d),
        out_shape=jax.ShapeDtypeStruct((s, d), F32),
        compiler_params=_params(("parallel",)),
        name="outproj",
    )(x1, o_gla, o_nsa, o_mem, w_out)


_PER_G = NSA_HPG * 3
_NG0 = _HEAD_W + _BODY_W
_MQ0 = _NG0 + NSA_KV_HEADS * _PER_G


def _split_w_in_body(w_ref, head_ref, body_ref, mq_ref, ng_ref):
    w = w_ref[...]
    head_ref[...] = w[:, :_HEAD_W].astype(BF16)
    body_ref[...] = w[:, _HEAD_W:_NG0].astype(BF16)
    mq_ref[...] = w[:, _MQ0:].astype(BF16)
    base = _NG0 // LANES * LANES
    slab = w[:, base:base + LANES]
    lane = lax.broadcasted_iota(jnp.int32, slab.shape, 1)
    ng = jnp.zeros_like(slab)
    for g in range(NSA_KV_HEADS):
        src = _NG0 - base + g * _PER_G
        rolled = pltpu.roll(slab, (g * _NG_PAD - src) % LANES, 1)
        ng = jnp.where((lane >= g * _NG_PAD) & (lane < g * _NG_PAD + _PER_G), rolled, ng)
    ng_ref[...] = ng.astype(BF16)


def _split_w_in(w_in, *, tr=256):
    d, n = w_in.shape
    assert n - _MQ0 == MEM_HEADS * HEAD_DIM and _NG0 // LANES == (_MQ0 - 1) // LANES
    widths = (_HEAD_W, _BODY_W, n - _MQ0, LANES)
    return pl.pallas_call(
        _split_w_in_body,
        grid=(d // tr,),
        in_specs=[pl.BlockSpec((tr, n), lambda i: (i, 0))],
        out_specs=[pl.BlockSpec((tr, w), lambda i: (i, 0)) for w in widths],
        out_shape=[jax.ShapeDtypeStruct((d, w), BF16) for w in widths],
        compiler_params=_params(("parallel",)),
        name="split_w_in",
    )(w_in)


def _layer(x, mem, positions, ffn1_norm, ffn1_w_gate, ffn1_w_up, ffn1_w_down, mix_norm, w_in,
           gla_w_a, gla_b_a, gla_o_norm, nsa_q_norm, nsa_k_norm, nsa_cmp_pos_k, nsa_cmp_w1_k,
           nsa_cmp_w2_k, nsa_cmp_pos_v, nsa_cmp_w1_v, nsa_cmp_w2_v, mem_in_norm, w_mem_kv,
           mem_q_norm, mem_k_norm, w_out, ffn2_norm, ffn2_w_gate, ffn2_w_up, ffn2_w_down, final_norm):
    s, d = x.shape
    row = lambda v: v.reshape(1, -1)
    bf = lambda v: v.astype(BF16)

    x1 = _ffn(x, row(ffn1_norm), ffn1_w_gate, ffn1_w_up, ffn1_w_down)

    half = HEAD_DIM // 2
    inv = ROPE_THETA ** (-jnp.arange(half, dtype=F32) / half)
    inv128 = jnp.concatenate([inv, inv]).reshape(1, HEAD_DIM)
    sgn128 = jnp.concatenate([-jnp.ones((half,), F32), jnp.ones((half,), F32)]).reshape(1, HEAD_DIM)
    kmem, vmem = _memkv(mem, row(mem_in_norm), bf(w_mem_kv), row(mem_k_norm))
    (q, ksel, vselt, kwin, vwint, o_mem, gqk, gv, gr, kcvc, ga, ngt) = _proj(
        x1, row(mix_norm), *_split_w_in(w_in), positions.reshape(s, 1), inv128, sgn128,
        row(nsa_q_norm), nsa_k_norm, row(mem_q_norm), kmem, vmem)

    o_gla = _gla(gqk, gv, gr, ga, gla_w_a, row(gla_b_a), row(gla_o_norm))

    units = s // CMP_STRIDE
    n_cmp = (s - CMP_LEN) // CMP_STRIDE + 1
    cmp_last = jnp.arange(units) * CMP_STRIDE + CMP_LEN - 1
    pos_cmp = positions[jnp.minimum(cmp_last, s - 1)].reshape(units, 1)
    kcmp, vcmpt = _compress(kcvc, bf(nsa_cmp_w1_k), bf(nsa_cmp_w2_k), nsa_cmp_pos_k,
                            bf(nsa_cmp_w1_v), bf(nsa_cmp_w2_v.T), nsa_cmp_pos_v, nsa_k_norm,
                            pos_cmp, inv128, sgn128)
    n_sel = s // SEL_LEN
    cmp_start = np.arange(units) * CMP_STRIDE
    sel_start = np.arange(LANES) * SEL_LEN
    overlap = np.clip(np.minimum(cmp_start[:, None] + CMP_LEN, sel_start[None, :] + SEL_LEN)
                      - np.maximum(cmp_start[:, None], sel_start[None, :]), 0, None) / CMP_STRIDE
    overlap = overlap * (np.arange(units)[:, None] < n_cmp) * (np.arange(LANES)[None, :] < n_sel)
    o_nsa = _nsa(q, kcmp, vcmpt, ksel, vselt, kwin, vwint, ngt, jnp.asarray(overlap.T, BF16))

    x2 = _outproj(x1, o_gla, o_nsa, o_mem, bf(w_out))
    return _ffn(x2, row(ffn2_norm), ffn2_w_gate, ffn2_w_up, ffn2_w_down, row(final_norm))


def kernel(x, mem, positions, ffn1_norm, ffn1_w_gate, ffn1_w_up, ffn1_w_down, mix_norm, w_in, gla_w_a, gla_b_a, gla_o_norm, nsa_q_norm, nsa_k_norm, nsa_cmp_pos_k, nsa_cmp_w1_k, nsa_cmp_w2_k, nsa_cmp_pos_v, nsa_cmp_w1_v, nsa_cmp_w2_v, mem_in_norm, w_mem_kv, mem_q_norm, mem_k_norm, w_out, ffn2_norm, ffn2_w_gate, ffn2_w_up, ffn2_w_down, final_norm):
    depth = ffn1_norm.shape[0]
    batch, s, d = x.shape
    outs = []
    for b in range(batch):
        xb, mem_b, pos_b = (x.reshape(s, d), mem.reshape(mem.shape[1:]), positions.reshape(s)) if batch == 1 \
            else (x[b], mem[b], positions[b])
        for l in range(depth):
            xb = _layer(xb, mem_b, pos_b, ffn1_norm[l], ffn1_w_gate[l], ffn1_w_up[l], ffn1_w_down[l],
                        mix_norm[l], w_in[l], gla_w_a[l], gla_b_a[l], gla_o_norm[l], nsa_q_norm[l],
                        nsa_k_norm[l], nsa_cmp_pos_k[l], nsa_cmp_w1_k[l], nsa_cmp_w2_k[l], nsa_cmp_pos_v[l],
                        nsa_cmp_w1_v[l], nsa_cmp_w2_v[l], mem_in_norm[l], w_mem_kv[l], mem_q_norm[l],
                        mem_k_norm[l], w_out[l], ffn2_norm[l], ffn2_w_gate[l], ffn2_w_up[l], ffn2_w_down[l],
                        final_norm[l])
        outs.append(xb)
    return outs[0].reshape(1, s, d) if batch == 1 else jnp.stack(outs)
```

```python
import functools

import numpy as np
import jax
import jax.numpy as jnp
from jax import lax
from jax.experimental import pallas as pl
from jax.experimental.pallas import tpu as pltpu

F32 = jnp.float32
BF16 = jnp.bfloat16

HEAD_DIM = 128
GLA_HEADS = 4
GLA_DK = 64
GLA_DV = 128
GLA_LOWRANK = 16
GLA_TAU = 16.0
NSA_HEADS = 8
NSA_KV_HEADS = 2
NSA_HPG = NSA_HEADS // NSA_KV_HEADS
CMP_LEN = 32
CMP_STRIDE = 16
SEL_LEN = 64
SEL_TOPK = 16
WINDOW = 512
MEM_HEADS = 4
MACARON_W = 0.5
QBLK = 128
ROPE_THETA = 10000.0
EPS = 1e-6
NEG_INF = -1e30
TINY = 1e-30
FORCE_SCORE = 1e4
LOG2_E = 1.4426950408889634

LANES = 128
VMEM_LIMIT = 56 * 1024 * 1024

GLA_SUB = 16
GLA_ROWS = 128
SEL_KT = 256
SEL_SPAN_TILES = 4
V_ROWS = HEAD_DIM + 16
SEL_MASK_BIAS = -32768.0
SEL_PAD_SCORE = -3e4


def _dot(a, b):
    return jnp.dot(a, b, preferred_element_type=F32)


def _dot_nt(a, b):
    return lax.dot_general(a, b, (((1,), (1,)), ((), ())), preferred_element_type=F32)


def _dot_tn(a, b):
    return lax.dot_general(a, b, (((0,), (0,)), ((), ())), preferred_element_type=F32)


def _split2(x):
    hi = x.astype(BF16)
    return hi, (x - hi.astype(F32)).astype(BF16)


def _rms(x, g):
    return x * lax.rsqrt(jnp.mean(x * x, axis=-1, keepdims=True) + EPS) * g


def _sigmoid(x):
    return 1.0 / (1.0 + jnp.exp(-x))


def _params(sem):
    return pltpu.CompilerParams(dimension_semantics=sem, vmem_limit_bytes=VMEM_LIMIT)


def _resident(shape):
    nd = len(shape)
    return pl.BlockSpec(shape, lambda *_: (0,) * nd, pipeline_mode=pl.Buffered(1))


def _ffn_body(*refs, final, nf):
    if final:
        x_ref, g_ref, wg_ref, wu_ref, wd_ref, fg_ref, o_ref, h_ref = refs
    else:
        x_ref, g_ref, wg_ref, wu_ref, wd_ref, o_ref, h_ref = refs
    f = pl.program_id(1)

    @pl.when(f == 0)
    def _():
        x = x_ref[...]
        h_ref[...] = _rms(x, g_ref[...]).astype(BF16)
        o_ref[...] = x

    h = h_ref[...]
    g = _dot(h, wg_ref[...].astype(BF16))
    u = _dot(h, wu_ref[...].astype(BF16))
    a = (g * _sigmoid(g)) * u * MACARON_W
    o_ref[...] += _dot(a.astype(BF16), wd_ref[...].astype(BF16))

    if final:
        @pl.when(f == nf - 1)
        def _():
            o_ref[...] = _rms(o_ref[...], fg_ref[...])


def _ffn(x, norm_g, wg, wu, wd, final_g=None, *, tm=1024, tf=256):
    s, d = x.shape
    ff = wg.shape[1]
    nf = ff // tf
    final = final_g is not None
    in_specs = [
        pl.BlockSpec((tm, d), lambda i, f: (i, 0)),
        pl.BlockSpec((1, d), lambda i, f: (0, 0)),
        pl.BlockSpec((d, tf), lambda i, f: (0, f)),
        pl.BlockSpec((d, tf), lambda i, f: (0, f)),
        pl.BlockSpec((tf, d), lambda i, f: (f, 0)),
    ]
    args = [x, norm_g, wg, wu, wd]
    if final:
        in_specs.append(pl.BlockSpec((1, d), lambda i, f: (0, 0)))
        args.append(final_g)
    return pl.pallas_call(
        functools.partial(_ffn_body, final=final, nf=nf),
        grid=(s // tm, nf),
        in_specs=in_specs,
        out_specs=pl.BlockSpec((tm, d), lambda i, f: (i, 0)),
        out_shape=jax.ShapeDtypeStruct((s, d), F32),
        scratch_shapes=[pltpu.VMEM((tm, d), BF16)],
        compiler_params=_params(("parallel", "arbitrary")),
        name="ffn_final" if final else "ffn",
    )(*args)


_GQK_W = 2 * GLA_HEADS * GLA_DK
_GV_W = GLA_HEADS * GLA_DV
_H_GV = _GQK_W
_H_GR = _H_GV + _GV_W
_H_GA = _H_GR + _GV_W
_HEAD_W = _H_GA + GLA_LOWRANK
_KV_W = NSA_KV_HEADS * HEAD_DIM
_B_KC = NSA_HEADS * HEAD_DIM
_B_KS = _B_KC + 2 * _KV_W
_B_KW = _B_KS + 2 * _KV_W
_BODY_W = _B_KW + 2 * _KV_W
_NG_PAD = 16


def _rope(x, cos, sin_signed):
    return x * cos + pltpu.roll(x, HEAD_DIM // 2, 1) * sin_signed


def _proj_body(x_ref, g_ref, wh_ref, wb_ref, wmq_ref, wng_ref, pos_ref, inv_ref, sgn_ref, qn_ref, kn_ref,
               mqn_ref, kmem_ref, vmem_ref, q_ref, ksel_ref, vselt_ref, kwin_ref, vwint_ref, omem_ref, gqk_ref,
               gv_ref, gr_ref, kcvc_ref, ga_ref, ngt_ref, *, tm):
    i = pl.program_id(0)
    h = _rms(x_ref[...], g_ref[...]).astype(BF16)

    def proj(w_ref, c0, width):
        return _dot(h, w_ref[:, c0:c0 + width])

    ang = pos_ref[...].astype(F32) * inv_ref[...]
    cos = jnp.cos(ang)
    sin_signed = jnp.sin(ang) * sgn_ref[...]

    def norm_rope(t, gain):
        return _rope(_rms(t, gain), cos, sin_signed)

    scale = HEAD_DIM ** -0.5
    nq = proj(wb_ref, 0, NSA_HEADS * HEAD_DIM)
    for hd in range(NSA_HEADS):
        sl = slice(hd * HEAD_DIM, (hd + 1) * HEAD_DIM)
        q_ref[:, sl] = (norm_rope(nq[:, sl], qn_ref[...]) * (scale * LOG2_E)).astype(BF16)

    tok = i * tm + lax.broadcasted_iota(jnp.int32, (tm, LANES), 0)
    blk = lax.broadcasted_iota(jnp.int32, (tm, LANES), 1)
    onehot = jnp.where(lax.shift_right_logical(tok, 6) == blk, 1.0, 0.0).astype(BF16)
    ksv = proj(wb_ref, _B_KS, 2 * _KV_W)
    kwv = proj(wb_ref, _B_KW, 2 * _KV_W)
    ng = _dot(h, wng_ref[...])
    ones_rows = jnp.where(lax.broadcasted_iota(jnp.int32, (V_ROWS - HEAD_DIM, tm), 0) == 0, 1.0, 0.0).astype(BF16)
    for g in range(NSA_KV_HEADS):
        sl = slice(g * HEAD_DIM, (g + 1) * HEAD_DIM)
        slv = slice(_KV_W + g * HEAD_DIM, _KV_W + (g + 1) * HEAD_DIM)
        ksel_ref[g, :, 0:HEAD_DIM] = norm_rope(ksv[:, sl], kn_ref[1:2, :]).astype(BF16)
        ksel_ref[g, :, HEAD_DIM:2 * HEAD_DIM] = onehot
        kwin_ref[g] = norm_rope(kwv[:, sl], kn_ref[2:3, :]).astype(BF16)
        for c in range(tm // LANES):
            rc = slice(c * LANES, (c + 1) * LANES)
            vselt_ref[g, 0:HEAD_DIM, rc] = ksv[rc, slv].T.astype(BF16)
            vwint_ref[g, 0:HEAD_DIM, rc] = kwv[rc, slv].T.astype(BF16)
            ngt_ref[g, :, rc] = ng[rc, :].T[g * _NG_PAD:(g + 1) * _NG_PAD, :]
        vselt_ref[g, HEAD_DIM:V_ROWS, :] = ones_rows
        vwint_ref[g, HEAD_DIM:V_ROWS, :] = ones_rows

    mq = _dot(h, wmq_ref[...])
    for hd in range(MEM_HEADS):
        sl = slice(hd * HEAD_DIM, (hd + 1) * HEAD_DIM)
        qh = (_rms(mq[:, sl], mqn_ref[...]) * scale).astype(BF16)
        sc = _dot_nt(qh, kmem_ref[:, sl])
        e = jnp.exp(sc - jnp.max(sc, axis=-1, keepdims=True))
        p = e / jnp.sum(e, axis=-1, keepdims=True)
        omem_ref[:, sl] = _dot(p.astype(BF16), vmem_ref[:, sl]).astype(BF16)

    gqk_ref[...] = proj(wh_ref, 0, _GQK_W)
    gv_ref[...] = proj(wh_ref, _H_GV, _GV_W).astype(BF16)
    gr_ref[...] = proj(wh_ref, _H_GR, _GV_W)
    ga_ref[...] = proj(wh_ref, _H_GA, GLA_LOWRANK)
    kcvc = proj(wb_ref, _B_KC, 2 * _KV_W)
    for j in range(2 * NSA_KV_HEADS):
        kcvc_ref[j] = kcvc[:, j * HEAD_DIM:(j + 1) * HEAD_DIM]


def _proj(x1, mix_g, w_head, w_body, w_mq, w_ng, pos_col, inv128, sgn128, q_norm, k_norm, mq_norm, kmem, vmem,
          *, tm=512):
    s, d = x1.shape
    row = lambda w: pl.BlockSpec((tm, w), lambda i: (i, 0))
    grp = lambda w: pl.BlockSpec((NSA_KV_HEADS, tm, w), lambda i: (0, i, 0))
    grpt = pl.BlockSpec((NSA_KV_HEADS, V_ROWS, tm), lambda i: (0, 0, i))
    vt_shape = jax.ShapeDtypeStruct((NSA_KV_HEADS, V_ROWS, s), BF16)
    out_shapes = [
        (jax.ShapeDtypeStruct((s, NSA_HEADS * HEAD_DIM), BF16), row(NSA_HEADS * HEAD_DIM)),
        (jax.ShapeDtypeStruct((NSA_KV_HEADS, s, 2 * HEAD_DIM), BF16), grp(2 * HEAD_DIM)),
        (vt_shape, grpt),
        (jax.ShapeDtypeStruct((NSA_KV_HEADS, s, HEAD_DIM), BF16), grp(HEAD_DIM)),
        (vt_shape, grpt),
        (jax.ShapeDtypeStruct((s, MEM_HEADS * HEAD_DIM), BF16), row(MEM_HEADS * HEAD_DIM)),
        (jax.ShapeDtypeStruct((s, 512), F32), row(512)),
        (jax.ShapeDtypeStruct((s, 512), BF16), row(512)),
        (jax.ShapeDtypeStruct((s, 512), F32), row(512)),
        (jax.ShapeDtypeStruct((2 * NSA_KV_HEADS, s, HEAD_DIM), F32),
         pl.BlockSpec((2 * NSA_KV_HEADS, tm, HEAD_DIM), lambda i: (0, i, 0))),
        (jax.ShapeDtypeStruct((s, GLA_LOWRANK), F32), row(GLA_LOWRANK)),
        (jax.ShapeDtypeStruct((NSA_KV_HEADS, _NG_PAD, s), F32),
         pl.BlockSpec((NSA_KV_HEADS, _NG_PAD, tm), lambda i: (0, 0, i))),
    ]
    return pl.pallas_call(
        functools.partial(_proj_body, tm=tm),
        grid=(s // tm,),
        in_specs=[
            row(d),
            _resident((1, d)),
            _resident(w_head.shape), _resident(w_body.shape), _resident(w_mq.shape), _resident(w_ng.shape),
            pl.BlockSpec((tm, 1), lambda i: (i, 0)),
            _resident((1, LANES)), _resident((1, LANES)),
            _resident((1, HEAD_DIM)), _resident((3, HEAD_DIM)), _resident((1, HEAD_DIM)),
            _resident(kmem.shape), _resident(vmem.shape),
        ],
        out_specs=[o[1] for o in out_shapes],
        out_shape=[o[0] for o in out_shapes],
        compiler_params=_params(("parallel",)),
        name="proj",
    )(x1, mix_g, w_head, w_body, w_mq, w_ng, pos_col, inv128, sgn128, q_norm, k_norm, mq_norm, kmem, vmem)


def _compress_body(kcvc_ref, w1k_ref, w2k_ref, pek_ref, w1v_ref, w2v_ref, pev_ref, kn_ref,
                   pos_ref, inv_ref, sgn_ref, kcmp_ref, vcmp_ref, *, units):
    half = CMP_LEN // 2
    ang = pos_ref[...].astype(F32) * inv_ref[...]
    cos = jnp.cos(ang)
    sin_signed = jnp.sin(ang) * sgn_ref[...]
    for kind, (w1_ref, w2_ref, pe_ref) in enumerate(((w1k_ref, w2k_ref, pek_ref),
                                                     (w1v_ref, w2v_ref, pev_ref))):
        for g in range(NSA_KV_HEADS):
            slab = kind * NSA_KV_HEADS + g
            a = jnp.zeros((units, w1_ref.shape[1]), F32)
            b = jnp.zeros((units, w1_ref.shape[1]), F32)
            for l in range(half):
                t = kcvc_ref[slab, pl.ds(l, units, stride=CMP_STRIDE), :]
                a = a + _dot((t + pe_ref[l:l + 1, :]).astype(BF16),
                             w1_ref[l * HEAD_DIM:(l + 1) * HEAD_DIM, :])
                b = b + _dot((t + pe_ref[half + l:half + l + 1, :]).astype(BF16),
                             w1_ref[(half + l) * HEAD_DIM:(half + l + 1) * HEAD_DIM, :])
            hid = a + pltpu.roll(b, units - 1, 0)
            act = (hid * _sigmoid(hid)).astype(BF16)
            if kind == 0:
                c = _rope(_rms(_dot(act, w2_ref[...]), kn_ref[0:1, :]), cos, sin_signed)
                kcmp_ref[g] = c.astype(BF16)
            else:
                vcmp_ref[g] = _dot_nt(w2_ref[...], act).astype(BF16)


def _compress(kcvc, w1k, w2k, pek, w1v, w2v, pev, k_norm, pos_cmp, inv128, sgn128):
    s = kcvc.shape[1]
    units = s // CMP_STRIDE
    shp = jax.ShapeDtypeStruct((NSA_KV_HEADS, units, HEAD_DIM), BF16)
    shp_t = jax.ShapeDtypeStruct((NSA_KV_HEADS, HEAD_DIM, units), BF16)
    return pl.pallas_call(
        functools.partial(_compress_body, units=units),
        out_shape=[shp, shp_t],
        compiler_params=pltpu.CompilerParams(vmem_limit_bytes=VMEM_LIMIT),
        name="compress",
    )(kcvc, w1k, w2k, pek, w1v, w2v, pev, k_norm, pos_cmp, inv128, sgn128)


def _gla_body(gqk_ref, gv_ref, gr_ref, ga_ref, ga_next_ref, wa_ref, ba_ref, on_ref, tcum_ref, bd_ref, hsel_ref,
              o_ref, st_ref, b_s):
    rows = GLA_ROWS
    npair = GLA_HEADS // 2

    def cum_log_decay(ga):
        z = ba_ref[...]
        for ga_t in _split2(ga):
            for wa_t in _split2(wa_ref[...]):
                z = z + _dot(ga_t, wa_t)
        la = (jnp.minimum(z, 0.0) - jnp.log(1.0 + jnp.exp(-jnp.abs(z)))) / GLA_TAU
        bcum = jnp.zeros_like(la)
        for la_t in _split2(la):
            bcum = bcum + _dot(tcum_ref[...], la_t)
        return bcum * LOG2_E

    @pl.when(pl.program_id(0) == 0)
    def _():
        st_ref[...] = jnp.zeros_like(st_ref)
        b_s[...] = cum_log_decay(ga_ref[...])

    b_all = b_s[...]
    q_all = gqk_ref[:, 0:256] * (GLA_DK ** -0.5)
    k_all = gqk_ref[:, 256:512]
    v_all = gv_ref[...]

    row_i = lax.broadcasted_iota(jnp.int32, (GLA_SUB, LANES), 0)

    sts = [st_ref[p] for p in range(npair)]
    o_rows = []
    for sb in range(rows // GLA_SUB):
        rs = slice(sb * GLA_SUB, (sb + 1) * GLA_SUB)
        o_pairs = []
        for p in range(npair):
            cs = slice(p * LANES, (p + 1) * LANES)
            vs = slice(p * 2 * GLA_DV, (p + 1) * 2 * GLA_DV)
            qs = q_all[rs, cs]
            kk = k_all[rs, cs]
            bb = b_all[rs, cs]
            vp = v_all[rs, vs]
            vpf = vp.astype(F32)
            blast = bb[GLA_SUB - 1:GLA_SUB, :]
            st = sts[p]
            o_inter = _dot_nt((qs * jnp.exp2(bb)).astype(BF16), st.astype(BF16))
            xs = []
            for j in range(GLA_SUB):
                dlt = jnp.where(row_i >= j, bb - bb[j:j + 1, :], NEG_INF)
                xs.append(qs * jnp.exp2(dlt) * kk[j:j + 1, :])
            red = _dot(jnp.concatenate(xs, axis=0).astype(BF16), hsel_ref[...])
            acc = o_inter
            for j in range(GLA_SUB):
                acc = acc + red[j * GLA_SUB:(j + 1) * GLA_SUB, :] * vpf[j:j + 1, :]
            o_pairs.append(acc)
            kd = (kk * jnp.exp2(blast - bb)).astype(BF16)
            upd = _dot_tn(vp, kd)
            sts[p] = st * jnp.exp2(blast) + upd * bd_ref[...]
        o_rows.append(jnp.concatenate(o_pairs, axis=1))
    for p in range(npair):
        st_ref[p] = sts[p]
    b_s[...] = cum_log_decay(ga_next_ref[...])

    o_all = jnp.concatenate(o_rows, axis=0)
    gr = gr_ref[...]
    for hd in range(GLA_HEADS):
        sl = slice(hd * GLA_DV, (hd + 1) * GLA_DV)
        r = gr[:, sl]
        o_ref[:, sl] = (_rms(o_all[:, sl], on_ref[...]) * (r * _sigmoid(r))).astype(BF16)


def _gla(gqk, gv, gr, ga, wa, ba, o_norm):
    s = gqk.shape[0]
    rows = GLA_ROWS
    idx = np.arange(rows)
    tcum = ((idx[:, None] >= idx[None, :]) & (idx[:, None] // GLA_SUB == idx[None, :] // GLA_SUB))
    tcum = jnp.asarray(tcum, BF16)
    r256 = np.arange(2 * GLA_DV)[:, None] // GLA_DV
    c128 = np.arange(LANES)[None, :] // GLA_DK
    bdmask = jnp.asarray(r256 == c128, F32)
    hsel = jnp.asarray((r256 == c128).T, BF16)
    row = lambda w: pl.BlockSpec((rows, w), lambda i: (i, 0))
    nblk = s // rows
    ga_next = pl.BlockSpec((rows, GLA_LOWRANK), lambda i: (jnp.minimum(i + 1, nblk - 1), 0))
    return pl.pallas_call(
        _gla_body,
        grid=(nblk,),
        in_specs=[row(512), row(512), row(512), row(GLA_LOWRANK), ga_next,
                  _resident(wa.shape), _resident(ba.shape), _resident(o_norm.shape),
                  _resident(tcum.shape), _resident(bdmask.shape), _resident(hsel.shape)],
        out_specs=row(512),
        out_shape=jax.ShapeDtypeStruct((s, GLA_HEADS * GLA_DV), BF16),
        scratch_shapes=[pltpu.VMEM((GLA_HEADS // 2, 2 * GLA_DV, LANES), F32),
                        pltpu.VMEM((rows, GLA_HEADS * GLA_DK), F32)],
        compiler_params=_params(("arbitrary",)),
        name="gla",
    )(gqk, gv, gr, ga, ga, wa, ba, o_norm, tcum, bdmask, hsel)


def _nsa_body(q_ref, kcmp_ref, vcmpt_ref, ksel_ref, vselt_ref, kwin_ref, vwint_ref, ngt_ref, ovlt_ref,
              o_ref, qt_s, s_s, p_s, acc_s, *, n_sel):
    qb = pl.program_id(0)
    t0 = qb * QBLK
    cols = NSA_HPG * QBLK
    gw = NSA_HPG * HEAD_DIM
    ncmp = kcmp_ref.shape[1]
    groups = range(NSA_KV_HEADS)
    span = SEL_SPAN_TILES * SEL_KT

    def tq_of(rows):
        return t0 + (lax.broadcasted_iota(jnp.int32, (rows, cols), 1) & (QBLK - 1))

    def gate_row(gates_t, c):
        return jnp.concatenate([gates_t[3 * hd + c:3 * hd + c + 1, :] for hd in range(NSA_HPG)], axis=1)

    def span_scores(g, j):
        k0 = pl.multiple_of(j * span, span)
        return _dot(ksel_ref[g, pl.ds(k0, span), :], qt_s[g])

    def span_pv(g, j, p):
        k0 = pl.multiple_of(j * span, span)
        return _dot(vselt_ref[g, :, pl.ds(k0, span)], p)

    def prologue(g):
        qt = jnp.concatenate(
            [q_ref[:, g * gw + hd * HEAD_DIM:g * gw + (hd + 1) * HEAD_DIM].astype(F32).T.astype(BF16)
             for hd in range(NSA_HPG)], axis=1)
        gates_t = _sigmoid(ngt_ref[g])

        def win_part(start, length):
            start = pl.multiple_of(jnp.maximum(start, 0), QBLK)
            return (_dot(kwin_ref[g, pl.ds(start, length), :], qt), vwint_ref[g, :, pl.ds(start, length)], start)

        s_c = _dot(kcmp_ref[g], qt)
        s_old, v_old, _ = win_part(t0 - WINDOW, QBLK)
        s_mid, v_mid, mid0 = win_part(t0 - WINDOW + QBLK, WINDOW - QBLK)
        s_dg, v_dg, _ = win_part(t0, QBLK)
        yield

        n_row = lax.broadcasted_iota(jnp.int32, (ncmp, cols), 0)
        valid_c = n_row * CMP_STRIDE + (CMP_LEN - 1) <= tq_of(ncmp)
        s_c = jnp.where(valid_c, s_c, NEG_INF)
        e_c = jnp.where(valid_c, jnp.exp2(s_c - jnp.max(s_c, axis=0, keepdims=True)), 0.0)
        p_c = e_c / jnp.maximum(jnp.sum(e_c, axis=0, keepdims=True), TINY)
        out_pre = gate_row(gates_t, 0) * _dot(vcmpt_ref[g], p_c.astype(BF16))
        psum = p_c[:, 0:QBLK]
        for hd in range(1, NSA_HPG):
            psum = psum + p_c[:, hd * QBLK:(hd + 1) * QBLK]
        imp = jnp.zeros((LANES, QBLK), F32)
        for p_t in _split2(psum):
            imp = imp + _dot(ovlt_ref[...], p_t)
        yield

        tq = t0 + lax.broadcasted_iota(jnp.int32, (LANES, QBLK), 1)
        m_i = lax.broadcasted_iota(jnp.int32, (LANES, QBLK), 0)
        cur = lax.shift_right_logical(tq, 6)
        forced = (m_i == 0) | (m_i == cur) | (m_i == cur - 1)
        causal = m_i * SEL_LEN <= tq
        n_forced = 3
        score = jnp.where(causal, jnp.where(forced, -jnp.inf, imp), -FORCE_SCORE)
        score = jnp.where(m_i < n_sel, score, SEL_PAD_SCORE)
        m_f = m_i.astype(F32)
        bias = jnp.where(forced & causal, 0.0, SEL_MASK_BIAS)
        for _ in range(min(SEL_TOPK, n_sel) - n_forced):
            mx = jnp.max(score, axis=0, keepdims=True)
            first = jnp.min(jnp.where(score == mx, m_f, float(LANES)), axis=0, keepdims=True)
            pick = m_f == first
            bias = jnp.where(pick, 0.0, bias)
            score = jnp.where(pick, -jnp.inf, score)
        bias = bias.astype(BF16)
        qext = jnp.concatenate([qt, jnp.concatenate([bias] * NSA_HPG, axis=1)], axis=0)
        sc0 = _dot(ksel_ref[g, 0:span, :], qext)
        yield

        w_row = lax.broadcasted_iota(jnp.int32, (QBLK, cols), 0)
        tq_w = tq_of(QBLK)
        kp_old = t0 - WINDOW + w_row
        valid_old = (kp_old > tq_w - WINDOW) & (kp_old >= 0)
        s_old = jnp.where(valid_old, s_old, NEG_INF)
        mid_row = mid0 + lax.broadcasted_iota(jnp.int32, (WINDOW - QBLK, cols), 0)
        s_mid = jnp.where(mid_row < t0, s_mid, NEG_INF)
        valid_dg = t0 + w_row <= tq_w
        s_dg = jnp.where(valid_dg, s_dg, NEG_INF)
        m_w = jnp.maximum(jnp.maximum(jnp.max(s_old, axis=0, keepdims=True),
                                      jnp.max(s_mid, axis=0, keepdims=True)),
                          jnp.max(s_dg, axis=0, keepdims=True))
        p_old = jnp.where(valid_old, jnp.exp2(s_old - m_w), 0.0)
        p_mid = jnp.exp2(s_mid - m_w)
        p_dg = jnp.where(valid_dg, jnp.exp2(s_dg - m_w), 0.0)
        acc_w = (_dot(v_old, p_old.astype(BF16)) + _dot(v_mid, p_mid.astype(BF16))
                 + _dot(v_dg, p_dg.astype(BF16)))
        out_pre = out_pre + gate_row(gates_t, 2) * (acc_w[0:HEAD_DIM, :]
                                                    / jnp.maximum(acc_w[HEAD_DIM:HEAD_DIM + 1, :], TINY))

        return out_pre, gate_row(gates_t, 1), qext, sc0

    pre = {}
    gens = {g: prologue(g) for g in groups}
    order = [0] + [g for _ in range(8) for g in groups]
    for g in order:
        if g not in pre:
            try:
                next(gens[g])
            except StopIteration as done:
                pre[g] = done.value
    assert len(pre) == len(groups)
    for g in groups:
        qt_s[g] = pre[g][2]
        s_s[g, 0] = pre[g][3]
        p_s[g, 1] = jnp.zeros((span, cols), BF16)
        acc_s[g] = jnp.zeros((V_ROWS, cols), F32)

    def sel_step(cur, j, ms):
        nxt = 1 - cur
        out = []
        for g in groups:
            pv = span_pv(g, jnp.maximum(j - 1, 0), p_s[g, nxt])
            s = s_s[g, cur]
            m_new = jnp.maximum(ms[g], jnp.max(s, axis=0, keepdims=True))
            p_s[g, cur] = jnp.exp2(s - m_new).astype(BF16)
            acc_s[g] = jnp.exp2(ms[g] - m_new) * (acc_s[g] + pv)
            out.append(m_new)
        for g in groups:
            s_s[g, nxt] = span_scores(g, j + 1)
        return tuple(out)

    def sel_body(j, ms):
        return lax.cond((j & 1) == 0, lambda c: sel_step(0, j, c), lambda c: sel_step(1, j, c), ms)

    n_span = t0 // span
    ms = lax.fori_loop(0, n_span, sel_body, tuple(jnp.full((1, cols), NEG_INF, F32) for _ in groups))

    slot = n_span & 1
    base = pl.multiple_of(n_span * span, span)

    def diag(nk):
        valid = base + lax.broadcasted_iota(jnp.int32, (nk, cols), 0) <= tq_of(nk)
        for g in groups:
            pv = span_pv(g, jnp.maximum(n_span - 1, 0), p_s[g, 1 - slot])
            s = jnp.where(valid, s_s[g, slot, 0:nk, :], NEG_INF)
            m_new = jnp.maximum(ms[g], jnp.max(s, axis=0, keepdims=True))
            p = jnp.where(valid, jnp.exp2(s - m_new), 0.0).astype(BF16)
            acc_s[g] = (jnp.exp2(ms[g] - m_new) * (acc_s[g] + pv)
                        + _dot(vselt_ref[g, :, pl.ds(base, nk)], p))

    lax.cond(t0 - base >= span // 2, lambda: diag(span), lambda: diag(span // 2))

    for g in groups:
        out_pre, gate_sel = pre[g][0:2]
        acc = acc_s[g]
        o_slc = acc[0:HEAD_DIM, :] / jnp.maximum(acc[HEAD_DIM:HEAD_DIM + 1, :], TINY)
        out = out_pre + gate_sel * o_slc
        for hd in range(NSA_HPG):
            o_ref[:, g * gw + hd * HEAD_DIM:g * gw + (hd + 1) * HEAD_DIM] = (
                out[:, hd * QBLK:(hd + 1) * QBLK].T.astype(BF16))


def _nsa(q, kcmp, vcmpt, ksel, vselt, kwin, vwint, ngt, overlap_t):
    s = q.shape[0]
    n_sel = s // SEL_LEN
    span = SEL_SPAN_TILES * SEL_KT
    assert n_sel <= LANES and s % span == 0 and s >= WINDOW + QBLK
    cols = NSA_HPG * QBLK
    ng = NSA_KV_HEADS
    return pl.pallas_call(
        functools.partial(_nsa_body, n_sel=n_sel),
        grid=(s // QBLK,),
        in_specs=[
            pl.BlockSpec((QBLK, NSA_HEADS * HEAD_DIM), lambda b: (b, 0)),
            _resident(kcmp.shape), _resident(vcmpt.shape),
            _resident(ksel.shape), _resident(vselt.shape), _resident(kwin.shape), _resident(vwint.shape),
            pl.BlockSpec((ng, _NG_PAD, QBLK), lambda b: (0, 0, b)),
            _resident(overlap_t.shape),
        ],
        out_specs=pl.BlockSpec((QBLK, NSA_HEADS * HEAD_DIM), lambda b: (b, 0)),
        out_shape=jax.ShapeDtypeStruct((s, NSA_HEADS * HEAD_DIM), BF16),
        scratch_shapes=[pltpu.VMEM((ng, 2 * HEAD_DIM, cols), BF16),
                        pltpu.VMEM((ng, 2, span, cols), F32),
                        pltpu.VMEM((ng, 2, span, cols), BF16),
                        pltpu.VMEM((ng, V_ROWS, cols), F32)],
        compiler_params=_params(("arbitrary",)),
        name="nsa",
    )(q, kcmp, vcmpt, ksel, vselt, kwin, vwint, ngt, overlap_t)


def _memkv_body(mem_ref, g_ref, w_ref, kn_ref, k_ref, v_ref):
    kv = _dot(_rms(mem_ref[...], g_ref[...]).astype(BF16), w_ref[...])
    width = MEM_HEADS * HEAD_DIM
    for hd in range(MEM_HEADS):
        sl = slice(hd * HEAD_DIM, (hd + 1) * HEAD_DIM)
        k_ref[:, sl] = _rms(kv[:, sl], kn_ref[...]).astype(BF16)
    v_ref[...] = kv[:, width:].astype(BF16)


def _memkv(mem, in_g, w_kv, k_norm):
    m = mem.shape[0]
    shp = jax.ShapeDtypeStruct((m, MEM_HEADS * HEAD_DIM), BF16)
    return pl.pallas_call(
        _memkv_body, out_shape=[shp, shp],
        compiler_params=pltpu.CompilerParams(vmem_limit_bytes=VMEM_LIMIT),
        name="memkv",
    )(mem, in_g, w_kv, k_norm)


def _outproj_body(x_ref, a_ref, b_ref, c_ref, w_ref, o_ref):
    na, nb = a_ref.shape[1], b_ref.shape[1]
    o_ref[...] = (x_ref[...] + _dot(a_ref[...], w_ref[0:na, :]) + _dot(b_ref[...], w_ref[na:na + nb, :])
                  + _dot(c_ref[...], w_ref[na + nb:, :]))


def _outproj(x1, o_gla, o_nsa, o_mem, w_out, *, tm=512):
    s, d = x1.shape
    row = lambda w: pl.BlockSpec((tm, w), lambda i: (i, 0))
    return pl.pallas_call(
        _outproj_body,
        grid=(s // tm,),
        in_specs=[row(d), row(o_gla.shape[1]), row(o_nsa.shape[1]), row(o_mem.shape[1]),
                  _resident(w_out.shape)],
        out_specs=row(d),
        out_shape=jax.ShapeDtypeStruct((s, d), F32),
        compiler_params=_params(("parallel",)),
        name="outproj",
    )(x1, o_gla, o_nsa, o_mem, w_out)


_PER_G = NSA_HPG * 3
_NG0 = _HEAD_W + _BODY_W
_MQ0 = _NG0 + NSA_KV_HEADS * _PER_G


def _split_w_in_body(w_ref, head_ref, body_ref, mq_ref, ng_ref):
    w = w_ref[...]
    head_ref[...] = w[:, :_HEAD_W].astype(BF16)
    body_ref[...] = w[:, _HEAD_W:_NG0].astype(BF16)
    mq_ref[...] = w[:, _MQ0:].astype(BF16)
    base = _NG0 // LANES * LANES
    slab = w[:, base:base + LANES]
    lane = lax.broadcasted_iota(jnp.int32, slab.shape, 1)
    ng = jnp.zeros_like(slab)
    for g in range(NSA_KV_HEADS):
        src = _NG0 - base + g * _PER_G
        rolled = pltpu.roll(slab, (g * _NG_PAD - src) % LANES, 1)
        ng = jnp.where((lane >= g * _NG_PAD) & (lane < g * _NG_PAD + _PER_G), rolled, ng)
    ng_ref[...] = ng.astype(BF16)


def _split_w_in(w_in, *, tr=256):
    d, n = w_in.shape
    assert n - _MQ0 == MEM_HEADS * HEAD_DIM and _NG0 // LANES == (_MQ0 - 1) // LANES
    widths = (_HEAD_W, _BODY_W, n - _MQ0, LANES)
    return pl.pallas_call(
        _split_w_in_body,
        grid=(d // tr,),
        in_specs=[pl.BlockSpec((tr, n), lambda i: (i, 0))],
        out_specs=[pl.BlockSpec((tr, w), lambda i: (i, 0)) for w in widths],
        out_shape=[jax.ShapeDtypeStruct((d, w), BF16) for w in widths],
        compiler_params=_params(("parallel",)),
        name="split_w_in",
    )(w_in)


def _layer(x, mem, positions, ffn1_norm, ffn1_w_gate, ffn1_w_up, ffn1_w_down, mix_norm, w_in,
           gla_w_a, gla_b_a, gla_o_norm, nsa_q_norm, nsa_k_norm, nsa_cmp_pos_k, nsa_cmp_w1_k,
           nsa_cmp_w2_k, nsa_cmp_pos_v, nsa_cmp_w1_v, nsa_cmp_w2_v, mem_in_norm, w_mem_kv,
           mem_q_norm, mem_k_norm, w_out, ffn2_norm, ffn2_w_gate, ffn2_w_up, ffn2_w_down, final_norm):
    s, d = x.shape
    row = lambda v: v.reshape(1, -1)
    bf = lambda v: v.astype(BF16)

    x1 = _ffn(x, row(ffn1_norm), ffn1_w_gate, ffn1_w_up, ffn1_w_down)

    half = HEAD_DIM // 2
    inv = ROPE_THETA ** (-jnp.arange(half, dtype=F32) / half)
    inv128 = jnp.concatenate([inv, inv]).reshape(1, HEAD_DIM)
    sgn128 = jnp.concatenate([-jnp.ones((half,), F32), jnp.ones((half,), F32)]).reshape(1, HEAD_DIM)
    kmem, vmem = _memkv(mem, row(mem_in_norm), bf(w_mem_kv), row(mem_k_norm))
    (q, ksel, vselt, kwin, vwint, o_mem, gqk, gv, gr, kcvc, ga, ngt) = _proj(
        x1, row(mix_norm), *_split_w_in(w_in), positions.reshape(s, 1), inv128, sgn128,
        row(nsa_q_norm), nsa_k_norm, row(mem_q_norm), kmem, vmem)

    o_gla = _gla(gqk, gv, gr, ga, gla_w_a, row(gla_b_a), row(gla_o_norm))

    units = s // CMP_STRIDE
    n_cmp = (s - CMP_LEN) // CMP_STRIDE + 1
    cmp_last = jnp.arange(units) * CMP_STRIDE + CMP_LEN - 1
    pos_cmp = positions[jnp.minimum(cmp_last, s - 1)].reshape(units, 1)
    kcmp, vcmpt = _compress(kcvc, bf(nsa_cmp_w1_k), bf(nsa_cmp_w2_k), nsa_cmp_pos_k,
                            bf(nsa_cmp_w1_v), bf(nsa_cmp_w2_v.T), nsa_cmp_pos_v, nsa_k_norm,
                            pos_cmp, inv128, sgn128)
    n_sel = s // SEL_LEN
    cmp_start = np.arange(units) * CMP_STRIDE
    sel_start = np.arange(LANES) * SEL_LEN
    overlap = np.clip(np.minimum(cmp_start[:, None] + CMP_LEN, sel_start[None, :] + SEL_LEN)
                      - np.maximum(cmp_start[:, None], sel_start[None, :]), 0, None) / CMP_STRIDE
    overlap = overlap * (np.arange(units)[:, None] < n_cmp) * (np.arange(LANES)[None, :] < n_sel)
    o_nsa = _nsa(q, kcmp, vcmpt, ksel, vselt, kwin, vwint, ngt, jnp.asarray(overlap.T, BF16))

    x2 = _outproj(x1, o_gla, o_nsa, o_mem, bf(w_out))
    return _ffn(x2, row(ffn2_norm), ffn2_w_gate, ffn2_w_up, ffn2_w_down, row(final_norm))


def kernel(x, mem, positions, ffn1_norm, ffn1_w_gate, ffn1_w_up, ffn1_w_down, mix_norm, w_in, gla_w_a, gla_b_a, gla_o_norm, nsa_q_norm, nsa_k_norm, nsa_cmp_pos_k, nsa_cmp_w1_k, nsa_cmp_w2_k, nsa_cmp_pos_v, nsa_cmp_w1_v, nsa_cmp_w2_v, mem_in_norm, w_mem_kv, mem_q_norm, mem_k_norm, w_out, ffn2_norm, ffn2_w_gate, ffn2_w_up, ffn2_w_down, final_norm):
    depth = ffn1_norm.shape[0]
    batch, s, d = x.shape
    outs = []
    for b in range(batch):
        xb, mem_b, pos_b = (x.reshape(s, d), mem.reshape(mem.shape[1:]), positions.reshape(s)) if batch == 1 \
            else (x[b], mem[b], positions[b])
        for l in range(depth):
            xb = _layer(xb, mem_b, pos_b, ffn1_norm[l], ffn1_w_gate[l], ffn1_w_up[l], ffn1_w_down[l],
                        mix_norm[l], w_in[l], gla_w_a[l], gla_b_a[l], gla_o_norm[l], nsa_q_norm[l],
                        nsa_k_norm[l], nsa_cmp_pos_k[l], nsa_cmp_w1_k[l], nsa_cmp_w2_k[l], nsa_cmp_pos_v[l],
                        nsa_cmp_w1_v[l], nsa_cmp_w2_v[l], mem_in_norm[l], w_mem_kv[l], mem_q_norm[l],
                        mem_k_norm[l], w_out[l], ffn2_norm[l], ffn2_w_gate[l], ffn2_w_up[l], ffn2_w_down[l],
                        final_norm[l])
        outs.append(xb)
    return outs[0].reshape(1, s, d) if batch == 1 else jnp.stack(outs)
```

```python
import functools

import numpy as np
import jax
import jax.numpy as jnp
from jax import lax
from jax.experimental import pallas as pl
from jax.experimental.pallas import tpu as pltpu

F32 = jnp.float32
BF16 = jnp.bfloat16

HEAD_DIM = 128
GLA_HEADS = 4
GLA_DK = 64
GLA_DV = 128
GLA_LOWRANK = 16
GLA_TAU = 16.0
NSA_HEADS = 8
NSA_KV_HEADS = 2
NSA_HPG = NSA_HEADS // NSA_KV_HEADS
CMP_LEN = 32
CMP_STRIDE = 16
SEL_LEN = 64
SEL_TOPK = 16
WINDOW = 512
MEM_HEADS = 4
MACARON_W = 0.5
QBLK = 128
ROPE_THETA = 10000.0
EPS = 1e-6
NEG_INF = -1e30
TINY = 1e-30
FORCE_SCORE = 1e4
LOG2_E = 1.4426950408889634

LANES = 128
VMEM_LIMIT = 56 * 1024 * 1024

GLA_SUB = 16
GLA_ROWS = 128
SEL_KT = 256
SEL_SPAN_TILES = 4
V_ROWS = HEAD_DIM + 16
SEL_MASK_BIAS = -32768.0
SEL_PAD_SCORE = -3e4


def _dot(a, b):
    return jnp.dot(a, b, preferred_element_type=F32)


def _dot_nt(a, b):
    return lax.dot_general(a, b, (((1,), (1,)), ((), ())), preferred_element_type=F32)


def _dot_tn(a, b):
    return lax.dot_general(a, b, (((0,), (0,)), ((), ())), preferred_element_type=F32)


def _split2(x):
    hi = x.astype(BF16)
    return hi, (x - hi.astype(F32)).astype(BF16)


def _rms(x, g):
    return x * lax.rsqrt(jnp.mean(x * x, axis=-1, keepdims=True) + EPS) * g


def _sigmoid(x):
    return 1.0 / (1.0 + jnp.exp(-x))


def _params(sem):
    return pltpu.CompilerParams(dimension_semantics=sem, vmem_limit_bytes=VMEM_LIMIT)


def _resident(shape):
    nd = len(shape)
    return pl.BlockSpec(shape, lambda *_: (0,) * nd, pipeline_mode=pl.Buffered(1))


def _ffn_body(*refs, final, nf):
    if final:
        x_ref, g_ref, wg_ref, wu_ref, wd_ref, fg_ref, o_ref, h_ref = refs
    else:
        x_ref, g_ref, wg_ref, wu_ref, wd_ref, o_ref, h_ref = refs
    f = pl.program_id(1)

    @pl.when(f == 0)
    def _():
        x = x_ref[...]
        h_ref[...] = _rms(x, g_ref[...]).astype(BF16)
        o_ref[...] = x

    h = h_ref[...]
    g = _dot(h, wg_ref[...].astype(BF16))
    u = _dot(h, wu_ref[...].astype(BF16))
    a = (g * _sigmoid(g)) * u * MACARON_W
    o_ref[...] += _dot(a.astype(BF16), wd_ref[...].astype(BF16))

    if final:
        @pl.when(f == nf - 1)
        def _():
            o_ref[...] = _rms(o_ref[...], fg_ref[...])


def _ffn(x, norm_g, wg, wu, wd, final_g=None, *, tm=1024, tf=256):
    s, d = x.shape
    ff = wg.shape[1]
    nf = ff // tf
    final = final_g is not None
    in_specs = [
        pl.BlockSpec((tm, d), lambda i, f: (i, 0)),
        pl.BlockSpec((1, d), lambda i, f: (0, 0)),
        pl.BlockSpec((d, tf), lambda i, f: (0, f)),
        pl.BlockSpec((d, tf), lambda i, f: (0, f)),
        pl.BlockSpec((tf, d), lambda i, f: (f, 0)),
    ]
    args = [x, norm_g, wg, wu, wd]
    if final:
        in_specs.append(pl.BlockSpec((1, d), lambda i, f: (0, 0)))
        args.append(final_g)
    return pl.pallas_call(
        functools.partial(_ffn_body, final=final, nf=nf),
        grid=(s // tm, nf),
        in_specs=in_specs,
        out_specs=pl.BlockSpec((tm, d), lambda i, f: (i, 0)),
        out_shape=jax.ShapeDtypeStruct((s, d), F32),
        scratch_shapes=[pltpu.VMEM((tm, d), BF16)],
        compiler_params=_params(("parallel", "arbitrary")),
        name="ffn_final" if final else "ffn",
    )(*args)


_GQK_W = 2 * GLA_HEADS * GLA_DK
_GV_W = GLA_HEADS * GLA_DV
_KV_W = NSA_KV_HEADS * HEAD_DIM
_PER_G = NSA_HPG * 3
_NG_PAD = 16
_R_GQK = 0
_R_GV = _R_GQK + _GQK_W
_R_GR = _R_GV + _GV_W
_R_GA = _R_GR + _GV_W
_R_NQ = _R_GA + GLA_LOWRANK
_R_KC = _R_NQ + NSA_HEADS * HEAD_DIM
_R_KS = _R_KC + 2 * _KV_W
_R_VS = _R_KS + _KV_W
_R_KW = _R_VS + _KV_W
_R_VW = _R_KW + _KV_W
_R_NG = _R_VW + _KV_W
_R_MQ = _R_NG + NSA_KV_HEADS * _PER_G
_R_END = _R_MQ + MEM_HEADS * HEAD_DIM


def _rope(x, cos, sin_signed):
    return x * cos + pltpu.roll(x, HEAD_DIM // 2, 1) * sin_signed


def _proj_body(x_ref, g_ref, wt_ref, wmq_ref, pos_ref, inv_ref, sgn_ref, qn_ref, kn_ref,
               mqn_ref, kmem_ref, vmem_ref, q_ref, ksel_ref, vselt_ref, kwin_ref, vwint_ref, omem_ref, gqk_ref,
               gv_ref, gr_ref, kcvc_ref, ga_ref, ngt_ref, *, tm):
    i = pl.program_id(0)
    h = _rms(x_ref[...], g_ref[...]).astype(BF16)

    def proj(r0, width):
        return _dot_nt(h, wt_ref[r0:r0 + width, :])

    def proj_t(r0, width):
        return _dot_nt(wt_ref[r0:r0 + width, :], h)

    ang = pos_ref[...].astype(F32) * inv_ref[...]
    cos = jnp.cos(ang)
    sin_signed = jnp.sin(ang) * sgn_ref[...]

    def norm_rope(t, gain):
        return _rope(_rms(t, gain), cos, sin_signed)

    scale = HEAD_DIM ** -0.5
    nq = proj(_R_NQ, NSA_HEADS * HEAD_DIM)
    for hd in range(NSA_HEADS):
        sl = slice(hd * HEAD_DIM, (hd + 1) * HEAD_DIM)
        q_ref[:, sl] = (norm_rope(nq[:, sl], qn_ref[...]) * (scale * LOG2_E)).astype(BF16)

    tok = i * tm + lax.broadcasted_iota(jnp.int32, (tm, LANES), 0)
    blk = lax.broadcasted_iota(jnp.int32, (tm, LANES), 1)
    onehot = jnp.where(lax.shift_right_logical(tok, 6) == blk, 1.0, 0.0).astype(BF16)
    ks = proj(_R_KS, _KV_W)
    kw = proj(_R_KW, _KV_W)
    vst = proj_t(_R_VS, _KV_W)
    vwt = proj_t(_R_VW, _KV_W)
    ngt = proj_t(_R_NG, NSA_KV_HEADS * _PER_G)
    ones_rows = jnp.where(lax.broadcasted_iota(jnp.int32, (V_ROWS - HEAD_DIM, tm), 0) == 0, 1.0, 0.0).astype(BF16)
    for g in range(NSA_KV_HEADS):
        sl = slice(g * HEAD_DIM, (g + 1) * HEAD_DIM)
        ksel_ref[g, :, 0:HEAD_DIM] = norm_rope(ks[:, sl], kn_ref[1:2, :]).astype(BF16)
        ksel_ref[g, :, HEAD_DIM:2 * HEAD_DIM] = onehot
        kwin_ref[g] = norm_rope(kw[:, sl], kn_ref[2:3, :]).astype(BF16)
        vselt_ref[g, 0:HEAD_DIM, :] = vst[sl, :].astype(BF16)
        vwint_ref[g, 0:HEAD_DIM, :] = vwt[sl, :].astype(BF16)
        vselt_ref[g, HEAD_DIM:V_ROWS, :] = ones_rows
        vwint_ref[g, HEAD_DIM:V_ROWS, :] = ones_rows
        ngt_ref[g, 0:_PER_G, :] = ngt[g * _PER_G:(g + 1) * _PER_G, :]
        ngt_ref[g, _PER_G:_NG_PAD, :] = jnp.zeros((_NG_PAD - _PER_G, tm), F32)

    mq = _dot_nt(h, wmq_ref[...])
    for hd in range(MEM_HEADS):
        sl = slice(hd * HEAD_DIM, (hd + 1) * HEAD_DIM)
        qh = (_rms(mq[:, sl], mqn_ref[...]) * scale).astype(BF16)
        sc = _dot_nt(qh, kmem_ref[:, sl])
        e = jnp.exp(sc - jnp.max(sc, axis=-1, keepdims=True))
        p = e / jnp.sum(e, axis=-1, keepdims=True)
        omem_ref[:, sl] = _dot(p.astype(BF16), vmem_ref[:, sl]).astype(BF16)

    gqk_ref[...] = proj(_R_GQK, _GQK_W)
    gv_ref[...] = proj(_R_GV, _GV_W).astype(BF16)
    gr_ref[...] = proj(_R_GR, _GV_W)
    ga_ref[...] = proj(_R_GA, GLA_LOWRANK)
    kcvc = proj(_R_KC, 2 * _KV_W)
    for j in range(2 * NSA_KV_HEADS):
        kcvc_ref[j] = kcvc[:, j * HEAD_DIM:(j + 1) * HEAD_DIM]


def _proj(x1, mix_g, w_t, w_mq_t, pos_col, inv128, sgn128, q_norm, k_norm, mq_norm, kmem, vmem, *, tm=512):
    s, d = x1.shape
    row = lambda w: pl.BlockSpec((tm, w), lambda i: (i, 0))
    grp = lambda w: pl.BlockSpec((NSA_KV_HEADS, tm, w), lambda i: (0, i, 0))
    grpt = pl.BlockSpec((NSA_KV_HEADS, V_ROWS, tm), lambda i: (0, 0, i))
    vt_shape = jax.ShapeDtypeStruct((NSA_KV_HEADS, V_ROWS, s), BF16)
    out_shapes = [
        (jax.ShapeDtypeStruct((s, NSA_HEADS * HEAD_DIM), BF16), row(NSA_HEADS * HEAD_DIM)),
        (jax.ShapeDtypeStruct((NSA_KV_HEADS, s, 2 * HEAD_DIM), BF16), grp(2 * HEAD_DIM)),
        (vt_shape, grpt),
        (jax.ShapeDtypeStruct((NSA_KV_HEADS, s, HEAD_DIM), BF16), grp(HEAD_DIM)),
        (vt_shape, grpt),
        (jax.ShapeDtypeStruct((s, MEM_HEADS * HEAD_DIM), BF16), row(MEM_HEADS * HEAD_DIM)),
        (jax.ShapeDtypeStruct((s, 512), F32), row(512)),
        (jax.ShapeDtypeStruct((s, 512), BF16), row(512)),
        (jax.ShapeDtypeStruct((s, 512), F32), row(512)),
        (jax.ShapeDtypeStruct((2 * NSA_KV_HEADS, s, HEAD_DIM), F32),
         pl.BlockSpec((2 * NSA_KV_HEADS, tm, HEAD_DIM), lambda i: (0, i, 0))),
        (jax.ShapeDtypeStruct((s, GLA_LOWRANK), F32), row(GLA_LOWRANK)),
        (jax.ShapeDtypeStruct((NSA_KV_HEADS, _NG_PAD, s), F32),
         pl.BlockSpec((NSA_KV_HEADS, _NG_PAD, tm), lambda i: (0, 0, i))),
    ]
    return pl.pallas_call(
        functools.partial(_proj_body, tm=tm),
        grid=(s // tm,),
        in_specs=[
            row(d),
            _resident((1, d)),
            _resident(w_t.shape), _resident(w_mq_t.shape),
            pl.BlockSpec((tm, 1), lambda i: (i, 0)),
            _resident((1, LANES)), _resident((1, LANES)),
            _resident((1, HEAD_DIM)), _resident((3, HEAD_DIM)), _resident((1, HEAD_DIM)),
            _resident(kmem.shape), _resident(vmem.shape),
        ],
        out_specs=[o[1] for o in out_shapes],
        out_shape=[o[0] for o in out_shapes],
        compiler_params=_params(("parallel",)),
        name="proj",
    )(x1, mix_g, w_t, w_mq_t, pos_col, inv128, sgn128, q_norm, k_norm, mq_norm, kmem, vmem)


def _compress_body(kcvc_ref, w1k_ref, w2k_ref, pek_ref, w1v_ref, w2v_ref, pev_ref, kn_ref,
                   pos_ref, inv_ref, sgn_ref, kcmp_ref, vcmp_ref, *, units):
    half = CMP_LEN // 2
    ang = pos_ref[...].astype(F32) * inv_ref[...]
    cos = jnp.cos(ang)
    sin_signed = jnp.sin(ang) * sgn_ref[...]
    for kind, (w1_ref, w2_ref, pe_ref) in enumerate(((w1k_ref, w2k_ref, pek_ref),
                                                     (w1v_ref, w2v_ref, pev_ref))):
        for g in range(NSA_KV_HEADS):
            slab = kind * NSA_KV_HEADS + g
            a = jnp.zeros((units, w1_ref.shape[1]), F32)
            b = jnp.zeros((units, w1_ref.shape[1]), F32)
            for l in range(half):
                t = kcvc_ref[slab, pl.ds(l, units, stride=CMP_STRIDE), :]
                a = a + _dot((t + pe_ref[l:l + 1, :]).astype(BF16),
                             w1_ref[l * HEAD_DIM:(l + 1) * HEAD_DIM, :])
                b = b + _dot((t + pe_ref[half + l:half + l + 1, :]).astype(BF16),
                             w1_ref[(half + l) * HEAD_DIM:(half + l + 1) * HEAD_DIM, :])
            hid = a + pltpu.roll(b, units - 1, 0)
            act = (hid * _sigmoid(hid)).astype(BF16)
            if kind == 0:
                c = _rope(_rms(_dot(act, w2_ref[...]), kn_ref[0:1, :]), cos, sin_signed)
                kcmp_ref[g] = c.astype(BF16)
            else:
                vcmp_ref[g] = _dot_nt(w2_ref[...], act).astype(BF16)


def _compress(kcvc, w1k, w2k, pek, w1v, w2v, pev, k_norm, pos_cmp, inv128, sgn128):
    s = kcvc.shape[1]
    units = s // CMP_STRIDE
    shp = jax.ShapeDtypeStruct((NSA_KV_HEADS, units, HEAD_DIM), BF16)
    shp_t = jax.ShapeDtypeStruct((NSA_KV_HEADS, HEAD_DIM, units), BF16)
    return pl.pallas_call(
        functools.partial(_compress_body, units=units),
        out_shape=[shp, shp_t],
        compiler_params=pltpu.CompilerParams(vmem_limit_bytes=VMEM_LIMIT),
        name="compress",
    )(kcvc, w1k, w2k, pek, w1v, w2v, pev, k_norm, pos_cmp, inv128, sgn128)


def _gla_body(gqk_ref, gv_ref, gr_ref, ga_ref, ga_next_ref, wa_ref, ba_ref, on_ref, tcum_ref, bd_ref, hsel_ref,
              o_ref, st_ref, b_s):
    rows = GLA_ROWS
    npair = GLA_HEADS // 2

    def cum_log_decay(ga):
        z = ba_ref[...]
        for ga_t in _split2(ga):
            for wa_t in _split2(wa_ref[...]):
                z = z + _dot(ga_t, wa_t)
        la = (jnp.minimum(z, 0.0) - jnp.log(1.0 + jnp.exp(-jnp.abs(z)))) / GLA_TAU
        bcum = jnp.zeros_like(la)
        for la_t in _split2(la):
            bcum = bcum + _dot(tcum_ref[...], la_t)
        return bcum * LOG2_E

    @pl.when(pl.program_id(0) == 0)
    def _():
        st_ref[...] = jnp.zeros_like(st_ref)
        b_s[...] = cum_log_decay(ga_ref[...])

    b_all = b_s[...]
    q_all = gqk_ref[:, 0:256] * (GLA_DK ** -0.5)
    k_all = gqk_ref[:, 256:512]
    v_all = gv_ref[...]

    row_i = lax.broadcasted_iota(jnp.int32, (GLA_SUB, LANES), 0)

    sts = [st_ref[p] for p in range(npair)]
    o_rows = []
    for sb in range(rows // GLA_SUB):
        rs = slice(sb * GLA_SUB, (sb + 1) * GLA_SUB)
        o_pairs = []
        for p in range(npair):
            cs = slice(p * LANES, (p + 1) * LANES)
            vs = slice(p * 2 * GLA_DV, (p + 1) * 2 * GLA_DV)
            qs = q_all[rs, cs]
            kk = k_all[rs, cs]
            bb = b_all[rs, cs]
            vp = v_all[rs, vs]
            vpf = vp.astype(F32)
            blast = bb[GLA_SUB - 1:GLA_SUB, :]
            st = sts[p]
            o_inter = _dot_nt((qs * jnp.exp2(bb)).astype(BF16), st.astype(BF16))
            xs = []
            for j in range(GLA_SUB):
                dlt = jnp.where(row_i >= j, bb - bb[j:j + 1, :], NEG_INF)
                xs.append(qs * jnp.exp2(dlt) * kk[j:j + 1, :])
            red = _dot(jnp.concatenate(xs, axis=0).astype(BF16), hsel_ref[...])
            acc = o_inter
            for j in range(GLA_SUB):
                acc = acc + red[j * GLA_SUB:(j + 1) * GLA_SUB, :] * vpf[j:j + 1, :]
            o_pairs.append(acc)
            kd = (kk * jnp.exp2(blast - bb)).astype(BF16)
            upd = _dot_tn(vp, kd)
            sts[p] = st * jnp.exp2(blast) + upd * bd_ref[...]
        o_rows.append(jnp.concatenate(o_pairs, axis=1))
    for p in range(npair):
        st_ref[p] = sts[p]
    b_s[...] = cum_log_decay(ga_next_ref[...])

    o_all = jnp.concatenate(o_rows, axis=0)
    gr = gr_ref[...]
    for hd in range(GLA_HEADS):
        sl = slice(hd * GLA_DV, (hd + 1) * GLA_DV)
        r = gr[:, sl]
        o_ref[:, sl] = (_rms(o_all[:, sl], on_ref[...]) * (r * _sigmoid(r))).astype(BF16)


def _gla(gqk, gv, gr, ga, wa, ba, o_norm):
    s = gqk.shape[0]
    rows = GLA_ROWS
    idx = np.arange(rows)
    tcum = ((idx[:, None] >= idx[None, :]) & (idx[:, None] // GLA_SUB == idx[None, :] // GLA_SUB))
    tcum = jnp.asarray(tcum, BF16)
    r256 = np.arange(2 * GLA_DV)[:, None] // GLA_DV
    c128 = np.arange(LANES)[None, :] // GLA_DK
    bdmask = jnp.asarray(r256 == c128, F32)
    hsel = jnp.asarray((r256 == c128).T, BF16)
    row = lambda w: pl.BlockSpec((rows, w), lambda i: (i, 0))
    nblk = s // rows
    ga_next = pl.BlockSpec((rows, GLA_LOWRANK), lambda i: (jnp.minimum(i + 1, nblk - 1), 0))
    return pl.pallas_call(
        _gla_body,
        grid=(nblk,),
        in_specs=[row(512), row(512), row(512), row(GLA_LOWRANK), ga_next,
                  _resident(wa.shape), _resident(ba.shape), _resident(o_norm.shape),
                  _resident(tcum.shape), _resident(bdmask.shape), _resident(hsel.shape)],
        out_specs=row(512),
        out_shape=jax.ShapeDtypeStruct((s, GLA_HEADS * GLA_DV), BF16),
        scratch_shapes=[pltpu.VMEM((GLA_HEADS // 2, 2 * GLA_DV, LANES), F32),
                        pltpu.VMEM((rows, GLA_HEADS * GLA_DK), F32)],
        compiler_params=_params(("arbitrary",)),
        name="gla",
    )(gqk, gv, gr, ga, ga, wa, ba, o_norm, tcum, bdmask, hsel)


def _nsa_body(q_ref, kcmp_ref, vcmpt_ref, ksel_ref, vselt_ref, kwin_ref, vwint_ref, ngt_ref, ovlt_ref,
              o_ref, qt_s, s_s, p_s, acc_s, *, n_sel):
    qb = pl.program_id(0)
    t0 = qb * QBLK
    cols = NSA_HPG * QBLK
    gw = NSA_HPG * HEAD_DIM
    ncmp = kcmp_ref.shape[1]
    groups = range(NSA_KV_HEADS)
    span = SEL_SPAN_TILES * SEL_KT

    def tq_of(rows):
        return t0 + (lax.broadcasted_iota(jnp.int32, (rows, cols), 1) & (QBLK - 1))

    def gate_row(gates_t, c):
        return jnp.concatenate([gates_t[3 * hd + c:3 * hd + c + 1, :] for hd in range(NSA_HPG)], axis=1)

    def span_scores(g, j):
        k0 = pl.multiple_of(j * span, span)
        return _dot(ksel_ref[g, pl.ds(k0, span), :], qt_s[g])

    def span_pv(g, j, p):
        k0 = pl.multiple_of(j * span, span)
        return _dot(vselt_ref[g, :, pl.ds(k0, span)], p)

    def prologue(g):
        qt = jnp.concatenate(
            [q_ref[:, g * gw + hd * HEAD_DIM:g * gw + (hd + 1) * HEAD_DIM].astype(F32).T.astype(BF16)
             for hd in range(NSA_HPG)], axis=1)
        gates_t = _sigmoid(ngt_ref[g])

        def win_part(start, length):
            start = pl.multiple_of(jnp.maximum(start, 0), QBLK)
            return (_dot(kwin_ref[g, pl.ds(start, length), :], qt), vwint_ref[g, :, pl.ds(start, length)], start)

        s_c = _dot(kcmp_ref[g], qt)
        s_old, v_old, _ = win_part(t0 - WINDOW, QBLK)
        s_mid, v_mid, mid0 = win_part(t0 - WINDOW + QBLK, WINDOW - QBLK)
        s_dg, v_dg, _ = win_part(t0, QBLK)
        yield

        n_row = lax.broadcasted_iota(jnp.int32, (ncmp, cols), 0)
        valid_c = n_row * CMP_STRIDE + (CMP_LEN - 1) <= tq_of(ncmp)
        s_c = jnp.where(valid_c, s_c, NEG_INF)
        e_c = jnp.where(valid_c, jnp.exp2(s_c - jnp.max(s_c, axis=0, keepdims=True)), 0.0)
        p_c = e_c / jnp.maximum(jnp.sum(e_c, axis=0, keepdims=True), TINY)
        out_pre = gate_row(gates_t, 0) * _dot(vcmpt_ref[g], p_c.astype(BF16))
        psum = p_c[:, 0:QBLK]
        for hd in range(1, NSA_HPG):
            psum = psum + p_c[:, hd * QBLK:(hd + 1) * QBLK]
        imp = jnp.zeros((LANES, QBLK), F32)
        for p_t in _split2(psum):
            imp = imp + _dot(ovlt_ref[...], p_t)
        yield

        tq = t0 + lax.broadcasted_iota(jnp.int32, (LANES, QBLK), 1)
        m_i = lax.broadcasted_iota(jnp.int32, (LANES, QBLK), 0)
        cur = lax.shift_right_logical(tq, 6)
        forced = (m_i == 0) | (m_i == cur) | (m_i == cur - 1)
        causal = m_i * SEL_LEN <= tq
        n_forced = 3
        score = jnp.where(causal, jnp.where(forced, -jnp.inf, imp), -FORCE_SCORE)
        score = jnp.where(m_i < n_sel, score, SEL_PAD_SCORE)
        m_f = m_i.astype(F32)
        bias = jnp.where(forced & causal, 0.0, SEL_MASK_BIAS)
        for _ in range(min(SEL_TOPK, n_sel) - n_forced):
            mx = jnp.max(score, axis=0, keepdims=True)
            first = jnp.min(jnp.where(score == mx, m_f, float(LANES)), axis=0, keepdims=True)
            pick = m_f == first
            bias = jnp.where(pick, 0.0, bias)
            score = jnp.where(pick, -jnp.inf, score)
        bias = bias.astype(BF16)
        qext = jnp.concatenate([qt, jnp.concatenate([bias] * NSA_HPG, axis=1)], axis=0)
        sc0 = _dot(ksel_ref[g, 0:span, :], qext)
        yield

        w_row = lax.broadcasted_iota(jnp.int32, (QBLK, cols), 0)
        tq_w = tq_of(QBLK)
        kp_old = t0 - WINDOW + w_row
        valid_old = (kp_old > tq_w - WINDOW) & (kp_old >= 0)
        s_old = jnp.where(valid_old, s_old, NEG_INF)
        mid_row = mid0 + lax.broadcasted_iota(jnp.int32, (WINDOW - QBLK, cols), 0)
        s_mid = jnp.where(mid_row < t0, s_mid, NEG_INF)
        valid_dg = t0 + w_row <= tq_w
        s_dg = jnp.where(valid_dg, s_dg, NEG_INF)
        m_w = jnp.maximum(jnp.maximum(jnp.max(s_old, axis=0, keepdims=True),
                                      jnp.max(s_mid, axis=0, keepdims=True)),
                          jnp.max(s_dg, axis=0, keepdims=True))
        p_old = jnp.where(valid_old, jnp.exp2(s_old - m_w), 0.0)
        p_mid = jnp.exp2(s_mid - m_w)
        p_dg = jnp.where(valid_dg, jnp.exp2(s_dg - m_w), 0.0)
        acc_w = (_dot(v_old, p_old.astype(BF16)) + _dot(v_mid, p_mid.astype(BF16))
                 + _dot(v_dg, p_dg.astype(BF16)))
        out_pre = out_pre + gate_row(gates_t, 2) * (acc_w[0:HEAD_DIM, :]
                                                    / jnp.maximum(acc_w[HEAD_DIM:HEAD_DIM + 1, :], TINY))

        return out_pre, gate_row(gates_t, 1), qext, sc0

    pre = {}
    gens = {g: prologue(g) for g in groups}
    order = [0] + [g for _ in range(8) for g in groups]
    for g in order:
        if g not in pre:
            try:
                next(gens[g])
            except StopIteration as done:
                pre[g] = done.value
    assert len(pre) == len(groups)
    for g in groups:
        qt_s[g] = pre[g][2]
        s_s[g, 0] = pre[g][3]
        p_s[g, 1] = jnp.zeros((span, cols), BF16)
        acc_s[g] = jnp.zeros((V_ROWS, cols), F32)

    def sel_step(cur, j, ms):
        nxt = 1 - cur
        out = []
        for g in groups:
            pv = span_pv(g, jnp.maximum(j - 1, 0), p_s[g, nxt])
            s = s_s[g, cur]
            m_new = jnp.maximum(ms[g], jnp.max(s, axis=0, keepdims=True))
            p_s[g, cur] = jnp.exp2(s - m_new).astype(BF16)
            acc_s[g] = jnp.exp2(ms[g] - m_new) * (acc_s[g] + pv)
            out.append(m_new)
        for g in groups:
            s_s[g, nxt] = span_scores(g, j + 1)
        return tuple(out)

    def sel_body(j, ms):
        return lax.cond((j & 1) == 0, lambda c: sel_step(0, j, c), lambda c: sel_step(1, j, c), ms)

    n_span = t0 // span
    ms = lax.fori_loop(0, n_span, sel_body, tuple(jnp.full((1, cols), NEG_INF, F32) for _ in groups))

    slot = n_span & 1
    base = pl.multiple_of(n_span * span, span)

    def diag(nk):
        valid = base + lax.broadcasted_iota(jnp.int32, (nk, cols), 0) <= tq_of(nk)
        for g in groups:
            pv = span_pv(g, jnp.maximum(n_span - 1, 0), p_s[g, 1 - slot])
            s = jnp.where(valid, s_s[g, slot, 0:nk, :], NEG_INF)
            m_new = jnp.maximum(ms[g], jnp.max(s, axis=0, keepdims=True))
            p = jnp.where(valid, jnp.exp2(s - m_new), 0.0).astype(BF16)
            acc_s[g] = (jnp.exp2(ms[g] - m_new) * (acc_s[g] + pv)
                        + _dot(vselt_ref[g, :, pl.ds(base, nk)], p))

    lax.cond(t0 - base >= span // 2, lambda: diag(span), lambda: diag(span // 2))

    for g in groups:
        out_pre, gate_sel = pre[g][0:2]
        acc = acc_s[g]
        o_slc = acc[0:HEAD_DIM, :] / jnp.maximum(acc[HEAD_DIM:HEAD_DIM + 1, :], TINY)
        out = out_pre + gate_sel * o_slc
        for hd in range(NSA_HPG):
            o_ref[:, g * gw + hd * HEAD_DIM:g * gw + (hd + 1) * HEAD_DIM] = (
                out[:, hd * QBLK:(hd + 1) * QBLK].T.astype(BF16))


def _nsa(q, kcmp, vcmpt, ksel, vselt, kwin, vwint, ngt, overlap_t):
    s = q.shape[0]
    n_sel = s // SEL_LEN
    span = SEL_SPAN_TILES * SEL_KT
    assert n_sel <= LANES and s % span == 0 and s >= WINDOW + QBLK
    cols = NSA_HPG * QBLK
    ng = NSA_KV_HEADS
    return pl.pallas_call(
        functools.partial(_nsa_body, n_sel=n_sel),
        grid=(s // QBLK,),
        in_specs=[
            pl.BlockSpec((QBLK, NSA_HEADS * HEAD_DIM), lambda b: (b, 0)),
            _resident(kcmp.shape), _resident(vcmpt.shape),
            _resident(ksel.shape), _resident(vselt.shape), _resident(kwin.shape), _resident(vwint.shape),
            pl.BlockSpec((ng, _NG_PAD, QBLK), lambda b: (0, 0, b)),
            _resident(overlap_t.shape),
        ],
        out_specs=pl.BlockSpec((QBLK, NSA_HEADS * HEAD_DIM), lambda b: (b, 0)),
        out_shape=jax.ShapeDtypeStruct((s, NSA_HEADS * HEAD_DIM), BF16),
        scratch_shapes=[pltpu.VMEM((ng, 2 * HEAD_DIM, cols), BF16),
                        pltpu.VMEM((ng, 2, span, cols), F32),
                        pltpu.VMEM((ng, 2, span, cols), BF16),
                        pltpu.VMEM((ng, V_ROWS, cols), F32)],
        compiler_params=_params(("arbitrary",)),
        name="nsa",
    )(q, kcmp, vcmpt, ksel, vselt, kwin, vwint, ngt, overlap_t)


def _memkv_body(mem_ref, g_ref, w_ref, kn_ref, k_ref, v_ref):
    kv = _dot(_rms(mem_ref[...], g_ref[...]).astype(BF16), w_ref[...])
    width = MEM_HEADS * HEAD_DIM
    for hd in range(MEM_HEADS):
        sl = slice(hd * HEAD_DIM, (hd + 1) * HEAD_DIM)
        k_ref[:, sl] = _rms(kv[:, sl], kn_ref[...]).astype(BF16)
    v_ref[...] = kv[:, width:].astype(BF16)


def _memkv(mem, in_g, w_kv, k_norm):
    m = mem.shape[0]
    shp = jax.ShapeDtypeStruct((m, MEM_HEADS * HEAD_DIM), BF16)
    return pl.pallas_call(
        _memkv_body, out_shape=[shp, shp],
        compiler_params=pltpu.CompilerParams(vmem_limit_bytes=VMEM_LIMIT),
        name="memkv",
    )(mem, in_g, w_kv, k_norm)


def _outproj_body(x_ref, a_ref, b_ref, c_ref, w_ref, o_ref):
    na, nb = a_ref.shape[1], b_ref.shape[1]
    o_ref[...] = (x_ref[...] + _dot(a_ref[...], w_ref[0:na, :]) + _dot(b_ref[...], w_ref[na:na + nb, :])
                  + _dot(c_ref[...], w_ref[na + nb:, :]))


def _outproj(x1, o_gla, o_nsa, o_mem, w_out, *, tm=512):
    s, d = x1.shape
    row = lambda w: pl.BlockSpec((tm, w), lambda i: (i, 0))
    return pl.pallas_call(
        _outproj_body,
        grid=(s // tm,),
        in_specs=[row(d), row(o_gla.shape[1]), row(o_nsa.shape[1]), row(o_mem.shape[1]),
                  _resident(w_out.shape)],
        out_specs=row(d),
        out_shape=jax.ShapeDtypeStruct((s, d), F32),
        compiler_params=_params(("parallel",)),
        name="outproj",
    )(x1, o_gla, o_nsa, o_mem, w_out)


def _layer(x, mem, positions, ffn1_norm, ffn1_w_gate, ffn1_w_up, ffn1_w_down, mix_norm, w_in,
           gla_w_a, gla_b_a, gla_o_norm, nsa_q_norm, nsa_k_norm, nsa_cmp_pos_k, nsa_cmp_w1_k,
           nsa_cmp_w2_k, nsa_cmp_pos_v, nsa_cmp_w1_v, nsa_cmp_w2_v, mem_in_norm, w_mem_kv,
           mem_q_norm, mem_k_norm, w_out, ffn2_norm, ffn2_w_gate, ffn2_w_up, ffn2_w_down, final_norm):
    s, d = x.shape
    row = lambda v: v.reshape(1, -1)
    bf = lambda v: v.astype(BF16)

    x1 = _ffn(x, row(ffn1_norm), ffn1_w_gate, ffn1_w_up, ffn1_w_down)

    half = HEAD_DIM // 2
    inv = ROPE_THETA ** (-jnp.arange(half, dtype=F32) / half)
    inv128 = jnp.concatenate([inv, inv]).reshape(1, HEAD_DIM)
    sgn128 = jnp.concatenate([-jnp.ones((half,), F32), jnp.ones((half,), F32)]).reshape(1, HEAD_DIM)
    kmem, vmem = _memkv(mem, row(mem_in_norm), bf(w_mem_kv), row(mem_k_norm))
    assert w_in.shape[1] == _R_END
    w_t = bf(w_in.T)
    (q, ksel, vselt, kwin, vwint, o_mem, gqk, gv, gr, kcvc, ga, ngt) = _proj(
        x1, row(mix_norm), w_t, w_t[_R_MQ:], positions.reshape(s, 1), inv128, sgn128,
        row(nsa_q_norm), nsa_k_norm, row(mem_q_norm), kmem, vmem)

    o_gla = _gla(gqk, gv, gr, ga, gla_w_a, row(gla_b_a), row(gla_o_norm))

    units = s // CMP_STRIDE
    n_cmp = (s - CMP_LEN) // CMP_STRIDE + 1
    cmp_last = jnp.arange(units) * CMP_STRIDE + CMP_LEN - 1
    pos_cmp = positions[jnp.minimum(cmp_last, s - 1)].reshape(units, 1)
    kcmp, vcmpt = _compress(kcvc, bf(nsa_cmp_w1_k), bf(nsa_cmp_w2_k), nsa_cmp_pos_k,
                            bf(nsa_cmp_w1_v), bf(nsa_cmp_w2_v.T), nsa_cmp_pos_v, nsa_k_norm,
                            pos_cmp, inv128, sgn128)
    n_sel = s // SEL_LEN
    cmp_start = np.arange(units) * CMP_STRIDE
    sel_start = np.arange(LANES) * SEL_LEN
    overlap = np.clip(np.minimum(cmp_start[:, None] + CMP_LEN, sel_start[None, :] + SEL_LEN)
                      - np.maximum(cmp_start[:, None], sel_start[None, :]), 0, None) / CMP_STRIDE
    overlap = overlap * (np.arange(units)[:, None] < n_cmp) * (np.arange(LANES)[None, :] < n_sel)
    o_nsa = _nsa(q, kcmp, vcmpt, ksel, vselt, kwin, vwint, ngt, jnp.asarray(overlap.T, BF16))

    x2 = _outproj(x1, o_gla, o_nsa, o_mem, bf(w_out))
    return _ffn(x2, row(ffn2_norm), ffn2_w_gate, ffn2_w_up, ffn2_w_down, row(final_norm))


def kernel(x, mem, positions, ffn1_norm, ffn1_w_gate, ffn1_w_up, ffn1_w_down, mix_norm, w_in, gla_w_a, gla_b_a, gla_o_norm, nsa_q_norm, nsa_k_norm, nsa_cmp_pos_k, nsa_cmp_w1_k, nsa_cmp_w2_k, nsa_cmp_pos_v, nsa_cmp_w1_v, nsa_cmp_w2_v, mem_in_norm, w_mem_kv, mem_q_norm, mem_k_norm, w_out, ffn2_norm, ffn2_w_gate, ffn2_w_up, ffn2_w_down, final_norm):
    depth = ffn1_norm.shape[0]
    batch, s, d = x.shape
    outs = []
    for b in range(batch):
        xb, mem_b, pos_b = (x.reshape(s, d), mem.reshape(mem.shape[1:]), positions.reshape(s)) if batch == 1 \
            else (x[b], mem[b], positions[b])
        for l in range(depth):
            xb = _layer(xb, mem_b, pos_b, ffn1_norm[l], ffn1_w_gate[l], ffn1_w_up[l], ffn1_w_down[l],
                        mix_norm[l], w_in[l], gla_w_a[l], gla_b_a[l], gla_o_norm[l], nsa_q_norm[l],
                        nsa_k_norm[l], nsa_cmp_pos_k[l], nsa_cmp_w1_k[l], nsa_cmp_w2_k[l], nsa_cmp_pos_v[l],
                        nsa_cmp_w1_v[l], nsa_cmp_w2_v[l], mem_in_norm[l], w_mem_kv[l], mem_q_norm[l],
                        mem_k_norm[l], w_out[l], ffn2_norm[l], ffn2_w_gate[l], ffn2_w_up[l], ffn2_w_down[l],
                        final_norm[l])
        outs.append(xb)
    return outs[0].reshape(1, s, d) if batch == 1 else jnp.stack(outs)
```

```python
import functools

import numpy as np
import jax
import jax.numpy as jnp
from jax import lax
from jax.experimental import pallas as pl
from jax.experimental.pallas import tpu as pltpu

F32 = jnp.float32
BF16 = jnp.bfloat16

HEAD_DIM = 128
GLA_HEADS = 4
GLA_DK = 64
GLA_DV = 128
GLA_LOWRANK = 16
GLA_TAU = 16.0
NSA_HEADS = 8
NSA_KV_HEADS = 2
NSA_HPG = NSA_HEADS // NSA_KV_HEADS
CMP_LEN = 32
CMP_STRIDE = 16
SEL_LEN = 64
SEL_TOPK = 16
WINDOW = 512
MEM_HEADS = 4
MACARON_W = 0.5
QBLK = 128
ROPE_THETA = 10000.0
EPS = 1e-6
NEG_INF = -1e30
TINY = 1e-30
FORCE_SCORE = 1e4
LOG2_E = 1.4426950408889634

LANES = 128
VMEM_LIMIT = 56 * 1024 * 1024

GLA_SUB = 16
GLA_ROWS = 128
SEL_KT = 256
SEL_SPAN_TILES = 4
V_ROWS = HEAD_DIM + 16
SEL_MASK_BIAS = -32768.0
SEL_PAD_SCORE = -3e4


def _dot(a, b):
    return jnp.dot(a, b, preferred_element_type=F32)


def _dot_nt(a, b):
    return lax.dot_general(a, b, (((1,), (1,)), ((), ())), preferred_element_type=F32)


def _dot_tn(a, b):
    return lax.dot_general(a, b, (((0,), (0,)), ((), ())), preferred_element_type=F32)


def _split2(x):
    hi = x.astype(BF16)
    return hi, (x - hi.astype(F32)).astype(BF16)


def _rms(x, g):
    return x * lax.rsqrt(jnp.mean(x * x, axis=-1, keepdims=True) + EPS) * g


def _sigmoid(x):
    return 1.0 / (1.0 + jnp.exp(-x))


def _params(sem):
    return pltpu.CompilerParams(dimension_semantics=sem, vmem_limit_bytes=VMEM_LIMIT)


def _resident(shape):
    nd = len(shape)
    return pl.BlockSpec(shape, lambda *_: (0,) * nd, pipeline_mode=pl.Buffered(1))


def _ffn_body(*refs, final, nf):
    if final:
        x_ref, g_ref, wg_ref, wu_ref, wd_ref, fg_ref, o_ref, h_ref = refs
    else:
        x_ref, g_ref, wg_ref, wu_ref, wd_ref, o_ref, h_ref = refs
    f = pl.program_id(1)

    @pl.when(f == 0)
    def _():
        x = x_ref[...]
        h_ref[...] = _rms(x, g_ref[...]).astype(BF16)
        o_ref[...] = x

    h = h_ref[...]
    g = _dot(h, wg_ref[...].astype(BF16))
    u = _dot(h, wu_ref[...].astype(BF16))
    a = (g * _sigmoid(g)) * u * MACARON_W
    o_ref[...] += _dot(a.astype(BF16), wd_ref[...].astype(BF16))

    if final:
        @pl.when(f == nf - 1)
        def _():
            o_ref[...] = _rms(o_ref[...], fg_ref[...])


def _ffn(x, norm_g, wg, wu, wd, final_g=None, *, tm=1024, tf=256):
    s, d = x.shape
    ff = wg.shape[1]
    nf = ff // tf
    final = final_g is not None
    in_specs = [
        pl.BlockSpec((tm, d), lambda i, f: (i, 0)),
        pl.BlockSpec((1, d), lambda i, f: (0, 0)),
        pl.BlockSpec((d, tf), lambda i, f: (0, f)),
        pl.BlockSpec((d, tf), lambda i, f: (0, f)),
        pl.BlockSpec((tf, d), lambda i, f: (f, 0)),
    ]
    args = [x, norm_g, wg, wu, wd]
    if final:
        in_specs.append(pl.BlockSpec((1, d), lambda i, f: (0, 0)))
        args.append(final_g)
    return pl.pallas_call(
        functools.partial(_ffn_body, final=final, nf=nf),
        grid=(s // tm, nf),
        in_specs=in_specs,
        out_specs=pl.BlockSpec((tm, d), lambda i, f: (i, 0)),
        out_shape=jax.ShapeDtypeStruct((s, d), F32),
        scratch_shapes=[pltpu.VMEM((tm, d), BF16)],
        compiler_params=_params(("parallel", "arbitrary")),
        name="ffn_final" if final else "ffn",
    )(*args)


_GQK_W = 2 * GLA_HEADS * GLA_DK
_GV_W = GLA_HEADS * GLA_DV
_KV_W = NSA_KV_HEADS * HEAD_DIM
_PER_G = NSA_HPG * 3
_NG_PAD = 16
_R_GQK = 0
_R_GV = _R_GQK + _GQK_W
_R_GR = _R_GV + _GV_W
_R_GA = _R_GR + _GV_W
_R_NQ = _R_GA + GLA_LOWRANK
_R_KC = _R_NQ + NSA_HEADS * HEAD_DIM
_R_KS = _R_KC + 2 * _KV_W
_R_VS = _R_KS + _KV_W
_R_KW = _R_VS + _KV_W
_R_VW = _R_KW + _KV_W
_R_NG = _R_VW + _KV_W
_R_MQ = _R_NG + NSA_KV_HEADS * _PER_G
_R_END = _R_MQ + MEM_HEADS * HEAD_DIM


def _rope(x, cos, sin_signed):
    return x * cos + pltpu.roll(x, HEAD_DIM // 2, 1) * sin_signed


def _proj_body(x_ref, g_ref, wt_ref, wmq_ref, pos_ref, inv_ref, sgn_ref, qn_ref, kn_ref,
               mqn_ref, kmem_ref, vmem_ref, q_ref, ksel_ref, vselt_ref, kwin_ref, vwint_ref, omem_ref, gqk_ref,
               gv_ref, gr_ref, kcvc_ref, ga_ref, ngt_ref, *, tm):
    i = pl.program_id(0)
    h = _rms(x_ref[...], g_ref[...]).astype(BF16)

    def proj(r0, width):
        return _dot_nt(h, wt_ref[r0:r0 + width, :])

    def proj_t(r0, width):
        return _dot_nt(wt_ref[r0:r0 + width, :], h)

    ang = pos_ref[...].astype(F32) * inv_ref[...]
    cos = jnp.cos(ang)
    sin_signed = jnp.sin(ang) * sgn_ref[...]

    def norm_rope(t, gain):
        return _rope(_rms(t, gain), cos, sin_signed)

    scale = HEAD_DIM ** -0.5
    nq = proj(_R_NQ, NSA_HEADS * HEAD_DIM)
    for hd in range(NSA_HEADS):
        sl = slice(hd * HEAD_DIM, (hd + 1) * HEAD_DIM)
        q_ref[:, sl] = (norm_rope(nq[:, sl], qn_ref[...]) * (scale * LOG2_E)).astype(BF16)

    tok = i * tm + lax.broadcasted_iota(jnp.int32, (tm, LANES), 0)
    blk = lax.broadcasted_iota(jnp.int32, (tm, LANES), 1)
    onehot = jnp.where(lax.shift_right_logical(tok, 6) == blk, 1.0, 0.0).astype(BF16)
    ks = proj(_R_KS, _KV_W)
    kw = proj(_R_KW, _KV_W)
    vst = proj_t(_R_VS, _KV_W)
    vwt = proj_t(_R_VW, _KV_W)
    ngt = proj_t(_R_NG, NSA_KV_HEADS * _PER_G)
    ones_rows = jnp.where(lax.broadcasted_iota(jnp.int32, (V_ROWS - HEAD_DIM, tm), 0) == 0, 1.0, 0.0).astype(BF16)
    for g in range(NSA_KV_HEADS):
        sl = slice(g * HEAD_DIM, (g + 1) * HEAD_DIM)
        ksel_ref[g, :, 0:HEAD_DIM] = norm_rope(ks[:, sl], kn_ref[1:2, :]).astype(BF16)
        ksel_ref[g, :, HEAD_DIM:2 * HEAD_DIM] = onehot
        kwin_ref[g] = norm_rope(kw[:, sl], kn_ref[2:3, :]).astype(BF16)
        vselt_ref[g, 0:HEAD_DIM, :] = vst[sl, :].astype(BF16)
        vwint_ref[g, 0:HEAD_DIM, :] = vwt[sl, :].astype(BF16)
        vselt_ref[g, HEAD_DIM:V_ROWS, :] = ones_rows
        vwint_ref[g, HEAD_DIM:V_ROWS, :] = ones_rows
        ngt_ref[g, 0:_PER_G, :] = ngt[g * _PER_G:(g + 1) * _PER_G, :]
        ngt_ref[g, _PER_G:_NG_PAD, :] = jnp.zeros((_NG_PAD - _PER_G, tm), F32)

    mq = _dot_nt(h, wmq_ref[...])
    for hd in range(MEM_HEADS):
        sl = slice(hd * HEAD_DIM, (hd + 1) * HEAD_DIM)
        qh = (_rms(mq[:, sl], mqn_ref[...]) * scale).astype(BF16)
        sc = _dot_nt(qh, kmem_ref[:, sl])
        e = jnp.exp(sc - jnp.max(sc, axis=-1, keepdims=True))
        p = e / jnp.sum(e, axis=-1, keepdims=True)
        omem_ref[:, sl] = _dot(p.astype(BF16), vmem_ref[:, sl]).astype(BF16)

    gqk_ref[...] = proj(_R_GQK, _GQK_W)
    gv_ref[...] = proj(_R_GV, _GV_W).astype(BF16)
    gr_ref[...] = proj(_R_GR, _GV_W)
    ga_ref[...] = proj(_R_GA, GLA_LOWRANK)
    kcvc = proj(_R_KC, 2 * _KV_W)
    for j in range(2 * NSA_KV_HEADS):
        kcvc_ref[j] = kcvc[:, j * HEAD_DIM:(j + 1) * HEAD_DIM]


def _proj(x1, mix_g, w_t, w_mq_t, pos_col, inv128, sgn128, q_norm, k_norm, mq_norm, kmem, vmem, *, tm=512):
    s, d = x1.shape
    row = lambda w: pl.BlockSpec((tm, w), lambda i: (i, 0))
    grp = lambda w: pl.BlockSpec((NSA_KV_HEADS, tm, w), lambda i: (0, i, 0))
    grpt = pl.BlockSpec((NSA_KV_HEADS, V_ROWS, tm), lambda i: (0, 0, i))
    vt_shape = jax.ShapeDtypeStruct((NSA_KV_HEADS, V_ROWS, s), BF16)
    out_shapes = [
        (jax.ShapeDtypeStruct((s, NSA_HEADS * HEAD_DIM), BF16), row(NSA_HEADS * HEAD_DIM)),
        (jax.ShapeDtypeStruct((NSA_KV_HEADS, s, 2 * HEAD_DIM), BF16), grp(2 * HEAD_DIM)),
        (vt_shape, grpt),
        (jax.ShapeDtypeStruct((NSA_KV_HEADS, s, HEAD_DIM), BF16), grp(HEAD_DIM)),
        (vt_shape, grpt),
        (jax.ShapeDtypeStruct((s, MEM_HEADS * HEAD_DIM), BF16), row(MEM_HEADS * HEAD_DIM)),
        (jax.ShapeDtypeStruct((s, 512), F32), row(512)),
        (jax.ShapeDtypeStruct((s, 512), BF16), row(512)),
        (jax.ShapeDtypeStruct((s, 512), F32), row(512)),
        (jax.ShapeDtypeStruct((2 * NSA_KV_HEADS, s, HEAD_DIM), F32),
         pl.BlockSpec((2 * NSA_KV_HEADS, tm, HEAD_DIM), lambda i: (0, i, 0))),
        (jax.ShapeDtypeStruct((s, GLA_LOWRANK), F32), row(GLA_LOWRANK)),
        (jax.ShapeDtypeStruct((NSA_KV_HEADS, _NG_PAD, s), F32),
         pl.BlockSpec((NSA_KV_HEADS, _NG_PAD, tm), lambda i: (0, 0, i))),
    ]
    return pl.pallas_call(
        functools.partial(_proj_body, tm=tm),
        grid=(s // tm,),
        in_specs=[
            row(d),
            _resident((1, d)),
            _resident(w_t.shape), _resident(w_mq_t.shape),
            pl.BlockSpec((tm, 1), lambda i: (i, 0)),
            _resident((1, LANES)), _resident((1, LANES)),
            _resident((1, HEAD_DIM)), _resident((3, HEAD_DIM)), _resident((1, HEAD_DIM)),
            _resident(kmem.shape), _resident(vmem.shape),
        ],
        out_specs=[o[1] for o in out_shapes],
        out_shape=[o[0] for o in out_shapes],
        compiler_params=_params(("parallel",)),
        name="proj",
    )(x1, mix_g, w_t, w_mq_t, pos_col, inv128, sgn128, q_norm, k_norm, mq_norm, kmem, vmem)


def _compress_body(kcvc_ref, w1k_ref, w2k_ref, pek_ref, w1v_ref, w2v_ref, pev_ref, kn_ref,
                   pos_ref, inv_ref, sgn_ref, kcmp_ref, vcmp_ref, *, units):
    half = CMP_LEN // 2
    ang = pos_ref[...].astype(F32) * inv_ref[...]
    cos = jnp.cos(ang)
    sin_signed = jnp.sin(ang) * sgn_ref[...]
    for kind, (w1_ref, w2_ref, pe_ref) in enumerate(((w1k_ref, w2k_ref, pek_ref),
                                                     (w1v_ref, w2v_ref, pev_ref))):
        for g in range(NSA_KV_HEADS):
            slab = kind * NSA_KV_HEADS + g
            a = jnp.zeros((units, w1_ref.shape[1]), F32)
            b = jnp.zeros((units, w1_ref.shape[1]), F32)
            for l in range(half):
                t = kcvc_ref[slab, pl.ds(l, units, stride=CMP_STRIDE), :]
                a = a + _dot((t + pe_ref[l:l + 1, :]).astype(BF16),
                             w1_ref[l * HEAD_DIM:(l + 1) * HEAD_DIM, :].astype(BF16))
                b = b + _dot((t + pe_ref[half + l:half + l + 1, :]).astype(BF16),
                             w1_ref[(half + l) * HEAD_DIM:(half + l + 1) * HEAD_DIM, :].astype(BF16))
            hid = a + pltpu.roll(b, units - 1, 0)
            act = (hid * _sigmoid(hid)).astype(BF16)
            if kind == 0:
                c = _rope(_rms(_dot(act, w2_ref[...]), kn_ref[0:1, :]), cos, sin_signed)
                kcmp_ref[g] = c.astype(BF16)
            else:
                vcmp_ref[g] = _dot_nt(w2_ref[...], act).astype(BF16)


def _compress(kcvc, w1k, w2k, pek, w1v, w2v, pev, k_norm, pos_cmp, inv128, sgn128):
    s = kcvc.shape[1]
    units = s // CMP_STRIDE
    shp = jax.ShapeDtypeStruct((NSA_KV_HEADS, units, HEAD_DIM), BF16)
    shp_t = jax.ShapeDtypeStruct((NSA_KV_HEADS, HEAD_DIM, units), BF16)
    return pl.pallas_call(
        functools.partial(_compress_body, units=units),
        out_shape=[shp, shp_t],
        compiler_params=pltpu.CompilerParams(vmem_limit_bytes=VMEM_LIMIT),
        name="compress",
    )(kcvc, w1k, w2k, pek, w1v, w2v, pev, k_norm, pos_cmp, inv128, sgn128)


def _gla_body(gqk_ref, gv_ref, gr_ref, ga_ref, ga_next_ref, wa_ref, ba_ref, on_ref, tcum_ref, bd_ref, hsel_ref,
              o_ref, st_ref, b_s):
    rows = GLA_ROWS
    npair = GLA_HEADS // 2

    def cum_log_decay(ga):
        z = ba_ref[...]
        for ga_t in _split2(ga):
            for wa_t in _split2(wa_ref[...]):
                z = z + _dot(ga_t, wa_t)
        la = (jnp.minimum(z, 0.0) - jnp.log(1.0 + jnp.exp(-jnp.abs(z)))) / GLA_TAU
        bcum = jnp.zeros_like(la)
        for la_t in _split2(la):
            bcum = bcum + _dot(tcum_ref[...], la_t)
        return bcum * LOG2_E

    @pl.when(pl.program_id(0) == 0)
    def _():
        st_ref[...] = jnp.zeros_like(st_ref)
        b_s[...] = cum_log_decay(ga_ref[...])

    b_all = b_s[...]
    q_all = gqk_ref[:, 0:256] * (GLA_DK ** -0.5)
    k_all = gqk_ref[:, 256:512]
    v_all = gv_ref[...]

    row_i = lax.broadcasted_iota(jnp.int32, (GLA_SUB, LANES), 0)

    sts = [st_ref[p] for p in range(npair)]
    o_rows = []
    for sb in range(rows // GLA_SUB):
        rs = slice(sb * GLA_SUB, (sb + 1) * GLA_SUB)
        o_pairs = []
        for p in range(npair):
            cs = slice(p * LANES, (p + 1) * LANES)
            vs = slice(p * 2 * GLA_DV, (p + 1) * 2 * GLA_DV)
            qs = q_all[rs, cs]
            kk = k_all[rs, cs]
            bb = b_all[rs, cs]
            vp = v_all[rs, vs]
            vpf = vp.astype(F32)
            blast = bb[GLA_SUB - 1:GLA_SUB, :]
            st = sts[p]
            o_inter = _dot_nt((qs * jnp.exp2(bb)).astype(BF16), st.astype(BF16))
            xs = []
            for j in range(GLA_SUB):
                dlt = jnp.where(row_i >= j, bb - bb[j:j + 1, :], NEG_INF)
                xs.append(qs * jnp.exp2(dlt) * kk[j:j + 1, :])
            red = _dot(jnp.concatenate(xs, axis=0).astype(BF16), hsel_ref[...])
            acc = o_inter
            for j in range(GLA_SUB):
                acc = acc + red[j * GLA_SUB:(j + 1) * GLA_SUB, :] * vpf[j:j + 1, :]
            o_pairs.append(acc)
            kd = (kk * jnp.exp2(blast - bb)).astype(BF16)
            upd = _dot_tn(vp, kd)
            sts[p] = st * jnp.exp2(blast) + upd * bd_ref[...]
        o_rows.append(jnp.concatenate(o_pairs, axis=1))
    for p in range(npair):
        st_ref[p] = sts[p]
    b_s[...] = cum_log_decay(ga_next_ref[...])

    o_all = jnp.concatenate(o_rows, axis=0)
    gr = gr_ref[...]
    for hd in range(GLA_HEADS):
        sl = slice(hd * GLA_DV, (hd + 1) * GLA_DV)
        r = gr[:, sl]
        o_ref[:, sl] = (_rms(o_all[:, sl], on_ref[...]) * (r * _sigmoid(r))).astype(BF16)


def _gla(gqk, gv, gr, ga, wa, ba, o_norm):
    s = gqk.shape[0]
    rows = GLA_ROWS
    idx = np.arange(rows)
    tcum = ((idx[:, None] >= idx[None, :]) & (idx[:, None] // GLA_SUB == idx[None, :] // GLA_SUB))
    tcum = jnp.asarray(tcum, BF16)
    r256 = np.arange(2 * GLA_DV)[:, None] // GLA_DV
    c128 = np.arange(LANES)[None, :] // GLA_DK
    bdmask = jnp.asarray(r256 == c128, F32)
    hsel = jnp.asarray((r256 == c128).T, BF16)
    row = lambda w: pl.BlockSpec((rows, w), lambda i: (i, 0))
    nblk = s // rows
    ga_next = pl.BlockSpec((rows, GLA_LOWRANK), lambda i: (jnp.minimum(i + 1, nblk - 1), 0))
    return pl.pallas_call(
        _gla_body,
        grid=(nblk,),
        in_specs=[row(512), row(512), row(512), row(GLA_LOWRANK), ga_next,
                  _resident(wa.shape), _resident(ba.shape), _resident(o_norm.shape),
                  _resident(tcum.shape), _resident(bdmask.shape), _resident(hsel.shape)],
        out_specs=row(512),
        out_shape=jax.ShapeDtypeStruct((s, GLA_HEADS * GLA_DV), BF16),
        scratch_shapes=[pltpu.VMEM((GLA_HEADS // 2, 2 * GLA_DV, LANES), F32),
                        pltpu.VMEM((rows, GLA_HEADS * GLA_DK), F32)],
        compiler_params=_params(("arbitrary",)),
        name="gla",
    )(gqk, gv, gr, ga, ga, wa, ba, o_norm, tcum, bdmask, hsel)


def _nsa_body(q_ref, kcmp_ref, vcmpt_ref, ksel_ref, vselt_ref, kwin_ref, vwint_ref, ngt_ref, ovlt_ref,
              o_ref, qt_s, s_s, p_s, acc_s, *, n_sel):
    qb = pl.program_id(0)
    t0 = qb * QBLK
    cols = NSA_HPG * QBLK
    gw = NSA_HPG * HEAD_DIM
    ncmp = kcmp_ref.shape[1]
    groups = range(NSA_KV_HEADS)
    span = SEL_SPAN_TILES * SEL_KT

    def tq_of(rows):
        return t0 + (lax.broadcasted_iota(jnp.int32, (rows, cols), 1) & (QBLK - 1))

    def gate_row(gates_t, c):
        return jnp.concatenate([gates_t[3 * hd + c:3 * hd + c + 1, :] for hd in range(NSA_HPG)], axis=1)

    def span_scores(g, j):
        k0 = pl.multiple_of(j * span, span)
        return _dot(ksel_ref[g, pl.ds(k0, span), :], qt_s[g])

    def span_pv(g, j, p):
        k0 = pl.multiple_of(j * span, span)
        return _dot(vselt_ref[g, :, pl.ds(k0, span)], p)

    def prologue(g):
        qt = jnp.concatenate(
            [q_ref[:, g * gw + hd * HEAD_DIM:g * gw + (hd + 1) * HEAD_DIM].astype(F32).T.astype(BF16)
             for hd in range(NSA_HPG)], axis=1)
        gates_t = _sigmoid(ngt_ref[g])

        def win_part(start, length):
            start = pl.multiple_of(jnp.maximum(start, 0), QBLK)
            return (_dot(kwin_ref[g, pl.ds(start, length), :], qt), vwint_ref[g, :, pl.ds(start, length)], start)

        s_c = _dot(kcmp_ref[g], qt)
        s_old, v_old, _ = win_part(t0 - WINDOW, QBLK)
        s_mid, v_mid, mid0 = win_part(t0 - WINDOW + QBLK, WINDOW - QBLK)
        s_dg, v_dg, _ = win_part(t0, QBLK)
        yield

        n_row = lax.broadcasted_iota(jnp.int32, (ncmp, cols), 0)
        valid_c = n_row <= lax.shift_right_arithmetic(tq_of(1) - (CMP_LEN - 1), CMP_STRIDE.bit_length() - 1)
        s_c = jnp.where(valid_c, s_c, NEG_INF)
        e_c = jnp.where(valid_c, jnp.exp2(s_c - jnp.max(s_c, axis=0, keepdims=True)), 0.0)
        p_c = e_c / jnp.maximum(jnp.sum(e_c, axis=0, keepdims=True), TINY)
        out_pre = gate_row(gates_t, 0) * _dot(vcmpt_ref[g], p_c.astype(BF16))
        psum = p_c[:, 0:QBLK]
        for hd in range(1, NSA_HPG):
            psum = psum + p_c[:, hd * QBLK:(hd + 1) * QBLK]
        imp = jnp.zeros((LANES, QBLK), F32)
        for p_t in _split2(psum):
            imp = imp + _dot(ovlt_ref[...], p_t)
        yield

        tq = t0 + lax.broadcasted_iota(jnp.int32, (LANES, QBLK), 1)
        m_i = lax.broadcasted_iota(jnp.int32, (LANES, QBLK), 0)
        cur = lax.shift_right_logical(tq, 6)
        forced = (m_i == 0) | (m_i == cur) | (m_i == cur - 1)
        causal = m_i * SEL_LEN <= tq
        n_forced = 3
        score = jnp.where(causal, jnp.where(forced, -jnp.inf, imp), -FORCE_SCORE)
        score = jnp.where(m_i < n_sel, score, SEL_PAD_SCORE)
        m_f = m_i.astype(F32)
        bias = jnp.where(forced & causal, 0.0, SEL_MASK_BIAS)
        for _ in range(min(SEL_TOPK, n_sel) - n_forced):
            mx = jnp.max(score, axis=0, keepdims=True)
            first = jnp.min(jnp.where(score == mx, m_f, float(LANES)), axis=0, keepdims=True)
            pick = m_f == first
            bias = jnp.where(pick, 0.0, bias)
            score = jnp.where(pick, -jnp.inf, score)
        bias = bias.astype(BF16)
        qext = jnp.concatenate([qt, jnp.concatenate([bias] * NSA_HPG, axis=1)], axis=0)
        sc0 = _dot(ksel_ref[g, 0:span, :], qext)
        yield

        w_row = lax.broadcasted_iota(jnp.int32, (QBLK, cols), 0)
        tq_w = tq_of(QBLK)
        kp_old = t0 - WINDOW + w_row
        valid_old = (kp_old > tq_w - WINDOW) & (kp_old >= 0)
        s_old = jnp.where(valid_old, s_old, NEG_INF)
        mid_row = mid0 + lax.broadcasted_iota(jnp.int32, (WINDOW - QBLK, cols), 0)
        s_mid = jnp.where(mid_row < t0, s_mid, NEG_INF)
        valid_dg = t0 + w_row <= tq_w
        s_dg = jnp.where(valid_dg, s_dg, NEG_INF)
        m_w = jnp.maximum(jnp.maximum(jnp.max(s_old, axis=0, keepdims=True),
                                      jnp.max(s_mid, axis=0, keepdims=True)),
                          jnp.max(s_dg, axis=0, keepdims=True))
        p_old = jnp.where(valid_old, jnp.exp2(s_old - m_w), 0.0)
        p_mid = jnp.exp2(s_mid - m_w)
        p_dg = jnp.where(valid_dg, jnp.exp2(s_dg - m_w), 0.0)
        acc_w = (_dot(v_old, p_old.astype(BF16)) + _dot(v_mid, p_mid.astype(BF16))
                 + _dot(v_dg, p_dg.astype(BF16)))
        out_pre = out_pre + gate_row(gates_t, 2) * (acc_w[0:HEAD_DIM, :]
                                                    / jnp.maximum(acc_w[HEAD_DIM:HEAD_DIM + 1, :], TINY))

        return out_pre, gate_row(gates_t, 1), qext, sc0

    pre = {}
    gens = {g: prologue(g) for g in groups}
    order = [0] + [g for _ in range(8) for g in groups]
    for g in order:
        if g not in pre:
            try:
                next(gens[g])
            except StopIteration as done:
                pre[g] = done.value
    assert len(pre) == len(groups)
    for g in groups:
        qt_s[g] = pre[g][2]
        s_s[g, 0] = pre[g][3]
        p_s[g, 1] = jnp.zeros((span, cols), BF16)
        acc_s[g] = jnp.zeros((V_ROWS, cols), F32)

    def sel_step(cur, j, ms):
        nxt = 1 - cur
        out = []
        for g in groups:
            pv = span_pv(g, jnp.maximum(j - 1, 0), p_s[g, nxt])
            s = s_s[g, cur]
            m_new = jnp.maximum(ms[g], jnp.max(s, axis=0, keepdims=True))
            p_s[g, cur] = jnp.exp2(s - m_new).astype(BF16)
            acc_s[g] = jnp.exp2(ms[g] - m_new) * (acc_s[g] + pv)
            out.append(m_new)
        for g in groups:
            s_s[g, nxt] = span_scores(g, j + 1)
        return tuple(out)

    def sel_body(j, ms):
        return lax.cond((j & 1) == 0, lambda c: sel_step(0, j, c), lambda c: sel_step(1, j, c), ms)

    n_span = t0 // span
    ms = lax.fori_loop(0, n_span, sel_body, tuple(jnp.full((1, cols), NEG_INF, F32) for _ in groups))

    slot = n_span & 1
    base = pl.multiple_of(n_span * span, span)

    def diag(nk):
        valid = lax.broadcasted_iota(jnp.int32, (nk, cols), 0) <= tq_of(1) - base
        for g in groups:
            pv = span_pv(g, jnp.maximum(n_span - 1, 0), p_s[g, 1 - slot])
            s = jnp.where(valid, s_s[g, slot, 0:nk, :], NEG_INF)
            m_new = jnp.maximum(ms[g], jnp.max(s, axis=0, keepdims=True))
            p = jnp.where(valid, jnp.exp2(s - m_new), 0.0).astype(BF16)
            acc_s[g] = (jnp.exp2(ms[g] - m_new) * (acc_s[g] + pv)
                        + _dot(vselt_ref[g, :, pl.ds(base, nk)], p))

    lax.cond(t0 - base >= span // 2, lambda: diag(span), lambda: diag(span // 2))

    for g in groups:
        out_pre, gate_sel = pre[g][0:2]
        acc = acc_s[g]
        o_slc = acc[0:HEAD_DIM, :] / jnp.maximum(acc[HEAD_DIM:HEAD_DIM + 1, :], TINY)
        out = out_pre + gate_sel * o_slc
        for hd in range(NSA_HPG):
            o_ref[:, g * gw + hd * HEAD_DIM:g * gw + (hd + 1) * HEAD_DIM] = (
                out[:, hd * QBLK:(hd + 1) * QBLK].T.astype(BF16))


def _nsa(q, kcmp, vcmpt, ksel, vselt, kwin, vwint, ngt, overlap_t):
    s = q.shape[0]
    n_sel = s // SEL_LEN
    span = SEL_SPAN_TILES * SEL_KT
    assert n_sel <= LANES and s % span == 0 and s >= WINDOW + QBLK and CMP_STRIDE & (CMP_STRIDE - 1) == 0
    cols = NSA_HPG * QBLK
    ng = NSA_KV_HEADS
    return pl.pallas_call(
        functools.partial(_nsa_body, n_sel=n_sel),
        grid=(s // QBLK,),
        in_specs=[
            pl.BlockSpec((QBLK, NSA_HEADS * HEAD_DIM), lambda b: (b, 0)),
            _resident(kcmp.shape), _resident(vcmpt.shape),
            _resident(ksel.shape), _resident(vselt.shape), _resident(kwin.shape), _resident(vwint.shape),
            pl.BlockSpec((ng, _NG_PAD, QBLK), lambda b: (0, 0, b)),
            _resident(overlap_t.shape),
        ],
        out_specs=pl.BlockSpec((QBLK, NSA_HEADS * HEAD_DIM), lambda b: (b, 0)),
        out_shape=jax.ShapeDtypeStruct((s, NSA_HEADS * HEAD_DIM), BF16),
        scratch_shapes=[pltpu.VMEM((ng, 2 * HEAD_DIM, cols), BF16),
                        pltpu.VMEM((ng, 2, span, cols), F32),
                        pltpu.VMEM((ng, 2, span, cols), BF16),
                        pltpu.VMEM((ng, V_ROWS, cols), F32)],
        compiler_params=_params(("arbitrary",)),
        name="nsa",
    )(q, kcmp, vcmpt, ksel, vselt, kwin, vwint, ngt, overlap_t)


def _memkv_body(mem_ref, g_ref, w_ref, kn_ref, k_ref, v_ref):
    kv = _dot(_rms(mem_ref[...], g_ref[...]).astype(BF16), w_ref[...].astype(BF16))
    width = MEM_HEADS * HEAD_DIM
    for hd in range(MEM_HEADS):
        sl = slice(hd * HEAD_DIM, (hd + 1) * HEAD_DIM)
        k_ref[:, sl] = _rms(kv[:, sl], kn_ref[...]).astype(BF16)
    v_ref[...] = kv[:, width:].astype(BF16)


def _memkv(mem, in_g, w_kv, k_norm):
    m = mem.shape[0]
    shp = jax.ShapeDtypeStruct((m, MEM_HEADS * HEAD_DIM), BF16)
    return pl.pallas_call(
        _memkv_body, out_shape=[shp, shp],
        compiler_params=pltpu.CompilerParams(vmem_limit_bytes=VMEM_LIMIT),
        name="memkv",
    )(mem, in_g, w_kv, k_norm)


def _outproj_body(x_ref, a_ref, b_ref, c_ref, w_ref, o_ref):
    na, nb = a_ref.shape[1], b_ref.shape[1]
    o_ref[...] = (x_ref[...] + _dot(a_ref[...], w_ref[0:na, :].astype(BF16))
                  + _dot(b_ref[...], w_ref[na:na + nb, :].astype(BF16))
                  + _dot(c_ref[...], w_ref[na + nb:, :].astype(BF16)))


def _outproj(x1, o_gla, o_nsa, o_mem, w_out, *, tm=512):
    s, d = x1.shape
    row = lambda w: pl.BlockSpec((tm, w), lambda i: (i, 0))
    return pl.pallas_call(
        _outproj_body,
        grid=(s // tm,),
        in_specs=[row(d), row(o_gla.shape[1]), row(o_nsa.shape[1]), row(o_mem.shape[1]),
                  _resident(w_out.shape)],
        out_specs=row(d),
        out_shape=jax.ShapeDtypeStruct((s, d), F32),
        compiler_params=_params(("parallel",)),
        name="outproj",
    )(x1, o_gla, o_nsa, o_mem, w_out)


def _layer(x, mem, positions, ffn1_norm, ffn1_w_gate, ffn1_w_up, ffn1_w_down, mix_norm, w_in,
           gla_w_a, gla_b_a, gla_o_norm, nsa_q_norm, nsa_k_norm, nsa_cmp_pos_k, nsa_cmp_w1_k,
           nsa_cmp_w2_k, nsa_cmp_pos_v, nsa_cmp_w1_v, nsa_cmp_w2_v, mem_in_norm, w_mem_kv,
           mem_q_norm, mem_k_norm, w_out, ffn2_norm, ffn2_w_gate, ffn2_w_up, ffn2_w_down, final_norm):
    s, d = x.shape
    row = lambda v: v.reshape(1, -1)
    bf = lambda v: v.astype(BF16)

    x1 = _ffn(x, row(ffn1_norm), ffn1_w_gate, ffn1_w_up, ffn1_w_down)

    half = HEAD_DIM // 2
    inv = ROPE_THETA ** (-jnp.arange(half, dtype=F32) / half)
    inv128 = jnp.concatenate([inv, inv]).reshape(1, HEAD_DIM)
    sgn128 = jnp.concatenate([-jnp.ones((half,), F32), jnp.ones((half,), F32)]).reshape(1, HEAD_DIM)
    kmem, vmem = _memkv(mem, row(mem_in_norm), w_mem_kv, row(mem_k_norm))
    assert w_in.shape[1] == _R_END
    w_t = bf(w_in.T)
    (q, ksel, vselt, kwin, vwint, o_mem, gqk, gv, gr, kcvc, ga, ngt) = _proj(
        x1, row(mix_norm), w_t, w_t[_R_MQ:], positions.reshape(s, 1), inv128, sgn128,
        row(nsa_q_norm), nsa_k_norm, row(mem_q_norm), kmem, vmem)

    o_gla = _gla(gqk, gv, gr, ga, gla_w_a, row(gla_b_a), row(gla_o_norm))

    units = s // CMP_STRIDE
    n_cmp = (s - CMP_LEN) // CMP_STRIDE + 1
    cmp_last = jnp.arange(units) * CMP_STRIDE + CMP_LEN - 1
    pos_cmp = positions[jnp.minimum(cmp_last, s - 1)].reshape(units, 1)
    kcmp, vcmpt = _compress(kcvc, nsa_cmp_w1_k, bf(nsa_cmp_w2_k), nsa_cmp_pos_k,
                            nsa_cmp_w1_v, bf(nsa_cmp_w2_v.T), nsa_cmp_pos_v, nsa_k_norm,
                            pos_cmp, inv128, sgn128)
    n_sel = s // SEL_LEN
    cmp_start = np.arange(units) * CMP_STRIDE
    sel_start = np.arange(LANES) * SEL_LEN
    overlap = np.clip(np.minimum(cmp_start[:, None] + CMP_LEN, sel_start[None, :] + SEL_LEN)
                      - np.maximum(cmp_start[:, None], sel_start[None, :]), 0, None) / CMP_STRIDE
    overlap = overlap * (np.arange(units)[:, None] < n_cmp) * (np.arange(LANES)[None, :] < n_sel)
    o_nsa = _nsa(q, kcmp, vcmpt, ksel, vselt, kwin, vwint, ngt, jnp.asarray(overlap.T, BF16))

    x2 = _outproj(x1, o_gla, o_nsa, o_mem, w_out)
    return _ffn(x2, row(ffn2_norm), ffn2_w_gate, ffn2_w_up, ffn2_w_down, row(final_norm))


def kernel(x, mem, positions, ffn1_norm, ffn1_w_gate, ffn1_w_up, ffn1_w_down, mix_norm, w_in, gla_w_a, gla_b_a, gla_o_norm, nsa_q_norm, nsa_k_norm, nsa_cmp_pos_k, nsa_cmp_w1_k, nsa_cmp_w2_k, nsa_cmp_pos_v, nsa_cmp_w1_v, nsa_cmp_w2_v, mem_in_norm, w_mem_kv, mem_q_norm, mem_k_norm, w_out, ffn2_norm, ffn2_w_gate, ffn2_w_up, ffn2_w_down, final_norm):
    depth = ffn1_norm.shape[0]
    batch, s, d = x.shape
    outs = []
    for b in range(batch):
        xb, mem_b, pos_b = (x.reshape(s, d), mem.reshape(mem.shape[1:]), positions.reshape(s)) if batch == 1 \
            else (x[b], mem[b], positions[b])
        for l in range(depth):
            xb = _layer(xb, mem_b, pos_b, ffn1_norm[l], ffn1_w_gate[l], ffn1_w_up[l], ffn1_w_down[l],
                        mix_norm[l], w_in[l], gla_w_a[l], gla_b_a[l], gla_o_norm[l], nsa_q_norm[l],
                        nsa_k_norm[l], nsa_cmp_pos_k[l], nsa_cmp_w1_k[l], nsa_cmp_w2_k[l], nsa_cmp_pos_v[l],
                        nsa_cmp_w1_v[l], nsa_cmp_w2_v[l], mem_in_norm[l], w_mem_kv[l], mem_q_norm[l],
                        mem_k_norm[l], w_out[l], ffn2_norm[l], ffn2_w_gate[l], ffn2_w_up[l], ffn2_w_down[l],
                        final_norm[l])
        outs.append(xb)
    return outs[0].reshape(1, s, d) if batch == 1 else jnp.stack(outs)
```

```python
import functools

import numpy as np
import jax
import jax.numpy as jnp
from jax import lax
from jax.experimental import pallas as pl
from jax.experimental.pallas import tpu as pltpu

F32 = jnp.float32
BF16 = jnp.bfloat16

HEAD_DIM = 128
GLA_HEADS = 4
GLA_DK = 64
GLA_DV = 128
GLA_LOWRANK = 16
GLA_TAU = 16.0
NSA_HEADS = 8
NSA_KV_HEADS = 2
NSA_HPG = NSA_HEADS // NSA_KV_HEADS
CMP_LEN = 32
CMP_STRIDE = 16
SEL_LEN = 64
SEL_TOPK = 16
WINDOW = 512
MEM_HEADS = 4
MACARON_W = 0.5
QBLK = 128
ROPE_THETA = 10000.0
EPS = 1e-6
NEG_INF = -1e30
TINY = 1e-30
FORCE_SCORE = 1e4
LOG2_E = 1.4426950408889634

LANES = 128
VMEM_LIMIT = 56 * 1024 * 1024

GLA_SUB = 16
GLA_ROWS = 128
SEL_KT = 256
SEL_SPAN_TILES = 4
V_ROWS = HEAD_DIM + 16
SEL_MASK_BIAS = -32768.0
SEL_PAD_SCORE = -3e4


def _dot(a, b):
    return jnp.dot(a, b, preferred_element_type=F32)


def _dot_nt(a, b):
    return lax.dot_general(a, b, (((1,), (1,)), ((), ())), preferred_element_type=F32)


def _dot_tn(a, b):
    return lax.dot_general(a, b, (((0,), (0,)), ((), ())), preferred_element_type=F32)


def _split2(x):
    hi = x.astype(BF16)
    return hi, (x - hi.astype(F32)).astype(BF16)


def _rms(x, g):
    return x * lax.rsqrt(jnp.mean(x * x, axis=-1, keepdims=True) + EPS) * g


def _sigmoid(x):
    return 1.0 / (1.0 + jnp.exp(-x))


def _params(sem):
    return pltpu.CompilerParams(dimension_semantics=sem, vmem_limit_bytes=VMEM_LIMIT)


def _resident(shape):
    nd = len(shape)
    return pl.BlockSpec(shape, lambda *_: (0,) * nd, pipeline_mode=pl.Buffered(1))


def _ffn_body(*refs, final, nf):
    if final:
        x_ref, g_ref, wg_ref, wu_ref, wd_ref, fg_ref, o_ref, h_ref = refs
    else:
        x_ref, g_ref, wg_ref, wu_ref, wd_ref, o_ref, h_ref = refs
    f = pl.program_id(1)

    @pl.when(f == 0)
    def _():
        x = x_ref[...]
        h_ref[...] = _rms(x, g_ref[...]).astype(BF16)
        o_ref[...] = x

    h = h_ref[...]
    g = _dot(h, wg_ref[...].astype(BF16))
    u = _dot(h, wu_ref[...].astype(BF16))
    a = (g * _sigmoid(g)) * u * MACARON_W
    o_ref[...] += _dot(a.astype(BF16), wd_ref[...].astype(BF16))

    if final:
        @pl.when(f == nf - 1)
        def _():
            o_ref[...] = _rms(o_ref[...], fg_ref[...])


def _ffn(x, norm_g, wg, wu, wd, final_g=None, *, tm=1024, tf=256):
    s, d = x.shape
    ff = wg.shape[1]
    nf = ff // tf
    final = final_g is not None
    in_specs = [
        pl.BlockSpec((tm, d), lambda i, f: (i, 0)),
        pl.BlockSpec((1, d), lambda i, f: (0, 0)),
        pl.BlockSpec((d, tf), lambda i, f: (0, f)),
        pl.BlockSpec((d, tf), lambda i, f: (0, f)),
        pl.BlockSpec((tf, d), lambda i, f: (f, 0)),
    ]
    args = [x, norm_g, wg, wu, wd]
    if final:
        in_specs.append(pl.BlockSpec((1, d), lambda i, f: (0, 0)))
        args.append(final_g)
    return pl.pallas_call(
        functools.partial(_ffn_body, final=final, nf=nf),
        grid=(s // tm, nf),
        in_specs=in_specs,
        out_specs=pl.BlockSpec((tm, d), lambda i, f: (i, 0)),
        out_shape=jax.ShapeDtypeStruct((s, d), F32),
        scratch_shapes=[pltpu.VMEM((tm, d), BF16)],
        compiler_params=_params(("parallel", "arbitrary")),
        name="ffn_final" if final else "ffn",
    )(*args)


_GQK_W = 2 * GLA_HEADS * GLA_DK
_GV_W = GLA_HEADS * GLA_DV
_KV_W = NSA_KV_HEADS * HEAD_DIM
_PER_G = NSA_HPG * 3
_NG_PAD = 16
_R_GQK = 0
_R_GV = _R_GQK + _GQK_W
_R_GR = _R_GV + _GV_W
_R_GA = _R_GR + _GV_W
_R_NQ = _R_GA + GLA_LOWRANK
_R_KC = _R_NQ + NSA_HEADS * HEAD_DIM
_R_KS = _R_KC + 2 * _KV_W
_R_VS = _R_KS + _KV_W
_R_KW = _R_VS + _KV_W
_R_VW = _R_KW + _KV_W
_R_NG = _R_VW + _KV_W
_R_MQ = _R_NG + NSA_KV_HEADS * _PER_G
_R_END = _R_MQ + MEM_HEADS * HEAD_DIM


def _rope(x, cos, sin_signed):
    return x * cos + pltpu.roll(x, HEAD_DIM // 2, 1) * sin_signed


def _proj_body(x_ref, g_ref, wt_ref, wmq_ref, pos_ref, inv_ref, sgn_ref, qn_ref, kn_ref,
               mqn_ref, q_ref, ksel_ref, vselt_ref, kwin_ref, vwint_ref, mq_ref, gqk_ref,
               gv_ref, gr_ref, kcvc_ref, ga_ref, ngt_ref, *, tm):
    i = pl.program_id(0)
    h = _rms(x_ref[...], g_ref[...]).astype(BF16)

    def proj(r0, width):
        return _dot_nt(h, wt_ref[r0:r0 + width, :])

    def proj_t(r0, width):
        return _dot_nt(wt_ref[r0:r0 + width, :], h)

    ang = pos_ref[...].astype(F32) * inv_ref[...]
    cos = jnp.cos(ang)
    sin_signed = jnp.sin(ang) * sgn_ref[...]

    def norm_rope(t, gain):
        return _rope(_rms(t, gain), cos, sin_signed)

    scale = HEAD_DIM ** -0.5
    nq = proj(_R_NQ, NSA_HEADS * HEAD_DIM)
    for hd in range(NSA_HEADS):
        sl = slice(hd * HEAD_DIM, (hd + 1) * HEAD_DIM)
        q_ref[:, sl] = (norm_rope(nq[:, sl], qn_ref[...]) * (scale * LOG2_E)).astype(BF16)

    tok = i * tm + lax.broadcasted_iota(jnp.int32, (tm, LANES), 0)
    blk = lax.broadcasted_iota(jnp.int32, (tm, LANES), 1)
    onehot = jnp.where(lax.shift_right_logical(tok, 6) == blk, 1.0, 0.0).astype(BF16)
    ks = proj(_R_KS, _KV_W)
    kw = proj(_R_KW, _KV_W)
    vst = proj_t(_R_VS, _KV_W)
    vwt = proj_t(_R_VW, _KV_W)
    ngt = proj_t(_R_NG, NSA_KV_HEADS * _PER_G)
    ones_rows = jnp.where(lax.broadcasted_iota(jnp.int32, (V_ROWS - HEAD_DIM, tm), 0) == 0, 1.0, 0.0).astype(BF16)
    for g in range(NSA_KV_HEADS):
        sl = slice(g * HEAD_DIM, (g + 1) * HEAD_DIM)
        ksel_ref[g, :, 0:HEAD_DIM] = norm_rope(ks[:, sl], kn_ref[1:2, :]).astype(BF16)
        ksel_ref[g, :, HEAD_DIM:2 * HEAD_DIM] = onehot
        kwin_ref[g] = norm_rope(kw[:, sl], kn_ref[2:3, :]).astype(BF16)
        vselt_ref[g, 0:HEAD_DIM, :] = vst[sl, :].astype(BF16)
        vwint_ref[g, 0:HEAD_DIM, :] = vwt[sl, :].astype(BF16)
        vselt_ref[g, HEAD_DIM:V_ROWS, :] = ones_rows
        vwint_ref[g, HEAD_DIM:V_ROWS, :] = ones_rows
        ngt_ref[g, 0:_PER_G, :] = ngt[g * _PER_G:(g + 1) * _PER_G, :]
        ngt_ref[g, _PER_G:_NG_PAD, :] = jnp.zeros((_NG_PAD - _PER_G, tm), F32)

    mq = _dot_nt(h, wmq_ref[...])
    for hd in range(MEM_HEADS):
        sl = slice(hd * HEAD_DIM, (hd + 1) * HEAD_DIM)
        mq_ref[:, sl] = (_rms(mq[:, sl], mqn_ref[...]) * scale).astype(BF16)

    gqk_ref[...] = proj(_R_GQK, _GQK_W)
    gv_ref[...] = proj(_R_GV, _GV_W).astype(BF16)
    gr_ref[...] = proj(_R_GR, _GV_W)
    ga_ref[...] = proj(_R_GA, GLA_LOWRANK)
    kcvc = proj(_R_KC, 2 * _KV_W)
    for j in range(2 * NSA_KV_HEADS):
        kcvc_ref[j] = kcvc[:, j * HEAD_DIM:(j + 1) * HEAD_DIM]


def _proj(x1, mix_g, w_t, w_mq_t, pos_col, inv128, sgn128, q_norm, k_norm, mq_norm, *, tm=512):
    s, d = x1.shape
    row = lambda w: pl.BlockSpec((tm, w), lambda i: (i, 0))
    grp = lambda w: pl.BlockSpec((NSA_KV_HEADS, tm, w), lambda i: (0, i, 0))
    grpt = pl.BlockSpec((NSA_KV_HEADS, V_ROWS, tm), lambda i: (0, 0, i))
    vt_shape = jax.ShapeDtypeStruct((NSA_KV_HEADS, V_ROWS, s), BF16)
    out_shapes = [
        (jax.ShapeDtypeStruct((s, NSA_HEADS * HEAD_DIM), BF16), row(NSA_HEADS * HEAD_DIM)),
        (jax.ShapeDtypeStruct((NSA_KV_HEADS, s, 2 * HEAD_DIM), BF16), grp(2 * HEAD_DIM)),
        (vt_shape, grpt),
        (jax.ShapeDtypeStruct((NSA_KV_HEADS, s, HEAD_DIM), BF16), grp(HEAD_DIM)),
        (vt_shape, grpt),
        (jax.ShapeDtypeStruct((s, MEM_HEADS * HEAD_DIM), BF16), row(MEM_HEADS * HEAD_DIM)),
        (jax.ShapeDtypeStruct((s, 512), F32), row(512)),
        (jax.ShapeDtypeStruct((s, 512), BF16), row(512)),
        (jax.ShapeDtypeStruct((s, 512), F32), row(512)),
        (jax.ShapeDtypeStruct((2 * NSA_KV_HEADS, s, HEAD_DIM), F32),
         pl.BlockSpec((2 * NSA_KV_HEADS, tm, HEAD_DIM), lambda i: (0, i, 0))),
        (jax.ShapeDtypeStruct((s, GLA_LOWRANK), F32), row(GLA_LOWRANK)),
        (jax.ShapeDtypeStruct((NSA_KV_HEADS, _NG_PAD, s), F32),
         pl.BlockSpec((NSA_KV_HEADS, _NG_PAD, tm), lambda i: (0, 0, i))),
    ]
    return pl.pallas_call(
        functools.partial(_proj_body, tm=tm),
        grid=(s // tm,),
        in_specs=[
            row(d),
            _resident((1, d)),
            _resident(w_t.shape), _resident(w_mq_t.shape),
            pl.BlockSpec((tm, 1), lambda i: (i, 0)),
            _resident((1, LANES)), _resident((1, LANES)),
            _resident((1, HEAD_DIM)), _resident((3, HEAD_DIM)), _resident((1, HEAD_DIM)),
        ],
        out_specs=[o[1] for o in out_shapes],
        out_shape=[o[0] for o in out_shapes],
        compiler_params=_params(("parallel",)),
        name="proj",
    )(x1, mix_g, w_t, w_mq_t, pos_col, inv128, sgn128, q_norm, k_norm, mq_norm)


def _compress_body(kcvc_ref, w1k_ref, w2k_ref, pek_ref, w1v_ref, w2v_ref, pev_ref, kn_ref,
                   pos_ref, inv_ref, sgn_ref, kcmp_ref, vcmp_ref, *, units):
    half = CMP_LEN // 2
    ang = pos_ref[...].astype(F32) * inv_ref[...]
    cos = jnp.cos(ang)
    sin_signed = jnp.sin(ang) * sgn_ref[...]
    for kind, (w1_ref, w2_ref, pe_ref) in enumerate(((w1k_ref, w2k_ref, pek_ref),
                                                     (w1v_ref, w2v_ref, pev_ref))):
        for g in range(NSA_KV_HEADS):
            slab = kind * NSA_KV_HEADS + g
            a = jnp.zeros((units, w1_ref.shape[1]), F32)
            b = jnp.zeros((units, w1_ref.shape[1]), F32)
            for l in range(half):
                t = kcvc_ref[slab, pl.ds(l, units, stride=CMP_STRIDE), :]
                a = a + _dot((t + pe_ref[l:l + 1, :]).astype(BF16),
                             w1_ref[l * HEAD_DIM:(l + 1) * HEAD_DIM, :].astype(BF16))
                b = b + _dot((t + pe_ref[half + l:half + l + 1, :]).astype(BF16),
                             w1_ref[(half + l) * HEAD_DIM:(half + l + 1) * HEAD_DIM, :].astype(BF16))
            hid = a + pltpu.roll(b, units - 1, 0)
            act = (hid * _sigmoid(hid)).astype(BF16)
            if kind == 0:
                c = _rope(_rms(_dot(act, w2_ref[...]), kn_ref[0:1, :]), cos, sin_signed)
                kcmp_ref[g] = c.astype(BF16)
            else:
                vcmp_ref[g] = _dot_nt(w2_ref[...], act).astype(BF16)


def _compress(kcvc, w1k, w2k, pek, w1v, w2v, pev, k_norm, pos_cmp, inv128, sgn128):
    s = kcvc.shape[1]
    units = s // CMP_STRIDE
    shp = jax.ShapeDtypeStruct((NSA_KV_HEADS, units, HEAD_DIM), BF16)
    shp_t = jax.ShapeDtypeStruct((NSA_KV_HEADS, HEAD_DIM, units), BF16)
    return pl.pallas_call(
        functools.partial(_compress_body, units=units),
        out_shape=[shp, shp_t],
        compiler_params=pltpu.CompilerParams(vmem_limit_bytes=VMEM_LIMIT),
        name="compress",
    )(kcvc, w1k, w2k, pek, w1v, w2v, pev, k_norm, pos_cmp, inv128, sgn128)


def _gla_body(gqk_ref, gv_ref, gr_ref, ga_ref, ga_next_ref, wa_ref, ba_ref, on_ref, tcum_ref, bd_ref, hsel_ref,
              o_ref, st_ref, b_s):
    rows = GLA_ROWS
    npair = GLA_HEADS // 2

    def cum_log_decay(ga):
        z = ba_ref[...]
        for ga_t in _split2(ga):
            for wa_t in _split2(wa_ref[...]):
                z = z + _dot(ga_t, wa_t)
        la = (jnp.minimum(z, 0.0) - jnp.log(1.0 + jnp.exp(-jnp.abs(z)))) / GLA_TAU
        bcum = jnp.zeros_like(la)
        for la_t in _split2(la):
            bcum = bcum + _dot(tcum_ref[...], la_t)
        return bcum * LOG2_E

    @pl.when(pl.program_id(0) == 0)
    def _():
        st_ref[...] = jnp.zeros_like(st_ref)
        b_s[...] = cum_log_decay(ga_ref[...])

    b_all = b_s[...]
    q_all = gqk_ref[:, 0:256] * (GLA_DK ** -0.5)
    k_all = gqk_ref[:, 256:512]
    v_all = gv_ref[...]

    row_i = lax.broadcasted_iota(jnp.int32, (GLA_SUB, LANES), 0)

    sts = [st_ref[p] for p in range(npair)]
    o_rows = []
    for sb in range(rows // GLA_SUB):
        rs = slice(sb * GLA_SUB, (sb + 1) * GLA_SUB)
        o_pairs = []
        for p in range(npair):
            cs = slice(p * LANES, (p + 1) * LANES)
            vs = slice(p * 2 * GLA_DV, (p + 1) * 2 * GLA_DV)
            qs = q_all[rs, cs]
            kk = k_all[rs, cs]
            bb = b_all[rs, cs]
            vp = v_all[rs, vs]
            vpf = vp.astype(F32)
            blast = bb[GLA_SUB - 1:GLA_SUB, :]
            st = sts[p]
            o_inter = _dot_nt((qs * jnp.exp2(bb)).astype(BF16), st.astype(BF16))
            xs = []
            for j in range(GLA_SUB):
                dlt = jnp.where(row_i >= j, bb - bb[j:j + 1, :], NEG_INF)
                xs.append(qs * jnp.exp2(dlt) * kk[j:j + 1, :])
            red = _dot(jnp.concatenate(xs, axis=0).astype(BF16), hsel_ref[...])
            acc = o_inter
            for j in range(GLA_SUB):
                acc = acc + red[j * GLA_SUB:(j + 1) * GLA_SUB, :] * vpf[j:j + 1, :]
            o_pairs.append(acc)
            kd = (kk * jnp.exp2(blast - bb)).astype(BF16)
            upd = _dot_tn(vp, kd)
            sts[p] = st * jnp.exp2(blast) + upd * bd_ref[...]
        o_rows.append(jnp.concatenate(o_pairs, axis=1))
    for p in range(npair):
        st_ref[p] = sts[p]
    b_s[...] = cum_log_decay(ga_next_ref[...])

    o_all = jnp.concatenate(o_rows, axis=0)
    gr = gr_ref[...]
    for hd in range(GLA_HEADS):
        sl = slice(hd * GLA_DV, (hd + 1) * GLA_DV)
        r = gr[:, sl]
        o_ref[:, sl] = (_rms(o_all[:, sl], on_ref[...]) * (r * _sigmoid(r))).astype(BF16)


def _gla(gqk, gv, gr, ga, wa, ba, o_norm):
    s = gqk.shape[0]
    rows = GLA_ROWS
    idx = np.arange(rows)
    tcum = ((idx[:, None] >= idx[None, :]) & (idx[:, None] // GLA_SUB == idx[None, :] // GLA_SUB))
    tcum = jnp.asarray(tcum, BF16)
    r256 = np.arange(2 * GLA_DV)[:, None] // GLA_DV
    c128 = np.arange(LANES)[None, :] // GLA_DK
    bdmask = jnp.asarray(r256 == c128, F32)
    hsel = jnp.asarray((r256 == c128).T, BF16)
    row = lambda w: pl.BlockSpec((rows, w), lambda i: (i, 0))
    nblk = s // rows
    ga_next = pl.BlockSpec((rows, GLA_LOWRANK), lambda i: (jnp.minimum(i + 1, nblk - 1), 0))
    return pl.pallas_call(
        _gla_body,
        grid=(nblk,),
        in_specs=[row(512), row(512), row(512), row(GLA_LOWRANK), ga_next,
                  _resident(wa.shape), _resident(ba.shape), _resident(o_norm.shape),
                  _resident(tcum.shape), _resident(bdmask.shape), _resident(hsel.shape)],
        out_specs=row(512),
        out_shape=jax.ShapeDtypeStruct((s, GLA_HEADS * GLA_DV), BF16),
        scratch_shapes=[pltpu.VMEM((GLA_HEADS // 2, 2 * GLA_DV, LANES), F32),
                        pltpu.VMEM((rows, GLA_HEADS * GLA_DK), F32)],
        compiler_params=_params(("arbitrary",)),
        name="gla",
    )(gqk, gv, gr, ga, ga, wa, ba, o_norm, tcum, bdmask, hsel)


def _nsa_body(q_ref, kcmp_ref, vcmpt_ref, ksel_ref, vselt_ref, kwin_ref, vwint_ref, ngt_ref, ovlt_ref,
              o_ref, qt_s, s_s, p_s, acc_s, *, n_sel):
    qb = pl.program_id(0)
    t0 = qb * QBLK
    cols = NSA_HPG * QBLK
    gw = NSA_HPG * HEAD_DIM
    ncmp = kcmp_ref.shape[1]
    groups = range(NSA_KV_HEADS)
    span = SEL_SPAN_TILES * SEL_KT

    def tq_of(rows):
        return t0 + (lax.broadcasted_iota(jnp.int32, (rows, cols), 1) & (QBLK - 1))

    def gate_row(gates_t, c):
        return jnp.concatenate([gates_t[3 * hd + c:3 * hd + c + 1, :] for hd in range(NSA_HPG)], axis=1)

    def span_scores(g, j):
        k0 = pl.multiple_of(j * span, span)
        return _dot(ksel_ref[g, pl.ds(k0, span), :], qt_s[g])

    def span_pv(g, j, p):
        k0 = pl.multiple_of(j * span, span)
        return _dot(vselt_ref[g, :, pl.ds(k0, span)], p)

    def prologue(g):
        qt = jnp.concatenate(
            [q_ref[:, g * gw + hd * HEAD_DIM:g * gw + (hd + 1) * HEAD_DIM].astype(F32).T.astype(BF16)
             for hd in range(NSA_HPG)], axis=1)
        gates_t = _sigmoid(ngt_ref[g])

        def win_part(start, length):
            start = pl.multiple_of(jnp.maximum(start, 0), QBLK)
            return (_dot(kwin_ref[g, pl.ds(start, length), :], qt), vwint_ref[g, :, pl.ds(start, length)], start)

        s_c = _dot(kcmp_ref[g], qt)
        s_old, v_old, _ = win_part(t0 - WINDOW, QBLK)
        s_mid, v_mid, mid0 = win_part(t0 - WINDOW + QBLK, WINDOW - QBLK)
        s_dg, v_dg, _ = win_part(t0, QBLK)
        yield

        n_row = lax.broadcasted_iota(jnp.int32, (ncmp, cols), 0)
        valid_c = n_row <= lax.shift_right_arithmetic(tq_of(1) - (CMP_LEN - 1), CMP_STRIDE.bit_length() - 1)
        s_c = jnp.where(valid_c, s_c, NEG_INF)
        e_c = jnp.where(valid_c, jnp.exp2(s_c - jnp.max(s_c, axis=0, keepdims=True)), 0.0)
        p_c = e_c / jnp.maximum(jnp.sum(e_c, axis=0, keepdims=True), TINY)
        out_pre = gate_row(gates_t, 0) * _dot(vcmpt_ref[g], p_c.astype(BF16))
        psum = p_c[:, 0:QBLK]
        for hd in range(1, NSA_HPG):
            psum = psum + p_c[:, hd * QBLK:(hd + 1) * QBLK]
        imp = jnp.zeros((LANES, QBLK), F32)
        for p_t in _split2(psum):
            imp = imp + _dot(ovlt_ref[...], p_t)
        yield

        tq = t0 + lax.broadcasted_iota(jnp.int32, (LANES, QBLK), 1)
        m_i = lax.broadcasted_iota(jnp.int32, (LANES, QBLK), 0)
        cur = lax.shift_right_logical(tq, 6)
        forced = (m_i == 0) | (m_i == cur) | (m_i == cur - 1)
        causal = m_i * SEL_LEN <= tq
        n_forced = 3
        score = jnp.where(causal, jnp.where(forced, -jnp.inf, imp), -FORCE_SCORE)
        score = jnp.where(m_i < n_sel, score, SEL_PAD_SCORE)
        m_f = m_i.astype(F32)
        bias = jnp.where(forced & causal, 0.0, SEL_MASK_BIAS)
        for _ in range(min(SEL_TOPK, n_sel) - n_forced):
            mx = jnp.max(score, axis=0, keepdims=True)
            first = jnp.min(jnp.where(score == mx, m_f, float(LANES)), axis=0, keepdims=True)
            pick = m_f == first
            bias = jnp.where(pick, 0.0, bias)
            score = jnp.where(pick, -jnp.inf, score)
        bias = bias.astype(BF16)
        qext = jnp.concatenate([qt, jnp.concatenate([bias] * NSA_HPG, axis=1)], axis=0)
        sc0 = _dot(ksel_ref[g, 0:span, :], qext)
        yield

        w_row = lax.broadcasted_iota(jnp.int32, (QBLK, cols), 0)
        tq_w = tq_of(QBLK)
        kp_old = t0 - WINDOW + w_row
        valid_old = (kp_old > tq_w - WINDOW) & (kp_old >= 0)
        s_old = jnp.where(valid_old, s_old, NEG_INF)
        mid_row = mid0 + lax.broadcasted_iota(jnp.int32, (WINDOW - QBLK, cols), 0)
        s_mid = jnp.where(mid_row < t0, s_mid, NEG_INF)
        valid_dg = t0 + w_row <= tq_w
        s_dg = jnp.where(valid_dg, s_dg, NEG_INF)
        m_w = jnp.maximum(jnp.maximum(jnp.max(s_old, axis=0, keepdims=True),
                                      jnp.max(s_mid, axis=0, keepdims=True)),
                          jnp.max(s_dg, axis=0, keepdims=True))
        p_old = jnp.where(valid_old, jnp.exp2(s_old - m_w), 0.0)
        p_mid = jnp.exp2(s_mid - m_w)
        p_dg = jnp.where(valid_dg, jnp.exp2(s_dg - m_w), 0.0)
        acc_w = (_dot(v_old, p_old.astype(BF16)) + _dot(v_mid, p_mid.astype(BF16))
                 + _dot(v_dg, p_dg.astype(BF16)))
        out_pre = out_pre + gate_row(gates_t, 2) * (acc_w[0:HEAD_DIM, :]
                                                    / jnp.maximum(acc_w[HEAD_DIM:HEAD_DIM + 1, :], TINY))

        return out_pre, gate_row(gates_t, 1), qext, sc0

    pre = {}
    gens = {g: prologue(g) for g in groups}
    order = [0] + [g for _ in range(8) for g in groups]
    for g in order:
        if g not in pre:
            try:
                next(gens[g])
            except StopIteration as done:
                pre[g] = done.value
    assert len(pre) == len(groups)
    for g in groups:
        qt_s[g] = pre[g][2]
        s_s[g, 0] = pre[g][3]
        p_s[g, 1] = jnp.zeros((span, cols), BF16)
        acc_s[g] = jnp.zeros((V_ROWS, cols), F32)

    def sel_step(cur, j, ms):
        nxt = 1 - cur
        out = []
        for g in groups:
            pv = span_pv(g, jnp.maximum(j - 1, 0), p_s[g, nxt])
            s = s_s[g, cur]
            m_new = jnp.maximum(ms[g], jnp.max(s, axis=0, keepdims=True))
            p_s[g, cur] = jnp.exp2(s - m_new).astype(BF16)
            acc_s[g] = jnp.exp2(ms[g] - m_new) * (acc_s[g] + pv)
            out.append(m_new)
        for g in groups:
            s_s[g, nxt] = span_scores(g, j + 1)
        return tuple(out)

    def sel_body(j, ms):
        return lax.cond((j & 1) == 0, lambda c: sel_step(0, j, c), lambda c: sel_step(1, j, c), ms)

    n_span = t0 // span
    ms = lax.fori_loop(0, n_span, sel_body, tuple(jnp.full((1, cols), NEG_INF, F32) for _ in groups))

    slot = n_span & 1
    base = pl.multiple_of(n_span * span, span)

    def diag(nk):
        valid = lax.broadcasted_iota(jnp.int32, (nk, cols), 0) <= tq_of(1) - base
        for g in groups:
            pv = span_pv(g, jnp.maximum(n_span - 1, 0), p_s[g, 1 - slot])
            s = jnp.where(valid, s_s[g, slot, 0:nk, :], NEG_INF)
            m_new = jnp.maximum(ms[g], jnp.max(s, axis=0, keepdims=True))
            p = jnp.where(valid, jnp.exp2(s - m_new), 0.0).astype(BF16)
            acc_s[g] = (jnp.exp2(ms[g] - m_new) * (acc_s[g] + pv)
                        + _dot(vselt_ref[g, :, pl.ds(base, nk)], p))

    lax.cond(t0 - base >= span // 2, lambda: diag(span), lambda: diag(span // 2))

    for g in groups:
        out_pre, gate_sel = pre[g][0:2]
        acc = acc_s[g]
        o_slc = acc[0:HEAD_DIM, :] / jnp.maximum(acc[HEAD_DIM:HEAD_DIM + 1, :], TINY)
        out = out_pre + gate_sel * o_slc
        for hd in range(NSA_HPG):
            o_ref[:, g * gw + hd * HEAD_DIM:g * gw + (hd + 1) * HEAD_DIM] = (
                out[:, hd * QBLK:(hd + 1) * QBLK].T.astype(BF16))


def _nsa(q, kcmp, vcmpt, ksel, vselt, kwin, vwint, ngt, overlap_t):
    s = q.shape[0]
    n_sel = s // SEL_LEN
    span = SEL_SPAN_TILES * SEL_KT
    assert n_sel <= LANES and s % span == 0 and s >= WINDOW + QBLK and CMP_STRIDE & (CMP_STRIDE - 1) == 0
    cols = NSA_HPG * QBLK
    ng = NSA_KV_HEADS
    return pl.pallas_call(
        functools.partial(_nsa_body, n_sel=n_sel),
        grid=(s // QBLK,),
        in_specs=[
            pl.BlockSpec((QBLK, NSA_HEADS * HEAD_DIM), lambda b: (b, 0)),
            _resident(kcmp.shape), _resident(vcmpt.shape),
            _resident(ksel.shape), _resident(vselt.shape), _resident(kwin.shape), _resident(vwint.shape),
            pl.BlockSpec((ng, _NG_PAD, QBLK), lambda b: (0, 0, b)),
            _resident(overlap_t.shape),
        ],
        out_specs=pl.BlockSpec((QBLK, NSA_HEADS * HEAD_DIM), lambda b: (b, 0)),
        out_shape=jax.ShapeDtypeStruct((s, NSA_HEADS * HEAD_DIM), BF16),
        scratch_shapes=[pltpu.VMEM((ng, 2 * HEAD_DIM, cols), BF16),
                        pltpu.VMEM((ng, 2, span, cols), F32),
                        pltpu.VMEM((ng, 2, span, cols), BF16),
                        pltpu.VMEM((ng, V_ROWS, cols), F32)],
        compiler_params=_params(("arbitrary",)),
        name="nsa",
    )(q, kcmp, vcmpt, ksel, vselt, kwin, vwint, ngt, overlap_t)


def _memkv_body(mem_ref, g_ref, w_ref, kn_ref, k_ref, v_ref):
    kv = _dot(_rms(mem_ref[...], g_ref[...]).astype(BF16), w_ref[...].astype(BF16))
    width = MEM_HEADS * HEAD_DIM
    for hd in range(MEM_HEADS):
        sl = slice(hd * HEAD_DIM, (hd + 1) * HEAD_DIM)
        k_ref[:, sl] = _rms(kv[:, sl], kn_ref[...]).astype(BF16)
    v_ref[...] = kv[:, width:].astype(BF16)


def _memkv(mem, in_g, w_kv, k_norm):
    m = mem.shape[0]
    shp = jax.ShapeDtypeStruct((m, MEM_HEADS * HEAD_DIM), BF16)
    return pl.pallas_call(
        _memkv_body, out_shape=[shp, shp],
        compiler_params=pltpu.CompilerParams(vmem_limit_bytes=VMEM_LIMIT),
        name="memkv",
    )(mem, in_g, w_kv, k_norm)


def _memattn_body(q_ref, k_ref, v_ref, o_ref):
    for hd in range(MEM_HEADS):
        sl = slice(hd * HEAD_DIM, (hd + 1) * HEAD_DIM)
        sc = _dot_nt(q_ref[:, sl], k_ref[:, sl])
        e = jnp.exp(sc - jnp.max(sc, axis=-1, keepdims=True))
        p = e / jnp.sum(e, axis=-1, keepdims=True)
        o_ref[:, sl] = _dot(p.astype(BF16), v_ref[:, sl]).astype(BF16)


def _memattn(q, k, v, *, tm=512):
    s, w = q.shape
    return pl.pallas_call(
        _memattn_body,
        grid=(s // tm,),
        in_specs=[pl.BlockSpec((tm, w), lambda i: (i, 0)), _resident(k.shape), _resident(v.shape)],
        out_specs=pl.BlockSpec((tm, w), lambda i: (i, 0)),
        out_shape=jax.ShapeDtypeStruct((s, w), BF16),
        compiler_params=_params(("parallel",)),
        name="memattn",
    )(q, k, v)


def _outproj_body(x_ref, a_ref, b_ref, c_ref, w_ref, o_ref):
    na, nb = a_ref.shape[1], b_ref.shape[1]
    o_ref[...] = (x_ref[...] + _dot(a_ref[...], w_ref[0:na, :].astype(BF16))
                  + _dot(b_ref[...], w_ref[na:na + nb, :].astype(BF16))
                  + _dot(c_ref[...], w_ref[na + nb:, :].astype(BF16)))


def _outproj(x1, o_gla, o_nsa, o_mem, w_out, *, tm=512):
    s, d = x1.shape
    row = lambda w: pl.BlockSpec((tm, w), lambda i: (i, 0))
    return pl.pallas_call(
        _outproj_body,
        grid=(s // tm,),
        in_specs=[row(d), row(o_gla.shape[1]), row(o_nsa.shape[1]), row(o_mem.shape[1]),
                  _resident(w_out.shape)],
        out_specs=row(d),
        out_shape=jax.ShapeDtypeStruct((s, d), F32),
        compiler_params=_params(("parallel",)),
        name="outproj",
    )(x1, o_gla, o_nsa, o_mem, w_out)


def _layer(x, mem, positions, ffn1_norm, ffn1_w_gate, ffn1_w_up, ffn1_w_down, mix_norm, w_in,
           gla_w_a, gla_b_a, gla_o_norm, nsa_q_norm, nsa_k_norm, nsa_cmp_pos_k, nsa_cmp_w1_k,
           nsa_cmp_w2_k, nsa_cmp_pos_v, nsa_cmp_w1_v, nsa_cmp_w2_v, mem_in_norm, w_mem_kv,
           mem_q_norm, mem_k_norm, w_out, ffn2_norm, ffn2_w_gate, ffn2_w_up, ffn2_w_down, final_norm):
    s, d = x.shape
    row = lambda v: v.reshape(1, -1)
    bf = lambda v: v.astype(BF16)

    x1 = _ffn(x, row(ffn1_norm), ffn1_w_gate, ffn1_w_up, ffn1_w_down)

    half = HEAD_DIM // 2
    inv = ROPE_THETA ** (-jnp.arange(half, dtype=F32) / half)
    inv128 = jnp.concatenate([inv, inv]).reshape(1, HEAD_DIM)
    sgn128 = jnp.concatenate([-jnp.ones((half,), F32), jnp.ones((half,), F32)]).reshape(1, HEAD_DIM)
    assert w_in.shape[1] == _R_END
    w_t = bf(w_in.T)
    (q, ksel, vselt, kwin, vwint, mq, gqk, gv, gr, kcvc, ga, ngt) = _proj(
        x1, row(mix_norm), w_t, w_t[_R_MQ:], positions.reshape(s, 1), inv128, sgn128,
        row(nsa_q_norm), nsa_k_norm, row(mem_q_norm))

    o_gla = _gla(gqk, gv, gr, ga, gla_w_a, row(gla_b_a), row(gla_o_norm))

    units = s // CMP_STRIDE
    n_cmp = (s - CMP_LEN) // CMP_STRIDE + 1
    cmp_last = jnp.arange(units) * CMP_STRIDE + CMP_LEN - 1
    pos_cmp = positions[jnp.minimum(cmp_last, s - 1)].reshape(units, 1)
    kcmp, vcmpt = _compress(kcvc, nsa_cmp_w1_k, bf(nsa_cmp_w2_k), nsa_cmp_pos_k,
                            nsa_cmp_w1_v, bf(nsa_cmp_w2_v.T), nsa_cmp_pos_v, nsa_k_norm,
                            pos_cmp, inv128, sgn128)
    n_sel = s // SEL_LEN
    cmp_start = np.arange(units) * CMP_STRIDE
    sel_start = np.arange(LANES) * SEL_LEN
    overlap = np.clip(np.minimum(cmp_start[:, None] + CMP_LEN, sel_start[None, :] + SEL_LEN)
                      - np.maximum(cmp_start[:, None], sel_start[None, :]), 0, None) / CMP_STRIDE
    overlap = overlap * (np.arange(units)[:, None] < n_cmp) * (np.arange(LANES)[None, :] < n_sel)
    o_nsa = _nsa(q, kcmp, vcmpt, ksel, vselt, kwin, vwint, ngt, jnp.asarray(overlap.T, BF16))

    kmem, vmem = _memkv(mem, row(mem_in_norm), w_mem_kv, row(mem_k_norm))
    o_mem = _memattn(mq, kmem, vmem)

    x2 = _outproj(x1, o_gla, o_nsa, o_mem, w_out)
    return _ffn(x2, row(ffn2_norm), ffn2_w_gate, ffn2_w_up, ffn2_w_down, row(final_norm))


def kernel(x, mem, positions, ffn1_norm, ffn1_w_gate, ffn1_w_up, ffn1_w_down, mix_norm, w_in, gla_w_a, gla_b_a, gla_o_norm, nsa_q_norm, nsa_k_norm, nsa_cmp_pos_k, nsa_cmp_w1_k, nsa_cmp_w2_k, nsa_cmp_pos_v, nsa_cmp_w1_v, nsa_cmp_w2_v, mem_in_norm, w_mem_kv, mem_q_norm, mem_k_norm, w_out, ffn2_norm, ffn2_w_gate, ffn2_w_up, ffn2_w_down, final_norm):
    depth = ffn1_norm.shape[0]
    batch, s, d = x.shape
    outs = []
    for b in range(batch):
        xb, mem_b, pos_b = (x.reshape(s, d), mem.reshape(mem.shape[1:]), positions.reshape(s)) if batch == 1 \
            else (x[b], mem[b], positions[b])
        for l in range(depth):
            xb = _layer(xb, mem_b, pos_b, ffn1_norm[l], ffn1_w_gate[l], ffn1_w_up[l], ffn1_w_down[l],
                        mix_norm[l], w_in[l], gla_w_a[l], gla_b_a[l], gla_o_norm[l], nsa_q_norm[l],
                        nsa_k_norm[l], nsa_cmp_pos_k[l], nsa_cmp_w1_k[l], nsa_cmp_w2_k[l], nsa_cmp_pos_v[l],
                        nsa_cmp_w1_v[l], nsa_cmp_w2_v[l], mem_in_norm[l], w_mem_kv[l], mem_q_norm[l],
                        mem_k_norm[l], w_out[l], ffn2_norm[l], ffn2_w_gate[l], ffn2_w_up[l], ffn2_w_down[l],
                        final_norm[l])
        outs.append(xb)
    return outs[0].reshape(1, s, d) if batch == 1 else jnp.stack(outs)
```

```python
import functools

import numpy as np
import jax
import jax.numpy as jnp
from jax import lax
from jax.experimental import pallas as pl
from jax.experimental.pallas import tpu as pltpu

F32 = jnp.float32
BF16 = jnp.bfloat16

HEAD_DIM = 128
GLA_HEADS = 4
GLA_DK = 64
GLA_DV = 128
GLA_LOWRANK = 16
GLA_TAU = 16.0
NSA_HEADS = 8
NSA_KV_HEADS = 2
NSA_HPG = NSA_HEADS // NSA_KV_HEADS
CMP_LEN = 32
CMP_STRIDE = 16
SEL_LEN = 64
SEL_TOPK = 16
WINDOW = 512
MEM_HEADS = 4
MACARON_W = 0.5
QBLK = 128
ROPE_THETA = 10000.0
EPS = 1e-6
NEG_INF = -1e30
TINY = 1e-30
FORCE_SCORE = 1e4
LOG2_E = 1.4426950408889634

LANES = 128
VMEM_LIMIT = 56 * 1024 * 1024

GLA_SUB = 16
GLA_ROWS = 256
SEL_KT = 256
SEL_SPAN_TILES = 4
V_ROWS = HEAD_DIM + 16
SEL_MASK_BIAS = -32768.0
SEL_PAD_SCORE = -3e4


def _dot(a, b):
    return jnp.dot(a, b, preferred_element_type=F32)


def _dot_nt(a, b):
    return lax.dot_general(a, b, (((1,), (1,)), ((), ())), preferred_element_type=F32)


def _dot_tn(a, b):
    return lax.dot_general(a, b, (((0,), (0,)), ((), ())), preferred_element_type=F32)


def _split2(x):
    hi = x.astype(BF16)
    return hi, (x - hi.astype(F32)).astype(BF16)


def _rms(x, g):
    return x * lax.rsqrt(jnp.mean(x * x, axis=-1, keepdims=True) + EPS) * g


def _sigmoid(x):
    return 1.0 / (1.0 + jnp.exp(-x))


def _params(sem):
    return pltpu.CompilerParams(dimension_semantics=sem, vmem_limit_bytes=VMEM_LIMIT)


def _resident(shape):
    nd = len(shape)
    return pl.BlockSpec(shape, lambda *_: (0,) * nd, pipeline_mode=pl.Buffered(1))


def _ffn_body(*refs, final, nf):
    if final:
        x_ref, g_ref, wg_ref, wu_ref, wd_ref, fg_ref, o_ref, h_ref = refs
    else:
        x_ref, g_ref, wg_ref, wu_ref, wd_ref, o_ref, h_ref = refs
    f = pl.program_id(1)

    @pl.when(f == 0)
    def _():
        x = x_ref[...]
        h_ref[...] = _rms(x, g_ref[...]).astype(BF16)
        o_ref[...] = x

    h = h_ref[...]
    g = _dot(h, wg_ref[...].astype(BF16))
    u = _dot(h, wu_ref[...].astype(BF16))
    a = (g * _sigmoid(g)) * u * MACARON_W
    o_ref[...] += _dot(a.astype(BF16), wd_ref[...].astype(BF16))

    if final:
        @pl.when(f == nf - 1)
        def _():
            o_ref[...] = _rms(o_ref[...], fg_ref[...])


def _ffn(x, norm_g, wg, wu, wd, final_g=None, *, tm=1024, tf=256):
    s, d = x.shape
    ff = wg.shape[1]
    nf = ff // tf
    final = final_g is not None
    in_specs = [
        pl.BlockSpec((tm, d), lambda i, f: (i, 0)),
        pl.BlockSpec((1, d), lambda i, f: (0, 0)),
        pl.BlockSpec((d, tf), lambda i, f: (0, f)),
        pl.BlockSpec((d, tf), lambda i, f: (0, f)),
        pl.BlockSpec((tf, d), lambda i, f: (f, 0)),
    ]
    args = [x, norm_g, wg, wu, wd]
    if final:
        in_specs.append(pl.BlockSpec((1, d), lambda i, f: (0, 0)))
        args.append(final_g)
    return pl.pallas_call(
        functools.partial(_ffn_body, final=final, nf=nf),
        grid=(s // tm, nf),
        in_specs=in_specs,
        out_specs=pl.BlockSpec((tm, d), lambda i, f: (i, 0)),
        out_shape=jax.ShapeDtypeStruct((s, d), F32),
        scratch_shapes=[pltpu.VMEM((tm, d), BF16)],
        compiler_params=_params(("parallel", "arbitrary")),
        name="ffn_final" if final else "ffn",
    )(*args)


_GQK_W = 2 * GLA_HEADS * GLA_DK
_GV_W = GLA_HEADS * GLA_DV
_KV_W = NSA_KV_HEADS * HEAD_DIM
_PER_G = NSA_HPG * 3
_NG_PAD = 16
_R_GQK = 0
_R_GV = _R_GQK + _GQK_W
_R_GR = _R_GV + _GV_W
_R_GA = _R_GR + _GV_W
_R_NQ = _R_GA + GLA_LOWRANK
_R_KC = _R_NQ + NSA_HEADS * HEAD_DIM
_R_KS = _R_KC + 2 * _KV_W
_R_VS = _R_KS + _KV_W
_R_KW = _R_VS + _KV_W
_R_VW = _R_KW + _KV_W
_R_NG = _R_VW + _KV_W
_R_MQ = _R_NG + NSA_KV_HEADS * _PER_G
_R_END = _R_MQ + MEM_HEADS * HEAD_DIM


def _rope(x, cos, sin_signed):
    return x * cos + pltpu.roll(x, HEAD_DIM // 2, 1) * sin_signed


def _proj_body(x_ref, g_ref, wt_ref, wmq_ref, pos_ref, inv_ref, sgn_ref, qn_ref, kn_ref,
               mqn_ref, q_ref, ksel_ref, vselt_ref, kwin_ref, vwint_ref, mq_ref, gqk_ref,
               gv_ref, gr_ref, kcvc_ref, ga_ref, ngt_ref, *, tm):
    i = pl.program_id(0)
    h = _rms(x_ref[...], g_ref[...]).astype(BF16)

    def proj(r0, width):
        return _dot_nt(h, wt_ref[r0:r0 + width, :])

    def proj_t(r0, width):
        return _dot_nt(wt_ref[r0:r0 + width, :], h)

    ang = pos_ref[...].astype(F32) * inv_ref[...]
    cos = jnp.cos(ang)
    sin_signed = jnp.sin(ang) * sgn_ref[...]

    def norm_rope(t, gain):
        return _rope(_rms(t, gain), cos, sin_signed)

    scale = HEAD_DIM ** -0.5
    nq = proj(_R_NQ, NSA_HEADS * HEAD_DIM)
    for hd in range(NSA_HEADS):
        sl = slice(hd * HEAD_DIM, (hd + 1) * HEAD_DIM)
        q_ref[:, sl] = (norm_rope(nq[:, sl], qn_ref[...]) * (scale * LOG2_E)).astype(BF16)

    tok = i * tm + lax.broadcasted_iota(jnp.int32, (tm, LANES), 0)
    blk = lax.broadcasted_iota(jnp.int32, (tm, LANES), 1)
    onehot = jnp.where(lax.shift_right_logical(tok, 6) == blk, 1.0, 0.0).astype(BF16)
    ks = proj(_R_KS, _KV_W)
    kw = proj(_R_KW, _KV_W)
    vst = proj_t(_R_VS, _KV_W)
    vwt = proj_t(_R_VW, _KV_W)
    ngt = proj_t(_R_NG, NSA_KV_HEADS * _PER_G)
    ones_rows = jnp.where(lax.broadcasted_iota(jnp.int32, (V_ROWS - HEAD_DIM, tm), 0) == 0, 1.0, 0.0).astype(BF16)
    for g in range(NSA_KV_HEADS):
        sl = slice(g * HEAD_DIM, (g + 1) * HEAD_DIM)
        ksel_ref[g, :, 0:HEAD_DIM] = norm_rope(ks[:, sl], kn_ref[1:2, :]).astype(BF16)
        ksel_ref[g, :, HEAD_DIM:2 * HEAD_DIM] = onehot
        kwin_ref[g] = norm_rope(kw[:, sl], kn_ref[2:3, :]).astype(BF16)
        vselt_ref[g, 0:HEAD_DIM, :] = vst[sl, :].astype(BF16)
        vwint_ref[g, 0:HEAD_DIM, :] = vwt[sl, :].astype(BF16)
        vselt_ref[g, HEAD_DIM:V_ROWS, :] = ones_rows
        vwint_ref[g, HEAD_DIM:V_ROWS, :] = ones_rows
        ngt_ref[g, 0:_PER_G, :] = ngt[g * _PER_G:(g + 1) * _PER_G, :]
        ngt_ref[g, _PER_G:_NG_PAD, :] = jnp.zeros((_NG_PAD - _PER_G, tm), F32)

    mq = _dot_nt(h, wmq_ref[...])
    for hd in range(MEM_HEADS):
        sl = slice(hd * HEAD_DIM, (hd + 1) * HEAD_DIM)
        mq_ref[:, sl] = (_rms(mq[:, sl], mqn_ref[...]) * scale).astype(BF16)

    gqk_ref[...] = proj(_R_GQK, _GQK_W)
    gv_ref[...] = proj(_R_GV, _GV_W).astype(BF16)
    gr_ref[...] = proj(_R_GR, _GV_W)
    ga_ref[...] = proj(_R_GA, GLA_LOWRANK)
    kcvc = proj(_R_KC, 2 * _KV_W)
    for j in range(2 * NSA_KV_HEADS):
        kcvc_ref[j] = kcvc[:, j * HEAD_DIM:(j + 1) * HEAD_DIM]


def _proj(x1, mix_g, w_t, w_mq_t, pos_col, inv128, sgn128, q_norm, k_norm, mq_norm, *, tm=512):
    s, d = x1.shape
    row = lambda w: pl.BlockSpec((tm, w), lambda i: (i, 0))
    grp = lambda w: pl.BlockSpec((NSA_KV_HEADS, tm, w), lambda i: (0, i, 0))
    grpt = pl.BlockSpec((NSA_KV_HEADS, V_ROWS, tm), lambda i: (0, 0, i))
    vt_shape = jax.ShapeDtypeStruct((NSA_KV_HEADS, V_ROWS, s), BF16)
    out_shapes = [
        (jax.ShapeDtypeStruct((s, NSA_HEADS * HEAD_DIM), BF16), row(NSA_HEADS * HEAD_DIM)),
        (jax.ShapeDtypeStruct((NSA_KV_HEADS, s, 2 * HEAD_DIM), BF16), grp(2 * HEAD_DIM)),
        (vt_shape, grpt),
        (jax.ShapeDtypeStruct((NSA_KV_HEADS, s, HEAD_DIM), BF16), grp(HEAD_DIM)),
        (vt_shape, grpt),
        (jax.ShapeDtypeStruct((s, MEM_HEADS * HEAD_DIM), BF16), row(MEM_HEADS * HEAD_DIM)),
        (jax.ShapeDtypeStruct((s, 512), F32), row(512)),
        (jax.ShapeDtypeStruct((s, 512), BF16), row(512)),
        (jax.ShapeDtypeStruct((s, 512), F32), row(512)),
        (jax.ShapeDtypeStruct((2 * NSA_KV_HEADS, s, HEAD_DIM), F32),
         pl.BlockSpec((2 * NSA_KV_HEADS, tm, HEAD_DIM), lambda i: (0, i, 0))),
        (jax.ShapeDtypeStruct((s, GLA_LOWRANK), F32), row(GLA_LOWRANK)),
        (jax.ShapeDtypeStruct((NSA_KV_HEADS, _NG_PAD, s), F32),
         pl.BlockSpec((NSA_KV_HEADS, _NG_PAD, tm), lambda i: (0, 0, i))),
    ]
    return pl.pallas_call(
        functools.partial(_proj_body, tm=tm),
        grid=(s // tm,),
        in_specs=[
            row(d),
            _resident((1, d)),
            _resident(w_t.shape), _resident(w_mq_t.shape),
            pl.BlockSpec((tm, 1), lambda i: (i, 0)),
            _resident((1, LANES)), _resident((1, LANES)),
            _resident((1, HEAD_DIM)), _resident((3, HEAD_DIM)), _resident((1, HEAD_DIM)),
        ],
        out_specs=[o[1] for o in out_shapes],
        out_shape=[o[0] for o in out_shapes],
        compiler_params=_params(("parallel",)),
        name="proj",
    )(x1, mix_g, w_t, w_mq_t, pos_col, inv128, sgn128, q_norm, k_norm, mq_norm)


def _compress_body(kcvc_ref, w1k_ref, w2k_ref, pek_ref, w1v_ref, w2v_ref, pev_ref, kn_ref,
                   pos_ref, inv_ref, sgn_ref, kcmp_ref, vcmp_ref, *, units):
    half = CMP_LEN // 2
    ang = pos_ref[...].astype(F32) * inv_ref[...]
    cos = jnp.cos(ang)
    sin_signed = jnp.sin(ang) * sgn_ref[...]
    for kind, (w1_ref, w2_ref, pe_ref) in enumerate(((w1k_ref, w2k_ref, pek_ref),
                                                     (w1v_ref, w2v_ref, pev_ref))):
        for g in range(NSA_KV_HEADS):
            slab = kind * NSA_KV_HEADS + g
            a = jnp.zeros((units, w1_ref.shape[1]), F32)
            b = jnp.zeros((units, w1_ref.shape[1]), F32)
            for l in range(half):
                t = kcvc_ref[slab, pl.ds(l, units, stride=CMP_STRIDE), :]
                a = a + _dot((t + pe_ref[l:l + 1, :]).astype(BF16),
                             w1_ref[l * HEAD_DIM:(l + 1) * HEAD_DIM, :].astype(BF16))
                b = b + _dot((t + pe_ref[half + l:half + l + 1, :]).astype(BF16),
                             w1_ref[(half + l) * HEAD_DIM:(half + l + 1) * HEAD_DIM, :].astype(BF16))
            hid = a + pltpu.roll(b, units - 1, 0)
            act = (hid * _sigmoid(hid)).astype(BF16)
            if kind == 0:
                c = _rope(_rms(_dot(act, w2_ref[...]), kn_ref[0:1, :]), cos, sin_signed)
                kcmp_ref[g] = c.astype(BF16)
            else:
                vcmp_ref[g] = _dot_nt(w2_ref[...], act).astype(BF16)


def _compress(kcvc, w1k, w2k, pek, w1v, w2v, pev, k_norm, pos_cmp, inv128, sgn128):
    s = kcvc.shape[1]
    units = s // CMP_STRIDE
    shp = jax.ShapeDtypeStruct((NSA_KV_HEADS, units, HEAD_DIM), BF16)
    shp_t = jax.ShapeDtypeStruct((NSA_KV_HEADS, HEAD_DIM, units), BF16)
    return pl.pallas_call(
        functools.partial(_compress_body, units=units),
        out_shape=[shp, shp_t],
        compiler_params=pltpu.CompilerParams(vmem_limit_bytes=VMEM_LIMIT),
        name="compress",
    )(kcvc, w1k, w2k, pek, w1v, w2v, pev, k_norm, pos_cmp, inv128, sgn128)


def _gla_body(gqk_ref, gv_ref, gr_ref, ga_ref, ga_next_ref, wa_ref, ba_ref, on_ref, tcum_ref, bd_ref, hsel_ref,
              o_ref, st_ref, b_s):
    rows = GLA_ROWS
    npair = GLA_HEADS // 2

    def cum_log_decay(ga):
        z = ba_ref[...]
        for ga_t in _split2(ga):
            for wa_t in _split2(wa_ref[...]):
                z = z + _dot(ga_t, wa_t)
        la = (jnp.minimum(z, 0.0) - jnp.log(1.0 + jnp.exp(-jnp.abs(z)))) / GLA_TAU
        bcum = jnp.zeros_like(la)
        for la_t in _split2(la):
            bcum = bcum + _dot(tcum_ref[...], la_t)
        return bcum * LOG2_E

    @pl.when(pl.program_id(0) == 0)
    def _():
        st_ref[...] = jnp.zeros_like(st_ref)
        b_s[...] = cum_log_decay(ga_ref[...])

    b_all = b_s[...]
    q_all = gqk_ref[:, 0:256] * (GLA_DK ** -0.5)
    k_all = gqk_ref[:, 256:512]
    v_all = gv_ref[...]

    row_i = lax.broadcasted_iota(jnp.int32, (GLA_SUB, LANES), 0)

    sts = [st_ref[p] for p in range(npair)]
    o_rows = []
    for sb in range(rows // GLA_SUB):
        rs = slice(sb * GLA_SUB, (sb + 1) * GLA_SUB)
        o_pairs = []
        for p in range(npair):
            cs = slice(p * LANES, (p + 1) * LANES)
            vs = slice(p * 2 * GLA_DV, (p + 1) * 2 * GLA_DV)
            qs = q_all[rs, cs]
            kk = k_all[rs, cs]
            bb = b_all[rs, cs]
            vp = v_all[rs, vs]
            vpf = vp.astype(F32)
            blast = bb[GLA_SUB - 1:GLA_SUB, :]
            st = sts[p]
            o_inter = _dot_nt((qs * jnp.exp2(bb)).astype(BF16), st.astype(BF16))
            xs = []
            for j in range(GLA_SUB):
                dlt = jnp.where(row_i >= j, bb - bb[j:j + 1, :], NEG_INF)
                xs.append(qs * jnp.exp2(dlt) * kk[j:j + 1, :])
            red = _dot(jnp.concatenate(xs, axis=0).astype(BF16), hsel_ref[...])
            acc = o_inter
            for j in range(GLA_SUB):
                acc = acc + red[j * GLA_SUB:(j + 1) * GLA_SUB, :] * vpf[j:j + 1, :]
            o_pairs.append(acc)
            kd = (kk * jnp.exp2(blast - bb)).astype(BF16)
            upd = _dot_tn(vp, kd)
            sts[p] = st * jnp.exp2(blast) + upd * bd_ref[...]
        o_rows.append(jnp.concatenate(o_pairs, axis=1))
    for p in range(npair):
        st_ref[p] = sts[p]
    b_s[...] = cum_log_decay(ga_next_ref[...])

    o_all = jnp.concatenate(o_rows, axis=0)
    gr = gr_ref[...]
    for hd in range(GLA_HEADS):
        sl = slice(hd * GLA_DV, (hd + 1) * GLA_DV)
        r = gr[:, sl]
        o_ref[:, sl] = (_rms(o_all[:, sl], on_ref[...]) * (r * _sigmoid(r))).astype(BF16)


def _gla(gqk, gv, gr, ga, wa, ba, o_norm):
    s = gqk.shape[0]
    rows = GLA_ROWS
    idx = np.arange(rows)
    tcum = ((idx[:, None] >= idx[None, :]) & (idx[:, None] // GLA_SUB == idx[None, :] // GLA_SUB))
    tcum = jnp.asarray(tcum, BF16)
    r256 = np.arange(2 * GLA_DV)[:, None] // GLA_DV
    c128 = np.arange(LANES)[None, :] // GLA_DK
    bdmask = jnp.asarray(r256 == c128, F32)
    hsel = jnp.asarray((r256 == c128).T, BF16)
    row = lambda w: pl.BlockSpec((rows, w), lambda i: (i, 0))
    nblk = s // rows
    ga_next = pl.BlockSpec((rows, GLA_LOWRANK), lambda i: (jnp.minimum(i + 1, nblk - 1), 0))
    return pl.pallas_call(
        _gla_body,
        grid=(nblk,),
        in_specs=[row(512), row(512), row(512), row(GLA_LOWRANK), ga_next,
                  _resident(wa.shape), _resident(ba.shape), _resident(o_norm.shape),
                  _resident(tcum.shape), _resident(bdmask.shape), _resident(hsel.shape)],
        out_specs=row(512),
        out_shape=jax.ShapeDtypeStruct((s, GLA_HEADS * GLA_DV), BF16),
        scratch_shapes=[pltpu.VMEM((GLA_HEADS // 2, 2 * GLA_DV, LANES), F32),
                        pltpu.VMEM((rows, GLA_HEADS * GLA_DK), F32)],
        compiler_params=_params(("arbitrary",)),
        name="gla",
    )(gqk, gv, gr, ga, ga, wa, ba, o_norm, tcum, bdmask, hsel)


def _nsa_body(q_ref, kcmp_ref, vcmpt_ref, ksel_ref, vselt_ref, kwin_ref, vwint_ref, ngt_ref, ovlt_ref,
              o_ref, qt_s, s_s, p_s, acc_s, *, n_sel):
    qb = pl.program_id(0)
    t0 = qb * QBLK
    cols = NSA_HPG * QBLK
    gw = NSA_HPG * HEAD_DIM
    ncmp = kcmp_ref.shape[1]
    groups = range(NSA_KV_HEADS)
    span = SEL_SPAN_TILES * SEL_KT

    def tq_of(rows):
        return t0 + (lax.broadcasted_iota(jnp.int32, (rows, cols), 1) & (QBLK - 1))

    def gate_row(gates_t, c):
        return jnp.concatenate([gates_t[3 * hd + c:3 * hd + c + 1, :] for hd in range(NSA_HPG)], axis=1)

    def span_scores(g, j):
        k0 = pl.multiple_of(j * span, span)
        return _dot(ksel_ref[g, pl.ds(k0, span), :], qt_s[g])

    def span_pv(g, j, p):
        k0 = pl.multiple_of(j * span, span)
        return _dot(vselt_ref[g, :, pl.ds(k0, span)], p)

    def prologue(g):
        qt = jnp.concatenate(
            [q_ref[:, g * gw + hd * HEAD_DIM:g * gw + (hd + 1) * HEAD_DIM].astype(F32).T.astype(BF16)
             for hd in range(NSA_HPG)], axis=1)
        gates_t = _sigmoid(ngt_ref[g])

        def win_part(start, length):
            start = pl.multiple_of(jnp.maximum(start, 0), QBLK)
            return (_dot(kwin_ref[g, pl.ds(start, length), :], qt), vwint_ref[g, :, pl.ds(start, length)], start)

        s_c = _dot(kcmp_ref[g], qt)
        s_old, v_old, _ = win_part(t0 - WINDOW, QBLK)
        s_mid, v_mid, mid0 = win_part(t0 - WINDOW + QBLK, WINDOW - QBLK)
        s_dg, v_dg, _ = win_part(t0, QBLK)
        yield

        n_row = lax.broadcasted_iota(jnp.int32, (ncmp, cols), 0)
        valid_c = n_row <= lax.shift_right_arithmetic(tq_of(1) - (CMP_LEN - 1), CMP_STRIDE.bit_length() - 1)
        s_c = jnp.where(valid_c, s_c, NEG_INF)
        e_c = jnp.where(valid_c, jnp.exp2(s_c - jnp.max(s_c, axis=0, keepdims=True)), 0.0)
        p_c = e_c / jnp.maximum(jnp.sum(e_c, axis=0, keepdims=True), TINY)
        out_pre = gate_row(gates_t, 0) * _dot(vcmpt_ref[g], p_c.astype(BF16))
        psum = p_c[:, 0:QBLK]
        for hd in range(1, NSA_HPG):
            psum = psum + p_c[:, hd * QBLK:(hd + 1) * QBLK]
        imp = jnp.zeros((LANES, QBLK), F32)
        for p_t in _split2(psum):
            imp = imp + _dot(ovlt_ref[...], p_t)
        yield

        tq = t0 + lax.broadcasted_iota(jnp.int32, (LANES, QBLK), 1)
        m_i = lax.broadcasted_iota(jnp.int32, (LANES, QBLK), 0)
        cur = lax.shift_right_logical(tq, 6)
        forced = (m_i == 0) | (m_i == cur) | (m_i == cur - 1)
        causal = m_i * SEL_LEN <= tq
        n_forced = 3
        score = jnp.where(causal, jnp.where(forced, -jnp.inf, imp), -FORCE_SCORE)
        score = jnp.where(m_i < n_sel, score, SEL_PAD_SCORE)
        m_f = m_i.astype(F32)
        bias = jnp.where(forced & causal, 0.0, SEL_MASK_BIAS)
        for _ in range(min(SEL_TOPK, n_sel) - n_forced):
            mx = jnp.max(score, axis=0, keepdims=True)
            first = jnp.min(jnp.where(score == mx, m_f, float(LANES)), axis=0, keepdims=True)
            pick = m_f == first
            bias = jnp.where(pick, 0.0, bias)
            score = jnp.where(pick, -jnp.inf, score)
        bias = bias.astype(BF16)
        qext = jnp.concatenate([qt, jnp.concatenate([bias] * NSA_HPG, axis=1)], axis=0)
        sc0 = _dot(ksel_ref[g, 0:span, :], qext)
        yield

        w_row = lax.broadcasted_iota(jnp.int32, (QBLK, cols), 0)
        tq_w = tq_of(QBLK)
        kp_old = t0 - WINDOW + w_row
        valid_old = (kp_old > tq_w - WINDOW) & (kp_old >= 0)
        s_old = jnp.where(valid_old, s_old, NEG_INF)
        mid_row = mid0 + lax.broadcasted_iota(jnp.int32, (WINDOW - QBLK, cols), 0)
        s_mid = jnp.where(mid_row < t0, s_mid, NEG_INF)
        valid_dg = t0 + w_row <= tq_w
        s_dg = jnp.where(valid_dg, s_dg, NEG_INF)
        m_w = jnp.maximum(jnp.maximum(jnp.max(s_old, axis=0, keepdims=True),
                                      jnp.max(s_mid, axis=0, keepdims=True)),
                          jnp.max(s_dg, axis=0, keepdims=True))
        p_old = jnp.where(valid_old, jnp.exp2(s_old - m_w), 0.0)
        p_mid = jnp.exp2(s_mid - m_w)
        p_dg = jnp.where(valid_dg, jnp.exp2(s_dg - m_w), 0.0)
        acc_w = (_dot(v_old, p_old.astype(BF16)) + _dot(v_mid, p_mid.astype(BF16))
                 + _dot(v_dg, p_dg.astype(BF16)))
        out_pre = out_pre + gate_row(gates_t, 2) * (acc_w[0:HEAD_DIM, :]
                                                    / jnp.maximum(acc_w[HEAD_DIM:HEAD_DIM + 1, :], TINY))

        return out_pre, gate_row(gates_t, 1), qext, sc0

    pre = {}
    gens = {g: prologue(g) for g in groups}
    order = [0] + [g for _ in range(8) for g in groups]
    for g in order:
        if g not in pre:
            try:
                next(gens[g])
            except StopIteration as done:
                pre[g] = done.value
    assert len(pre) == len(groups)
    for g in groups:
        qt_s[g] = pre[g][2]
        s_s[g, 0] = pre[g][3]
        p_s[g, 1] = jnp.zeros((span, cols), BF16)
        acc_s[g] = jnp.zeros((V_ROWS, cols), F32)

    def sel_step(cur, j, ms):
        nxt = 1 - cur
        out = []
        for g in groups:
            pv = span_pv(g, jnp.maximum(j - 1, 0), p_s[g, nxt])
            s = s_s[g, cur]
            m_new = jnp.maximum(ms[g], jnp.max(s, axis=0, keepdims=True))
            p_s[g, cur] = jnp.exp2(s - m_new).astype(BF16)
            acc_s[g] = jnp.exp2(ms[g] - m_new) * (acc_s[g] + pv)
            out.append(m_new)
        for g in groups:
            s_s[g, nxt] = span_scores(g, j + 1)
        return tuple(out)

    def sel_body(j, ms):
        return lax.cond((j & 1) == 0, lambda c: sel_step(0, j, c), lambda c: sel_step(1, j, c), ms)

    n_span = t0 // span
    ms = lax.fori_loop(0, n_span, sel_body, tuple(jnp.full((1, cols), NEG_INF, F32) for _ in groups))

    slot = n_span & 1
    base = pl.multiple_of(n_span * span, span)

    def diag(nk):
        valid = lax.broadcasted_iota(jnp.int32, (nk, cols), 0) <= tq_of(1) - base
        for g in groups:
            pv = span_pv(g, jnp.maximum(n_span - 1, 0), p_s[g, 1 - slot])
            s = jnp.where(valid, s_s[g, slot, 0:nk, :], NEG_INF)
            m_new = jnp.maximum(ms[g], jnp.max(s, axis=0, keepdims=True))
            p = jnp.where(valid, jnp.exp2(s - m_new), 0.0).astype(BF16)
            acc_s[g] = (jnp.exp2(ms[g] - m_new) * (acc_s[g] + pv)
                        + _dot(vselt_ref[g, :, pl.ds(base, nk)], p))

    lax.cond(t0 - base >= span // 2, lambda: diag(span), lambda: diag(span // 2))

    for g in groups:
        out_pre, gate_sel = pre[g][0:2]
        acc = acc_s[g]
        o_slc = acc[0:HEAD_DIM, :] / jnp.maximum(acc[HEAD_DIM:HEAD_DIM + 1, :], TINY)
        out = out_pre + gate_sel * o_slc
        for hd in range(NSA_HPG):
            o_ref[:, g * gw + hd * HEAD_DIM:g * gw + (hd + 1) * HEAD_DIM] = (
                out[:, hd * QBLK:(hd + 1) * QBLK].T.astype(BF16))


def _nsa(q, kcmp, vcmpt, ksel, vselt, kwin, vwint, ngt, overlap_t):
    s = q.shape[0]
    n_sel = s // SEL_LEN
    span = SEL_SPAN_TILES * SEL_KT
    assert n_sel <= LANES and s % span == 0 and s >= WINDOW + QBLK and CMP_STRIDE & (CMP_STRIDE - 1) == 0
    cols = NSA_HPG * QBLK
    ng = NSA_KV_HEADS
    return pl.pallas_call(
        functools.partial(_nsa_body, n_sel=n_sel),
        grid=(s // QBLK,),
        in_specs=[
            pl.BlockSpec((QBLK, NSA_HEADS * HEAD_DIM), lambda b: (b, 0)),
            _resident(kcmp.shape), _resident(vcmpt.shape),
            _resident(ksel.shape), _resident(vselt.shape), _resident(kwin.shape), _resident(vwint.shape),
            pl.BlockSpec((ng, _NG_PAD, QBLK), lambda b: (0, 0, b)),
            _resident(overlap_t.shape),
        ],
        out_specs=pl.BlockSpec((QBLK, NSA_HEADS * HEAD_DIM), lambda b: (b, 0)),
        out_shape=jax.ShapeDtypeStruct((s, NSA_HEADS * HEAD_DIM), BF16),
        scratch_shapes=[pltpu.VMEM((ng, 2 * HEAD_DIM, cols), BF16),
                        pltpu.VMEM((ng, 2, span, cols), F32),
                        pltpu.VMEM((ng, 2, span, cols), BF16),
                        pltpu.VMEM((ng, V_ROWS, cols), F32)],
        compiler_params=_params(("arbitrary",)),
        name="nsa",
    )(q, kcmp, vcmpt, ksel, vselt, kwin, vwint, ngt, overlap_t)


def _memkv_body(mem_ref, g_ref, w_ref, kn_ref, k_ref, v_ref):
    kv = _dot(_rms(mem_ref[...], g_ref[...]).astype(BF16), w_ref[...].astype(BF16))
    width = MEM_HEADS * HEAD_DIM
    for hd in range(MEM_HEADS):
        sl = slice(hd * HEAD_DIM, (hd + 1) * HEAD_DIM)
        k_ref[:, sl] = _rms(kv[:, sl], kn_ref[...]).astype(BF16)
    v_ref[...] = kv[:, width:].astype(BF16)


def _memkv(mem, in_g, w_kv, k_norm):
    m = mem.shape[0]
    shp = jax.ShapeDtypeStruct((m, MEM_HEADS * HEAD_DIM), BF16)
    return pl.pallas_call(
        _memkv_body, out_shape=[shp, shp],
        compiler_params=pltpu.CompilerParams(vmem_limit_bytes=VMEM_LIMIT),
        name="memkv",
    )(mem, in_g, w_kv, k_norm)


def _memattn_body(q_ref, k_ref, v_ref, o_ref):
    for hd in range(MEM_HEADS):
        sl = slice(hd * HEAD_DIM, (hd + 1) * HEAD_DIM)
        sc = _dot_nt(q_ref[:, sl], k_ref[:, sl])
        e = jnp.exp(sc - jnp.max(sc, axis=-1, keepdims=True))
        p = e / jnp.sum(e, axis=-1, keepdims=True)
        o_ref[:, sl] = _dot(p.astype(BF16), v_ref[:, sl]).astype(BF16)


def _memattn(q, k, v, *, tm=512):
    s, w = q.shape
    return pl.pallas_call(
        _memattn_body,
        grid=(s // tm,),
        in_specs=[pl.BlockSpec((tm, w), lambda i: (i, 0)), _resident(k.shape), _resident(v.shape)],
        out_specs=pl.BlockSpec((tm, w), lambda i: (i, 0)),
        out_shape=jax.ShapeDtypeStruct((s, w), BF16),
        compiler_params=_params(("parallel",)),
        name="memattn",
    )(q, k, v)


def _outproj_body(x_ref, a_ref, b_ref, c_ref, w_ref, o_ref):
    na, nb = a_ref.shape[1], b_ref.shape[1]
    o_ref[...] = (x_ref[...] + _dot(a_ref[...], w_ref[0:na, :].astype(BF16))
                  + _dot(b_ref[...], w_ref[na:na + nb, :].astype(BF16))
                  + _dot(c_ref[...], w_ref[na + nb:, :].astype(BF16)))


def _outproj(x1, o_gla, o_nsa, o_mem, w_out, *, tm=512):
    s, d = x1.shape
    row = lambda w: pl.BlockSpec((tm, w), lambda i: (i, 0))
    return pl.pallas_call(
        _outproj_body,
        grid=(s // tm,),
        in_specs=[row(d), row(o_gla.shape[1]), row(o_nsa.shape[1]), row(o_mem.shape[1]),
                  _resident(w_out.shape)],
        out_specs=row(d),
        out_shape=jax.ShapeDtypeStruct((s, d), F32),
        compiler_params=_params(("parallel",)),
        name="outproj",
    )(x1, o_gla, o_nsa, o_mem, w_out)


def _layer(x, mem, positions, ffn1_norm, ffn1_w_gate, ffn1_w_up, ffn1_w_down, mix_norm, w_in,
           gla_w_a, gla_b_a, gla_o_norm, nsa_q_norm, nsa_k_norm, nsa_cmp_pos_k, nsa_cmp_w1_k,
           nsa_cmp_w2_k, nsa_cmp_pos_v, nsa_cmp_w1_v, nsa_cmp_w2_v, mem_in_norm, w_mem_kv,
           mem_q_norm, mem_k_norm, w_out, ffn2_norm, ffn2_w_gate, ffn2_w_up, ffn2_w_down, final_norm):
    s, d = x.shape
    row = lambda v: v.reshape(1, -1)
    bf = lambda v: v.astype(BF16)

    x1 = _ffn(x, row(ffn1_norm), ffn1_w_gate, ffn1_w_up, ffn1_w_down)

    half = HEAD_DIM // 2
    inv = ROPE_THETA ** (-jnp.arange(half, dtype=F32) / half)
    inv128 = jnp.concatenate([inv, inv]).reshape(1, HEAD_DIM)
    sgn128 = jnp.concatenate([-jnp.ones((half,), F32), jnp.ones((half,), F32)]).reshape(1, HEAD_DIM)
    assert w_in.shape[1] == _R_END
    w_t = bf(w_in.T)
    (q, ksel, vselt, kwin, vwint, mq, gqk, gv, gr, kcvc, ga, ngt) = _proj(
        x1, row(mix_norm), w_t, w_t[_R_MQ:], positions.reshape(s, 1), inv128, sgn128,
        row(nsa_q_norm), nsa_k_norm, row(mem_q_norm))

    o_gla = _gla(gqk, gv, gr, ga, gla_w_a, row(gla_b_a), row(gla_o_norm))

    units = s // CMP_STRIDE
    n_cmp = (s - CMP_LEN) // CMP_STRIDE + 1
    cmp_last = jnp.arange(units) * CMP_STRIDE + CMP_LEN - 1
    pos_cmp = positions[jnp.minimum(cmp_last, s - 1)].reshape(units, 1)
    kcmp, vcmpt = _compress(kcvc, nsa_cmp_w1_k, bf(nsa_cmp_w2_k), nsa_cmp_pos_k,
                            nsa_cmp_w1_v, bf(nsa_cmp_w2_v.T), nsa_cmp_pos_v, nsa_k_norm,
                            pos_cmp, inv128, sgn128)
    n_sel = s // SEL_LEN
    cmp_start = np.arange(units) * CMP_STRIDE
    sel_start = np.arange(LANES) * SEL_LEN
    overlap = np.clip(np.minimum(cmp_start[:, None] + CMP_LEN, sel_start[None, :] + SEL_LEN)
                      - np.maximum(cmp_start[:, None], sel_start[None, :]), 0, None) / CMP_STRIDE
    overlap = overlap * (np.arange(units)[:, None] < n_cmp) * (np.arange(LANES)[None, :] < n_sel)
    o_nsa = _nsa(q, kcmp, vcmpt, ksel, vselt, kwin, vwint, ngt, jnp.asarray(overlap.T, BF16))

    kmem, vmem = _memkv(mem, row(mem_in_norm), w_mem_kv, row(mem_k_norm))
    o_mem = _memattn(mq, kmem, vmem)

    x2 = _outproj(x1, o_gla, o_nsa, o_mem, w_out)
    return _ffn(x2, row(ffn2_norm), ffn2_w_gate, ffn2_w_up, ffn2_w_down, row(final_norm))


def kernel(x, mem, positions, ffn1_norm, ffn1_w_gate, ffn1_w_up, ffn1_w_down, mix_norm, w_in, gla_w_a, gla_b_a, gla_o_norm, nsa_q_norm, nsa_k_norm, nsa_cmp_pos_k, nsa_cmp_w1_k, nsa_cmp_w2_k, nsa_cmp_pos_v, nsa_cmp_w1_v, nsa_cmp_w2_v, mem_in_norm, w_mem_kv, mem_q_norm, mem_k_norm, w_out, ffn2_norm, ffn2_w_gate, ffn2_w_up, ffn2_w_down, final_norm):
    depth = ffn1_norm.shape[0]
    batch, s, d = x.shape
    outs = []
    for b in range(batch):
        xb, mem_b, pos_b = (x.reshape(s, d), mem.reshape(mem.shape[1:]), positions.reshape(s)) if batch == 1 \
            else (x[b], mem[b], positions[b])
        for l in range(depth):
            xb = _layer(xb, mem_b, pos_b, ffn1_norm[l], ffn1_w_gate[l], ffn1_w_up[l], ffn1_w_down[l],
                        mix_norm[l], w_in[l], gla_w_a[l], gla_b_a[l], gla_o_norm[l], nsa_q_norm[l],
                        nsa_k_norm[l], nsa_cmp_pos_k[l], nsa_cmp_w1_k[l], nsa_cmp_w2_k[l], nsa_cmp_pos_v[l],
                        nsa_cmp_w1_v[l], nsa_cmp_w2_v[l], mem_in_norm[l], w_mem_kv[l], mem_q_norm[l],
                        mem_k_norm[l], w_out[l], ffn2_norm[l], ffn2_w_gate[l], ffn2_w_up[l], ffn2_w_down[l],
                        final_norm[l])
        outs.append(xb)
    return outs[0].reshape(1, s, d) if batch == 1 else jnp.stack(outs)
```

```python
import functools

import numpy as np
import jax
import jax.numpy as jnp
from jax import lax
from jax.experimental import pallas as pl
from jax.experimental.pallas import tpu as pltpu

F32 = jnp.float32
BF16 = jnp.bfloat16

HEAD_DIM = 128
GLA_HEADS = 4
GLA_DK = 64
GLA_DV = 128
GLA_LOWRANK = 16
GLA_TAU = 16.0
NSA_HEADS = 8
NSA_KV_HEADS = 2
NSA_HPG = NSA_HEADS // NSA_KV_HEADS
CMP_LEN = 32
CMP_STRIDE = 16
SEL_LEN = 64
SEL_TOPK = 16
WINDOW = 512
MEM_HEADS = 4
MACARON_W = 0.5
QBLK = 128
ROPE_THETA = 10000.0
EPS = 1e-6
NEG_INF = -1e30
TINY = 1e-30
FORCE_SCORE = 1e4
LOG2_E = 1.4426950408889634

LANES = 128
VMEM_LIMIT = 56 * 1024 * 1024

GLA_SUB = 16
GLA_ROWS = 128
SEL_KT = 256
SEL_SPAN_TILES = 4
V_ROWS = HEAD_DIM + 16
SEL_MASK_BIAS = -32768.0
SEL_PAD_SCORE = -3e4


def _dot(a, b):
    return jnp.dot(a, b, preferred_element_type=F32)


def _dot_nt(a, b):
    return lax.dot_general(a, b, (((1,), (1,)), ((), ())), preferred_element_type=F32)


def _dot_tn(a, b):
    return lax.dot_general(a, b, (((0,), (0,)), ((), ())), preferred_element_type=F32)


def _split2(x):
    hi = x.astype(BF16)
    return hi, (x - hi.astype(F32)).astype(BF16)


def _rms(x, g):
    return x * lax.rsqrt(jnp.mean(x * x, axis=-1, keepdims=True) + EPS) * g


def _sigmoid(x):
    return 1.0 / (1.0 + jnp.exp(-x))


def _params(sem):
    return pltpu.CompilerParams(dimension_semantics=sem, vmem_limit_bytes=VMEM_LIMIT)


def _resident(shape):
    nd = len(shape)
    return pl.BlockSpec(shape, lambda *_: (0,) * nd, pipeline_mode=pl.Buffered(1))


def _ffn_body(*refs, final, nf):
    if final:
        x_ref, g_ref, wg_ref, wu_ref, wd_ref, fg_ref, o_ref, h_ref = refs
    else:
        x_ref, g_ref, wg_ref, wu_ref, wd_ref, o_ref, h_ref = refs
    f = pl.program_id(1)

    @pl.when(f == 0)
    def _():
        x = x_ref[...]
        h_ref[...] = _rms(x, g_ref[...]).astype(BF16)
        o_ref[...] = x

    h = h_ref[...]
    g = _dot(h, wg_ref[...].astype(BF16))
    u = _dot(h, wu_ref[...].astype(BF16))
    a = (g * _sigmoid(g)) * u * MACARON_W
    o_ref[...] += _dot(a.astype(BF16), wd_ref[...].astype(BF16))

    if final:
        @pl.when(f == nf - 1)
        def _():
            o_ref[...] = _rms(o_ref[...], fg_ref[...])


def _ffn(x, norm_g, wg, wu, wd, final_g=None, *, tm=1024, tf=256):
    s, d = x.shape
    ff = wg.shape[1]
    nf = ff // tf
    final = final_g is not None
    in_specs = [
        pl.BlockSpec((tm, d), lambda i, f: (i, 0)),
        pl.BlockSpec((1, d), lambda i, f: (0, 0)),
        pl.BlockSpec((d, tf), lambda i, f: (0, f)),
        pl.BlockSpec((d, tf), lambda i, f: (0, f)),
        pl.BlockSpec((tf, d), lambda i, f: (f, 0)),
    ]
    args = [x, norm_g, wg, wu, wd]
    if final:
        in_specs.append(pl.BlockSpec((1, d), lambda i, f: (0, 0)))
        args.append(final_g)
    return pl.pallas_call(
        functools.partial(_ffn_body, final=final, nf=nf),
        grid=(s // tm, nf),
        in_specs=in_specs,
        out_specs=pl.BlockSpec((tm, d), lambda i, f: (i, 0)),
        out_shape=jax.ShapeDtypeStruct((s, d), F32),
        scratch_shapes=[pltpu.VMEM((tm, d), BF16)],
        compiler_params=_params(("parallel", "arbitrary")),
        name="ffn_final" if final else "ffn",
    )(*args)


_GQK_W = 2 * GLA_HEADS * GLA_DK
_GV_W = GLA_HEADS * GLA_DV
_KV_W = NSA_KV_HEADS * HEAD_DIM
_PER_G = NSA_HPG * 3
_NG_PAD = 16
_R_GQK = 0
_R_GV = _R_GQK + _GQK_W
_R_GR = _R_GV + _GV_W
_R_GA = _R_GR + _GV_W
_R_NQ = _R_GA + GLA_LOWRANK
_R_KC = _R_NQ + NSA_HEADS * HEAD_DIM
_R_KS = _R_KC + 2 * _KV_W
_R_VS = _R_KS + _KV_W
_R_KW = _R_VS + _KV_W
_R_VW = _R_KW + _KV_W
_R_NG = _R_VW + _KV_W
_R_MQ = _R_NG + NSA_KV_HEADS * _PER_G
_R_END = _R_MQ + MEM_HEADS * HEAD_DIM


def _rope(x, cos, sin_signed):
    return x * cos + pltpu.roll(x, HEAD_DIM // 2, 1) * sin_signed


def _proj_body(x_ref, g_ref, wt_ref, wmq_ref, pos_ref, inv_ref, sgn_ref, qn_ref, kn_ref,
               mqn_ref, q_ref, ksel_ref, vselt_ref, kwin_ref, vwint_ref, mq_ref, gqk_ref,
               gv_ref, gr_ref, kcvc_ref, ga_ref, ngt_ref, *, tm):
    i = pl.program_id(0)
    h = _rms(x_ref[...], g_ref[...]).astype(BF16)

    def proj(r0, width):
        return _dot_nt(h, wt_ref[r0:r0 + width, :])

    def proj_t(r0, width):
        return _dot_nt(wt_ref[r0:r0 + width, :], h)

    ang = pos_ref[...].astype(F32) * inv_ref[...]
    cos = jnp.cos(ang)
    sin_signed = jnp.sin(ang) * sgn_ref[...]

    def norm_rope(t, gain):
        return _rope(_rms(t, gain), cos, sin_signed)

    scale = HEAD_DIM ** -0.5
    nq = proj(_R_NQ, NSA_HEADS * HEAD_DIM)
    for hd in range(NSA_HEADS):
        sl = slice(hd * HEAD_DIM, (hd + 1) * HEAD_DIM)
        q_ref[:, sl] = (norm_rope(nq[:, sl], qn_ref[...]) * (scale * LOG2_E)).astype(BF16)

    tok = i * tm + lax.broadcasted_iota(jnp.int32, (tm, LANES), 0)
    blk = lax.broadcasted_iota(jnp.int32, (tm, LANES), 1)
    onehot = jnp.where(lax.shift_right_logical(tok, 6) == blk, 1.0, 0.0).astype(BF16)
    ks = proj(_R_KS, _KV_W)
    kw = proj(_R_KW, _KV_W)
    vst = proj_t(_R_VS, _KV_W)
    vwt = proj_t(_R_VW, _KV_W)
    ngt = proj_t(_R_NG, NSA_KV_HEADS * _PER_G)
    ones_rows = jnp.where(lax.broadcasted_iota(jnp.int32, (V_ROWS - HEAD_DIM, tm), 0) == 0, 1.0, 0.0).astype(BF16)
    for g in range(NSA_KV_HEADS):
        sl = slice(g * HEAD_DIM, (g + 1) * HEAD_DIM)
        ksel_ref[g, :, 0:HEAD_DIM] = norm_rope(ks[:, sl], kn_ref[1:2, :]).astype(BF16)
        ksel_ref[g, :, HEAD_DIM:2 * HEAD_DIM] = onehot
        kwin_ref[g] = norm_rope(kw[:, sl], kn_ref[2:3, :]).astype(BF16)
        vselt_ref[g, 0:HEAD_DIM, :] = vst[sl, :].astype(BF16)
        vwint_ref[g, 0:HEAD_DIM, :] = vwt[sl, :].astype(BF16)
        vselt_ref[g, HEAD_DIM:V_ROWS, :] = ones_rows
        vwint_ref[g, HEAD_DIM:V_ROWS, :] = ones_rows
        ngt_ref[g, 0:_PER_G, :] = ngt[g * _PER_G:(g + 1) * _PER_G, :]
        ngt_ref[g, _PER_G:_NG_PAD, :] = jnp.zeros((_NG_PAD - _PER_G, tm), F32)

    mq = _dot_nt(h, wmq_ref[...])
    for hd in range(MEM_HEADS):
        sl = slice(hd * HEAD_DIM, (hd + 1) * HEAD_DIM)
        mq_ref[:, sl] = (_rms(mq[:, sl], mqn_ref[...]) * scale).astype(BF16)

    gqk_ref[...] = proj(_R_GQK, _GQK_W)
    gv_ref[...] = proj(_R_GV, _GV_W).astype(BF16)
    gr_ref[...] = proj(_R_GR, _GV_W)
    ga_ref[...] = proj(_R_GA, GLA_LOWRANK)
    kcvc = proj(_R_KC, 2 * _KV_W)
    for j in range(2 * NSA_KV_HEADS):
        kcvc_ref[j] = kcvc[:, j * HEAD_DIM:(j + 1) * HEAD_DIM]


def _proj(x1, mix_g, w_t, w_mq_t, pos_col, inv128, sgn128, q_norm, k_norm, mq_norm, *, tm=512):
    s, d = x1.shape
    row = lambda w: pl.BlockSpec((tm, w), lambda i: (i, 0))
    grp = lambda w: pl.BlockSpec((NSA_KV_HEADS, tm, w), lambda i: (0, i, 0))
    grpt = pl.BlockSpec((NSA_KV_HEADS, V_ROWS, tm), lambda i: (0, 0, i))
    vt_shape = jax.ShapeDtypeStruct((NSA_KV_HEADS, V_ROWS, s), BF16)
    out_shapes = [
        (jax.ShapeDtypeStruct((s, NSA_HEADS * HEAD_DIM), BF16), row(NSA_HEADS * HEAD_DIM)),
        (jax.ShapeDtypeStruct((NSA_KV_HEADS, s, 2 * HEAD_DIM), BF16), grp(2 * HEAD_DIM)),
        (vt_shape, grpt),
        (jax.ShapeDtypeStruct((NSA_KV_HEADS, s, HEAD_DIM), BF16), grp(HEAD_DIM)),
        (vt_shape, grpt),
        (jax.ShapeDtypeStruct((s, MEM_HEADS * HEAD_DIM), BF16), row(MEM_HEADS * HEAD_DIM)),
        (jax.ShapeDtypeStruct((s, 512), F32), row(512)),
        (jax.ShapeDtypeStruct((s, 512), BF16), row(512)),
        (jax.ShapeDtypeStruct((s, 512), F32), row(512)),
        (jax.ShapeDtypeStruct((2 * NSA_KV_HEADS, s, HEAD_DIM), F32),
         pl.BlockSpec((2 * NSA_KV_HEADS, tm, HEAD_DIM), lambda i: (0, i, 0))),
        (jax.ShapeDtypeStruct((s, GLA_LOWRANK), F32), row(GLA_LOWRANK)),
        (jax.ShapeDtypeStruct((NSA_KV_HEADS, _NG_PAD, s), F32),
         pl.BlockSpec((NSA_KV_HEADS, _NG_PAD, tm), lambda i: (0, 0, i))),
    ]
    return pl.pallas_call(
        functools.partial(_proj_body, tm=tm),
        grid=(s // tm,),
        in_specs=[
            row(d),
            _resident((1, d)),
            _resident(w_t.shape), _resident(w_mq_t.shape),
            pl.BlockSpec((tm, 1), lambda i: (i, 0)),
            _resident((1, LANES)), _resident((1, LANES)),
            _resident((1, HEAD_DIM)), _resident((3, HEAD_DIM)), _resident((1, HEAD_DIM)),
        ],
        out_specs=[o[1] for o in out_shapes],
        out_shape=[o[0] for o in out_shapes],
        compiler_params=_params(("parallel",)),
        name="proj",
    )(x1, mix_g, w_t, w_mq_t, pos_col, inv128, sgn128, q_norm, k_norm, mq_norm)


def _compress_body(kcvc_ref, w1k_ref, w2k_ref, pek_ref, w1v_ref, w2v_ref, pev_ref, kn_ref,
                   pos_ref, inv_ref, sgn_ref, kcmp_ref, vcmp_ref, *, units):
    half = CMP_LEN // 2
    ang = pos_ref[...].astype(F32) * inv_ref[...]
    cos = jnp.cos(ang)
    sin_signed = jnp.sin(ang) * sgn_ref[...]
    for kind, (w1_ref, w2_ref, pe_ref) in enumerate(((w1k_ref, w2k_ref, pek_ref),
                                                     (w1v_ref, w2v_ref, pev_ref))):
        for g in range(NSA_KV_HEADS):
            slab = kind * NSA_KV_HEADS + g
            a = jnp.zeros((units, w1_ref.shape[1]), F32)
            b = jnp.zeros((units, w1_ref.shape[1]), F32)
            for l in range(half):
                t = kcvc_ref[slab, pl.ds(l, units, stride=CMP_STRIDE), :]
                a = a + _dot((t + pe_ref[l:l + 1, :]).astype(BF16),
                             w1_ref[l * HEAD_DIM:(l + 1) * HEAD_DIM, :].astype(BF16))
                b = b + _dot((t + pe_ref[half + l:half + l + 1, :]).astype(BF16),
                             w1_ref[(half + l) * HEAD_DIM:(half + l + 1) * HEAD_DIM, :].astype(BF16))
            hid = a + pltpu.roll(b, units - 1, 0)
            act = (hid * _sigmoid(hid)).astype(BF16)
            if kind == 0:
                c = _rope(_rms(_dot(act, w2_ref[...]), kn_ref[0:1, :]), cos, sin_signed)
                kcmp_ref[g] = c.astype(BF16)
            else:
                vcmp_ref[g] = _dot_nt(w2_ref[...], act).astype(BF16)


def _compress(kcvc, w1k, w2k, pek, w1v, w2v, pev, k_norm, pos_cmp, inv128, sgn128):
    s = kcvc.shape[1]
    units = s // CMP_STRIDE
    shp = jax.ShapeDtypeStruct((NSA_KV_HEADS, units, HEAD_DIM), BF16)
    shp_t = jax.ShapeDtypeStruct((NSA_KV_HEADS, HEAD_DIM, units), BF16)
    return pl.pallas_call(
        functools.partial(_compress_body, units=units),
        out_shape=[shp, shp_t],
        compiler_params=pltpu.CompilerParams(vmem_limit_bytes=VMEM_LIMIT),
        name="compress",
    )(kcvc, w1k, w2k, pek, w1v, w2v, pev, k_norm, pos_cmp, inv128, sgn128)


def _gla_body(gqk_ref, gv_ref, gr_ref, ga_ref, ga_next_ref, wa_ref, ba_ref, on_ref, tcum_ref, bd_ref, hsel_ref,
              o_ref, st_ref, b_s):
    rows = GLA_ROWS
    npair = GLA_HEADS // 2

    def cum_log_decay(ga):
        z = ba_ref[...]
        for ga_t in _split2(ga):
            for wa_t in _split2(wa_ref[...]):
                z = z + _dot(ga_t, wa_t)
        la = (jnp.minimum(z, 0.0) - jnp.log(1.0 + jnp.exp(-jnp.abs(z)))) / GLA_TAU
        bcum = jnp.zeros_like(la)
        for la_t in _split2(la):
            bcum = bcum + _dot(tcum_ref[...], la_t)
        return bcum * LOG2_E

    @pl.when(pl.program_id(0) == 0)
    def _():
        st_ref[...] = jnp.zeros_like(st_ref)
        b_s[...] = cum_log_decay(ga_ref[...])

    b_all = b_s[...]
    q_all = gqk_ref[:, 0:256] * (GLA_DK ** -0.5)
    k_all = gqk_ref[:, 256:512]
    v_all = gv_ref[...]

    row_i = lax.broadcasted_iota(jnp.int32, (GLA_SUB, LANES), 0)

    sts = [st_ref[p] for p in range(npair)]
    o_rows = []
    for sb in range(rows // GLA_SUB):
        rs = slice(sb * GLA_SUB, (sb + 1) * GLA_SUB)
        o_pairs = []
        for p in range(npair):
            cs = slice(p * LANES, (p + 1) * LANES)
            vs = slice(p * 2 * GLA_DV, (p + 1) * 2 * GLA_DV)
            qs = q_all[rs, cs]
            kk = k_all[rs, cs]
            bb = b_all[rs, cs]
            vp = v_all[rs, vs]
            vpf = vp.astype(F32)
            blast = bb[GLA_SUB - 1:GLA_SUB, :]
            st = sts[p]
            o_inter = _dot_nt((qs * jnp.exp2(bb)).astype(BF16), st.astype(BF16))
            xs = []
            for j in range(GLA_SUB):
                dlt = jnp.where(row_i >= j, bb - bb[j:j + 1, :], NEG_INF)
                xs.append(qs * jnp.exp2(dlt) * kk[j:j + 1, :])
            red = _dot(jnp.concatenate(xs, axis=0).astype(BF16), hsel_ref[...])
            acc = o_inter
            for j in range(GLA_SUB):
                acc = acc + red[j * GLA_SUB:(j + 1) * GLA_SUB, :] * vpf[j:j + 1, :]
            o_pairs.append(acc)
            kd = (kk * jnp.exp2(blast - bb)).astype(BF16)
            upd = _dot_tn(vp, kd)
            sts[p] = st * jnp.exp2(blast) + upd * bd_ref[...]
        o_rows.append(jnp.concatenate(o_pairs, axis=1))
    for p in range(npair):
        st_ref[p] = sts[p]
    b_s[...] = cum_log_decay(ga_next_ref[...])

    o_all = jnp.concatenate(o_rows, axis=0)
    gr = gr_ref[...]
    for hd in range(GLA_HEADS):
        sl = slice(hd * GLA_DV, (hd + 1) * GLA_DV)
        r = gr[:, sl]
        o_ref[:, sl] = (_rms(o_all[:, sl], on_ref[...]) * (r * _sigmoid(r))).astype(BF16)


def _gla(gqk, gv, gr, ga, wa, ba, o_norm):
    s = gqk.shape[0]
    rows = GLA_ROWS
    idx = np.arange(rows)
    tcum = ((idx[:, None] >= idx[None, :]) & (idx[:, None] // GLA_SUB == idx[None, :] // GLA_SUB))
    tcum = jnp.asarray(tcum, BF16)
    r256 = np.arange(2 * GLA_DV)[:, None] // GLA_DV
    c128 = np.arange(LANES)[None, :] // GLA_DK
    bdmask = jnp.asarray(r256 == c128, F32)
    hsel = jnp.asarray((r256 == c128).T, BF16)
    row = lambda w: pl.BlockSpec((rows, w), lambda i: (i, 0))
    nblk = s // rows
    ga_next = pl.BlockSpec((rows, GLA_LOWRANK), lambda i: (jnp.minimum(i + 1, nblk - 1), 0))
    return pl.pallas_call(
        _gla_body,
        grid=(nblk,),
        in_specs=[row(512), row(512), row(512), row(GLA_LOWRANK), ga_next,
                  _resident(wa.shape), _resident(ba.shape), _resident(o_norm.shape),
                  _resident(tcum.shape), _resident(bdmask.shape), _resident(hsel.shape)],
        out_specs=row(512),
        out_shape=jax.ShapeDtypeStruct((s, GLA_HEADS * GLA_DV), BF16),
        scratch_shapes=[pltpu.VMEM((GLA_HEADS // 2, 2 * GLA_DV, LANES), F32),
                        pltpu.VMEM((rows, GLA_HEADS * GLA_DK), F32)],
        compiler_params=_params(("arbitrary",)),
        name="gla",
    )(gqk, gv, gr, ga, ga, wa, ba, o_norm, tcum, bdmask, hsel)


def _nsa_body(q_ref, kcmp_ref, vcmpt_ref, ksel_ref, vselt_ref, kwin_ref, vwint_ref, ngt_ref, ovlt_ref,
              o_ref, qt_s, s_s, p_s, acc_s, *, n_sel):
    qb = pl.program_id(0)
    t0 = qb * QBLK
    cols = NSA_HPG * QBLK
    gw = NSA_HPG * HEAD_DIM
    ncmp = kcmp_ref.shape[1]
    groups = range(NSA_KV_HEADS)
    span = SEL_SPAN_TILES * SEL_KT

    def tq_of(rows):
        return t0 + (lax.broadcasted_iota(jnp.int32, (rows, cols), 1) & (QBLK - 1))

    def gate_row(gates_t, c):
        return jnp.concatenate([gates_t[3 * hd + c:3 * hd + c + 1, :] for hd in range(NSA_HPG)], axis=1)

    def span_scores(g, j):
        k0 = pl.multiple_of(j * span, span)
        return _dot(ksel_ref[g, pl.ds(k0, span), :], qt_s[g])

    def span_pv(g, j, p):
        k0 = pl.multiple_of(j * span, span)
        return _dot(vselt_ref[g, :, pl.ds(k0, span)], p)

    def prologue(g):
        qt = jnp.concatenate(
            [q_ref[:, g * gw + hd * HEAD_DIM:g * gw + (hd + 1) * HEAD_DIM].astype(F32).T.astype(BF16)
             for hd in range(NSA_HPG)], axis=1)
        gates_t = _sigmoid(ngt_ref[g])

        def win_part(start, length):
            start = pl.multiple_of(jnp.maximum(start, 0), QBLK)
            return (_dot(kwin_ref[g, pl.ds(start, length), :], qt), vwint_ref[g, :, pl.ds(start, length)], start)

        s_c = _dot(kcmp_ref[g], qt)
        s_old, v_old, _ = win_part(t0 - WINDOW, QBLK)
        s_mid, v_mid, mid0 = win_part(t0 - WINDOW + QBLK, WINDOW - QBLK)
        s_dg, v_dg, _ = win_part(t0, QBLK)
        yield

        n_row = lax.broadcasted_iota(jnp.int32, (ncmp, cols), 0)
        valid_c = n_row <= lax.shift_right_arithmetic(tq_of(1) - (CMP_LEN - 1), CMP_STRIDE.bit_length() - 1)
        s_c = jnp.where(valid_c, s_c, NEG_INF)
        e_c = jnp.where(valid_c, jnp.exp2(s_c - jnp.max(s_c, axis=0, keepdims=True)), 0.0)
        p_c = e_c / jnp.maximum(jnp.sum(e_c, axis=0, keepdims=True), TINY)
        out_pre = gate_row(gates_t, 0) * _dot(vcmpt_ref[g], p_c.astype(BF16))
        psum = p_c[:, 0:QBLK]
        for hd in range(1, NSA_HPG):
            psum = psum + p_c[:, hd * QBLK:(hd + 1) * QBLK]
        imp = jnp.zeros((LANES, QBLK), F32)
        for p_t in _split2(psum):
            imp = imp + _dot(ovlt_ref[...], p_t)
        yield

        tq = t0 + lax.broadcasted_iota(jnp.int32, (LANES, QBLK), 1)
        m_i = lax.broadcasted_iota(jnp.int32, (LANES, QBLK), 0)
        cur = lax.shift_right_logical(tq, 6)
        forced = (m_i == 0) | (m_i == cur) | (m_i == cur - 1)
        causal = m_i * SEL_LEN <= tq
        n_forced = 3
        score = jnp.where(causal, jnp.where(forced, -jnp.inf, imp), -FORCE_SCORE)
        score = jnp.where(m_i < n_sel, score, SEL_PAD_SCORE)
        m_f = m_i.astype(F32)
        for _ in range(min(SEL_TOPK, n_sel) - n_forced):
            mx = jnp.max(score, axis=0, keepdims=True)
            first = jnp.min(jnp.where(score == mx, m_f, float(LANES)), axis=0, keepdims=True)
            score = jnp.where(m_f == first, -jnp.inf, score)
        bias = jnp.where(score == -jnp.inf, 0.0, SEL_MASK_BIAS).astype(BF16)
        qext = jnp.concatenate([qt, jnp.concatenate([bias] * NSA_HPG, axis=1)], axis=0)
        sc0 = _dot(ksel_ref[g, 0:span, :], qext)
        yield

        w_row = lax.broadcasted_iota(jnp.int32, (QBLK, cols), 0)
        tq_w = tq_of(QBLK)
        kp_old = t0 - WINDOW + w_row
        valid_old = (kp_old > tq_w - WINDOW) & (kp_old >= 0)
        s_old = jnp.where(valid_old, s_old, NEG_INF)
        mid_row = mid0 + lax.broadcasted_iota(jnp.int32, (WINDOW - QBLK, cols), 0)
        s_mid = jnp.where(mid_row < t0, s_mid, NEG_INF)
        valid_dg = t0 + w_row <= tq_w
        s_dg = jnp.where(valid_dg, s_dg, NEG_INF)
        m_w = jnp.maximum(jnp.maximum(jnp.max(s_old, axis=0, keepdims=True),
                                      jnp.max(s_mid, axis=0, keepdims=True)),
                          jnp.max(s_dg, axis=0, keepdims=True))
        p_old = jnp.where(valid_old, jnp.exp2(s_old - m_w), 0.0)
        p_mid = jnp.exp2(s_mid - m_w)
        p_dg = jnp.where(valid_dg, jnp.exp2(s_dg - m_w), 0.0)
        acc_w = (_dot(v_old, p_old.astype(BF16)) + _dot(v_mid, p_mid.astype(BF16))
                 + _dot(v_dg, p_dg.astype(BF16)))
        out_pre = out_pre + gate_row(gates_t, 2) * (acc_w[0:HEAD_DIM, :]
                                                    / jnp.maximum(acc_w[HEAD_DIM:HEAD_DIM + 1, :], TINY))

        return out_pre, gate_row(gates_t, 1), qext, sc0

    pre = {}
    gens = {g: prologue(g) for g in groups}
    order = [0] + [g for _ in range(8) for g in groups]
    for g in order:
        if g not in pre:
            try:
                next(gens[g])
            except StopIteration as done:
                pre[g] = done.value
    assert len(pre) == len(groups)
    for g in groups:
        qt_s[g] = pre[g][2]
        s_s[g, 0] = pre[g][3]
        p_s[g, 1] = jnp.zeros((span, cols), BF16)
        acc_s[g] = jnp.zeros((V_ROWS, cols), F32)

    def sel_step(cur, j, ms):
        nxt = 1 - cur
        out = []
        for g in groups:
            pv = span_pv(g, jnp.maximum(j - 1, 0), p_s[g, nxt])
            s = s_s[g, cur]
            m_new = jnp.maximum(ms[g], jnp.max(s, axis=0, keepdims=True))
            p_s[g, cur] = jnp.exp2(s - m_new).astype(BF16)
            acc_s[g] = jnp.exp2(ms[g] - m_new) * (acc_s[g] + pv)
            out.append(m_new)
        for g in groups:
            s_s[g, nxt] = span_scores(g, j + 1)
        return tuple(out)

    def sel_body(j, ms):
        return lax.cond((j & 1) == 0, lambda c: sel_step(0, j, c), lambda c: sel_step(1, j, c), ms)

    n_span = t0 // span
    ms = lax.fori_loop(0, n_span, sel_body, tuple(jnp.full((1, cols), NEG_INF, F32) for _ in groups))

    slot = n_span & 1
    base = pl.multiple_of(n_span * span, span)

    def diag(nk):
        valid = lax.broadcasted_iota(jnp.int32, (nk, cols), 0) <= tq_of(1) - base
        for g in groups:
            pv = span_pv(g, jnp.maximum(n_span - 1, 0), p_s[g, 1 - slot])
            s = jnp.where(valid, s_s[g, slot, 0:nk, :], NEG_INF)
            m_new = jnp.maximum(ms[g], jnp.max(s, axis=0, keepdims=True))
            p = jnp.where(valid, jnp.exp2(s - m_new), 0.0).astype(BF16)
            acc_s[g] = (jnp.exp2(ms[g] - m_new) * (acc_s[g] + pv)
                        + _dot(vselt_ref[g, :, pl.ds(base, nk)], p))

    lax.cond(t0 - base >= span // 2, lambda: diag(span), lambda: diag(span // 2))

    for g in groups:
        out_pre, gate_sel = pre[g][0:2]
        acc = acc_s[g]
        o_slc = acc[0:HEAD_DIM, :] / jnp.maximum(acc[HEAD_DIM:HEAD_DIM + 1, :], TINY)
        out = out_pre + gate_sel * o_slc
        for hd in range(NSA_HPG):
            o_ref[:, g * gw + hd * HEAD_DIM:g * gw + (hd + 1) * HEAD_DIM] = (
                out[:, hd * QBLK:(hd + 1) * QBLK].T.astype(BF16))


def _nsa(q, kcmp, vcmpt, ksel, vselt, kwin, vwint, ngt, overlap_t):
    s = q.shape[0]
    n_sel = s // SEL_LEN
    span = SEL_SPAN_TILES * SEL_KT
    assert n_sel <= LANES and s % span == 0 and s >= WINDOW + QBLK and CMP_STRIDE & (CMP_STRIDE - 1) == 0
    cols = NSA_HPG * QBLK
    ng = NSA_KV_HEADS
    return pl.pallas_call(
        functools.partial(_nsa_body, n_sel=n_sel),
        grid=(s // QBLK,),
        in_specs=[
            pl.BlockSpec((QBLK, NSA_HEADS * HEAD_DIM), lambda b: (b, 0)),
            _resident(kcmp.shape), _resident(vcmpt.shape),
            _resident(ksel.shape), _resident(vselt.shape), _resident(kwin.shape), _resident(vwint.shape),
            pl.BlockSpec((ng, _NG_PAD, QBLK), lambda b: (0, 0, b)),
            _resident(overlap_t.shape),
        ],
        out_specs=pl.BlockSpec((QBLK, NSA_HEADS * HEAD_DIM), lambda b: (b, 0)),
        out_shape=jax.ShapeDtypeStruct((s, NSA_HEADS * HEAD_DIM), BF16),
        scratch_shapes=[pltpu.VMEM((ng, 2 * HEAD_DIM, cols), BF16),
                        pltpu.VMEM((ng, 2, span, cols), F32),
                        pltpu.VMEM((ng, 2, span, cols), BF16),
                        pltpu.VMEM((ng, V_ROWS, cols), F32)],
        compiler_params=_params(("arbitrary",)),
        name="nsa",
    )(q, kcmp, vcmpt, ksel, vselt, kwin, vwint, ngt, overlap_t)


def _memkv_body(mem_ref, g_ref, w_ref, kn_ref, k_ref, v_ref):
    kv = _dot(_rms(mem_ref[...], g_ref[...]).astype(BF16), w_ref[...].astype(BF16))
    width = MEM_HEADS * HEAD_DIM
    for hd in range(MEM_HEADS):
        sl = slice(hd * HEAD_DIM, (hd + 1) * HEAD_DIM)
        k_ref[:, sl] = _rms(kv[:, sl], kn_ref[...]).astype(BF16)
    v_ref[...] = kv[:, width:].astype(BF16)


def _memkv(mem, in_g, w_kv, k_norm):
    m = mem.shape[0]
    shp = jax.ShapeDtypeStruct((m, MEM_HEADS * HEAD_DIM), BF16)
    return pl.pallas_call(
        _memkv_body, out_shape=[shp, shp],
        compiler_params=pltpu.CompilerParams(vmem_limit_bytes=VMEM_LIMIT),
        name="memkv",
    )(mem, in_g, w_kv, k_norm)


def _memattn_body(q_ref, k_ref, v_ref, o_ref):
    for hd in range(MEM_HEADS):
        sl = slice(hd * HEAD_DIM, (hd + 1) * HEAD_DIM)
        sc = _dot_nt(q_ref[:, sl], k_ref[:, sl])
        e = jnp.exp(sc - jnp.max(sc, axis=-1, keepdims=True))
        p = e / jnp.sum(e, axis=-1, keepdims=True)
        o_ref[:, sl] = _dot(p.astype(BF16), v_ref[:, sl]).astype(BF16)


def _memattn(q, k, v, *, tm=512):
    s, w = q.shape
    return pl.pallas_call(
        _memattn_body,
        grid=(s // tm,),
        in_specs=[pl.BlockSpec((tm, w), lambda i: (i, 0)), _resident(k.shape), _resident(v.shape)],
        out_specs=pl.BlockSpec((tm, w), lambda i: (i, 0)),
        out_shape=jax.ShapeDtypeStruct((s, w), BF16),
        compiler_params=_params(("parallel",)),
        name="memattn",
    )(q, k, v)


def _outproj_body(x_ref, a_ref, b_ref, c_ref, w_ref, o_ref):
    na, nb = a_ref.shape[1], b_ref.shape[1]
    o_ref[...] = (x_ref[...] + _dot(a_ref[...], w_ref[0:na, :].astype(BF16))
                  + _dot(b_ref[...], w_ref[na:na + nb, :].astype(BF16))
                  + _dot(c_ref[...], w_ref[na + nb:, :].astype(BF16)))


def _outproj(x1, o_gla, o_nsa, o_mem, w_out, *, tm=512):
    s, d = x1.shape
    row = lambda w: pl.BlockSpec((tm, w), lambda i: (i, 0))
    return pl.pallas_call(
        _outproj_body,
        grid=(s // tm,),
        in_specs=[row(d), row(o_gla.shape[1]), row(o_nsa.shape[1]), row(o_mem.shape[1]),
                  _resident(w_out.shape)],
        out_specs=row(d),
        out_shape=jax.ShapeDtypeStruct((s, d), F32),
        compiler_params=_params(("parallel",)),
        name="outproj",
    )(x1, o_gla, o_nsa, o_mem, w_out)


def _layer(x, mem, positions, ffn1_norm, ffn1_w_gate, ffn1_w_up, ffn1_w_down, mix_norm, w_in,
           gla_w_a, gla_b_a, gla_o_norm, nsa_q_norm, nsa_k_norm, nsa_cmp_pos_k, nsa_cmp_w1_k,
           nsa_cmp_w2_k, nsa_cmp_pos_v, nsa_cmp_w1_v, nsa_cmp_w2_v, mem_in_norm, w_mem_kv,
           mem_q_norm, mem_k_norm, w_out, ffn2_norm, ffn2_w_gate, ffn2_w_up, ffn2_w_down, final_norm):
    s, d = x.shape
    row = lambda v: v.reshape(1, -1)
    bf = lambda v: v.astype(BF16)

    x1 = _ffn(x, row(ffn1_norm), ffn1_w_gate, ffn1_w_up, ffn1_w_down)

    half = HEAD_DIM // 2
    inv = ROPE_THETA ** (-jnp.arange(half, dtype=F32) / half)
    inv128 = jnp.concatenate([inv, inv]).reshape(1, HEAD_DIM)
    sgn128 = jnp.concatenate([-jnp.ones((half,), F32), jnp.ones((half,), F32)]).reshape(1, HEAD_DIM)
    assert w_in.shape[1] == _R_END
    w_t = bf(w_in.T)
    (q, ksel, vselt, kwin, vwint, mq, gqk, gv, gr, kcvc, ga, ngt) = _proj(
        x1, row(mix_norm), w_t, w_t[_R_MQ:], positions.reshape(s, 1), inv128, sgn128,
        row(nsa_q_norm), nsa_k_norm, row(mem_q_norm))

    o_gla = _gla(gqk, gv, gr, ga, gla_w_a, row(gla_b_a), row(gla_o_norm))

    units = s // CMP_STRIDE
    n_cmp = (s - CMP_LEN) // CMP_STRIDE + 1
    cmp_last = jnp.arange(units) * CMP_STRIDE + CMP_LEN - 1
    pos_cmp = positions[jnp.minimum(cmp_last, s - 1)].reshape(units, 1)
    kcmp, vcmpt = _compress(kcvc, nsa_cmp_w1_k, bf(nsa_cmp_w2_k), nsa_cmp_pos_k,
                            nsa_cmp_w1_v, bf(nsa_cmp_w2_v.T), nsa_cmp_pos_v, nsa_k_norm,
                            pos_cmp, inv128, sgn128)
    n_sel = s // SEL_LEN
    cmp_start = np.arange(units) * CMP_STRIDE
    sel_start = np.arange(LANES) * SEL_LEN
    overlap = np.clip(np.minimum(cmp_start[:, None] + CMP_LEN, sel_start[None, :] + SEL_LEN)
                      - np.maximum(cmp_start[:, None], sel_start[None, :]), 0, None) / CMP_STRIDE
    overlap = overlap * (np.arange(units)[:, None] < n_cmp) * (np.arange(LANES)[None, :] < n_sel)
    o_nsa = _nsa(q, kcmp, vcmpt, ksel, vselt, kwin, vwint, ngt, jnp.asarray(overlap.T, BF16))

    kmem, vmem = _memkv(mem, row(mem_in_norm), w_mem_kv, row(mem_k_norm))
    o_mem = _memattn(mq, kmem, vmem)

    x2 = _outproj(x1, o_gla, o_nsa, o_mem, w_out)
    return _ffn(x2, row(ffn2_norm), ffn2_w_gate, ffn2_w_up, ffn2_w_down, row(final_norm))


def kernel(x, mem, positions, ffn1_norm, ffn1_w_gate, ffn1_w_up, ffn1_w_down, mix_norm, w_in, gla_w_a, gla_b_a, gla_o_norm, nsa_q_norm, nsa_k_norm, nsa_cmp_pos_k, nsa_cmp_w1_k, nsa_cmp_w2_k, nsa_cmp_pos_v, nsa_cmp_w1_v, nsa_cmp_w2_v, mem_in_norm, w_mem_kv, mem_q_norm, mem_k_norm, w_out, ffn2_norm, ffn2_w_gate, ffn2_w_up, ffn2_w_down, final_norm):
    depth = ffn1_norm.shape[0]
    batch, s, d = x.shape
    outs = []
    for b in range(batch):
        xb, mem_b, pos_b = (x.reshape(s, d), mem.reshape(mem.shape[1:]), positions.reshape(s)) if batch == 1 \
            else (x[b], mem[b], positions[b])
        for l in range(depth):
            xb = _layer(xb, mem_b, pos_b, ffn1_norm[l], ffn1_w_gate[l], ffn1_w_up[l], ffn1_w_down[l],
                        mix_norm[l], w_in[l], gla_w_a[l], gla_b_a[l], gla_o_norm[l], nsa_q_norm[l],
                        nsa_k_norm[l], nsa_cmp_pos_k[l], nsa_cmp_w1_k[l], nsa_cmp_w2_k[l], nsa_cmp_pos_v[l],
                        nsa_cmp_w1_v[l], nsa_cmp_w2_v[l], mem_in_norm[l], w_mem_kv[l], mem_q_norm[l],
                        mem_k_norm[l], w_out[l], ffn2_norm[l], ffn2_w_gate[l], ffn2_w_up[l], ffn2_w_down[l],
                        final_norm[l])
        outs.append(xb)
    return outs[0].reshape(1, s, d) if batch == 1 else jnp.stack(outs)
```

```python
import functools

import numpy as np
import jax
import jax.numpy as jnp
from jax import lax
from jax.experimental import pallas as pl
from jax.experimental.pallas import tpu as pltpu

F32 = jnp.float32
BF16 = jnp.bfloat16

HEAD_DIM = 128
GLA_HEADS = 4
GLA_DK = 64
GLA_DV = 128
GLA_LOWRANK = 16
GLA_TAU = 16.0
NSA_HEADS = 8
NSA_KV_HEADS = 2
NSA_HPG = NSA_HEADS // NSA_KV_HEADS
CMP_LEN = 32
CMP_STRIDE = 16
SEL_LEN = 64
SEL_TOPK = 16
WINDOW = 512
MEM_HEADS = 4
MACARON_W = 0.5
QBLK = 128
ROPE_THETA = 10000.0
EPS = 1e-6
NEG_INF = -1e30
TINY = 1e-30
FORCE_SCORE = 1e4
LOG2_E = 1.4426950408889634

LANES = 128
VMEM_LIMIT = 56 * 1024 * 1024

GLA_SUB = 16
GLA_ROWS = 256
SEL_KT = 256
SEL_SPAN_TILES = 4
V_ROWS = HEAD_DIM + 16
SEL_MASK_BIAS = -32768.0
SEL_PAD_SCORE = -3e4


def _dot(a, b):
    return jnp.dot(a, b, preferred_element_type=F32)


def _dot_nt(a, b):
    return lax.dot_general(a, b, (((1,), (1,)), ((), ())), preferred_element_type=F32)


def _dot_tn(a, b):
    return lax.dot_general(a, b, (((0,), (0,)), ((), ())), preferred_element_type=F32)


def _split2(x):
    hi = x.astype(BF16)
    return hi, (x - hi.astype(F32)).astype(BF16)


def _rms(x, g):
    return x * lax.rsqrt(jnp.mean(x * x, axis=-1, keepdims=True) + EPS) * g


def _sigmoid(x):
    return 1.0 / (1.0 + jnp.exp(-x))


def _params(sem):
    return pltpu.CompilerParams(dimension_semantics=sem, vmem_limit_bytes=VMEM_LIMIT)


def _resident(shape):
    nd = len(shape)
    return pl.BlockSpec(shape, lambda *_: (0,) * nd, pipeline_mode=pl.Buffered(1))


def _ffn_body(*refs, final, nf):
    if final:
        x_ref, g_ref, wg_ref, wu_ref, wd_ref, fg_ref, o_ref, h_ref = refs
    else:
        x_ref, g_ref, wg_ref, wu_ref, wd_ref, o_ref, h_ref = refs
    f = pl.program_id(1)

    @pl.when(f == 0)
    def _():
        x = x_ref[...]
        h_ref[...] = _rms(x, g_ref[...]).astype(BF16)
        o_ref[...] = x

    h = h_ref[...]
    g = _dot(h, wg_ref[...].astype(BF16))
    u = _dot(h, wu_ref[...].astype(BF16))
    a = (g * _sigmoid(g)) * u * MACARON_W
    o_ref[...] += _dot(a.astype(BF16), wd_ref[...].astype(BF16))

    if final:
        @pl.when(f == nf - 1)
        def _():
            o_ref[...] = _rms(o_ref[...], fg_ref[...])


def _ffn(x, norm_g, wg, wu, wd, final_g=None, *, tm=1024, tf=256):
    s, d = x.shape
    ff = wg.shape[1]
    nf = ff // tf
    final = final_g is not None
    in_specs = [
        pl.BlockSpec((tm, d), lambda i, f: (i, 0)),
        pl.BlockSpec((1, d), lambda i, f: (0, 0)),
        pl.BlockSpec((d, tf), lambda i, f: (0, f)),
        pl.BlockSpec((d, tf), lambda i, f: (0, f)),
        pl.BlockSpec((tf, d), lambda i, f: (f, 0)),
    ]
    args = [x, norm_g, wg, wu, wd]
    if final:
        in_specs.append(pl.BlockSpec((1, d), lambda i, f: (0, 0)))
        args.append(final_g)
    return pl.pallas_call(
        functools.partial(_ffn_body, final=final, nf=nf),
        grid=(s // tm, nf),
        in_specs=in_specs,
        out_specs=pl.BlockSpec((tm, d), lambda i, f: (i, 0)),
        out_shape=jax.ShapeDtypeStruct((s, d), F32),
        scratch_shapes=[pltpu.VMEM((tm, d), BF16)],
        compiler_params=_params(("parallel", "arbitrary")),
        name="ffn_final" if final else "ffn",
    )(*args)


_GQK_W = 2 * GLA_HEADS * GLA_DK
_GV_W = GLA_HEADS * GLA_DV
_KV_W = NSA_KV_HEADS * HEAD_DIM
_PER_G = NSA_HPG * 3
_NG_PAD = 16
_R_GQK = 0
_R_GV = _R_GQK + _GQK_W
_R_GR = _R_GV + _GV_W
_R_GA = _R_GR + _GV_W
_R_NQ = _R_GA + GLA_LOWRANK
_R_KC = _R_NQ + NSA_HEADS * HEAD_DIM
_R_KS = _R_KC + 2 * _KV_W
_R_VS = _R_KS + _KV_W
_R_KW = _R_VS + _KV_W
_R_VW = _R_KW + _KV_W
_R_NG = _R_VW + _KV_W
_R_MQ = _R_NG + NSA_KV_HEADS * _PER_G
_R_END = _R_MQ + MEM_HEADS * HEAD_DIM


def _rope(x, cos, sin_signed):
    return x * cos + pltpu.roll(x, HEAD_DIM // 2, 1) * sin_signed


def _proj_body(x_ref, g_ref, wt_ref, wmq_ref, pos_ref, inv_ref, sgn_ref, qn_ref, kn_ref,
               mqn_ref, q_ref, ksel_ref, vselt_ref, kwin_ref, vwint_ref, mq_ref, gqk_ref,
               gv_ref, gr_ref, kcvc_ref, ga_ref, ngt_ref, *, tm):
    i = pl.program_id(0)
    h = _rms(x_ref[...], g_ref[...]).astype(BF16)

    def proj(r0, width):
        return _dot_nt(h, wt_ref[r0:r0 + width, :])

    def proj_t(r0, width):
        return _dot_nt(wt_ref[r0:r0 + width, :], h)

    ang = pos_ref[...].astype(F32) * inv_ref[...]
    cos = jnp.cos(ang)
    sin_signed = jnp.sin(ang) * sgn_ref[...]

    def norm_rope(t, gain):
        return _rope(_rms(t, gain), cos, sin_signed)

    scale = HEAD_DIM ** -0.5
    nq = proj(_R_NQ, NSA_HEADS * HEAD_DIM)
    for hd in range(NSA_HEADS):
        sl = slice(hd * HEAD_DIM, (hd + 1) * HEAD_DIM)
        q_ref[:, sl] = (norm_rope(nq[:, sl], qn_ref[...]) * (scale * LOG2_E)).astype(BF16)

    tok = i * tm + lax.broadcasted_iota(jnp.int32, (tm, LANES), 0)
    blk = lax.broadcasted_iota(jnp.int32, (tm, LANES), 1)
    onehot = jnp.where(lax.shift_right_logical(tok, 6) == blk, 1.0, 0.0).astype(BF16)
    ks = proj(_R_KS, _KV_W)
    kw = proj(_R_KW, _KV_W)
    vst = proj_t(_R_VS, _KV_W)
    vwt = proj_t(_R_VW, _KV_W)
    ngt = proj_t(_R_NG, NSA_KV_HEADS * _PER_G)
    ones_rows = jnp.where(lax.broadcasted_iota(jnp.int32, (V_ROWS - HEAD_DIM, tm), 0) == 0, 1.0, 0.0).astype(BF16)
    for g in range(NSA_KV_HEADS):
        sl = slice(g * HEAD_DIM, (g + 1) * HEAD_DIM)
        ksel_ref[g, :, 0:HEAD_DIM] = norm_rope(ks[:, sl], kn_ref[1:2, :]).astype(BF16)
        ksel_ref[g, :, HEAD_DIM:2 * HEAD_DIM] = onehot
        kwin_ref[g] = norm_rope(kw[:, sl], kn_ref[2:3, :]).astype(BF16)
        vselt_ref[g, 0:HEAD_DIM, :] = vst[sl, :].astype(BF16)
        vwint_ref[g, 0:HEAD_DIM, :] = vwt[sl, :].astype(BF16)
        vselt_ref[g, HEAD_DIM:V_ROWS, :] = ones_rows
        vwint_ref[g, HEAD_DIM:V_ROWS, :] = ones_rows
        ngt_ref[g, 0:_PER_G, :] = ngt[g * _PER_G:(g + 1) * _PER_G, :]
        ngt_ref[g, _PER_G:_NG_PAD, :] = jnp.zeros((_NG_PAD - _PER_G, tm), F32)

    mq = _dot_nt(h, wmq_ref[...])
    for hd in range(MEM_HEADS):
        sl = slice(hd * HEAD_DIM, (hd + 1) * HEAD_DIM)
        mq_ref[:, sl] = (_rms(mq[:, sl], mqn_ref[...]) * scale).astype(BF16)

    gqk_ref[...] = proj(_R_GQK, _GQK_W)
    gv_ref[...] = proj(_R_GV, _GV_W).astype(BF16)
    gr_ref[...] = proj(_R_GR, _GV_W)
    ga_ref[...] = proj(_R_GA, GLA_LOWRANK)
    kcvc = proj(_R_KC, 2 * _KV_W)
    for j in range(2 * NSA_KV_HEADS):
        kcvc_ref[j] = kcvc[:, j * HEAD_DIM:(j + 1) * HEAD_DIM]


def _proj(x1, mix_g, w_t, w_mq_t, pos_col, inv128, sgn128, q_norm, k_norm, mq_norm, *, tm=512):
    s, d = x1.shape
    row = lambda w: pl.BlockSpec((tm, w), lambda i: (i, 0))
    grp = lambda w: pl.BlockSpec((NSA_KV_HEADS, tm, w), lambda i: (0, i, 0))
    grpt = pl.BlockSpec((NSA_KV_HEADS, V_ROWS, tm), lambda i: (0, 0, i))
    vt_shape = jax.ShapeDtypeStruct((NSA_KV_HEADS, V_ROWS, s), BF16)
    out_shapes = [
        (jax.ShapeDtypeStruct((s, NSA_HEADS * HEAD_DIM), BF16), row(NSA_HEADS * HEAD_DIM)),
        (jax.ShapeDtypeStruct((NSA_KV_HEADS, s, 2 * HEAD_DIM), BF16), grp(2 * HEAD_DIM)),
        (vt_shape, grpt),
        (jax.ShapeDtypeStruct((NSA_KV_HEADS, s, HEAD_DIM), BF16), grp(HEAD_DIM)),
        (vt_shape, grpt),
        (jax.ShapeDtypeStruct((s, MEM_HEADS * HEAD_DIM), BF16), row(MEM_HEADS * HEAD_DIM)),
        (jax.ShapeDtypeStruct((s, 512), F32), row(512)),
        (jax.ShapeDtypeStruct((s, 512), BF16), row(512)),
        (jax.ShapeDtypeStruct((s, 512), F32), row(512)),
        (jax.ShapeDtypeStruct((2 * NSA_KV_HEADS, s, HEAD_DIM), F32),
         pl.BlockSpec((2 * NSA_KV_HEADS, tm, HEAD_DIM), lambda i: (0, i, 0))),
        (jax.ShapeDtypeStruct((s, GLA_LOWRANK), F32), row(GLA_LOWRANK)),
        (jax.ShapeDtypeStruct((NSA_KV_HEADS, _NG_PAD, s), F32),
         pl.BlockSpec((NSA_KV_HEADS, _NG_PAD, tm), lambda i: (0, 0, i))),
    ]
    return pl.pallas_call(
        functools.partial(_proj_body, tm=tm),
        grid=(s // tm,),
        in_specs=[
            row(d),
            _resident((1, d)),
            _resident(w_t.shape), _resident(w_mq_t.shape),
            pl.BlockSpec((tm, 1), lambda i: (i, 0)),
            _resident((1, LANES)), _resident((1, LANES)),
            _resident((1, HEAD_DIM)), _resident((3, HEAD_DIM)), _resident((1, HEAD_DIM)),
        ],
        out_specs=[o[1] for o in out_shapes],
        out_shape=[o[0] for o in out_shapes],
        compiler_params=_params(("parallel",)),
        name="proj",
    )(x1, mix_g, w_t, w_mq_t, pos_col, inv128, sgn128, q_norm, k_norm, mq_norm)


def _compress_body(kcvc_ref, w1k_ref, w2k_ref, pek_ref, w1v_ref, w2v_ref, pev_ref, kn_ref,
                   pos_ref, inv_ref, sgn_ref, kcmp_ref, vcmp_ref, *, units):
    half = CMP_LEN // 2
    ang = pos_ref[...].astype(F32) * inv_ref[...]
    cos = jnp.cos(ang)
    sin_signed = jnp.sin(ang) * sgn_ref[...]
    for kind, (w1_ref, w2_ref, pe_ref) in enumerate(((w1k_ref, w2k_ref, pek_ref),
                                                     (w1v_ref, w2v_ref, pev_ref))):
        for g in range(NSA_KV_HEADS):
            slab = kind * NSA_KV_HEADS + g
            a = jnp.zeros((units, w1_ref.shape[1]), F32)
            b = jnp.zeros((units, w1_ref.shape[1]), F32)
            for l in range(half):
                t = kcvc_ref[slab, pl.ds(l, units, stride=CMP_STRIDE), :]
                a = a + _dot((t + pe_ref[l:l + 1, :]).astype(BF16),
                             w1_ref[l * HEAD_DIM:(l + 1) * HEAD_DIM, :].astype(BF16))
                b = b + _dot((t + pe_ref[half + l:half + l + 1, :]).astype(BF16),
                             w1_ref[(half + l) * HEAD_DIM:(half + l + 1) * HEAD_DIM, :].astype(BF16))
            hid = a + pltpu.roll(b, units - 1, 0)
            act = (hid * _sigmoid(hid)).astype(BF16)
            if kind == 0:
                c = _rope(_rms(_dot(act, w2_ref[...]), kn_ref[0:1, :]), cos, sin_signed)
                kcmp_ref[g] = c.astype(BF16)
            else:
                vcmp_ref[g] = _dot_nt(w2_ref[...], act).astype(BF16)


def _compress(kcvc, w1k, w2k, pek, w1v, w2v, pev, k_norm, pos_cmp, inv128, sgn128):
    s = kcvc.shape[1]
    units = s // CMP_STRIDE
    shp = jax.ShapeDtypeStruct((NSA_KV_HEADS, units, HEAD_DIM), BF16)
    shp_t = jax.ShapeDtypeStruct((NSA_KV_HEADS, HEAD_DIM, units), BF16)
    return pl.pallas_call(
        functools.partial(_compress_body, units=units),
        out_shape=[shp, shp_t],
        compiler_params=pltpu.CompilerParams(vmem_limit_bytes=VMEM_LIMIT),
        name="compress",
    )(kcvc, w1k, w2k, pek, w1v, w2v, pev, k_norm, pos_cmp, inv128, sgn128)


def _gla_body(gqk_ref, gv_ref, gr_ref, ga_ref, ga_next_ref, wa_ref, ba_ref, on_ref, tcum_ref, bd_ref, hsel_ref,
              o_ref, st_ref, b_s):
    rows = GLA_ROWS
    npair = GLA_HEADS // 2

    def cum_log_decay(ga):
        z = ba_ref[...]
        for ga_t in _split2(ga):
            for wa_t in _split2(wa_ref[...]):
                z = z + _dot(ga_t, wa_t)
        la = (jnp.minimum(z, 0.0) - jnp.log(1.0 + jnp.exp(-jnp.abs(z)))) / GLA_TAU
        bcum = jnp.zeros_like(la)
        for la_t in _split2(la):
            bcum = bcum + _dot(tcum_ref[...], la_t)
        return bcum * LOG2_E

    @pl.when(pl.program_id(0) == 0)
    def _():
        st_ref[...] = jnp.zeros_like(st_ref)
        b_s[...] = cum_log_decay(ga_ref[...])

    b_all = b_s[...]
    q_all = gqk_ref[:, 0:256] * (GLA_DK ** -0.5)
    k_all = gqk_ref[:, 256:512]
    v_all = gv_ref[...]

    row_i = lax.broadcasted_iota(jnp.int32, (GLA_SUB, LANES), 0)

    sts = [st_ref[p] for p in range(npair)]
    o_rows = []
    for sb in range(rows // GLA_SUB):
        rs = slice(sb * GLA_SUB, (sb + 1) * GLA_SUB)
        o_pairs = []
        for p in range(npair):
            cs = slice(p * LANES, (p + 1) * LANES)
            vs = slice(p * 2 * GLA_DV, (p + 1) * 2 * GLA_DV)
            qs = q_all[rs, cs]
            kk = k_all[rs, cs]
            bb = b_all[rs, cs]
            vp = v_all[rs, vs]
            vpf = vp.astype(F32)
            blast = bb[GLA_SUB - 1:GLA_SUB, :]
            st = sts[p]
            o_inter = _dot_nt((qs * jnp.exp2(bb)).astype(BF16), st.astype(BF16))
            xs = []
            for j in range(GLA_SUB):
                dlt = jnp.where(row_i >= j, bb - bb[j:j + 1, :], NEG_INF)
                xs.append(qs * jnp.exp2(dlt) * kk[j:j + 1, :])
            red = _dot(jnp.concatenate(xs, axis=0).astype(BF16), hsel_ref[...])
            acc = o_inter
            for j in range(GLA_SUB):
                acc = acc + red[j * GLA_SUB:(j + 1) * GLA_SUB, :] * vpf[j:j + 1, :]
            o_pairs.append(acc)
            kd = (kk * jnp.exp2(blast - bb)).astype(BF16)
            upd = _dot_tn(vp, kd)
            sts[p] = st * jnp.exp2(blast) + upd * bd_ref[...]
        o_rows.append(jnp.concatenate(o_pairs, axis=1))
    for p in range(npair):
        st_ref[p] = sts[p]
    b_s[...] = cum_log_decay(ga_next_ref[...])

    o_all = jnp.concatenate(o_rows, axis=0)
    gr = gr_ref[...]
    for hd in range(GLA_HEADS):
        sl = slice(hd * GLA_DV, (hd + 1) * GLA_DV)
        r = gr[:, sl]
        o_ref[:, sl] = (_rms(o_all[:, sl], on_ref[...]) * (r * _sigmoid(r))).astype(BF16)


def _gla(gqk, gv, gr, ga, wa, ba, o_norm):
    s = gqk.shape[0]
    rows = GLA_ROWS
    idx = np.arange(rows)
    tcum = ((idx[:, None] >= idx[None, :]) & (idx[:, None] // GLA_SUB == idx[None, :] // GLA_SUB))
    tcum = jnp.asarray(tcum, BF16)
    r256 = np.arange(2 * GLA_DV)[:, None] // GLA_DV
    c128 = np.arange(LANES)[None, :] // GLA_DK
    bdmask = jnp.asarray(r256 == c128, F32)
    hsel = jnp.asarray((r256 == c128).T, BF16)
    row = lambda w: pl.BlockSpec((rows, w), lambda i: (i, 0))
    nblk = s // rows
    ga_next = pl.BlockSpec((rows, GLA_LOWRANK), lambda i: (jnp.minimum(i + 1, nblk - 1), 0))
    return pl.pallas_call(
        _gla_body,
        grid=(nblk,),
        in_specs=[row(512), row(512), row(512), row(GLA_LOWRANK), ga_next,
                  _resident(wa.shape), _resident(ba.shape), _resident(o_norm.shape),
                  _resident(tcum.shape), _resident(bdmask.shape), _resident(hsel.shape)],
        out_specs=row(512),
        out_shape=jax.ShapeDtypeStruct((s, GLA_HEADS * GLA_DV), BF16),
        scratch_shapes=[pltpu.VMEM((GLA_HEADS // 2, 2 * GLA_DV, LANES), F32),
                        pltpu.VMEM((rows, GLA_HEADS * GLA_DK), F32)],
        compiler_params=_params(("arbitrary",)),
        name="gla",
    )(gqk, gv, gr, ga, ga, wa, ba, o_norm, tcum, bdmask, hsel)


def _nsa_body(q_ref, kcmp_ref, vcmpt_ref, ksel_ref, vselt_ref, kwin_ref, vwint_ref, ngt_ref, ovlt_ref,
              o_ref, qt_s, s_s, p_s, acc_s, *, n_sel):
    qb = pl.program_id(0)
    t0 = qb * QBLK
    cols = NSA_HPG * QBLK
    gw = NSA_HPG * HEAD_DIM
    ncmp = kcmp_ref.shape[1]
    groups = range(NSA_KV_HEADS)
    span = SEL_SPAN_TILES * SEL_KT

    def tq_of(rows):
        return t0 + (lax.broadcasted_iota(jnp.int32, (rows, cols), 1) & (QBLK - 1))

    def gate_row(gates_t, c):
        return jnp.concatenate([gates_t[3 * hd + c:3 * hd + c + 1, :] for hd in range(NSA_HPG)], axis=1)

    def span_scores(g, j):
        k0 = pl.multiple_of(j * span, span)
        return _dot(ksel_ref[g, pl.ds(k0, span), :], qt_s[g])

    def span_pv(g, j, p):
        k0 = pl.multiple_of(j * span, span)
        return _dot(vselt_ref[g, :, pl.ds(k0, span)], p)

    def prologue(g):
        qt = jnp.concatenate(
            [q_ref[:, g * gw + hd * HEAD_DIM:g * gw + (hd + 1) * HEAD_DIM].astype(F32).T.astype(BF16)
             for hd in range(NSA_HPG)], axis=1)
        gates_t = _sigmoid(ngt_ref[g])

        def win_part(start, length):
            start = pl.multiple_of(jnp.maximum(start, 0), QBLK)
            return (_dot(kwin_ref[g, pl.ds(start, length), :], qt), vwint_ref[g, :, pl.ds(start, length)], start)

        s_c = _dot(kcmp_ref[g], qt)
        s_old, v_old, _ = win_part(t0 - WINDOW, QBLK)
        s_mid, v_mid, mid0 = win_part(t0 - WINDOW + QBLK, WINDOW - QBLK)
        s_dg, v_dg, _ = win_part(t0, QBLK)
        yield

        n_row = lax.broadcasted_iota(jnp.int32, (ncmp, cols), 0)
        valid_c = n_row <= lax.shift_right_arithmetic(tq_of(1) - (CMP_LEN - 1), CMP_STRIDE.bit_length() - 1)
        s_c = jnp.where(valid_c, s_c, NEG_INF)
        e_c = jnp.where(valid_c, jnp.exp2(s_c - jnp.max(s_c, axis=0, keepdims=True)), 0.0)
        p_c = e_c / jnp.maximum(jnp.sum(e_c, axis=0, keepdims=True), TINY)
        out_pre = gate_row(gates_t, 0) * _dot(vcmpt_ref[g], p_c.astype(BF16))
        psum = p_c[:, 0:QBLK]
        for hd in range(1, NSA_HPG):
            psum = psum + p_c[:, hd * QBLK:(hd + 1) * QBLK]
        imp = jnp.zeros((LANES, QBLK), F32)
        for p_t in _split2(psum):
            imp = imp + _dot(ovlt_ref[...], p_t)
        yield

        tq = t0 + lax.broadcasted_iota(jnp.int32, (LANES, QBLK), 1)
        m_i = lax.broadcasted_iota(jnp.int32, (LANES, QBLK), 0)
        cur = lax.shift_right_logical(tq, 6)
        forced = (m_i == 0) | (m_i == cur) | (m_i == cur - 1)
        causal = m_i * SEL_LEN <= tq
        n_forced = 3
        score = jnp.where(causal, jnp.where(forced, -jnp.inf, imp), -FORCE_SCORE)
        score = jnp.where(m_i < n_sel, score, SEL_PAD_SCORE)
        m_f = m_i.astype(F32)
        for _ in range(min(SEL_TOPK, n_sel) - n_forced):
            mx = jnp.max(score, axis=0, keepdims=True)
            first = jnp.min(jnp.where(score == mx, m_f, float(LANES)), axis=0, keepdims=True)
            score = jnp.where(m_f == first, -jnp.inf, score)
        bias = jnp.where(score == -jnp.inf, 0.0, SEL_MASK_BIAS).astype(BF16)
        qext = jnp.concatenate([qt, jnp.concatenate([bias] * NSA_HPG, axis=1)], axis=0)
        sc0 = _dot(ksel_ref[g, 0:span, :], qext)
        yield

        w_row = lax.broadcasted_iota(jnp.int32, (QBLK, cols), 0)
        tq_w = tq_of(QBLK)
        kp_old = t0 - WINDOW + w_row
        valid_old = (kp_old > tq_w - WINDOW) & (kp_old >= 0)
        s_old = jnp.where(valid_old, s_old, NEG_INF)
        mid_row = mid0 + lax.broadcasted_iota(jnp.int32, (WINDOW - QBLK, cols), 0)
        s_mid = jnp.where(mid_row < t0, s_mid, NEG_INF)
        valid_dg = t0 + w_row <= tq_w
        s_dg = jnp.where(valid_dg, s_dg, NEG_INF)
        m_w = jnp.maximum(jnp.maximum(jnp.max(s_old, axis=0, keepdims=True),
                                      jnp.max(s_mid, axis=0, keepdims=True)),
                          jnp.max(s_dg, axis=0, keepdims=True))
        p_old = jnp.where(valid_old, jnp.exp2(s_old - m_w), 0.0)
        p_mid = jnp.exp2(s_mid - m_w)
        p_dg = jnp.where(valid_dg, jnp.exp2(s_dg - m_w), 0.0)
        acc_w = (_dot(v_old, p_old.astype(BF16)) + _dot(v_mid, p_mid.astype(BF16))
                 + _dot(v_dg, p_dg.astype(BF16)))
        out_pre = out_pre + gate_row(gates_t, 2) * (acc_w[0:HEAD_DIM, :]
                                                    / jnp.maximum(acc_w[HEAD_DIM:HEAD_DIM + 1, :], TINY))

        return out_pre, gate_row(gates_t, 1), qext, sc0

    pre = {}
    gens = {g: prologue(g) for g in groups}
    order = [0] + [g for _ in range(8) for g in groups]
    for g in order:
        if g not in pre:
            try:
                next(gens[g])
            except StopIteration as done:
                pre[g] = done.value
    assert len(pre) == len(groups)
    for g in groups:
        qt_s[g] = pre[g][2]
        s_s[g, 0] = pre[g][3]
        p_s[g, 1] = jnp.zeros((span, cols), BF16)
        acc_s[g] = jnp.zeros((V_ROWS, cols), F32)

    def sel_step(cur, j, ms):
        nxt = 1 - cur
        out = []
        for g in groups:
            pv = span_pv(g, jnp.maximum(j - 1, 0), p_s[g, nxt])
            s = s_s[g, cur]
            m_new = jnp.maximum(ms[g], jnp.max(s, axis=0, keepdims=True))
            p_s[g, cur] = jnp.exp2(s - m_new).astype(BF16)
            acc_s[g] = jnp.exp2(ms[g] - m_new) * (acc_s[g] + pv)
            out.append(m_new)
        for g in groups:
            s_s[g, nxt] = span_scores(g, j + 1)
        return tuple(out)

    def sel_body(j, ms):
        return lax.cond((j & 1) == 0, lambda c: sel_step(0, j, c), lambda c: sel_step(1, j, c), ms)

    n_span = t0 // span
    ms = lax.fori_loop(0, n_span, sel_body, tuple(jnp.full((1, cols), NEG_INF, F32) for _ in groups))

    slot = n_span & 1
    base = pl.multiple_of(n_span * span, span)

    def diag(nk):
        valid = lax.broadcasted_iota(jnp.int32, (nk, cols), 0) <= tq_of(1) - base
        for g in groups:
            pv = span_pv(g, jnp.maximum(n_span - 1, 0), p_s[g, 1 - slot])
            s = jnp.where(valid, s_s[g, slot, 0:nk, :], NEG_INF)
            m_new = jnp.maximum(ms[g], jnp.max(s, axis=0, keepdims=True))
            p = jnp.where(valid, jnp.exp2(s - m_new), 0.0).astype(BF16)
            acc_s[g] = (jnp.exp2(ms[g] - m_new) * (acc_s[g] + pv)
                        + _dot(vselt_ref[g, :, pl.ds(base, nk)], p))

    lax.cond(t0 - base >= span // 2, lambda: diag(span), lambda: diag(span // 2))

    for g in groups:
        out_pre, gate_sel = pre[g][0:2]
        acc = acc_s[g]
        o_slc = acc[0:HEAD_DIM, :] / jnp.maximum(acc[HEAD_DIM:HEAD_DIM + 1, :], TINY)
        out = out_pre + gate_sel * o_slc
        for hd in range(NSA_HPG):
            o_ref[:, g * gw + hd * HEAD_DIM:g * gw + (hd + 1) * HEAD_DIM] = (
                out[:, hd * QBLK:(hd + 1) * QBLK].T.astype(BF16))


def _nsa(q, kcmp, vcmpt, ksel, vselt, kwin, vwint, ngt, overlap_t):
    s = q.shape[0]
    n_sel = s // SEL_LEN
    span = SEL_SPAN_TILES * SEL_KT
    assert n_sel <= LANES and s % span == 0 and s >= WINDOW + QBLK and CMP_STRIDE & (CMP_STRIDE - 1) == 0
    cols = NSA_HPG * QBLK
    ng = NSA_KV_HEADS
    return pl.pallas_call(
        functools.partial(_nsa_body, n_sel=n_sel),
        grid=(s // QBLK,),
        in_specs=[
            pl.BlockSpec((QBLK, NSA_HEADS * HEAD_DIM), lambda b: (b, 0)),
            _resident(kcmp.shape), _resident(vcmpt.shape),
            _resident(ksel.shape), _resident(vselt.shape), _resident(kwin.shape), _resident(vwint.shape),
            pl.BlockSpec((ng, _NG_PAD, QBLK), lambda b: (0, 0, b)),
            _resident(overlap_t.shape),
        ],
        out_specs=pl.BlockSpec((QBLK, NSA_HEADS * HEAD_DIM), lambda b: (b, 0)),
        out_shape=jax.ShapeDtypeStruct((s, NSA_HEADS * HEAD_DIM), BF16),
        scratch_shapes=[pltpu.VMEM((ng, 2 * HEAD_DIM, cols), BF16),
                        pltpu.VMEM((ng, 2, span, cols), F32),
                        pltpu.VMEM((ng, 2, span, cols), BF16),
                        pltpu.VMEM((ng, V_ROWS, cols), F32)],
        compiler_params=_params(("arbitrary",)),
        name="nsa",
    )(q, kcmp, vcmpt, ksel, vselt, kwin, vwint, ngt, overlap_t)


def _memkv_body(mem_ref, g_ref, w_ref, kn_ref, k_ref, v_ref):
    kv = _dot(_rms(mem_ref[...], g_ref[...]).astype(BF16), w_ref[...].astype(BF16))
    width = MEM_HEADS * HEAD_DIM
    for hd in range(MEM_HEADS):
        sl = slice(hd * HEAD_DIM, (hd + 1) * HEAD_DIM)
        k_ref[:, sl] = _rms(kv[:, sl], kn_ref[...]).astype(BF16)
    v_ref[...] = kv[:, width:].astype(BF16)


def _memkv(mem, in_g, w_kv, k_norm):
    m = mem.shape[0]
    shp = jax.ShapeDtypeStruct((m, MEM_HEADS * HEAD_DIM), BF16)
    return pl.pallas_call(
        _memkv_body, out_shape=[shp, shp],
        compiler_params=pltpu.CompilerParams(vmem_limit_bytes=VMEM_LIMIT),
        name="memkv",
    )(mem, in_g, w_kv, k_norm)


def _memattn_body(q_ref, k_ref, v_ref, o_ref):
    for hd in range(MEM_HEADS):
        sl = slice(hd * HEAD_DIM, (hd + 1) * HEAD_DIM)
        sc = _dot_nt(q_ref[:, sl], k_ref[:, sl])
        e = jnp.exp(sc - jnp.max(sc, axis=-1, keepdims=True))
        p = e / jnp.sum(e, axis=-1, keepdims=True)
        o_ref[:, sl] = _dot(p.astype(BF16), v_ref[:, sl]).astype(BF16)


def _memattn(q, k, v, *, tm=512):
    s, w = q.shape
    return pl.pallas_call(
        _memattn_body,
        grid=(s // tm,),
        in_specs=[pl.BlockSpec((tm, w), lambda i: (i, 0)), _resident(k.shape), _resident(v.shape)],
        out_specs=pl.BlockSpec((tm, w), lambda i: (i, 0)),
        out_shape=jax.ShapeDtypeStruct((s, w), BF16),
        compiler_params=_params(("parallel",)),
        name="memattn",
    )(q, k, v)


def _outproj_body(x_ref, a_ref, b_ref, c_ref, w_ref, o_ref):
    na, nb = a_ref.shape[1], b_ref.shape[1]
    o_ref[...] = (x_ref[...] + _dot(a_ref[...], w_ref[0:na, :].astype(BF16))
                  + _dot(b_ref[...], w_ref[na:na + nb, :].astype(BF16))
                  + _dot(c_ref[...], w_ref[na + nb:, :].astype(BF16)))


def _outproj(x1, o_gla, o_nsa, o_mem, w_out, *, tm=512):
    s, d = x1.shape
    row = lambda w: pl.BlockSpec((tm, w), lambda i: (i, 0))
    return pl.pallas_call(
        _outproj_body,
        grid=(s // tm,),
        in_specs=[row(d), row(o_gla.shape[1]), row(o_nsa.shape[1]), row(o_mem.shape[1]),
                  _resident(w_out.shape)],
        out_specs=row(d),
        out_shape=jax.ShapeDtypeStruct((s, d), F32),
        compiler_params=_params(("parallel",)),
        name="outproj",
    )(x1, o_gla, o_nsa, o_mem, w_out)


def _layer(x, mem, positions, ffn1_norm, ffn1_w_gate, ffn1_w_up, ffn1_w_down, mix_norm, w_in,
           gla_w_a, gla_b_a, gla_o_norm, nsa_q_norm, nsa_k_norm, nsa_cmp_pos_k, nsa_cmp_w1_k,
           nsa_cmp_w2_k, nsa_cmp_pos_v, nsa_cmp_w1_v, nsa_cmp_w2_v, mem_in_norm, w_mem_kv,
           mem_q_norm, mem_k_norm, w_out, ffn2_norm, ffn2_w_gate, ffn2_w_up, ffn2_w_down, final_norm):
    s, d = x.shape
    row = lambda v: v.reshape(1, -1)
    bf = lambda v: v.astype(BF16)

    x1 = _ffn(x, row(ffn1_norm), ffn1_w_gate, ffn1_w_up, ffn1_w_down)

    half = HEAD_DIM // 2
    inv = ROPE_THETA ** (-jnp.arange(half, dtype=F32) / half)
    inv128 = jnp.concatenate([inv, inv]).reshape(1, HEAD_DIM)
    sgn128 = jnp.concatenate([-jnp.ones((half,), F32), jnp.ones((half,), F32)]).reshape(1, HEAD_DIM)
    assert w_in.shape[1] == _R_END
    w_t = bf(w_in.T)
    (q, ksel, vselt, kwin, vwint, mq, gqk, gv, gr, kcvc, ga, ngt) = _proj(
        x1, row(mix_norm), w_t, w_t[_R_MQ:], positions.reshape(s, 1), inv128, sgn128,
        row(nsa_q_norm), nsa_k_norm, row(mem_q_norm))

    o_gla = _gla(gqk, gv, gr, ga, gla_w_a, row(gla_b_a), row(gla_o_norm))

    units = s // CMP_STRIDE
    n_cmp = (s - CMP_LEN) // CMP_STRIDE + 1
    cmp_last = jnp.arange(units) * CMP_STRIDE + CMP_LEN - 1
    pos_cmp = positions[jnp.minimum(cmp_last, s - 1)].reshape(units, 1)
    kcmp, vcmpt = _compress(kcvc, nsa_cmp_w1_k, bf(nsa_cmp_w2_k), nsa_cmp_pos_k,
                            nsa_cmp_w1_v, bf(nsa_cmp_w2_v.T), nsa_cmp_pos_v, nsa_k_norm,
                            pos_cmp, inv128, sgn128)
    n_sel = s // SEL_LEN
    cmp_start = np.arange(units) * CMP_STRIDE
    sel_start = np.arange(LANES) * SEL_LEN
    overlap = np.clip(np.minimum(cmp_start[:, None] + CMP_LEN, sel_start[None, :] + SEL_LEN)
                      - np.maximum(cmp_start[:, None], sel_start[None, :]), 0, None) / CMP_STRIDE
    overlap = overlap * (np.arange(units)[:, None] < n_cmp) * (np.arange(LANES)[None, :] < n_sel)
    o_nsa = _nsa(q, kcmp, vcmpt, ksel, vselt, kwin, vwint, ngt, jnp.asarray(overlap.T, BF16))

    kmem, vmem = _memkv(mem, row(mem_in_norm), w_mem_kv, row(mem_k_norm))
    o_mem = _memattn(mq, kmem, vmem)

    x2 = _outproj(x1, o_gla, o_nsa, o_mem, w_out)
    return _ffn(x2, row(ffn2_norm), ffn2_w_gate, ffn2_w_up, ffn2_w_down, row(final_norm))


def kernel(x, mem, positions, ffn1_norm, ffn1_w_gate, ffn1_w_up, ffn1_w_down, mix_norm, w_in, gla_w_a, gla_b_a, gla_o_norm, nsa_q_norm, nsa_k_norm, nsa_cmp_pos_k, nsa_cmp_w1_k, nsa_cmp_w2_k, nsa_cmp_pos_v, nsa_cmp_w1_v, nsa_cmp_w2_v, mem_in_norm, w_mem_kv, mem_q_norm, mem_k_norm, w_out, ffn2_norm, ffn2_w_gate, ffn2_w_up, ffn2_w_down, final_norm):
    depth = ffn1_norm.shape[0]
    batch, s, d = x.shape
    outs = []
    for b in range(batch):
        xb, mem_b, pos_b = (x.reshape(s, d), mem.reshape(mem.shape[1:]), positions.reshape(s)) if batch == 1 \
            else (x[b], mem[b], positions[b])
        for l in range(depth):
            xb = _layer(xb, mem_b, pos_b, ffn1_norm[l], ffn1_w_gate[l], ffn1_w_up[l], ffn1_w_down[l],
                        mix_norm[l], w_in[l], gla_w_a[l], gla_b_a[l], gla_o_norm[l], nsa_q_norm[l],
                        nsa_k_norm[l], nsa_cmp_pos_k[l], nsa_cmp_w1_k[l], nsa_cmp_w2_k[l], nsa_cmp_pos_v[l],
                        nsa_cmp_w1_v[l], nsa_cmp_w2_v[l], mem_in_norm[l], w_mem_kv[l], mem_q_norm[l],
                        mem_k_norm[l], w_out[l], ffn2_norm[l], ffn2_w_gate[l], ffn2_w_up[l], ffn2_w_down[l],
                        final_norm[l])
        outs.append(xb)
    return outs[0].reshape(1, s, d) if batch == 1 else jnp.stack(outs)
```
